```python
import math
import jax, jax.numpy as jnp
from jax import lax
import numpy as np

D_MODEL = 1024
BATCH = 2
SEQ = 16384
DEPTH = 1
DEC_BATCH = 128
DEC_SEQ = 4
PAST_LEN = 8192
PAGE_SIZE = 128

HEAD_DIM = 64
HEADS_PER_GROUP = 4
WINDOWS = (128, 512, 2048)
DILATIONS = (1, 4, 16)
N_HEADS = HEADS_PER_GROUP * len(WINDOWS)
ATTN_W = N_HEADS * HEAD_DIM
ATTN_OUT_W = HEADS_PER_GROUP * HEAD_DIM
BLOCK = 128
POOL_WINDOWS = (2, 4, 8, 16)
POOL_GROUPS = len(POOL_WINDOWS)
POOL_GW = 128
POOL_W = POOL_GROUPS * POOL_GW
POOL_STATE = max(POOL_WINDOWS) - 1
N_EXPERT_GROUPS = 4
EXPERTS_PER_GROUP = 8
N_EXPERTS = N_EXPERT_GROUPS * EXPERTS_PER_GROUP
TOP_K = 2
D_EXPERT = 512
MOE_BLOCK = 128
IN_W = 3 * ATTN_W + POOL_W + 2 * D_MODEL
NEG = -1e30
EPS = 1e-6

kernel_name = "hybrid_dilated_pool_hmoe_step"


def rmsnorm(x, g):
    xf = x.astype(jnp.float32)
    y = xf * lax.rsqrt(jnp.mean(xf * xf, axis=-1, keepdims=True) + EPS)
    return (y * g.astype(jnp.float32)).astype(x.dtype)


def project(x, ln1, w_in, q_gain, k_gain):
    B, T, _ = x.shape
    z = rmsnorm(x, ln1) @ w_in
    q, k, v, u, ga, gb = jnp.split(
        z, [ATTN_W, 2 * ATTN_W, 3 * ATTN_W, 3 * ATTN_W + POOL_W, 3 * ATTN_W + POOL_W + D_MODEL], axis=-1)
    q = rmsnorm(q.reshape(B, T, N_HEADS, HEAD_DIM), q_gain)
    k = rmsnorm(k.reshape(B, T, N_HEADS, HEAD_DIM), k_gain)
    v = v.reshape(B, T, N_HEADS, HEAD_DIM)
    return q, k, v, u, ga, gb


def dilated_band_attention(q, k, v, slopes, dil, steps):
    B, T, H, Dh = q.shape
    L = T // dil
    Bs = B * dil

    def to_sub(a):
        return a.reshape(B, L, dil, H, Dh).transpose(0, 2, 1, 3, 4).reshape(Bs, L, H, Dh)

    nb = -(-L // BLOCK)
    Lp = nb * BLOCK
    pad = ((0, 0), (0, Lp - L), (0, 0), (0, 0))
    qs = jnp.pad(to_sub(q), pad).reshape(Bs, nb, BLOCK, H, Dh)
    ks = jnp.pad(to_sub(k), pad).reshape(Bs, nb, BLOCK, H, Dh)
    vs = jnp.pad(to_sub(v), pad).reshape(Bs, nb, BLOCK, H, Dh)
    shift = ((0, 0), (1, 0), (0, 0), (0, 0), (0, 0))
    kband = jnp.concatenate([jnp.pad(ks, shift)[:, :nb], ks], axis=2)
    vband = jnp.concatenate([jnp.pad(vs, shift)[:, :nb], vs], axis=2)

    s = jnp.einsum('bnqhd,bnkhd->bnhqk', qs, kband,
                   preferred_element_type=jnp.float32) * (HEAD_DIM ** -0.5)
    qi = jnp.arange(BLOCK)[:, None]
    kb = jnp.arange(2 * BLOCK)[None, :]
    rel = qi + BLOCK - kb
    keypos = (jnp.arange(nb)[:, None, None] - 1) * BLOCK + kb[None]
    valid = (rel >= 0) & (rel <= steps) & (keypos >= 0)
    bias = -slopes.astype(jnp.float32)[:, None, None] * (dil * rel)[None].astype(jnp.float32)
    s = jnp.where(valid[None, :, None], s + bias[None, None], NEG)
    m = jnp.max(s, axis=-1, keepdims=True)
    p = jnp.exp(s - m)
    den = jnp.sum(p, axis=-1, keepdims=True)
    o = jnp.einsum('bnhqk,bnkhd->bnqhd', p / den, vband.astype(jnp.float32))
    lse = (m + jnp.log(den))[..., 0]

    o = o.reshape(Bs, Lp, H, Dh)[:, :L].reshape(B, dil, L, H, Dh).transpose(0, 2, 1, 3, 4).reshape(B, T, H, Dh)
    lse = lse.transpose(0, 1, 3, 2).reshape(Bs, Lp, H)[:, :L].reshape(B, dil, L, H).transpose(0, 2, 1, 3).reshape(B, T, H)
    return o, lse


def dilated_cached_attention(q, k_new, v_new, k_buf, v_buf, slopes, dil, steps, window):
    Bd, S, H, Dh = q.shape
    Lb = k_buf.shape[1]
    kc = jnp.concatenate([k_buf, k_new.astype(k_buf.dtype)], axis=1)
    vc = jnp.concatenate([v_buf, v_new.astype(v_buf.dtype)], axis=1)
    j = jnp.arange(steps + 1)
    idx = Lb + jnp.arange(S)[:, None] - dil * j[None, :]
    valid = idx >= 0
    idxc = jnp.maximum(idx, 0)
    kg = kc[:, idxc]
    vg = vc[:, idxc]
    s = jnp.einsum('bshd,bsjhd->bhsj', q, kg, preferred_element_type=jnp.float32) * (HEAD_DIM ** -0.5)
    bias = -slopes.astype(jnp.float32)[:, None, None] * (dil * j).astype(jnp.float32)[None, None, :]
    s = jnp.where(valid[None, None], s + bias[None], NEG)
    m = jnp.max(s, axis=-1, keepdims=True)
    p = jnp.exp(s - m)
    den = jnp.sum(p, axis=-1, keepdims=True)
    o = jnp.einsum('bhsj,bsjhd->bshd', p / den, vg.astype(jnp.float32))
    lse = (m + jnp.log(den))[..., 0].transpose(0, 2, 1)
    keep = min(window, Lb + S)
    return o, lse, kc[:, Lb + S - keep:], vc[:, Lb + S - keep:]


def combine_groups(outs, lses, dtype):
    a = jax.nn.softmax(jnp.stack(lses, axis=0), axis=0)
    o = jnp.sum(a[..., None] * jnp.stack(outs, axis=0), axis=0)
    B, T, H, Dh = o.shape
    return o.reshape(B, T, H * Dh).astype(dtype)


def pool_mix(u_ext, pos0, n_new, pool_lin, pool_scale):
    B, Lext, _ = u_ext.shape
    uf = u_ext.astype(jnp.float32).reshape(B, Lext, POOL_GROUPS, POOL_GW)
    cs0 = jnp.pad(jnp.cumsum(uf, axis=1), ((0, 0), (1, 0), (0, 0), (0, 0)))
    rows = jnp.arange(Lext - n_new, Lext)
    pos = pos0 + rows
    means = []
    for g, w in enumerate(POOL_WINDOWS):
        lo = jnp.maximum(rows + 1 - w, 0)
        wsum = cs0[:, rows + 1, g] - cs0[:, lo, g]
        cnt = jnp.minimum(pos + 1, w).astype(jnp.float32)
        means.append(wsum / cnt[None, :, None])
    z = jnp.stack(means, axis=2) - uf[:, Lext - n_new:]
    z = jnp.einsum('btgc,gcd->btgd', z, pool_lin.astype(jnp.float32)).reshape(B, n_new, POOL_W)
    return (z * pool_scale.astype(jnp.float32)).astype(u_ext.dtype)


def merge(x, attn_o, pool_o, ga, gb, w_pa, w_pb, w_o):
    m = jax.nn.sigmoid(ga) * (attn_o @ w_pa) + jax.nn.sigmoid(gb) * (pool_o @ w_pb)
    return x + m @ w_o


def hmoe(h, ln2, w_rg, b_rg, w_re, b_re, w_gate, w_up, w_down):
    B, T, D = h.shape
    N = B * T
    xn = rmsnorm(h, ln2).reshape(N, D)
    gl = (xn @ w_rg).astype(jnp.float32) + b_rg.astype(jnp.float32)
    gsel = jnp.argmax(gl, axis=-1).astype(jnp.int32)
    p_g = jnp.take_along_axis(jax.nn.softmax(gl, axis=-1), gsel[:, None], axis=1)
    el = ((xn @ w_re).astype(jnp.float32) + b_re.astype(jnp.float32)).reshape(N, N_EXPERT_GROUPS, EXPERTS_PER_GROUP)
    el = jnp.take_along_axis(el, gsel[:, None, None], axis=1)[:, 0]
    top_v, top_i = lax.top_k(el, TOP_K)
    wts = p_g * jax.nn.softmax(top_v, axis=-1)
    eid = gsel[:, None] * EXPERTS_PER_GROUP + top_i.astype(jnp.int32)

    A = N * TOP_K
    e = eid.reshape(A)
    tok = jnp.repeat(jnp.arange(N, dtype=jnp.int32), TOP_K)
    w = wts.reshape(A)
    order = jnp.argsort(e)
    e_s, tok_s, w_s = e[order], tok[order], w[order]
    counts = jnp.bincount(e, length=N_EXPERTS)
    padded = (counts + MOE_BLOCK - 1) // MOE_BLOCK * MOE_BLOCK
    starts = jnp.cumsum(counts) - counts
    pad_ends = jnp.cumsum(padded)
    pad_starts = pad_ends - padded
    dest = pad_starts[e_s] + jnp.arange(A, dtype=jnp.int32) - starts[e_s]
    n_blocks = (A + N_EXPERTS * (MOE_BLOCK - 1) + MOE_BLOCK - 1) // MOE_BLOCK
    R = n_blocks * MOE_BLOCK
    row_tok = jnp.full((R,), N, jnp.int32).at[dest].set(tok_s)
    row_w = jnp.zeros((R,), jnp.float32).at[dest].set(w_s)
    blk_e = jnp.minimum(jnp.searchsorted(pad_ends, jnp.arange(n_blocks) * MOE_BLOCK, side='right'), N_EXPERTS - 1)
    x_pad = jnp.concatenate([xn, jnp.zeros((1, D), xn.dtype)], axis=0)
    xb = x_pad[row_tok].reshape(n_blocks, MOE_BLOCK, D)

    def expert_block(args):
        xblk, ex = args
        hid = jax.nn.silu(xblk @ w_gate[ex]) * (xblk @ w_up[ex])
        return hid @ w_down[ex]

    yb = lax.map(expert_block, (xb, blk_e)).reshape(R, D)
    y = jnp.zeros((N + 1, D), jnp.float32).at[row_tok].add(yb.astype(jnp.float32) * row_w[:, None])[:N]
    return (h.astype(jnp.float32) + y.reshape(B, T, D)).astype(h.dtype)


def setup_inputs(seed: int = 0) -> dict:
    key = jax.random.key(seed)
    ks = iter(jax.random.split(key, 32))

    def nrm(shape, scale=1.0):
        return jax.random.normal(next(ks), shape, jnp.float32) * scale

    inp = {}
    inp['x_prompt'] = nrm((BATCH, SEQ, D_MODEL))
    inp['x_sample'] = nrm((DEC_BATCH, DEC_SEQ, D_MODEL))
    for w in WINDOWS:
        lb = min(w, PAST_LEN)
        inp['cache_k_w%d' % w] = nrm((DEC_BATCH, lb, HEADS_PER_GROUP, HEAD_DIM))
        inp['cache_v_w%d' % w] = nrm((DEC_BATCH, lb, HEADS_PER_GROUP, HEAD_DIM))
    inp['state_pool'] = nrm((DEC_BATCH, POOL_STATE, POOL_W))
    inp['ln1'] = 1.0 + nrm((D_MODEL,), 0.02)
    inp['w_in'] = nrm((D_MODEL, IN_W), D_MODEL ** -0.5)
    inp['q_gain'] = 1.0 + nrm((N_HEADS, HEAD_DIM), 0.02)
    inp['k_gain'] = 1.0 + nrm((N_HEADS, HEAD_DIM), 0.02)
    inp['pool_lin'] = nrm((POOL_GROUPS, POOL_GW, POOL_GW), POOL_GW ** -0.5)
    inp['pool_scale'] = 1.0 + nrm((POOL_W,), 0.02)
    inp['w_pa'] = nrm((ATTN_OUT_W, D_MODEL), ATTN_OUT_W ** -0.5)
    inp['w_pb'] = nrm((POOL_W, D_MODEL), POOL_W ** -0.5)
    inp['w_o'] = nrm((D_MODEL, D_MODEL), D_MODEL ** -0.5)
    inp['ln2'] = 1.0 + nrm((D_MODEL,), 0.02)
    inp['w_rg'] = nrm((D_MODEL, N_EXPERT_GROUPS), D_MODEL ** -0.5)
    inp['b_rg'] = nrm((N_EXPERT_GROUPS,), 0.01)
    inp['w_re'] = nrm((D_MODEL, N_EXPERTS), D_MODEL ** -0.5)
    inp['b_re'] = nrm((N_EXPERTS,), 0.01)
    inp['w_gate'] = nrm((N_EXPERTS, D_MODEL, D_EXPERT), D_MODEL ** -0.5)
    inp['w_up'] = nrm((N_EXPERTS, D_MODEL, D_EXPERT), D_MODEL ** -0.5)
    inp['w_down'] = nrm((N_EXPERTS, D_EXPERT, D_MODEL), D_EXPERT ** -0.5)
    return inp


def reference(x_prompt, x_sample, cache_k_w128, cache_v_w128, cache_k_w512, cache_v_w512,
              cache_k_w2048, cache_v_w2048, state_pool, ln1, w_in, q_gain, k_gain, pool_lin, pool_scale,
              w_pa, w_pb, w_o, ln2, w_rg, b_rg, w_re, b_re, w_gate, w_up, w_down):
    slopes = jnp.exp2(-8.0 * jnp.arange(1, N_HEADS + 1, dtype=jnp.float32) / N_HEADS)
    caches = ((cache_k_w128, cache_v_w128), (cache_k_w512, cache_v_w512), (cache_k_w2048, cache_v_w2048))

    for _layer in range(DEPTH):
        T = x_prompt.shape[1]
        q, k, v, u, ga, gb = project(x_prompt, ln1, w_in, q_gain, k_gain)
        outs, lses, pkv = [], [], []
        for g, (w, d) in enumerate(zip(WINDOWS, DILATIONS)):
            hs = slice(g * HEADS_PER_GROUP, (g + 1) * HEADS_PER_GROUP)
            o, l = dilated_band_attention(q[:, :, hs], k[:, :, hs], v[:, :, hs], slopes[hs], d, w // d)
            outs.append(o)
            lses.append(l)
            keep = min(w, T)
            pkv.append((k[:, T - keep:, hs], v[:, T - keep:, hs]))
        attn_o = combine_groups(outs, lses, x_prompt.dtype)
        pool_o = pool_mix(u, 0, T, pool_lin, pool_scale)
        h = merge(x_prompt, attn_o, pool_o, ga, gb, w_pa, w_pb, w_o)
        y_prompt = hmoe(h, ln2, w_rg, b_rg, w_re, b_re, w_gate, w_up, w_down)
        p_pool = u[:, T - POOL_STATE:]

    for _layer in range(DEPTH):
        S = x_sample.shape[1]
        q, k, v, u, ga, gb = project(x_sample, ln1, w_in, q_gain, k_gain)
        outs, lses, skv = [], [], []
        for g, (w, d) in enumerate(zip(WINDOWS, DILATIONS)):
            hs = slice(g * HEADS_PER_GROUP, (g + 1) * HEADS_PER_GROUP)
            o, l, nk, nv = dilated_cached_attention(q[:, :, hs], k[:, :, hs], v[:, :, hs],
                                                    caches[g][0], caches[g][1], slopes[hs], d, w // d, w)
            outs.append(o)
            lses.append(l)
            skv.append((nk, nv))
        attn_o = combine_groups(outs, lses, x_sample.dtype)
        u_ext = jnp.concatenate([state_pool, u.astype(state_pool.dtype)], axis=1)
        pool_o = pool_mix(u_ext, PAST_LEN - state_pool.shape[1], S, pool_lin, pool_scale).astype(x_sample.dtype)
        h = merge(x_sample, attn_o, pool_o, ga, gb, w_pa, w_pb, w_o)
        y_sample = hmoe(h, ln2, w_rg, b_rg, w_re, b_re, w_gate, w_up, w_down)
        s_pool = u_ext[:, u_ext.shape[1] - POOL_STATE:]

    (pk128, pv128), (pk512, pv512), (pk2048, pv2048) = pkv
    (sk128, sv128), (sk512, sv512), (sk2048, sv2048) = skv
    return (y_prompt, y_sample,
            pk128, pv128, pk512, pv512, pk2048, pv2048, p_pool,
            sk128, sv128, sk512, sv512, sk2048, sv2048, s_pool)
```

```python
import functools

import jax
import jax.numpy as jnp
from jax import lax
from jax.experimental import pallas as pl
from jax.experimental.pallas import tpu as pltpu

D_MODEL = 1024
HEAD_DIM = 64
HEADS_PER_GROUP = 4
WINDOWS = (128, 512, 2048)
DILATIONS = (1, 4, 16)
N_GROUPS = len(WINDOWS)
N_HEADS = HEADS_PER_GROUP * N_GROUPS
ATTN_W = N_HEADS * HEAD_DIM
GROUP_W = HEADS_PER_GROUP * HEAD_DIM
BAND = 128
POOL_WINDOWS = (2, 4, 8, 16)
POOL_GW = 128
POOL_W = len(POOL_WINDOWS) * POOL_GW
POOL_STATE = max(POOL_WINDOWS) - 1
N_EXPERT_GROUPS = 4
EXPERTS_PER_GROUP = 8
N_EXPERTS = N_EXPERT_GROUPS * EXPERTS_PER_GROUP
D_EXPERT = 512
QKVU_W = 3 * ATTN_W + POOL_W
NEG = -1e30
EPS = 1e-6
LANES = 128
ROUTER_W = LANES
EXPERT_COL0 = 8
VMEM_LIMIT = 56 * 1024 * 1024

assert all(w // d == BAND for w, d in zip(WINDOWS, DILATIONS))

f32 = jnp.float32
bf16 = jnp.bfloat16


def _cparams(sem):
    return pltpu.CompilerParams(dimension_semantics=sem, vmem_limit_bytes=VMEM_LIMIT)


def _rmsnorm_rows(x, g):
    ms = jnp.mean(x * x, axis=-1, keepdims=True)
    return x * lax.rsqrt(ms + EPS) * g


def _head_rmsnorm_chunk(ch, gain):
    lane = lax.broadcasted_iota(jnp.int32, ch.shape, 1)
    lo_mask = lane < HEAD_DIM
    sq = ch * ch
    lo = jnp.sum(jnp.where(lo_mask, sq, 0.0), axis=-1, keepdims=True)
    hi = jnp.sum(jnp.where(lo_mask, 0.0, sq), axis=-1, keepdims=True)
    r = jnp.where(lo_mask, lax.rsqrt(lo * (1.0 / HEAD_DIM) + EPS), lax.rsqrt(hi * (1.0 / HEAD_DIM) + EPS))
    return ch * r * gain


def _project(x, ln1, w, qg, kg):
    xn = _rmsnorm_rows(x, ln1).astype(bf16)
    z = jnp.dot(xn, w, preferred_element_type=f32)
    nch = ATTN_W // LANES
    q = [_head_rmsnorm_chunk(z[:, c * LANES:(c + 1) * LANES], qg[:, c * LANES:(c + 1) * LANES]) * (HEAD_DIM ** -0.5)
         for c in range(nch)]
    k = [_head_rmsnorm_chunk(z[:, ATTN_W + c * LANES:ATTN_W + (c + 1) * LANES], kg[:, c * LANES:(c + 1) * LANES])
         for c in range(nch)]
    v = [z[:, 2 * ATTN_W + c * LANES:2 * ATTN_W + (c + 1) * LANES] for c in range(nch)]
    u = z[:, 3 * ATTN_W:3 * ATTN_W + POOL_W]
    return q, k, v, u


def _proj_prompt_kernel(x_ref, ln1_ref, w_ref, qg_ref, kg_ref, plin_ref, pscale_ref, *rest,
                        tm, n_tiles, tail_tiles):
    qkv_refs = rest[:3 * N_GROUPS]
    po_ref, kt_ref, vt_ref, ut_ref, ubuf, sbuf = rest[3 * N_GROUPS:]
    i = pl.program_id(1)
    q, k, v, u = _project(x_ref[0], ln1_ref[...], w_ref[...], qg_ref[...], kg_ref[...])
    cpg = GROUP_W // LANES
    for g, dil in enumerate(DILATIONS):
        for t, chunks in enumerate((q, k, v)):
            out_ref = qkv_refs[3 * g + t]
            for c in range(cpg):
                val = chunks[g * cpg + c]
                cols = slice(c * LANES, (c + 1) * LANES)
                if dil == 1:
                    out_ref[0, 0, :, cols] = val.astype(bf16)
                else:
                    sbuf[...] = val
                    for r in range(dil):
                        out_ref[0, r, :, cols] = sbuf[pl.ds(r, tm // dil, stride=dil), :].astype(bf16)

    @pl.when(i >= n_tiles - tail_tiles)
    def _():
        kt_ref[0] = jnp.concatenate(k, axis=-1).T
        vt_ref[0] = jnp.concatenate(v, axis=-1).T

    hist = POOL_STATE + 1

    @pl.when(i == 0)
    def _():
        ubuf[0:hist, :] = jnp.zeros((hist, POOL_W), f32)

    @pl.when(i > 0)
    def _():
        ubuf[0:hist, :] = ubuf[tm:tm + hist, :]

    ubuf[hist:hist + tm, :] = u
    pos = i * tm + lax.broadcasted_iota(jnp.int32, (tm, POOL_GW), 0)
    zs = []
    for g, w in enumerate(POOL_WINDOWS):
        cols = slice(g * POOL_GW, (g + 1) * POOL_GW)
        ug = u[:, cols]
        acc = ug
        for j in range(1, w):
            acc = acc + ubuf[hist - j:hist - j + tm, cols]
        cnt = jnp.minimum(pos + 1, w).astype(f32)
        zs.append(acc / cnt - ug)
    z = jnp.concatenate(zs, axis=-1).astype(bf16)
    po = jnp.dot(z, plin_ref[...], preferred_element_type=f32) * pscale_ref[...]
    po_ref[0] = po.astype(bf16)

    @pl.when(i == n_tiles - 1)
    def _():
        ut_ref[0] = u[tm - hist:tm, :]


def _proj_prompt(x, ln1, w_qkvu, qg, kg, plin_bd, pscale, *, tm):
    B, T, D = x.shape
    n_tiles = T // tm
    tail = max(WINDOWS)
    assert T % tm == 0 and tail % tm == 0 and T >= tail
    tail_tiles = tail // tm
    hist = POOL_STATE + 1
    kern = functools.partial(_proj_prompt_kernel, tm=tm, n_tiles=n_tiles, tail_tiles=tail_tiles)
    const = lambda b, i: (0, 0)
    assert all(tm % (16 * d) == 0 for d in DILATIONS)
    qkv_specs = [pl.BlockSpec((1, d, tm // d, GROUP_W), lambda b, i: (b, 0, i, 0)) for d in DILATIONS for _ in range(3)]
    qkv_shapes = [jax.ShapeDtypeStruct((B, d, T // d, GROUP_W), bf16) for d in DILATIONS for _ in range(3)]
    tail_spec = pl.BlockSpec((1, ATTN_W, tm), lambda b, i: (b, 0, jnp.maximum(i - (n_tiles - tail_tiles), 0)))
    return pl.pallas_call(
        kern,
        grid=(B, n_tiles),
        in_specs=[
            pl.BlockSpec((1, tm, D), lambda b, i: (b, i, 0)),
            pl.BlockSpec((1, D), const),
            pl.BlockSpec((D, QKVU_W), const),
            pl.BlockSpec((1, ATTN_W), const),
            pl.BlockSpec((1, ATTN_W), const),
            pl.BlockSpec((POOL_W, POOL_W), const),
            pl.BlockSpec((1, POOL_W), const),
        ],
        out_specs=qkv_specs
        + [pl.BlockSpec((1, tm, POOL_W), lambda b, i: (b, i, 0)),
           tail_spec, tail_spec,
           pl.BlockSpec((1, hist, POOL_W), lambda b, i: (b, 0, 0))],
        out_shape=qkv_shapes
        + [jax.ShapeDtypeStruct((B, T, POOL_W), bf16),
           jax.ShapeDtypeStruct((B, ATTN_W, tail), f32),
           jax.ShapeDtypeStruct((B, ATTN_W, tail), f32),
           jax.ShapeDtypeStruct((B, hist, POOL_W), f32)],
        scratch_shapes=[pltpu.VMEM((hist + tm, POOL_W), f32), pltpu.VMEM((tm, LANES), f32)],
        compiler_params=_cparams(("arbitrary", "arbitrary")),
        name="proj_prompt",
    )(x, ln1, w_qkvu, qg, kg, plin_bd, pscale)


def _proj_sample_kernel(x_ref, ln1_ref, w_ref, qg_ref, kg_ref, plin_ref, pscale_ref, state_ref,
                        q_ref, k_ref, v_ref, po_ref, st_ref, *, n_seq, n_new, past_len):
    q, k, v, u = _project(x_ref[...], ln1_ref[...], w_ref[...], qg_ref[...], kg_ref[...])
    q_ref[...] = jnp.concatenate(q, axis=-1)
    k_ref[...] = jnp.concatenate(k, axis=-1)
    v_ref[...] = jnp.concatenate(v, axis=-1)
    ext = [state_ref[j] for j in range(POOL_STATE)] + [u[s * n_seq:(s + 1) * n_seq, :] for s in range(n_new)]
    for s in range(n_new):
        zs = []
        for g, w in enumerate(POOL_WINDOWS):
            cols = slice(g * POOL_GW, (g + 1) * POOL_GW)
            cur = ext[POOL_STATE + s][:, cols]
            acc = cur
            for j in range(1, w):
                acc = acc + ext[POOL_STATE + s - j][:, cols]
            cnt = float(min(past_len + s + 1, w))
            zs.append(acc / cnt - cur)
        z = jnp.concatenate(zs, axis=-1).astype(bf16)
        po = jnp.dot(z, plin_ref[...], preferred_element_type=f32) * pscale_ref[...]
        po_ref[s * n_seq:(s + 1) * n_seq, :] = po.astype(bf16)
    for j in range(POOL_STATE):
        st_ref[j] = ext[j + n_new]


def _proj_sample(x, ln1, w_qkvu, qg, kg, plin_bd, pscale, state, *, n_seq, n_new, past_len):
    n = n_seq * n_new
    kern = functools.partial(_proj_sample_kernel, n_seq=n_seq, n_new=n_new, past_len=past_len)
    return pl.pallas_call(
        kern,
        out_shape=[jax.ShapeDtypeStruct((n, ATTN_W), f32)] * 3
        + [jax.ShapeDtypeStruct((n, POOL_W), bf16),
           jax.ShapeDtypeStruct((POOL_STATE, n_seq, POOL_W), f32)],
        compiler_params=pltpu.CompilerParams(vmem_limit_bytes=VMEM_LIMIT),
        name="proj_sample",
    )(x, ln1, w_qkvu, qg, kg, plin_bd, pscale, state)


def _head_masks(shape):
    lane = lax.broadcasted_iota(jnp.int32, shape, len(shape) - 1)
    return [(lane >= h * HEAD_DIM) & (lane < (h + 1) * HEAD_DIM) for h in range(HEADS_PER_GROUP)]


def _band_attn_kernel(q_ref, kc_ref, kp_ref, vc_ref, vp_ref, bias_ref, o_ref, l_ref, kbuf, vbuf, *, tl):
    i = pl.program_id(2)
    kbuf[0:BAND, :] = kp_ref[0, 0]
    kbuf[BAND:BAND + tl, :] = kc_ref[0, 0]
    vbuf[0:BAND, :] = vp_ref[0, 0]
    vbuf[BAND:BAND + tl, :] = vc_ref[0, 0]
    masks = _head_masks((BAND, GROUP_W))

    def body(j, carry):
        r0 = pl.multiple_of(j * BAND, BAND)
        q = q_ref[0, 0, pl.ds(r0, BAND), :]
        qm = jnp.concatenate([jnp.where(m, q, jnp.zeros_like(q)) for m in masks], axis=0)
        kk = kbuf[pl.ds(r0, 2 * BAND), :]
        vv = vbuf[pl.ds(r0, 2 * BAND), :]
        s = lax.dot_general(qm, kk, (((1,), (1,)), ((), ())), preferred_element_type=f32)
        var = jnp.where((i == 0) & (j == 0), 0, 1)
        s = s + bias_ref[var]
        m = jnp.max(s, axis=-1, keepdims=True)
        p = jnp.exp(s - m)
        den = jnp.sum(p, axis=-1, keepdims=True)
        pv = jnp.dot(p.astype(bf16), vv, preferred_element_type=f32) / den
        lse = m + jnp.log(den)
        o = jnp.zeros((BAND, GROUP_W), f32)
        l = jnp.zeros((BAND, GROUP_W), f32)
        for h, msk in enumerate(masks):
            rows = slice(h * BAND, (h + 1) * BAND)
            o = jnp.where(msk, pv[rows], o)
            l = jnp.where(msk, lse[rows], l)
        o_ref[0, 0, pl.ds(r0, BAND), :] = o.astype(bf16)
        l_ref[0, 0, pl.ds(r0, BAND), :] = l
        return carry

    lax.fori_loop(0, tl // BAND, body, 0)


def _band_bias(slopes_g, dil):
    qi = jnp.arange(BAND)[:, None]
    kb = jnp.arange(2 * BAND)[None, :]
    rel = qi + BAND - kb
    valid = (rel >= 0) & (rel <= BAND)
    alibi = -slopes_g[:, None, None] * (dil * rel)[None].astype(f32)
    variants = []
    for first in (True, False):
        ok = valid & (kb >= BAND) if first else valid
        variants.append(jnp.where(ok[None], alibi, NEG).reshape(HEADS_PER_GROUP * BAND, 2 * BAND))
    return jnp.stack(variants, axis=0)


def _band_attention(q, k, v, g, slopes_g, *, tl_max=1024):
    B, dil, L, _ = q.shape
    tl = min(tl_max, L)
    assert dil == DILATIONS[g] and L % tl == 0 and tl % BAND == 0
    nsub = tl // BAND
    bias = _band_bias(slopes_g, dil)
    cur = pl.BlockSpec((1, 1, tl, GROUP_W), lambda b, r, i: (b, r, i, 0))
    prev = pl.BlockSpec((1, 1, BAND, GROUP_W), lambda b, r, i: (b, r, jnp.maximum(i * nsub - 1, 0), 0))
    return pl.pallas_call(
        functools.partial(_band_attn_kernel, tl=tl),
        grid=(B, dil, L // tl),
        in_specs=[cur, cur, prev, cur, prev,
                  pl.BlockSpec((2, HEADS_PER_GROUP * BAND, 2 * BAND), lambda b, r, i: (0, 0, 0))],
        out_specs=[cur, cur],
        out_shape=[jax.ShapeDtypeStruct((B, dil, L, GROUP_W), bf16),
                   jax.ShapeDtypeStruct((B, dil, L, GROUP_W), f32)],
        scratch_shapes=[pltpu.VMEM((BAND + tl, GROUP_W), bf16), pltpu.VMEM((BAND + tl, GROUP_W), bf16)],
        compiler_params=_cparams(("arbitrary", "arbitrary", "arbitrary")),
        name="band_attn_g%d" % g,
    )(q, k, k, v, v, bias)


def _cached_attn_kernel(q_ref, kn_ref, vn_ref, kc_ref, vc_ref, bc_ref, bn_ref,
                        o_ref, l_ref, ko_ref, vo_ref, *, nb, n_new, win):
    masks8 = _head_masks((8, GROUP_W))
    lane_t = lax.broadcasted_iota(jnp.int32, (GROUP_W, LANES), 1)
    for bb in range(nb):
        q8 = q_ref[bb]
        kn8 = kn_ref[bb]
        vn8 = vn_ref[bb]
        kc = kc_ref[bb]
        vc = vc_ref[bb]
        qm = jnp.concatenate([jnp.where(m, q8, 0.0) for m in masks8], axis=0)
        sc = jnp.dot(qm.astype(bf16), kc.astype(bf16), preferred_element_type=f32) + bc_ref[...]
        m = jnp.max(sc, axis=-1, keepdims=True)
        sn = []
        for t in range(n_new):
            col = jnp.sum(qm * kn8[t:t + 1, :], axis=-1, keepdims=True) + bn_ref[:, t:t + 1]
            sn.append(col)
            m = jnp.maximum(m, col)
        pc = jnp.exp(sc - m)
        den = jnp.sum(pc, axis=-1, keepdims=True)
        acc = lax.dot_general(pc.astype(bf16), vc.astype(bf16), (((1,), (1,)), ((), ())),
                              preferred_element_type=f32)
        for t in range(n_new):
            pn = jnp.exp(sn[t] - m)
            den = den + pn
            acc = acc + pn * vn8[t:t + 1, :]
        acc = acc / den
        lse = m + jnp.log(den)
        o = jnp.zeros((8, GROUP_W), f32)
        l = jnp.zeros((8, GROUP_W), f32)
        for h, msk in enumerate(masks8):
            o = jnp.where(msk, acc[h * 8:(h + 1) * 8], o)
            l = jnp.where(msk, lse[h * 8:(h + 1) * 8], l)
        o_ref[bb] = o
        l_ref[bb] = l
        for c_ref, n8, out_ref in ((kc, kn8, ko_ref), (vc, vn8, vo_ref)):
            rolled = pltpu.roll(c_ref, win - n_new, axis=1)
            out_ref[bb] = rolled
            new_t = jnp.concatenate([n8, jnp.zeros((LANES - 8, GROUP_W), f32)], axis=0).T
            new_t = pltpu.roll(new_t, LANES - n_new, axis=1)
            out_ref[bb, :, win - LANES:win] = jnp.where(lane_t >= LANES - n_new, new_t, rolled[:, win - LANES:win])


def _cached_bias(slopes_g, dil, win, n_new):
    s = jnp.arange(8)[:, None]
    i = jnp.arange(win)[None, :]
    dist = win + s - i
    ok = (dist % dil == 0) & (dist // dil <= BAND) & (s < n_new)
    bc = jnp.where(ok[None], -slopes_g[:, None, None] * dist[None].astype(f32), NEG)
    t = jnp.arange(8)[None, :]
    dn = s - t
    okn = (dn >= 0) & (dn % dil == 0) & (dn // dil <= BAND) & (s < n_new) & (t < n_new)
    bn = jnp.where(okn[None], -slopes_g[:, None, None] * dn[None].astype(f32), NEG)
    pad = (s >= n_new)
    bc = jnp.where(pad[None], 0.0, bc)
    bn = jnp.where(pad[None], 0.0, bn)
    return bc.reshape(HEADS_PER_GROUP * 8, win), bn.reshape(HEADS_PER_GROUP * 8, 8)


def _cached_attention(q8, kn8, vn8, kc_t, vc_t, g, slopes_g, *, n_new):
    Bd, _, win = kc_t.shape
    dil = DILATIONS[g]
    assert win == WINDOWS[g] and win % LANES == 0
    nb = max(1, min(Bd, 2048 // win))
    assert Bd % nb == 0
    bc, bn = _cached_bias(slopes_g, dil, win, n_new)
    small = pl.BlockSpec((nb, 8, GROUP_W), lambda b: (b, 0, 0))
    cache = pl.BlockSpec((nb, GROUP_W, win), lambda b: (b, 0, 0))
    return pl.pallas_call(
        functools.partial(_cached_attn_kernel, nb=nb, n_new=n_new, win=win),
        grid=(Bd // nb,),
        in_specs=[small, small, small, cache, cache,
                  pl.BlockSpec(bc.shape, lambda b: (0, 0)), pl.BlockSpec(bn.shape, lambda b: (0, 0))],
        out_specs=[small, small, cache, cache],
        out_shape=[jax.ShapeDtypeStruct((Bd, 8, GROUP_W), f32)] * 2
        + [jax.ShapeDtypeStruct((Bd, GROUP_W, win), f32)] * 2,
        compiler_params=_cparams(("arbitrary",)),
        name="cached_attn_g%d" % g,
    )(q8, kn8, vn8, kc_t, vc_t, bc, bn)


def _merge_kernel(x_ref, o0_ref, o1_ref, o2_ref, l0_ref, l1_ref, l2_ref, po_ref,
                  ln1_ref, wg_ref, wpa_ref, wpb_ref, wo_ref, ln2_ref, wr_ref, br_ref, tri_ref,
                  h_ref, xn_ref, ri_ref, wc_ref, cnt_ref, carry, ibuf, *, tm, dils):
    step = pl.program_id(0) * pl.num_programs(1) + pl.program_id(1)

    @pl.when(step == 0)
    def _():
        carry[...] = jnp.zeros_like(carry)

    def token_order(ref, dil):
        if dil == 1:
            return ref[0, 0].astype(f32)
        chunks = []
        for c in range(GROUP_W // LANES):
            for r in range(dil):
                ibuf[pl.ds(r, tm // dil, stride=dil), :] = ref[0, r, :, c * LANES:(c + 1) * LANES].astype(f32)
            chunks.append(ibuf[...])
        return jnp.concatenate(chunks, axis=-1)

    x = x_ref[0]
    xn = _rmsnorm_rows(x, ln1_ref[...]).astype(bf16)
    gates = jnp.dot(xn, wg_ref[...], preferred_element_type=f32)
    l0, l1, l2 = (token_order(r, d) for r, d in zip((l0_ref, l1_ref, l2_ref), dils))
    lm = jnp.maximum(jnp.maximum(l0, l1), l2)
    e0, e1, e2 = jnp.exp(l0 - lm), jnp.exp(l1 - lm), jnp.exp(l2 - lm)
    o0, o1, o2 = (token_order(r, d) for r, d in zip((o0_ref, o1_ref, o2_ref), dils))
    attn = (e0 * o0 + e1 * o1 + e2 * o2) / (e0 + e1 + e2)
    ma = jnp.dot(attn.astype(bf16), wpa_ref[...], preferred_element_type=f32)
    mb = jnp.dot(po_ref[0], wpb_ref[...], preferred_element_type=f32)
    mix = jax.nn.sigmoid(gates[:, :D_MODEL]) * ma + jax.nn.sigmoid(gates[:, D_MODEL:]) * mb
    h = x + jnp.dot(mix.astype(bf16), wo_ref[...], preferred_element_type=f32)
    h_ref[...] = h
    xn2 = _rmsnorm_rows(h, ln2_ref[...])
    xn_ref[...] = xn2

    lt = (jnp.dot(xn2.astype(bf16), wr_ref[...], preferred_element_type=f32) + br_ref[...]).T
    row8 = lax.broadcasted_iota(jnp.int32, (8, tm), 0)
    gl = jnp.where(row8 < N_EXPERT_GROUPS, lt[0:8], -jnp.inf)
    gmax = jnp.max(gl, axis=0, keepdims=True)
    gidx = jnp.min(jnp.where(gl == gmax, row8, 8), axis=0, keepdims=True)
    pg = 1.0 / jnp.sum(jnp.exp(gl - gmax), axis=0, keepdims=True)
    sel = jnp.zeros((8, tm), f32)
    for g in range(N_EXPERT_GROUPS):
        lo = EXPERT_COL0 + g * EXPERTS_PER_GROUP
        sel = jnp.where(gidx == g, lt[lo:lo + EXPERTS_PER_GROUP], sel)
    v0 = jnp.max(sel, axis=0, keepdims=True)
    i0 = jnp.min(jnp.where(sel == v0, row8, 8), axis=0, keepdims=True)
    sel2 = jnp.where(row8 == i0, -jnp.inf, sel)
    v1 = jnp.max(sel2, axis=0, keepdims=True)
    i1 = jnp.min(jnp.where(sel2 == v1, row8, 8), axis=0, keepdims=True)
    t = jnp.exp(v1 - v0)
    w0 = pg / (1.0 + t)
    w1 = pg * t / (1.0 + t)
    eid0 = gidx * EXPERTS_PER_GROUP + i0
    eid1 = gidx * EXPERTS_PER_GROUP + i1
    erow = lax.broadcasted_iota(jnp.int32, (N_EXPERTS, tm), 0)
    oh0 = erow == eid0
    oh1 = erow == eid1
    cnt = jnp.where(oh0 | oh1, 1.0, 0.0)
    before = jnp.dot(cnt.astype(bf16), tri_ref[...], preferred_element_type=f32) + carry[:, 0:1]
    rank0 = jnp.sum(jnp.where(oh0, before, 0.0), axis=0, keepdims=True)
    rank1 = jnp.sum(jnp.where(oh1, before, 0.0), axis=0, keepdims=True)
    carry[...] = carry[...] + jnp.sum(cnt, axis=1, keepdims=True)
    cnt_ref[...] = carry[...]
    ri = jnp.zeros((8, tm), jnp.int32)
    for r, val in enumerate((eid0, eid1, rank0.astype(jnp.int32), rank1.astype(jnp.int32))):
        ri = jnp.where(row8 == r, val, ri)
    ri_ref[...] = ri
    rowl = lax.broadcasted_iota(jnp.int32, (LANES, tm), 0)
    wslab = jnp.where(rowl == 0, w0, jnp.where(rowl == 1, w1, 0.0))
    wc_ref[...] = wslab.T


def _merge(x, o, l, po, ln1, w_gates, w_pa, w_pb, w_o, ln2, w_router, b_router, *, tm):
    B, T, _ = x.shape
    assert T % tm == 0
    nt = T // tm
    n = B * T
    dils = tuple(a.shape[1] for a in o)
    assert all(tm % (8 * d) == 0 for d in dils)
    tri = (jnp.arange(tm)[:, None] < jnp.arange(tm)[None, :]).astype(bf16)
    rows3 = lambda w: pl.BlockSpec((1, tm, w), lambda b, i: (b, i, 0))
    flat = lambda w: pl.BlockSpec((tm, w), lambda b, i: (b * nt + i, 0))
    grp = [pl.BlockSpec((1, d, tm // d, GROUP_W), lambda b, i: (b, 0, i, 0)) for d in dils]
    full = lambda a: pl.BlockSpec(a.shape, lambda b, i: (0,) * a.ndim)
    weights = (ln1, w_gates, w_pa, w_pb, w_o, ln2, w_router, b_router, tri)
    return pl.pallas_call(
        functools.partial(_merge_kernel, tm=tm, dils=dils),
        grid=(B, nt),
        in_specs=[rows3(D_MODEL)] + grp + grp + [rows3(POOL_W)] + [full(a) for a in weights],
        out_specs=[flat(D_MODEL), flat(D_MODEL),
                   pl.BlockSpec((8, tm), lambda b, i: (0, b * nt + i)),
                   flat(LANES),
                   pl.BlockSpec((N_EXPERTS, LANES), lambda b, i: (0, 0))],
        out_shape=[jax.ShapeDtypeStruct((n, D_MODEL), f32), jax.ShapeDtypeStruct((n, D_MODEL), f32),
                   jax.ShapeDtypeStruct((8, n), jnp.int32),
                   jax.ShapeDtypeStruct((n, LANES), f32),
                   jax.ShapeDtypeStruct((N_EXPERTS, LANES), f32)],
        scratch_shapes=[pltpu.VMEM((N_EXPERTS, LANES), f32), pltpu.VMEM((tm, LANES), f32)],
        compiler_params=_cparams(("arbitrary", "arbitrary")),
        name="merge_router",
    )(x, *o, *l, po, *weights)


def _moe_kernel(blk_e_ref, nvalid_ref, row_a_ref, xn_hbm, wg_ref, wu_ref, wd_ref, ys_hbm,
                xbuf, ybuf, sem_in, sem_out, *, blk, n_tok):
    b = pl.program_id(0)
    nv = nvalid_ref[b]

    @pl.when(b == 0)
    def _():
        xbuf[...] = jnp.zeros_like(xbuf)

    def row_in(r):
        a = row_a_ref[b * blk + r]
        tok = jnp.where(a >= n_tok, a - n_tok, a)
        return pltpu.make_async_copy(xn_hbm.at[pl.ds(tok, 1)], xbuf.at[pl.ds(r, 1)], sem_in)

    def row_out(r):
        a = row_a_ref[b * blk + r]
        return pltpu.make_async_copy(ybuf.at[pl.ds(r, 1)], ys_hbm.at[pl.ds(a, 1)], sem_out)

    @pl.when(nv > 0)
    def _():
        def start_in(r, c):
            row_in(r).start()
            return c

        def wait_in(r, c):
            row_in(r).wait()
            return c

        lax.fori_loop(0, nv, start_in, 0)
        lax.fori_loop(0, nv, wait_in, 0)
        x = xbuf[...].astype(bf16)
        hid = jax.nn.silu(jnp.dot(x, wg_ref[0], preferred_element_type=f32)) * jnp.dot(x, wu_ref[0], preferred_element_type=f32)
        ybuf[...] = jnp.dot(hid.astype(bf16), wd_ref[0], preferred_element_type=f32)

        def start_out(r, c):
            row_out(r).start()
            return c

        def wait_out(r, c):
            row_out(r).wait()
            return c

        lax.fori_loop(0, nv, start_out, 0)
        lax.fori_loop(0, nv, wait_out, 0)


def _moe(xn2, blk_e, nvalid, row_a, w_gate, w_up, w_down, *, blk):
    n = xn2.shape[0]
    n_blocks = blk_e.shape[0]
    wspec = lambda shape: pl.BlockSpec((1,) + shape, lambda b, be, nv, ra: (be[b], 0, 0))
    return pl.pallas_call(
        functools.partial(_moe_kernel, blk=blk, n_tok=n),
        grid_spec=pltpu.PrefetchScalarGridSpec(
            num_scalar_prefetch=3,
            grid=(n_blocks,),
            in_specs=[pl.BlockSpec(memory_space=pl.ANY),
                      wspec((D_MODEL, D_EXPERT)), wspec((D_MODEL, D_EXPERT)), wspec((D_EXPERT, D_MODEL))],
            out_specs=pl.BlockSpec(memory_space=pl.ANY),
            scratch_shapes=[pltpu.VMEM((blk, D_MODEL), f32), pltpu.VMEM((blk, D_MODEL), f32),
                            pltpu.SemaphoreType.DMA, pltpu.SemaphoreType.DMA],
        ),
        out_shape=jax.ShapeDtypeStruct((2 * n, D_MODEL), f32),
        compiler_params=_cparams(("arbitrary",)),
        name="moe_experts",
    )(blk_e, nvalid, row_a, xn2, w_gate, w_up, w_down)


def _dispatch_tables(ri, counts, *, blk, n_blocks):
    n = ri.shape[1]
    counts = counts.astype(jnp.int32)
    padded = (counts + blk - 1) // blk * blk
    pad_ends = jnp.cumsum(padded)
    pad_starts = pad_ends - padded
    dest0 = pad_starts[ri[0]] + ri[2]
    dest1 = pad_starts[ri[1]] + ri[3]
    tok = jnp.arange(n, dtype=jnp.int32)
    row_a = jnp.zeros((n_blocks * blk,), jnp.int32).at[dest0].set(tok).at[dest1].set(tok + n)
    blk_start = jnp.arange(n_blocks, dtype=jnp.int32) * blk
    blk_e = jnp.minimum(jnp.searchsorted(pad_ends, blk_start, side='right'), N_EXPERTS - 1).astype(jnp.int32)
    nvalid = jnp.clip(pad_starts[blk_e] + counts[blk_e] - blk_start, 0, blk).astype(jnp.int32)
    return blk_e, nvalid, row_a


def _final_kernel(h_ref, ys_ref, wc_ref, y_ref):
    w = wc_ref[...]
    y_ref[...] = h_ref[...] + w[:, 0:1] * ys_ref[0] + w[:, 1:2] * ys_ref[1]


def _final(h, ys, wc, *, tm):
    n = h.shape[0]
    rows = lambda w: pl.BlockSpec((tm, w), lambda i: (i, 0))
    return pl.pallas_call(
        _final_kernel,
        grid=(n // tm,),
        in_specs=[rows(D_MODEL), pl.BlockSpec((2, tm, D_MODEL), lambda i: (0, i, 0)), rows(LANES)],
        out_specs=rows(D_MODEL),
        out_shape=jax.ShapeDtypeStruct((n, D_MODEL), f32),
        compiler_params=_cparams(("arbitrary",)),
        name="moe_combine",
    )(h, ys.reshape(2, n, D_MODEL), wc)


def _mix_and_moe(x, o, l, po, wts, *, tm, blk):
    n = x.shape[0] * x.shape[1]
    h, xn2, ri, wc, cnt = _merge(x, o, l, po, wts['ln1'], wts['w_gates'], wts['w_pa'], wts['w_pb'], wts['w_o'],
                                 wts['ln2'], wts['w_router'], wts['b_router'], tm=tm)
    n_blocks = (2 * n + N_EXPERTS * (blk - 1) + blk - 1) // blk
    blk_e, nvalid, row_a = _dispatch_tables(ri, cnt[:, 0], blk=blk, n_blocks=n_blocks)
    ys = _moe(xn2, blk_e, nvalid, row_a, wts['w_gate'], wts['w_up'], wts['w_down'], blk=blk)
    return _final(h, ys, wc, tm=tm)


def kernel(x_prompt, x_sample, cache_k_w128, cache_v_w128, cache_k_w512, cache_v_w512, cache_k_w2048, cache_v_w2048, state_pool, ln1, w_in, q_gain, k_gain, pool_lin, pool_scale, w_pa, w_pb, w_o, ln2, w_rg, b_rg, w_re, b_re, w_gate, w_up, w_down):
    B, T, D = x_prompt.shape
    Bd, S, _ = x_sample.shape
    past_len = 8192
    caches = ((cache_k_w128, cache_v_w128), (cache_k_w512, cache_v_w512), (cache_k_w2048, cache_v_w2048))
    slopes = jnp.exp2(-8.0 * jnp.arange(1, N_HEADS + 1, dtype=f32) / N_HEADS).reshape(N_GROUPS, HEADS_PER_GROUP)

    w_qkvu = w_in[:, :QKVU_W].astype(bf16)
    plin_bd = jnp.zeros((POOL_W, POOL_W), f32)
    for g in range(len(POOL_WINDOWS)):
        plin_bd = plin_bd.at[g * POOL_GW:(g + 1) * POOL_GW, g * POOL_GW:(g + 1) * POOL_GW].set(pool_lin[g])
    w_router = jnp.zeros((D, ROUTER_W), f32).at[:, :N_EXPERT_GROUPS].set(w_rg)
    w_router = w_router.at[:, EXPERT_COL0:EXPERT_COL0 + N_EXPERTS].set(w_re)
    b_router = jnp.zeros((1, ROUTER_W), f32).at[0, :N_EXPERT_GROUPS].set(b_rg)
    b_router = b_router.at[0, EXPERT_COL0:EXPERT_COL0 + N_EXPERTS].set(b_re)
    wts = dict(ln1=ln1.reshape(1, D), w_gates=w_in[:, QKVU_W:].astype(bf16), w_pa=w_pa.astype(bf16),
               w_pb=w_pb.astype(bf16), w_o=w_o.astype(bf16), ln2=ln2.reshape(1, D),
               w_router=w_router.astype(bf16), b_router=b_router,
               w_gate=w_gate.astype(bf16), w_up=w_up.astype(bf16), w_down=w_down.astype(bf16))
    proj_w = (wts['ln1'], w_qkvu, q_gain.reshape(1, ATTN_W), k_gain.reshape(1, ATTN_W),
              plin_bd.astype(bf16), pool_scale.reshape(1, POOL_W))

    *qkv, po, kt, vt, ut = _proj_prompt(x_prompt, *proj_w, tm=512)
    o, l = zip(*[_band_attention(*qkv[3 * g:3 * g + 3], g, slopes[g]) for g in range(N_GROUPS)])
    y_prompt = _mix_and_moe(x_prompt, o, l, po, wts, tm=512, blk=256).reshape(B, T, D)
    tail = kt.shape[2]
    pkv = []
    for g, w in enumerate(WINDOWS):
        for a in (kt, vt):
            a = a.reshape(B, N_HEADS, HEAD_DIM, tail)[:, g * HEADS_PER_GROUP:(g + 1) * HEADS_PER_GROUP, :, tail - w:]
            pkv.append(a.transpose(0, 3, 1, 2))
    p_pool = ut[:, 1:]

    n_s = Bd * S
    xs = x_sample.transpose(1, 0, 2).reshape(n_s, D)
    qs, ks, vs, pos, st = _proj_sample(xs, *proj_w, state_pool.transpose(1, 0, 2),
                                       n_seq=Bd, n_new=S, past_len=past_len)
    so, sl, skv = [], [], []
    pad8 = lambda a: jnp.pad(a.reshape(S, Bd, GROUP_W).transpose(1, 0, 2), ((0, 0), (0, 8 - S), (0, 0)))
    for g, w in enumerate(WINDOWS):
        cols = slice(g * GROUP_W, (g + 1) * GROUP_W)
        kc, vc = caches[g]
        og, lg, ko, vo = _cached_attention(
            pad8(qs[:, cols]), pad8(ks[:, cols]), pad8(vs[:, cols]),
            kc.transpose(0, 2, 3, 1).reshape(Bd, GROUP_W, w), vc.transpose(0, 2, 3, 1).reshape(Bd, GROUP_W, w),
            g, slopes[g], n_new=S)
        so.append(og[:, :S].transpose(1, 0, 2).reshape(1, 1, n_s, GROUP_W).astype(bf16))
        sl.append(lg[:, :S].transpose(1, 0, 2).reshape(1, 1, n_s, GROUP_W))
        for a in (ko, vo):
            skv.append(a.reshape(Bd, HEADS_PER_GROUP, HEAD_DIM, w).transpose(0, 3, 1, 2))
    y_sample = _mix_and_moe(xs[None], so, sl, pos[None], wts, tm=n_s, blk=128).reshape(S, Bd, D).transpose(1, 0, 2)
    s_pool = st.transpose(1, 0, 2)

    return (y_prompt, y_sample, *pkv, p_pool, *skv, s_pool)
```

```python
import functools

import jax
import jax.numpy as jnp
from jax import lax
from jax.experimental import pallas as pl
from jax.experimental.pallas import tpu as pltpu

D_MODEL = 1024
HEAD_DIM = 64
HEADS_PER_GROUP = 4
WINDOWS = (128, 512, 2048)
DILATIONS = (1, 4, 16)
N_GROUPS = len(WINDOWS)
N_HEADS = HEADS_PER_GROUP * N_GROUPS
ATTN_W = N_HEADS * HEAD_DIM
GROUP_W = HEADS_PER_GROUP * HEAD_DIM
BAND = 128
POOL_WINDOWS = (2, 4, 8, 16)
POOL_GW = 128
POOL_W = len(POOL_WINDOWS) * POOL_GW
POOL_STATE = max(POOL_WINDOWS) - 1
N_EXPERT_GROUPS = 4
EXPERTS_PER_GROUP = 8
N_EXPERTS = N_EXPERT_GROUPS * EXPERTS_PER_GROUP
D_EXPERT = 512
QKVU_W = 3 * ATTN_W + POOL_W
NEG = -1e30
EPS = 1e-6
LANES = 128
ROUTER_W = LANES
EXPERT_COL0 = 8
VMEM_LIMIT = 56 * 1024 * 1024

assert all(w // d == BAND for w, d in zip(WINDOWS, DILATIONS))

f32 = jnp.float32
bf16 = jnp.bfloat16


def _cparams(sem):
    return pltpu.CompilerParams(dimension_semantics=sem, vmem_limit_bytes=VMEM_LIMIT)


def _rmsnorm_rows(x, g):
    ms = jnp.mean(x * x, axis=-1, keepdims=True)
    return x * lax.rsqrt(ms + EPS) * g


def _head_rmsnorm_chunk(ch, gain):
    lane = lax.broadcasted_iota(jnp.int32, ch.shape, 1)
    lo_mask = lane < HEAD_DIM
    sq = ch * ch
    lo = jnp.sum(jnp.where(lo_mask, sq, 0.0), axis=-1, keepdims=True)
    hi = jnp.sum(jnp.where(lo_mask, 0.0, sq), axis=-1, keepdims=True)
    r = jnp.where(lo_mask, lax.rsqrt(lo * (1.0 / HEAD_DIM) + EPS), lax.rsqrt(hi * (1.0 / HEAD_DIM) + EPS))
    return ch * r * gain


def _project(x, ln1, w, qg, kg):
    xn = _rmsnorm_rows(x, ln1).astype(bf16)
    z = jnp.dot(xn, w, preferred_element_type=f32)
    nch = ATTN_W // LANES
    q = [_head_rmsnorm_chunk(z[:, c * LANES:(c + 1) * LANES], qg[:, c * LANES:(c + 1) * LANES]) * (HEAD_DIM ** -0.5)
         for c in range(nch)]
    k = [_head_rmsnorm_chunk(z[:, ATTN_W + c * LANES:ATTN_W + (c + 1) * LANES], kg[:, c * LANES:(c + 1) * LANES])
         for c in range(nch)]
    v = [z[:, 2 * ATTN_W + c * LANES:2 * ATTN_W + (c + 1) * LANES] for c in range(nch)]
    u = z[:, 3 * ATTN_W:3 * ATTN_W + POOL_W]
    return q, k, v, u


def _proj_prompt_kernel(x_ref, ln1_ref, w_ref, qg_ref, kg_ref, plin_ref, pscale_ref, *rest,
                        tm, n_tiles, tail_tiles):
    qkv_refs = rest[:3 * N_GROUPS]
    po_ref, kt_ref, vt_ref, ut_ref, ubuf, sbuf = rest[3 * N_GROUPS:]
    i = pl.program_id(1)
    q, k, v, u = _project(x_ref[0], ln1_ref[...], w_ref[...], qg_ref[...], kg_ref[...])
    cpg = GROUP_W // LANES
    for g, dil in enumerate(DILATIONS):
        for t, chunks in enumerate((q, k, v)):
            out_ref = qkv_refs[3 * g + t]
            for c in range(cpg):
                val = chunks[g * cpg + c]
                cols = slice(c * LANES, (c + 1) * LANES)
                if dil == 1:
                    out_ref[0, 0, :, cols] = val.astype(bf16)
                else:
                    sbuf[...] = val
                    for r in range(dil):
                        out_ref[0, r, :, cols] = sbuf[pl.ds(r, tm // dil, stride=dil), :].astype(bf16)

    @pl.when(i >= n_tiles - tail_tiles)
    def _():
        kt_ref[0] = jnp.concatenate(k, axis=-1).T
        vt_ref[0] = jnp.concatenate(v, axis=-1).T

    hist = POOL_STATE + 1

    @pl.when(i == 0)
    def _():
        ubuf[0:hist, :] = jnp.zeros((hist, POOL_W), f32)

    @pl.when(i > 0)
    def _():
        ubuf[0:hist, :] = ubuf[tm:tm + hist, :]

    ubuf[hist:hist + tm, :] = u
    pos = i * tm + lax.broadcasted_iota(jnp.int32, (tm, POOL_GW), 0)
    zs = []
    for g, w in enumerate(POOL_WINDOWS):
        cols = slice(g * POOL_GW, (g + 1) * POOL_GW)
        ug = u[:, cols]
        acc = ug
        for j in range(1, w):
            acc = acc + ubuf[hist - j:hist - j + tm, cols]
        cnt = jnp.minimum(pos + 1, w).astype(f32)
        zs.append(acc / cnt - ug)
    z = jnp.concatenate(zs, axis=-1).astype(bf16)
    po = jnp.dot(z, plin_ref[...], preferred_element_type=f32) * pscale_ref[...]
    po_ref[0] = po.astype(bf16)

    @pl.when(i == n_tiles - 1)
    def _():
        ut_ref[0] = u[tm - hist:tm, :]


def _proj_prompt(x, ln1, w_qkvu, qg, kg, plin_bd, pscale, *, tm):
    B, T, D = x.shape
    n_tiles = T // tm
    tail = max(WINDOWS)
    assert T % tm == 0 and tail % tm == 0 and T >= tail
    tail_tiles = tail // tm
    hist = POOL_STATE + 1
    kern = functools.partial(_proj_prompt_kernel, tm=tm, n_tiles=n_tiles, tail_tiles=tail_tiles)
    const = lambda b, i: (0, 0)
    assert all(tm % (16 * d) == 0 for d in DILATIONS)
    qkv_specs = [pl.BlockSpec((1, d, tm // d, GROUP_W), lambda b, i: (b, 0, i, 0)) for d in DILATIONS for _ in range(3)]
    qkv_shapes = [jax.ShapeDtypeStruct((B, d, T // d, GROUP_W), bf16) for d in DILATIONS for _ in range(3)]
    tail_spec = pl.BlockSpec((1, ATTN_W, tm), lambda b, i: (b, 0, jnp.maximum(i - (n_tiles - tail_tiles), 0)))
    return pl.pallas_call(
        kern,
        grid=(B, n_tiles),
        in_specs=[
            pl.BlockSpec((1, tm, D), lambda b, i: (b, i, 0)),
            pl.BlockSpec((1, D), const),
            pl.BlockSpec((D, QKVU_W), const),
            pl.BlockSpec((1, ATTN_W), const),
            pl.BlockSpec((1, ATTN_W), const),
            pl.BlockSpec((POOL_W, POOL_W), const),
            pl.BlockSpec((1, POOL_W), const),
        ],
        out_specs=qkv_specs
        + [pl.BlockSpec((1, tm, POOL_W), lambda b, i: (b, i, 0)),
           tail_spec, tail_spec,
           pl.BlockSpec((1, hist, POOL_W), lambda b, i: (b, 0, 0))],
        out_shape=qkv_shapes
        + [jax.ShapeDtypeStruct((B, T, POOL_W), bf16),
           jax.ShapeDtypeStruct((B, ATTN_W, tail), f32),
           jax.ShapeDtypeStruct((B, ATTN_W, tail), f32),
           jax.ShapeDtypeStruct((B, hist, POOL_W), f32)],
        scratch_shapes=[pltpu.VMEM((hist + tm, POOL_W), f32), pltpu.VMEM((tm, LANES), f32)],
        compiler_params=_cparams(("arbitrary", "arbitrary")),
        name="proj_prompt",
    )(x, ln1, w_qkvu, qg, kg, plin_bd, pscale)


def _proj_sample_kernel(x_ref, ln1_ref, w_ref, qg_ref, kg_ref, plin_ref, pscale_ref, state_ref,
                        q_ref, k_ref, v_ref, po_ref, st_ref, *, n_seq, n_new, past_len):
    q, k, v, u = _project(x_ref[...], ln1_ref[...], w_ref[...], qg_ref[...], kg_ref[...])
    q_ref[...] = jnp.concatenate(q, axis=-1)
    k_ref[...] = jnp.concatenate(k, axis=-1)
    v_ref[...] = jnp.concatenate(v, axis=-1)
    ext = [state_ref[j] for j in range(POOL_STATE)] + [u[s * n_seq:(s + 1) * n_seq, :] for s in range(n_new)]
    for s in range(n_new):
        zs = []
        for g, w in enumerate(POOL_WINDOWS):
            cols = slice(g * POOL_GW, (g + 1) * POOL_GW)
            cur = ext[POOL_STATE + s][:, cols]
            acc = cur
            for j in range(1, w):
                acc = acc + ext[POOL_STATE + s - j][:, cols]
            cnt = float(min(past_len + s + 1, w))
            zs.append(acc / cnt - cur)
        z = jnp.concatenate(zs, axis=-1).astype(bf16)
        po = jnp.dot(z, plin_ref[...], preferred_element_type=f32) * pscale_ref[...]
        po_ref[s * n_seq:(s + 1) * n_seq, :] = po.astype(bf16)
    for j in range(POOL_STATE):
        st_ref[j] = ext[j + n_new]


def _proj_sample(x, ln1, w_qkvu, qg, kg, plin_bd, pscale, state, *, n_seq, n_new, past_len):
    n = n_seq * n_new
    kern = functools.partial(_proj_sample_kernel, n_seq=n_seq, n_new=n_new, past_len=past_len)
    return pl.pallas_call(
        kern,
        out_shape=[jax.ShapeDtypeStruct((n, ATTN_W), f32)] * 3
        + [jax.ShapeDtypeStruct((n, POOL_W), bf16),
           jax.ShapeDtypeStruct((POOL_STATE, n_seq, POOL_W), f32)],
        compiler_params=pltpu.CompilerParams(vmem_limit_bytes=VMEM_LIMIT),
        name="proj_sample",
    )(x, ln1, w_qkvu, qg, kg, plin_bd, pscale, state)


def _head_masks(shape):
    lane = lax.broadcasted_iota(jnp.int32, shape, len(shape) - 1)
    return [(lane >= h * HEAD_DIM) & (lane < (h + 1) * HEAD_DIM) for h in range(HEADS_PER_GROUP)]


def _band_attn_kernel(q_ref, kc_ref, kp_ref, vc_ref, vp_ref, bias_ref, o_ref, l_ref, kbuf, vbuf, *, tl):
    i = pl.program_id(2)
    kbuf[0:BAND, :] = kp_ref[0, 0]
    kbuf[BAND:BAND + tl, :] = kc_ref[0, 0]
    vbuf[0:BAND, :] = vp_ref[0, 0]
    vbuf[BAND:BAND + tl, :] = vc_ref[0, 0]
    masks = _head_masks((BAND, GROUP_W))

    def body(j, carry):
        r0 = pl.multiple_of(j * BAND, BAND)
        q = q_ref[0, 0, pl.ds(r0, BAND), :]
        qm = jnp.concatenate([jnp.where(m, q, jnp.zeros_like(q)) for m in masks], axis=0)
        kk = kbuf[pl.ds(r0, 2 * BAND), :]
        vv = vbuf[pl.ds(r0, 2 * BAND), :]
        s = lax.dot_general(qm, kk, (((1,), (1,)), ((), ())), preferred_element_type=f32)
        var = jnp.where((i == 0) & (j == 0), 0, 1)
        s = s + bias_ref[var]
        m = jnp.max(s, axis=-1, keepdims=True)
        p = jnp.exp(s - m)
        den = jnp.sum(p, axis=-1, keepdims=True)
        pv = jnp.dot(p.astype(bf16), vv, preferred_element_type=f32) / den
        lse = m + jnp.log(den)
        o = jnp.zeros((BAND, GROUP_W), f32)
        l = jnp.zeros((BAND, GROUP_W), f32)
        for h, msk in enumerate(masks):
            rows = slice(h * BAND, (h + 1) * BAND)
            o = jnp.where(msk, pv[rows], o)
            l = jnp.where(msk, lse[rows], l)
        o_ref[0, 0, pl.ds(r0, BAND), :] = o.astype(bf16)
        l_ref[0, 0, pl.ds(r0, BAND), :] = l
        return carry

    lax.fori_loop(0, tl // BAND, body, 0, unroll=min(2, tl // BAND))


def _band_bias(slopes_g, dil):
    qi = jnp.arange(BAND)[:, None]
    kb = jnp.arange(2 * BAND)[None, :]
    rel = qi + BAND - kb
    valid = (rel >= 0) & (rel <= BAND)
    alibi = -slopes_g[:, None, None] * (dil * rel)[None].astype(f32)
    variants = []
    for first in (True, False):
        ok = valid & (kb >= BAND) if first else valid
        variants.append(jnp.where(ok[None], alibi, NEG).reshape(HEADS_PER_GROUP * BAND, 2 * BAND))
    return jnp.stack(variants, axis=0)


def _band_attention(q, k, v, g, slopes_g, *, tl_max=1024):
    B, dil, L, _ = q.shape
    tl = min(tl_max, L)
    assert dil == DILATIONS[g] and L % tl == 0 and tl % BAND == 0
    nsub = tl // BAND
    bias = _band_bias(slopes_g, dil)
    cur = pl.BlockSpec((1, 1, tl, GROUP_W), lambda b, r, i: (b, r, i, 0))
    prev = pl.BlockSpec((1, 1, BAND, GROUP_W), lambda b, r, i: (b, r, jnp.maximum(i * nsub - 1, 0), 0))
    return pl.pallas_call(
        functools.partial(_band_attn_kernel, tl=tl),
        grid=(B, dil, L // tl),
        in_specs=[cur, cur, prev, cur, prev,
                  pl.BlockSpec((2, HEADS_PER_GROUP * BAND, 2 * BAND), lambda b, r, i: (0, 0, 0))],
        out_specs=[cur, cur],
        out_shape=[jax.ShapeDtypeStruct((B, dil, L, GROUP_W), bf16),
                   jax.ShapeDtypeStruct((B, dil, L, GROUP_W), f32)],
        scratch_shapes=[pltpu.VMEM((BAND + tl, GROUP_W), bf16), pltpu.VMEM((BAND + tl, GROUP_W), bf16)],
        compiler_params=_cparams(("arbitrary", "arbitrary", "arbitrary")),
        name="band_attn_g%d" % g,
    )(q, k, k, v, v, bias)


def _cached_attn_kernel(q_ref, kn_ref, vn_ref, kc_ref, vc_ref, bc_ref, bn_ref,
                        o_ref, l_ref, ko_ref, vo_ref, *, nb, n_new, win):
    masks8 = _head_masks((8, GROUP_W))
    lane_t = lax.broadcasted_iota(jnp.int32, (GROUP_W, LANES), 1)
    for bb in range(nb):
        q8 = q_ref[bb]
        kn8 = kn_ref[bb]
        vn8 = vn_ref[bb]
        kc = kc_ref[bb]
        vc = vc_ref[bb]
        qm = jnp.concatenate([jnp.where(m, q8, 0.0) for m in masks8], axis=0)
        sc = jnp.dot(qm.astype(bf16), kc.astype(bf16), preferred_element_type=f32) + bc_ref[...]
        m = jnp.max(sc, axis=-1, keepdims=True)
        sn = []
        for t in range(n_new):
            col = jnp.sum(qm * kn8[t:t + 1, :], axis=-1, keepdims=True) + bn_ref[:, t:t + 1]
            sn.append(col)
            m = jnp.maximum(m, col)
        pc = jnp.exp(sc - m)
        den = jnp.sum(pc, axis=-1, keepdims=True)
        acc = lax.dot_general(pc.astype(bf16), vc.astype(bf16), (((1,), (1,)), ((), ())),
                              preferred_element_type=f32)
        for t in range(n_new):
            pn = jnp.exp(sn[t] - m)
            den = den + pn
            acc = acc + pn * vn8[t:t + 1, :]
        acc = acc / den
        lse = m + jnp.log(den)
        o = jnp.zeros((8, GROUP_W), f32)
        l = jnp.zeros((8, GROUP_W), f32)
        for h, msk in enumerate(masks8):
            o = jnp.where(msk, acc[h * 8:(h + 1) * 8], o)
            l = jnp.where(msk, lse[h * 8:(h + 1) * 8], l)
        o_ref[bb] = o
        l_ref[bb] = l
        for c_ref, n8, out_ref in ((kc, kn8, ko_ref), (vc, vn8, vo_ref)):
            rolled = pltpu.roll(c_ref, win - n_new, axis=1)
            out_ref[bb] = rolled
            new_t = jnp.concatenate([n8, jnp.zeros((LANES - 8, GROUP_W), f32)], axis=0).T
            new_t = pltpu.roll(new_t, LANES - n_new, axis=1)
            out_ref[bb, :, win - LANES:win] = jnp.where(lane_t >= LANES - n_new, new_t, rolled[:, win - LANES:win])


def _cached_bias(slopes_g, dil, win, n_new):
    s = jnp.arange(8)[:, None]
    i = jnp.arange(win)[None, :]
    dist = win + s - i
    ok = (dist % dil == 0) & (dist // dil <= BAND) & (s < n_new)
    bc = jnp.where(ok[None], -slopes_g[:, None, None] * dist[None].astype(f32), NEG)
    t = jnp.arange(8)[None, :]
    dn = s - t
    okn = (dn >= 0) & (dn % dil == 0) & (dn // dil <= BAND) & (s < n_new) & (t < n_new)
    bn = jnp.where(okn[None], -slopes_g[:, None, None] * dn[None].astype(f32), NEG)
    pad = (s >= n_new)
    bc = jnp.where(pad[None], 0.0, bc)
    bn = jnp.where(pad[None], 0.0, bn)
    return bc.reshape(HEADS_PER_GROUP * 8, win), bn.reshape(HEADS_PER_GROUP * 8, 8)


def _cached_attention(q8, kn8, vn8, kc_t, vc_t, g, slopes_g, *, n_new):
    Bd, _, win = kc_t.shape
    dil = DILATIONS[g]
    assert win == WINDOWS[g] and win % LANES == 0
    nb = max(1, min(Bd, 2048 // win))
    assert Bd % nb == 0
    bc, bn = _cached_bias(slopes_g, dil, win, n_new)
    small = pl.BlockSpec((nb, 8, GROUP_W), lambda b: (b, 0, 0))
    cache = pl.BlockSpec((nb, GROUP_W, win), lambda b: (b, 0, 0))
    return pl.pallas_call(
        functools.partial(_cached_attn_kernel, nb=nb, n_new=n_new, win=win),
        grid=(Bd // nb,),
        in_specs=[small, small, small, cache, cache,
                  pl.BlockSpec(bc.shape, lambda b: (0, 0)), pl.BlockSpec(bn.shape, lambda b: (0, 0))],
        out_specs=[small, small, cache, cache],
        out_shape=[jax.ShapeDtypeStruct((Bd, 8, GROUP_W), f32)] * 2
        + [jax.ShapeDtypeStruct((Bd, GROUP_W, win), f32)] * 2,
        compiler_params=_cparams(("arbitrary",)),
        name="cached_attn_g%d" % g,
    )(q8, kn8, vn8, kc_t, vc_t, bc, bn)


def _merge_kernel(x_ref, o0_ref, o1_ref, o2_ref, l0_ref, l1_ref, l2_ref, po_ref,
                  ln1_ref, wg_ref, wpa_ref, wpb_ref, wo_ref, ln2_ref, wr_ref, br_ref, tri_ref,
                  h_ref, xn_ref, ri_ref, wc_ref, cnt_ref, carry, ibuf, *, tm, dils):
    step = pl.program_id(0) * pl.num_programs(1) + pl.program_id(1)

    @pl.when(step == 0)
    def _():
        carry[...] = jnp.zeros_like(carry)

    def token_order(ref, dil):
        if dil == 1:
            return ref[0, 0].astype(f32)
        chunks = []
        for c in range(GROUP_W // LANES):
            for r in range(dil):
                ibuf[pl.ds(r, tm // dil, stride=dil), :] = ref[0, r, :, c * LANES:(c + 1) * LANES].astype(f32)
            chunks.append(ibuf[...])
        return jnp.concatenate(chunks, axis=-1)

    x = x_ref[0]
    xn = _rmsnorm_rows(x, ln1_ref[...]).astype(bf16)
    gates = jnp.dot(xn, wg_ref[...], preferred_element_type=f32)
    l0, l1, l2 = (token_order(r, d) for r, d in zip((l0_ref, l1_ref, l2_ref), dils))
    lm = jnp.maximum(jnp.maximum(l0, l1), l2)
    e0, e1, e2 = jnp.exp(l0 - lm), jnp.exp(l1 - lm), jnp.exp(l2 - lm)
    o0, o1, o2 = (token_order(r, d) for r, d in zip((o0_ref, o1_ref, o2_ref), dils))
    attn = (e0 * o0 + e1 * o1 + e2 * o2) / (e0 + e1 + e2)
    ma = jnp.dot(attn.astype(bf16), wpa_ref[...], preferred_element_type=f32)
    mb = jnp.dot(po_ref[0], wpb_ref[...], preferred_element_type=f32)
    mix = jax.nn.sigmoid(gates[:, :D_MODEL]) * ma + jax.nn.sigmoid(gates[:, D_MODEL:]) * mb
    h = x + jnp.dot(mix.astype(bf16), wo_ref[...], preferred_element_type=f32)
    h_ref[...] = h
    xn2 = _rmsnorm_rows(h, ln2_ref[...])
    xn_ref[...] = xn2

    lt = (jnp.dot(xn2.astype(bf16), wr_ref[...], preferred_element_type=f32) + br_ref[...]).T
    row8 = lax.broadcasted_iota(jnp.int32, (8, tm), 0)
    gl = jnp.where(row8 < N_EXPERT_GROUPS, lt[0:8], -jnp.inf)
    gmax = jnp.max(gl, axis=0, keepdims=True)
    gidx = jnp.min(jnp.where(gl == gmax, row8, 8), axis=0, keepdims=True)
    pg = 1.0 / jnp.sum(jnp.exp(gl - gmax), axis=0, keepdims=True)
    sel = jnp.zeros((8, tm), f32)
    for g in range(N_EXPERT_GROUPS):
        lo = EXPERT_COL0 + g * EXPERTS_PER_GROUP
        sel = jnp.where(gidx == g, lt[lo:lo + EXPERTS_PER_GROUP], sel)
    v0 = jnp.max(sel, axis=0, keepdims=True)
    i0 = jnp.min(jnp.where(sel == v0, row8, 8), axis=0, keepdims=True)
    sel2 = jnp.where(row8 == i0, -jnp.inf, sel)
    v1 = jnp.max(sel2, axis=0, keepdims=True)
    i1 = jnp.min(jnp.where(sel2 == v1, row8, 8), axis=0, keepdims=True)
    t = jnp.exp(v1 - v0)
    w0 = pg / (1.0 + t)
    w1 = pg * t / (1.0 + t)
    eid0 = gidx * EXPERTS_PER_GROUP + i0
    eid1 = gidx * EXPERTS_PER_GROUP + i1
    erow = lax.broadcasted_iota(jnp.int32, (N_EXPERTS, tm), 0)
    oh0 = erow == eid0
    oh1 = erow == eid1
    cnt = jnp.where(oh0 | oh1, 1.0, 0.0)
    before = jnp.dot(cnt.astype(bf16), tri_ref[...], preferred_element_type=f32) + carry[:, 0:1]
    rank0 = jnp.sum(jnp.where(oh0, before, 0.0), axis=0, keepdims=True)
    rank1 = jnp.sum(jnp.where(oh1, before, 0.0), axis=0, keepdims=True)
    carry[...] = carry[...] + jnp.sum(cnt, axis=1, keepdims=True)
    cnt_ref[...] = carry[...]
    ri = jnp.zeros((8, tm), jnp.int32)
    for r, val in enumerate((eid0, eid1, rank0.astype(jnp.int32), rank1.astype(jnp.int32))):
        ri = jnp.where(row8 == r, val, ri)
    ri_ref[...] = ri
    rowl = lax.broadcasted_iota(jnp.int32, (LANES, tm), 0)
    wslab = jnp.where(rowl == 0, w0, jnp.where(rowl == 1, w1, 0.0))
    wc_ref[...] = wslab.T


def _merge(x, o, l, po, ln1, w_gates, w_pa, w_pb, w_o, ln2, w_router, b_router, *, tm):
    B, T, _ = x.shape
    assert T % tm == 0
    nt = T // tm
    n = B * T
    dils = tuple(a.shape[1] for a in o)
    assert all(tm % (8 * d) == 0 for d in dils)
    tri = (jnp.arange(tm)[:, None] < jnp.arange(tm)[None, :]).astype(bf16)
    rows3 = lambda w: pl.BlockSpec((1, tm, w), lambda b, i: (b, i, 0))
    flat = lambda w: pl.BlockSpec((tm, w), lambda b, i: (b * nt + i, 0))
    grp = [pl.BlockSpec((1, d, tm // d, GROUP_W), lambda b, i: (b, 0, i, 0)) for d in dils]
    full = lambda a: pl.BlockSpec(a.shape, lambda b, i: (0,) * a.ndim)
    weights = (ln1, w_gates, w_pa, w_pb, w_o, ln2, w_router, b_router, tri)
    return pl.pallas_call(
        functools.partial(_merge_kernel, tm=tm, dils=dils),
        grid=(B, nt),
        in_specs=[rows3(D_MODEL)] + grp + grp + [rows3(POOL_W)] + [full(a) for a in weights],
        out_specs=[flat(D_MODEL), flat(D_MODEL),
                   pl.BlockSpec((8, tm), lambda b, i: (0, b * nt + i)),
                   flat(LANES),
                   pl.BlockSpec((N_EXPERTS, LANES), lambda b, i: (0, 0))],
        out_shape=[jax.ShapeDtypeStruct((n, D_MODEL), f32), jax.ShapeDtypeStruct((n, D_MODEL), f32),
                   jax.ShapeDtypeStruct((8, n), jnp.int32),
                   jax.ShapeDtypeStruct((n, LANES), f32),
                   jax.ShapeDtypeStruct((N_EXPERTS, LANES), f32)],
        scratch_shapes=[pltpu.VMEM((N_EXPERTS, LANES), f32), pltpu.VMEM((tm, LANES), f32)],
        compiler_params=_cparams(("arbitrary", "arbitrary")),
        name="merge_router",
    )(x, *o, *l, po, *weights)


def _dest_kernel(ri_ref, ps_ref, dest_ref):
    tn = ri_ref.shape[1]
    erow = lax.broadcasted_iota(jnp.int32, (N_EXPERTS, tn), 0)
    ps = ps_ref[:, 0:1]
    row8 = lax.broadcasted_iota(jnp.int32, (8, tn), 0)
    out = jnp.zeros((8, tn), jnp.int32)
    for k in range(2):
        start = jnp.sum(jnp.where(erow == ri_ref[k:k + 1, :], ps, 0.0), axis=0, keepdims=True)
        out = jnp.where(row8 == k, start.astype(jnp.int32) + ri_ref[2 + k:3 + k, :], out)
    dest_ref[...] = out


def _dest(ri, pad_starts, *, tn):
    n = ri.shape[1]
    ps = jnp.broadcast_to(pad_starts.astype(f32)[:, None], (N_EXPERTS, LANES))
    return pl.pallas_call(
        _dest_kernel,
        grid=(n // tn,),
        in_specs=[pl.BlockSpec((8, tn), lambda i: (0, i)), pl.BlockSpec((N_EXPERTS, LANES), lambda i: (0, 0))],
        out_specs=pl.BlockSpec((8, tn), lambda i: (0, i)),
        out_shape=jax.ShapeDtypeStruct((8, n), jnp.int32),
        compiler_params=_cparams(("arbitrary",)),
        name="moe_dest",
    )(ri, ps)


def _dispatch_kernel(meta_ref, dest_ref, x_ref, xs_hbm, zblk, sem, zsem, *, tm, blk, n_blocks):
    i = pl.program_id(0)

    @pl.when(i == 0)
    def _():
        zblk[...] = jnp.zeros_like(zblk)

        def zero_copy(r):
            return pltpu.make_async_copy(zblk.at[pl.ds(0, 1)], xs_hbm.at[pl.ds(r, 1)], zsem)

        def tail_copy(b):
            return pltpu.make_async_copy(zblk, xs_hbm.at[pl.ds(pl.multiple_of(b * blk, blk), blk)], zsem)

        last = N_EXPERTS - 1
        first_tail = (meta_ref[last] + meta_ref[2 * N_EXPERTS + last]) // blk

        def tail_start(b, c):
            tail_copy(b).start()
            return c

        def tail_wait(b, c):
            tail_copy(b).wait()
            return c

        lax.fori_loop(first_tail, n_blocks, tail_start, 0)
        lax.fori_loop(first_tail, n_blocks, tail_wait, 0)

        def per_expert(e, c):
            lo = meta_ref[e] + meta_ref[N_EXPERTS + e]
            hi = meta_ref[e] + meta_ref[2 * N_EXPERTS + e]

            def start(r, c2):
                zero_copy(r).start()
                return c2

            def wait(r, c2):
                zero_copy(r).wait()
                return c2

            lax.fori_loop(lo, hi, start, 0)
            lax.fori_loop(lo, hi, wait, 0)
            return c

        lax.fori_loop(0, N_EXPERTS, per_expert, 0)

    def issue(t, c):
        for k in range(2):
            pltpu.make_async_copy(x_ref.at[pl.ds(t, 1)], xs_hbm.at[pl.ds(dest_ref[k, t], 1)], sem).start()
        return c

    lax.fori_loop(0, tm, issue, 0, unroll=8)
    for k in range(2):
        pltpu.make_async_copy(x_ref, xs_hbm.at[pl.ds(0, tm)], sem).wait()


def _dispatch(xn2, dest, meta, *, tm, blk, n_blocks):
    n = xn2.shape[0]
    n_rows = n_blocks * blk
    return pl.pallas_call(
        functools.partial(_dispatch_kernel, tm=tm, blk=blk, n_blocks=n_blocks),
        grid_spec=pltpu.PrefetchScalarGridSpec(
            num_scalar_prefetch=1,
            grid=(n // tm,),
            in_specs=[pl.BlockSpec((8, tm), lambda i, m: (0, i), memory_space=pltpu.SMEM),
                      pl.BlockSpec((tm, D_MODEL), lambda i, m: (i, 0))],
            out_specs=pl.BlockSpec(memory_space=pl.ANY),
            scratch_shapes=[pltpu.VMEM((blk, D_MODEL), f32), pltpu.SemaphoreType.DMA, pltpu.SemaphoreType.DMA],
        ),
        out_shape=jax.ShapeDtypeStruct((n_rows, D_MODEL), f32),
        compiler_params=_cparams(("arbitrary",)),
        name="moe_dispatch",
    )(meta, dest, xn2)


def _moe_kernel(blk_e_ref, nvalid_ref, x_ref, wg_ref, wu_ref, wd_ref, y_ref):
    b = pl.program_id(0)

    @pl.when(nvalid_ref[b] > 0)
    def _():
        x = x_ref[...].astype(bf16)
        hid = jax.nn.silu(jnp.dot(x, wg_ref[0], preferred_element_type=f32)) * jnp.dot(x, wu_ref[0], preferred_element_type=f32)
        y_ref[...] = jnp.dot(hid.astype(bf16), wd_ref[0], preferred_element_type=f32)

    @pl.when(nvalid_ref[b] == 0)
    def _():
        y_ref[...] = jnp.zeros_like(y_ref)


def _moe(xs, blk_e, nvalid, w_gate, w_up, w_down, *, blk):
    n_blocks = blk_e.shape[0]
    wspec = lambda shape: pl.BlockSpec((1,) + shape, lambda b, be, nv: (be[b], 0, 0))
    rows = pl.BlockSpec((blk, D_MODEL), lambda b, be, nv: (b, 0))
    return pl.pallas_call(
        _moe_kernel,
        grid_spec=pltpu.PrefetchScalarGridSpec(
            num_scalar_prefetch=2,
            grid=(n_blocks,),
            in_specs=[rows, wspec((D_MODEL, D_EXPERT)), wspec((D_MODEL, D_EXPERT)), wspec((D_EXPERT, D_MODEL))],
            out_specs=rows,
        ),
        out_shape=jax.ShapeDtypeStruct((n_blocks * blk, D_MODEL), f32),
        compiler_params=_cparams(("arbitrary",)),
        name="moe_experts",
    )(blk_e, nvalid, xs, w_gate, w_up, w_down)


def _combine_kernel(dest_ref, h_ref, wc_ref, ys_hbm, y_ref, gbuf, sem, *, tm):
    def issue(t, c):
        for k in range(2):
            pltpu.make_async_copy(ys_hbm.at[pl.ds(dest_ref[k, t], 1)], gbuf.at[k, pl.ds(t, 1)], sem).start()
        return c

    lax.fori_loop(0, tm, issue, 0, unroll=8)
    for k in range(2):
        pltpu.make_async_copy(ys_hbm.at[pl.ds(0, tm)], gbuf.at[k], sem).wait()
    w = wc_ref[...]
    y_ref[...] = h_ref[...] + w[:, 0:1] * gbuf[0] + w[:, 1:2] * gbuf[1]


def _combine(h, ys, dest, wc, *, tm):
    n = h.shape[0]
    rows = lambda w: pl.BlockSpec((tm, w), lambda i: (i, 0))
    return pl.pallas_call(
        functools.partial(_combine_kernel, tm=tm),
        grid=(n // tm,),
        in_specs=[pl.BlockSpec((8, tm), lambda i: (0, i), memory_space=pltpu.SMEM),
                  rows(D_MODEL), rows(LANES), pl.BlockSpec(memory_space=pl.ANY)],
        out_specs=rows(D_MODEL),
        out_shape=jax.ShapeDtypeStruct((n, D_MODEL), f32),
        scratch_shapes=[pltpu.VMEM((2, tm, D_MODEL), f32), pltpu.SemaphoreType.DMA],
        compiler_params=_cparams(("arbitrary",)),
        name="moe_combine",
    )(dest, h, wc, ys)


def _mix_and_moe(x, o, l, po, wts, *, tm, blk):
    n = x.shape[0] * x.shape[1]
    h, xn2, ri, wc, cnt = _merge(x, o, l, po, wts['ln1'], wts['w_gates'], wts['w_pa'], wts['w_pb'], wts['w_o'],
                                 wts['ln2'], wts['w_router'], wts['b_router'], tm=tm)
    n_blocks = (2 * n + N_EXPERTS * (blk - 1) + blk - 1) // blk
    counts = cnt[:, 0].astype(jnp.int32)
    padded = (counts + blk - 1) // blk * blk
    pad_ends = jnp.cumsum(padded)
    pad_starts = pad_ends - padded
    blk_start = jnp.arange(n_blocks, dtype=jnp.int32) * blk
    blk_e = jnp.minimum(jnp.sum(pad_ends[None, :] <= blk_start[:, None], axis=1), N_EXPERTS - 1).astype(jnp.int32)
    nvalid = jnp.clip(pad_starts[blk_e] + counts[blk_e] - blk_start, 0, blk).astype(jnp.int32)
    meta = jnp.concatenate([pad_starts, counts, padded]).astype(jnp.int32)
    dest = _dest(ri, pad_starts, tn=tm)
    xs = _dispatch(xn2, dest, meta, tm=tm, blk=blk, n_blocks=n_blocks)
    ys = _moe(xs, blk_e, nvalid, wts['w_gate'], wts['w_up'], wts['w_down'], blk=blk)
    return _combine(h, ys, dest, wc, tm=tm)


def kernel(x_prompt, x_sample, cache_k_w128, cache_v_w128, cache_k_w512, cache_v_w512, cache_k_w2048, cache_v_w2048, state_pool, ln1, w_in, q_gain, k_gain, pool_lin, pool_scale, w_pa, w_pb, w_o, ln2, w_rg, b_rg, w_re, b_re, w_gate, w_up, w_down):
    B, T, D = x_prompt.shape
    Bd, S, _ = x_sample.shape
    past_len = 8192
    caches = ((cache_k_w128, cache_v_w128), (cache_k_w512, cache_v_w512), (cache_k_w2048, cache_v_w2048))
    slopes = jnp.exp2(-8.0 * jnp.arange(1, N_HEADS + 1, dtype=f32) / N_HEADS).reshape(N_GROUPS, HEADS_PER_GROUP)

    w_qkvu = w_in[:, :QKVU_W].astype(bf16)
    plin_bd = jnp.zeros((POOL_W, POOL_W), f32)
    for g in range(len(POOL_WINDOWS)):
        plin_bd = plin_bd.at[g * POOL_GW:(g + 1) * POOL_GW, g * POOL_GW:(g + 1) * POOL_GW].set(pool_lin[g])
    w_router = jnp.zeros((D, ROUTER_W), f32).at[:, :N_EXPERT_GROUPS].set(w_rg)
    w_router = w_router.at[:, EXPERT_COL0:EXPERT_COL0 + N_EXPERTS].set(w_re)
    b_router = jnp.zeros((1, ROUTER_W), f32).at[0, :N_EXPERT_GROUPS].set(b_rg)
    b_router = b_router.at[0, EXPERT_COL0:EXPERT_COL0 + N_EXPERTS].set(b_re)
    wts = dict(ln1=ln1.reshape(1, D), w_gates=w_in[:, QKVU_W:].astype(bf16), w_pa=w_pa.astype(bf16),
               w_pb=w_pb.astype(bf16), w_o=w_o.astype(bf16), ln2=ln2.reshape(1, D),
               w_router=w_router.astype(bf16), b_router=b_router,
               w_gate=w_gate.astype(bf16), w_up=w_up.astype(bf16), w_down=w_down.astype(bf16))
    proj_w = (wts['ln1'], w_qkvu, q_gain.reshape(1, ATTN_W), k_gain.reshape(1, ATTN_W),
              plin_bd.astype(bf16), pool_scale.reshape(1, POOL_W))

    *qkv, po, kt, vt, ut = _proj_prompt(x_prompt, *proj_w, tm=512)
    o, l = zip(*[_band_attention(*qkv[3 * g:3 * g + 3], g, slopes[g]) for g in range(N_GROUPS)])
    y_prompt = _mix_and_moe(x_prompt, o, l, po, wts, tm=512, blk=256).reshape(B, T, D)
    tail = kt.shape[2]
    pkv = []
    for g, w in enumerate(WINDOWS):
        for a in (kt, vt):
            a = a.reshape(B, N_HEADS, HEAD_DIM, tail)[:, g * HEADS_PER_GROUP:(g + 1) * HEADS_PER_GROUP, :, tail - w:]
            pkv.append(a.transpose(0, 3, 1, 2))
    p_pool = ut[:, 1:]

    n_s = Bd * S
    xs = x_sample.transpose(1, 0, 2).reshape(n_s, D)
    qs, ks, vs, pos, st = _proj_sample(xs, *proj_w, state_pool.transpose(1, 0, 2),
                                       n_seq=Bd, n_new=S, past_len=past_len)
    so, sl, skv = [], [], []
    pad8 = lambda a: jnp.pad(a.reshape(S, Bd, GROUP_W).transpose(1, 0, 2), ((0, 0), (0, 8 - S), (0, 0)))
    for g, w in enumerate(WINDOWS):
        cols = slice(g * GROUP_W, (g + 1) * GROUP_W)
        kc, vc = caches[g]
        og, lg, ko, vo = _cached_attention(
            pad8(qs[:, cols]), pad8(ks[:, cols]), pad8(vs[:, cols]),
            kc.transpose(0, 2, 3, 1).reshape(Bd, GROUP_W, w), vc.transpose(0, 2, 3, 1).reshape(Bd, GROUP_W, w),
            g, slopes[g], n_new=S)
        so.append(og[:, :S].transpose(1, 0, 2).reshape(1, 1, n_s, GROUP_W).astype(bf16))
        sl.append(lg[:, :S].transpose(1, 0, 2).reshape(1, 1, n_s, GROUP_W))
        for a in (ko, vo):
            skv.append(a.reshape(Bd, HEADS_PER_GROUP, HEAD_DIM, w).transpose(0, 3, 1, 2))
    y_sample = _mix_and_moe(xs[None], so, sl, pos[None], wts, tm=n_s, blk=128).reshape(S, Bd, D).transpose(1, 0, 2)
    s_pool = st.transpose(1, 0, 2)

    return (y_prompt, y_sample, *pkv, p_pool, *skv, s_pool)
```

```python
import functools

import jax
import jax.numpy as jnp
from jax import lax
from jax.experimental import pallas as pl
from jax.experimental.pallas import tpu as pltpu

D_MODEL = 1024
HEAD_DIM = 64
HEADS_PER_GROUP = 4
WINDOWS = (128, 512, 2048)
DILATIONS = (1, 4, 16)
N_GROUPS = len(WINDOWS)
N_HEADS = HEADS_PER_GROUP * N_GROUPS
ATTN_W = N_HEADS * HEAD_DIM
GROUP_W = HEADS_PER_GROUP * HEAD_DIM
BAND = 128
POOL_WINDOWS = (2, 4, 8, 16)
POOL_GW = 128
POOL_W = len(POOL_WINDOWS) * POOL_GW
POOL_STATE = max(POOL_WINDOWS) - 1
POOL_HIST = 32
assert POOL_WINDOWS == (2, 4, 8, 16)
N_EXPERT_GROUPS = 4
EXPERTS_PER_GROUP = 8
N_EXPERTS = N_EXPERT_GROUPS * EXPERTS_PER_GROUP
D_EXPERT = 512
QKVU_W = 3 * ATTN_W + POOL_W
NEG = -1e30
EPS = 1e-6
LANES = 128
ROUTER_W = LANES
EXPERT_COL0 = 8
VMEM_LIMIT = 56 * 1024 * 1024

assert all(w // d == BAND for w, d in zip(WINDOWS, DILATIONS))

f32 = jnp.float32
bf16 = jnp.bfloat16


def _cparams(sem):
    return pltpu.CompilerParams(dimension_semantics=sem, vmem_limit_bytes=VMEM_LIMIT)


def _rmsnorm_rows(x, g):
    ms = jnp.mean(x * x, axis=-1, keepdims=True)
    return x * lax.rsqrt(ms + EPS) * g


def _head_rmsnorm_chunk(ch, gain):
    lane = lax.broadcasted_iota(jnp.int32, ch.shape, 1)
    lo_mask = lane < HEAD_DIM
    sq = ch * ch
    lo = jnp.sum(jnp.where(lo_mask, sq, 0.0), axis=-1, keepdims=True)
    hi = jnp.sum(jnp.where(lo_mask, 0.0, sq), axis=-1, keepdims=True)
    ss = jnp.where(lo_mask, lo, hi)
    return ch * lax.rsqrt(ss * (1.0 / HEAD_DIM) + EPS) * gain


def _project(x, ln1, w_ref, qg, kg):
    xn = _rmsnorm_rows(x, ln1).astype(bf16)
    nch = ATTN_W // LANES
    qs = qg * (HEAD_DIM ** -0.5)
    zq = jnp.dot(xn, w_ref[:, 0:ATTN_W], preferred_element_type=f32)
    q = [_head_rmsnorm_chunk(zq[:, c * LANES:(c + 1) * LANES], qs[:, c * LANES:(c + 1) * LANES]) for c in range(nch)]
    zk = jnp.dot(xn, w_ref[:, ATTN_W:2 * ATTN_W], preferred_element_type=f32)
    k = [_head_rmsnorm_chunk(zk[:, c * LANES:(c + 1) * LANES], kg[:, c * LANES:(c + 1) * LANES]) for c in range(nch)]
    zv = jnp.dot(xn, w_ref[:, 2 * ATTN_W:3 * ATTN_W], preferred_element_type=f32)
    v = [zv[:, c * LANES:(c + 1) * LANES] for c in range(nch)]
    u = jnp.dot(xn, w_ref[:, 3 * ATTN_W:3 * ATTN_W + POOL_W], preferred_element_type=f32)
    return q, k, v, u


def _proj_prompt_kernel(x_ref, ln1_ref, w_ref, qg_ref, kg_ref, plin_ref, pscale_ref, *rest,
                        tm, n_tiles, tail_tiles):
    qkv_refs = rest[:3 * N_GROUPS]
    po_ref, kt_ref, vt_ref, ut_ref, ubuf, pa, pb, sbuf = rest[3 * N_GROUPS:]
    i = pl.program_id(1)
    hist = POOL_STATE + 1
    ph, rows = POOL_HIST, POOL_HIST + tm

    @pl.when(i == 0)
    def _():
        ubuf[0:ph, :] = jnp.zeros((ph, POOL_W), f32)

    @pl.when(i > 0)
    def _():
        ubuf[0:ph, :] = ubuf[tm:tm + ph, :]

    q, k, v, u = _project(x_ref[0], ln1_ref[...], w_ref, qg_ref[...], kg_ref[...])
    cpg = GROUP_W // LANES
    slot = 0
    for t, chunks in enumerate((q, k, v)):
        for g, dil in enumerate(DILATIONS):
            out_ref = qkv_refs[3 * g + t]
            for c in range(cpg):
                val = chunks[g * cpg + c]
                cols = slice(c * LANES, (c + 1) * LANES)
                if dil == 1:
                    out_ref[0, 0, :, cols] = val.astype(bf16)
                else:
                    sbuf[slot * tm:(slot + 1) * tm, :] = val
                    for r in range(dil):
                        out_ref[0, r, :, cols] = sbuf[pl.ds(slot * tm + r, tm // dil, stride=dil), :].astype(bf16)
                    slot += 1

    ubuf[ph:rows, :] = u
    gw = POOL_GW
    pa[8:rows, :] = ubuf[8:rows, :] + ubuf[7:rows - 1, :]
    pb[16:rows, gw:] = pa[16:rows, gw:] + pa[14:rows - 2, gw:]
    pa[24:rows, 2 * gw:] = pb[24:rows, 2 * gw:] + pb[20:rows - 4, 2 * gw:]
    pb[32:rows, 3 * gw:] = pa[32:rows, 3 * gw:] + pa[24:rows - 8, 3 * gw:]
    pos = i * tm + lax.broadcasted_iota(jnp.int32, (tm, POOL_GW), 0)
    zs = []
    for g, w in enumerate(POOL_WINDOWS):
        cols = slice(g * POOL_GW, (g + 1) * POOL_GW)
        wsum = (pa if g % 2 == 0 else pb)[ph:rows, cols]
        cnt = jnp.minimum(pos + 1, w).astype(f32)
        zs.append(wsum / cnt - u[:, cols])
    z = jnp.concatenate(zs, axis=-1).astype(bf16)
    po = jnp.dot(z, plin_ref[...], preferred_element_type=f32) * pscale_ref[...]
    po_ref[0] = po.astype(bf16)

    @pl.when(i >= n_tiles - tail_tiles)
    def _():
        kt_ref[0] = jnp.concatenate(k, axis=-1).T
        vt_ref[0] = jnp.concatenate(v, axis=-1).T

    @pl.when(i == n_tiles - 1)
    def _():
        ut_ref[0] = ubuf[rows - hist:rows, :]


def _proj_prompt(x, ln1, w_qkvu, qg, kg, plin_bd, pscale, *, tm):
    B, T, D = x.shape
    n_tiles = T // tm
    tail = max(WINDOWS)
    assert T % tm == 0 and tail % tm == 0 and T >= tail
    tail_tiles = tail // tm
    hist = POOL_STATE + 1
    kern = functools.partial(_proj_prompt_kernel, tm=tm, n_tiles=n_tiles, tail_tiles=tail_tiles)
    const = lambda b, i: (0, 0)
    assert all(tm % (16 * d) == 0 for d in DILATIONS)
    qkv_specs = [pl.BlockSpec((1, d, tm // d, GROUP_W), lambda b, i: (b, 0, i, 0)) for d in DILATIONS for _ in range(3)]
    qkv_shapes = [jax.ShapeDtypeStruct((B, d, T // d, GROUP_W), bf16) for d in DILATIONS for _ in range(3)]
    tail_spec = pl.BlockSpec((1, ATTN_W, tm), lambda b, i: (b, 0, jnp.maximum(i - (n_tiles - tail_tiles), 0)))
    return pl.pallas_call(
        kern,
        grid=(B, n_tiles),
        in_specs=[
            pl.BlockSpec((1, tm, D), lambda b, i: (b, i, 0)),
            pl.BlockSpec((1, D), const),
            pl.BlockSpec((D, QKVU_W), const),
            pl.BlockSpec((1, ATTN_W), const),
            pl.BlockSpec((1, ATTN_W), const),
            pl.BlockSpec((POOL_W, POOL_W), const),
            pl.BlockSpec((1, POOL_W), const),
        ],
        out_specs=qkv_specs
        + [pl.BlockSpec((1, tm, POOL_W), lambda b, i: (b, i, 0)),
           tail_spec, tail_spec,
           pl.BlockSpec((1, hist, POOL_W), lambda b, i: (b, 0, 0))],
        out_shape=qkv_shapes
        + [jax.ShapeDtypeStruct((B, T, POOL_W), bf16),
           jax.ShapeDtypeStruct((B, ATTN_W, tail), f32),
           jax.ShapeDtypeStruct((B, ATTN_W, tail), f32),
           jax.ShapeDtypeStruct((B, hist, POOL_W), f32)],
        scratch_shapes=[pltpu.VMEM((POOL_HIST + tm, POOL_W), f32)] * 3
                       + [pltpu.VMEM((3 * (N_GROUPS - 1) * (GROUP_W // LANES) * tm, LANES), f32)],
        compiler_params=_cparams(("arbitrary", "arbitrary")),
        name="proj_prompt",
    )(x, ln1, w_qkvu, qg, kg, plin_bd, pscale)


def _proj_sample_kernel(x_ref, ln1_ref, w_ref, qg_ref, kg_ref, plin_ref, pscale_ref, state_ref,
                        q_ref, k_ref, v_ref, po_ref, st_ref, *, n_seq, n_new, past_len):
    q, k, v, u = _project(x_ref[...], ln1_ref[...], w_ref, qg_ref[...], kg_ref[...])
    q_ref[...] = jnp.concatenate(q, axis=-1)
    k_ref[...] = jnp.concatenate(k, axis=-1)
    v_ref[...] = jnp.concatenate(v, axis=-1)
    ext = [state_ref[j] for j in range(POOL_STATE)] + [u[s * n_seq:(s + 1) * n_seq, :] for s in range(n_new)]
    for s in range(n_new):
        zs = []
        for g, w in enumerate(POOL_WINDOWS):
            cols = slice(g * POOL_GW, (g + 1) * POOL_GW)
            cur = ext[POOL_STATE + s][:, cols]
            acc = cur
            for j in range(1, w):
                acc = acc + ext[POOL_STATE + s - j][:, cols]
            cnt = float(min(past_len + s + 1, w))
            zs.append(acc / cnt - cur)
        z = jnp.concatenate(zs, axis=-1).astype(bf16)
        po = jnp.dot(z, plin_ref[...], preferred_element_type=f32) * pscale_ref[...]
        po_ref[s * n_seq:(s + 1) * n_seq, :] = po.astype(bf16)
    for j in range(POOL_STATE):
        st_ref[j] = ext[j + n_new]


def _proj_sample(x, ln1, w_qkvu, qg, kg, plin_bd, pscale, state, *, n_seq, n_new, past_len):
    n = n_seq * n_new
    kern = functools.partial(_proj_sample_kernel, n_seq=n_seq, n_new=n_new, past_len=past_len)
    return pl.pallas_call(
        kern,
        out_shape=[jax.ShapeDtypeStruct((n, ATTN_W), f32)] * 3
        + [jax.ShapeDtypeStruct((n, POOL_W), bf16),
           jax.ShapeDtypeStruct((POOL_STATE, n_seq, POOL_W), f32)],
        compiler_params=pltpu.CompilerParams(vmem_limit_bytes=VMEM_LIMIT),
        name="proj_sample",
    )(x, ln1, w_qkvu, qg, kg, plin_bd, pscale, state)


def _head_masks(shape):
    lane = lax.broadcasted_iota(jnp.int32, shape, len(shape) - 1)
    return [(lane >= h * HEAD_DIM) & (lane < (h + 1) * HEAD_DIM) for h in range(HEADS_PER_GROUP)]


def _band_attn_kernel(q_ref, kc_ref, kp_ref, vc_ref, vp_ref, bias_ref, o_ref, l_ref, kbuf, vbuf, *, tl, unroll):
    i = pl.program_id(2)
    kbuf[0:BAND, :] = kp_ref[0, 0]
    kbuf[BAND:2 * BAND, :] = kc_ref[0, 0, 0:BAND, :]
    vbuf[0:BAND, :] = vp_ref[0, 0]
    vbuf[BAND:2 * BAND, :] = vc_ref[0, 0, 0:BAND, :]
    masks = _head_masks((BAND, GROUP_W))

    def sub_block(j, kk, vv, var):
        r0 = j * BAND if isinstance(j, int) else pl.multiple_of(j * BAND, BAND)
        q = q_ref[0, 0, pl.ds(r0, BAND), :]
        qm = jnp.concatenate([jnp.where(m, q, jnp.zeros_like(q)) for m in masks], axis=0)
        s = lax.dot_general(qm, kk, (((1,), (1,)), ((), ())), preferred_element_type=f32)
        s = s + bias_ref[var]
        m = jnp.max(s, axis=-1, keepdims=True)
        p = jnp.exp(s - m)
        den = jnp.sum(p, axis=-1, keepdims=True)
        pv = jnp.dot(p.astype(bf16), vv, preferred_element_type=f32)
        o = jnp.zeros((BAND, GROUP_W), f32)
        ms = jnp.zeros((BAND, GROUP_W), f32)
        ds = jnp.ones((BAND, GROUP_W), f32)
        for h, msk in enumerate(masks):
            rows = slice(h * BAND, (h + 1) * BAND)
            o = jnp.where(msk, pv[rows], o)
            ms = jnp.where(msk, m[rows], ms)
            ds = jnp.where(msk, den[rows], ds)
        o_ref[0, 0, pl.ds(r0, BAND), :] = (o / ds).astype(bf16)
        l_ref[0, 0, pl.ds(r0, BAND), :] = ms + jnp.log(ds)

    sub_block(0, kbuf[...], vbuf[...], jnp.where(i == 0, 0, 1))

    def body(j, carry):
        k0 = pl.multiple_of((j - 1) * BAND, BAND)
        sub_block(j, kc_ref[0, 0, pl.ds(k0, 2 * BAND), :], vc_ref[0, 0, pl.ds(k0, 2 * BAND), :], 1)
        return carry

    if tl > BAND:
        lax.fori_loop(1, tl // BAND, body, 0, unroll=unroll)


def _band_bias(slopes_g, dil):
    qi = jnp.arange(BAND)[:, None]
    kb = jnp.arange(2 * BAND)[None, :]
    rel = qi + BAND - kb
    valid = (rel >= 0) & (rel <= BAND)
    alibi = -slopes_g[:, None, None] * (dil * rel)[None].astype(f32)
    variants = []
    for first in (True, False):
        ok = valid & (kb >= BAND) if first else valid
        variants.append(jnp.where(ok[None], alibi, NEG).reshape(HEADS_PER_GROUP * BAND, 2 * BAND))
    return jnp.stack(variants, axis=0)


def _band_attention(q, k, v, g, slopes_g, *, tl_max=1024, unroll=7):
    B, dil, L, _ = q.shape
    tl = min(tl_max, L)
    assert dil == DILATIONS[g] and L % tl == 0 and tl % BAND == 0
    nsub = tl // BAND
    bias = _band_bias(slopes_g, dil)
    cur = pl.BlockSpec((1, 1, tl, GROUP_W), lambda b, r, i: (b, r, i, 0))
    prev = pl.BlockSpec((1, 1, BAND, GROUP_W), lambda b, r, i: (b, r, jnp.maximum(i * nsub - 1, 0), 0))
    return pl.pallas_call(
        functools.partial(_band_attn_kernel, tl=tl, unroll=max(1, min(unroll, nsub - 1))),
        grid=(B, dil, L // tl),
        in_specs=[cur, cur, prev, cur, prev,
                  pl.BlockSpec((2, HEADS_PER_GROUP * BAND, 2 * BAND), lambda b, r, i: (0, 0, 0))],
        out_specs=[cur, cur],
        out_shape=[jax.ShapeDtypeStruct((B, dil, L, GROUP_W), bf16),
                   jax.ShapeDtypeStruct((B, dil, L, GROUP_W), f32)],
        scratch_shapes=[pltpu.VMEM((2 * BAND, GROUP_W), bf16), pltpu.VMEM((2 * BAND, GROUP_W), bf16)],
        compiler_params=_cparams(("arbitrary", "arbitrary", "arbitrary")),
        name="band_attn_g%d" % g,
    )(q, k, k, v, v, bias)


def _cached_attn_kernel(q_ref, kn_ref, vn_ref, kc_ref, vc_ref, bc_ref, bn_ref,
                        o_ref, l_ref, ko_ref, vo_ref, *, nb, n_new, win):
    masks8 = _head_masks((8, GROUP_W))
    lane_t = lax.broadcasted_iota(jnp.int32, (GROUP_W, LANES), 1)
    for bb in range(nb):
        q8 = q_ref[bb]
        kn8 = kn_ref[bb]
        vn8 = vn_ref[bb]
        kc = kc_ref[bb]
        vc = vc_ref[bb]
        qm = jnp.concatenate([jnp.where(m, q8, 0.0) for m in masks8], axis=0)
        sc = jnp.dot(qm.astype(bf16), kc.astype(bf16), preferred_element_type=f32) + bc_ref[...]
        m = jnp.max(sc, axis=-1, keepdims=True)
        sn = []
        for t in range(n_new):
            col = jnp.sum(qm * kn8[t:t + 1, :], axis=-1, keepdims=True) + bn_ref[:, t:t + 1]
            sn.append(col)
            m = jnp.maximum(m, col)
        pc = jnp.exp(sc - m)
        den = jnp.sum(pc, axis=-1, keepdims=True)
        acc = lax.dot_general(pc.astype(bf16), vc.astype(bf16), (((1,), (1,)), ((), ())),
                              preferred_element_type=f32)
        for t in range(n_new):
            pn = jnp.exp(sn[t] - m)
            den = den + pn
            acc = acc + pn * vn8[t:t + 1, :]
        acc = acc / den
        lse = m + jnp.log(den)
        o = jnp.zeros((8, GROUP_W), f32)
        l = jnp.zeros((8, GROUP_W), f32)
        for h, msk in enumerate(masks8):
            o = jnp.where(msk, acc[h * 8:(h + 1) * 8], o)
            l = jnp.where(msk, lse[h * 8:(h + 1) * 8], l)
        o_ref[bb] = o
        l_ref[bb] = l
        for c_ref, n8, out_ref in ((kc, kn8, ko_ref), (vc, vn8, vo_ref)):
            rolled = pltpu.roll(c_ref, win - n_new, axis=1)
            out_ref[bb] = rolled
            new_t = jnp.concatenate([n8, jnp.zeros((LANES - 8, GROUP_W), f32)], axis=0).T
            new_t = pltpu.roll(new_t, LANES - n_new, axis=1)
            out_ref[bb, :, win - LANES:win] = jnp.where(lane_t >= LANES - n_new, new_t, rolled[:, win - LANES:win])


def _cached_bias(slopes_g, dil, win, n_new):
    s = jnp.arange(8)[:, None]
    i = jnp.arange(win)[None, :]
    dist = win + s - i
    ok = (dist % dil == 0) & (dist // dil <= BAND) & (s < n_new)
    bc = jnp.where(ok[None], -slopes_g[:, None, None] * dist[None].astype(f32), NEG)
    t = jnp.arange(8)[None, :]
    dn = s - t
    okn = (dn >= 0) & (dn % dil == 0) & (dn // dil <= BAND) & (s < n_new) & (t < n_new)
    bn = jnp.where(okn[None], -slopes_g[:, None, None] * dn[None].astype(f32), NEG)
    pad = (s >= n_new)
    bc = jnp.where(pad[None], 0.0, bc)
    bn = jnp.where(pad[None], 0.0, bn)
    return bc.reshape(HEADS_PER_GROUP * 8, win), bn.reshape(HEADS_PER_GROUP * 8, 8)


def _cached_attention(q8, kn8, vn8, kc_t, vc_t, g, slopes_g, *, n_new):
    Bd, _, win = kc_t.shape
    dil = DILATIONS[g]
    assert win == WINDOWS[g] and win % LANES == 0
    nb = max(1, min(Bd, 2048 // win))
    assert Bd % nb == 0
    bc, bn = _cached_bias(slopes_g, dil, win, n_new)
    small = pl.BlockSpec((nb, 8, GROUP_W), lambda b: (b, 0, 0))
    cache = pl.BlockSpec((nb, GROUP_W, win), lambda b: (b, 0, 0))
    return pl.pallas_call(
        functools.partial(_cached_attn_kernel, nb=nb, n_new=n_new, win=win),
        grid=(Bd // nb,),
        in_specs=[small, small, small, cache, cache,
                  pl.BlockSpec(bc.shape, lambda b: (0, 0)), pl.BlockSpec(bn.shape, lambda b: (0, 0))],
        out_specs=[small, small, cache, cache],
        out_shape=[jax.ShapeDtypeStruct((Bd, 8, GROUP_W), f32)] * 2
        + [jax.ShapeDtypeStruct((Bd, GROUP_W, win), f32)] * 2,
        compiler_params=_cparams(("arbitrary",)),
        name="cached_attn_g%d" % g,
    )(q8, kn8, vn8, kc_t, vc_t, bc, bn)


def _merge_kernel(x_ref, o0_ref, o1_ref, o2_ref, l0_ref, l1_ref, l2_ref, po_ref,
                  ln1_ref, wg_ref, wpa_ref, wpb_ref, wo_ref, ln2_ref, wr_ref, br_ref, tri_ref,
                  h_ref, xn_ref, ri_ref, wc_ref, cnt_ref, carry, ibuf, *, tm, dils):
    step = pl.program_id(0) * pl.num_programs(1) + pl.program_id(1)

    @pl.when(step == 0)
    def _():
        carry[...] = jnp.zeros_like(carry)

    def token_order(ref, dil):
        if dil == 1:
            return ref[0, 0].astype(f32)
        chunks = []
        for c in range(GROUP_W // LANES):
            for r in range(dil):
                ibuf[pl.ds(r, tm // dil, stride=dil), :] = ref[0, r, :, c * LANES:(c + 1) * LANES].astype(f32)
            chunks.append(ibuf[...])
        return jnp.concatenate(chunks, axis=-1)

    x = x_ref[0]
    xn = _rmsnorm_rows(x, ln1_ref[...]).astype(bf16)
    gates = jnp.dot(xn, wg_ref[...], preferred_element_type=f32)
    l0, l1, l2 = (token_order(r, d) for r, d in zip((l0_ref, l1_ref, l2_ref), dils))
    lm = jnp.maximum(jnp.maximum(l0, l1), l2)
    e0, e1, e2 = jnp.exp(l0 - lm), jnp.exp(l1 - lm), jnp.exp(l2 - lm)
    o0, o1, o2 = (token_order(r, d) for r, d in zip((o0_ref, o1_ref, o2_ref), dils))
    attn = (e0 * o0 + e1 * o1 + e2 * o2) / (e0 + e1 + e2)
    ma = jnp.dot(attn.astype(bf16), wpa_ref[...], preferred_element_type=f32)
    mb = jnp.dot(po_ref[0], wpb_ref[...], preferred_element_type=f32)
    mix = jax.nn.sigmoid(gates[:, :D_MODEL]) * ma + jax.nn.sigmoid(gates[:, D_MODEL:]) * mb
    h = x + jnp.dot(mix.astype(bf16), wo_ref[...], preferred_element_type=f32)
    h_ref[...] = h
    xn2 = _rmsnorm_rows(h, ln2_ref[...])
    xn_ref[...] = xn2

    lt = (jnp.dot(xn2.astype(bf16), wr_ref[...], preferred_element_type=f32) + br_ref[...]).T
    row8 = lax.broadcasted_iota(jnp.int32, (8, tm), 0)
    gl = jnp.where(row8 < N_EXPERT_GROUPS, lt[0:8], -jnp.inf)
    gmax = jnp.max(gl, axis=0, keepdims=True)
    gidx = jnp.min(jnp.where(gl == gmax, row8, 8), axis=0, keepdims=True)
    pg = 1.0 / jnp.sum(jnp.exp(gl - gmax), axis=0, keepdims=True)
    sel = jnp.zeros((8, tm), f32)
    for g in range(N_EXPERT_GROUPS):
        lo = EXPERT_COL0 + g * EXPERTS_PER_GROUP
        sel = jnp.where(gidx == g, lt[lo:lo + EXPERTS_PER_GROUP], sel)
    v0 = jnp.max(sel, axis=0, keepdims=True)
    i0 = jnp.min(jnp.where(sel == v0, row8, 8), axis=0, keepdims=True)
    sel2 = jnp.where(row8 == i0, -jnp.inf, sel)
    v1 = jnp.max(sel2, axis=0, keepdims=True)
    i1 = jnp.min(jnp.where(sel2 == v1, row8, 8), axis=0, keepdims=True)
    t = jnp.exp(v1 - v0)
    w0 = pg / (1.0 + t)
    w1 = pg * t / (1.0 + t)
    eid0 = gidx * EXPERTS_PER_GROUP + i0
    eid1 = gidx * EXPERTS_PER_GROUP + i1
    erow = lax.broadcasted_iota(jnp.int32, (N_EXPERTS, tm), 0)
    oh0 = erow == eid0
    oh1 = erow == eid1
    cnt = jnp.where(oh0 | oh1, 1.0, 0.0)
    before = jnp.dot(cnt.astype(bf16), tri_ref[...], preferred_element_type=f32) + carry[:, 0:1]
    rank0 = jnp.sum(jnp.where(oh0, before, 0.0), axis=0, keepdims=True)
    rank1 = jnp.sum(jnp.where(oh1, before, 0.0), axis=0, keepdims=True)
    carry[...] = carry[...] + jnp.sum(cnt, axis=1, keepdims=True)
    cnt_ref[...] = carry[...]
    ri = jnp.zeros((8, tm), jnp.int32)
    for r, val in enumerate((eid0, eid1, rank0.astype(jnp.int32), rank1.astype(jnp.int32))):
        ri = jnp.where(row8 == r, val, ri)
    ri_ref[...] = ri
    rowl = lax.broadcasted_iota(jnp.int32, (LANES, tm), 0)
    wslab = jnp.where(rowl == 0, w0, jnp.where(rowl == 1, w1, 0.0))
    wc_ref[...] = wslab.T


def _merge(x, o, l, po, ln1, w_gates, w_pa, w_pb, w_o, ln2, w_router, b_router, *, tm):
    B, T, _ = x.shape
    assert T % tm == 0
    nt = T // tm
    n = B * T
    dils = tuple(a.shape[1] for a in o)
    assert all(tm % (8 * d) == 0 for d in dils)
    tri = (jnp.arange(tm)[:, None] < jnp.arange(tm)[None, :]).astype(bf16)
    rows3 = lambda w: pl.BlockSpec((1, tm, w), lambda b, i: (b, i, 0))
    flat = lambda w: pl.BlockSpec((tm, w), lambda b, i: (b * nt + i, 0))
    grp = [pl.BlockSpec((1, d, tm // d, GROUP_W), lambda b, i: (b, 0, i, 0)) for d in dils]
    full = lambda a: pl.BlockSpec(a.shape, lambda b, i: (0,) * a.ndim)
    weights = (ln1, w_gates, w_pa, w_pb, w_o, ln2, w_router, b_router, tri)
    return pl.pallas_call(
        functools.partial(_merge_kernel, tm=tm, dils=dils),
        grid=(B, nt),
        in_specs=[rows3(D_MODEL)] + grp + grp + [rows3(POOL_W)] + [full(a) for a in weights],
        out_specs=[flat(D_MODEL), flat(D_MODEL),
                   pl.BlockSpec((8, tm), lambda b, i: (0, b * nt + i)),
                   flat(LANES),
                   pl.BlockSpec((N_EXPERTS, LANES), lambda b, i: (0, 0))],
        out_shape=[jax.ShapeDtypeStruct((n, D_MODEL), f32), jax.ShapeDtypeStruct((n, D_MODEL), f32),
                   jax.ShapeDtypeStruct((8, n), jnp.int32),
                   jax.ShapeDtypeStruct((n, LANES), f32),
                   jax.ShapeDtypeStruct((N_EXPERTS, LANES), f32)],
        scratch_shapes=[pltpu.VMEM((N_EXPERTS, LANES), f32), pltpu.VMEM((tm, LANES), f32)],
        compiler_params=_cparams(("arbitrary", "arbitrary")),
        name="merge_router",
    )(x, *o, *l, po, *weights)


def _dest_kernel(ri_ref, ps_ref, dest_ref):
    tn = ri_ref.shape[1]
    erow = lax.broadcasted_iota(jnp.int32, (N_EXPERTS, tn), 0)
    ps = ps_ref[:, 0:1]
    row8 = lax.broadcasted_iota(jnp.int32, (8, tn), 0)
    out = jnp.zeros((8, tn), jnp.int32)
    for k in range(2):
        start = jnp.sum(jnp.where(erow == ri_ref[k:k + 1, :], ps, 0.0), axis=0, keepdims=True)
        out = jnp.where(row8 == k, start.astype(jnp.int32) + ri_ref[2 + k:3 + k, :], out)
    dest_ref[...] = out


def _dest(ri, pad_starts, *, tn):
    n = ri.shape[1]
    ps = jnp.broadcast_to(pad_starts.astype(f32)[:, None], (N_EXPERTS, LANES))
    return pl.pallas_call(
        _dest_kernel,
        grid=(n // tn,),
        in_specs=[pl.BlockSpec((8, tn), lambda i: (0, i)), pl.BlockSpec((N_EXPERTS, LANES), lambda i: (0, 0))],
        out_specs=pl.BlockSpec((8, tn), lambda i: (0, i)),
        out_shape=jax.ShapeDtypeStruct((8, n), jnp.int32),
        compiler_params=_cparams(("arbitrary",)),
        name="moe_dest",
    )(ri, ps)


def _dispatch_kernel(meta_ref, dest_ref, x_ref, xs_hbm, zblk, sem, zsem, *, tm, blk, n_blocks):
    i = pl.program_id(0)

    @pl.when(i == 0)
    def _():
        zblk[...] = jnp.zeros_like(zblk)

        def zero_copy(r):
            return pltpu.make_async_copy(zblk.at[pl.ds(0, 1)], xs_hbm.at[pl.ds(r, 1)], zsem)

        def tail_copy(b):
            return pltpu.make_async_copy(zblk, xs_hbm.at[pl.ds(pl.multiple_of(b * blk, blk), blk)], zsem)

        last = N_EXPERTS - 1
        first_tail = (meta_ref[last] + meta_ref[2 * N_EXPERTS + last]) // blk

        def tail_start(b, c):
            tail_copy(b).start()
            return c

        def tail_wait(b, c):
            tail_copy(b).wait()
            return c

        lax.fori_loop(first_tail, n_blocks, tail_start, 0)
        lax.fori_loop(first_tail, n_blocks, tail_wait, 0)

        def per_expert(e, c):
            lo = meta_ref[e] + meta_ref[N_EXPERTS + e]
            hi = meta_ref[e] + meta_ref[2 * N_EXPERTS + e]

            def start(r, c2):
                zero_copy(r).start()
                return c2

            def wait(r, c2):
                zero_copy(r).wait()
                return c2

            lax.fori_loop(lo, hi, start, 0)
            lax.fori_loop(lo, hi, wait, 0)
            return c

        lax.fori_loop(0, N_EXPERTS, per_expert, 0)

    def issue(t, c):
        for k in range(2):
            pltpu.make_async_copy(x_ref.at[pl.ds(t, 1)], xs_hbm.at[pl.ds(dest_ref[k, t], 1)], sem).start()
        return c

    lax.fori_loop(0, tm, issue, 0, unroll=8)
    for k in range(2):
        pltpu.make_async_copy(x_ref, xs_hbm.at[pl.ds(0, tm)], sem).wait()


def _dispatch(xn2, dest, meta, *, tm, blk, n_blocks):
    n = xn2.shape[0]
    n_rows = n_blocks * blk
    return pl.pallas_call(
        functools.partial(_dispatch_kernel, tm=tm, blk=blk, n_blocks=n_blocks),
        grid_spec=pltpu.PrefetchScalarGridSpec(
            num_scalar_prefetch=1,
            grid=(n // tm,),
            in_specs=[pl.BlockSpec((8, tm), lambda i, m: (0, i), memory_space=pltpu.SMEM),
                      pl.BlockSpec((tm, D_MODEL), lambda i, m: (i, 0))],
            out_specs=pl.BlockSpec(memory_space=pl.ANY),
            scratch_shapes=[pltpu.VMEM((blk, D_MODEL), f32), pltpu.SemaphoreType.DMA, pltpu.SemaphoreType.DMA],
        ),
        out_shape=jax.ShapeDtypeStruct((n_rows, D_MODEL), f32),
        compiler_params=_cparams(("arbitrary",)),
        name="moe_dispatch",
    )(meta, dest, xn2)


def _moe_kernel(blk_e_ref, nvalid_ref, x_ref, wg_ref, wu_ref, wd_ref, y_ref):
    b = pl.program_id(0)

    @pl.when(nvalid_ref[b] > 0)
    def _():
        x = x_ref[...].astype(bf16)
        hid = jax.nn.silu(jnp.dot(x, wg_ref[0], preferred_element_type=f32)) * jnp.dot(x, wu_ref[0], preferred_element_type=f32)
        y_ref[...] = jnp.dot(hid.astype(bf16), wd_ref[0], preferred_element_type=f32)

    @pl.when(nvalid_ref[b] == 0)
    def _():
        y_ref[...] = jnp.zeros_like(y_ref)


def _moe(xs, blk_e, nvalid, w_gate, w_up, w_down, *, blk):
    n_blocks = blk_e.shape[0]
    wspec = lambda shape: pl.BlockSpec((1,) + shape, lambda b, be, nv: (be[b], 0, 0))
    rows = pl.BlockSpec((blk, D_MODEL), lambda b, be, nv: (b, 0))
    return pl.pallas_call(
        _moe_kernel,
        grid_spec=pltpu.PrefetchScalarGridSpec(
            num_scalar_prefetch=2,
            grid=(n_blocks,),
            in_specs=[rows, wspec((D_MODEL, D_EXPERT)), wspec((D_MODEL, D_EXPERT)), wspec((D_EXPERT, D_MODEL))],
            out_specs=rows,
        ),
        out_shape=jax.ShapeDtypeStruct((n_blocks * blk, D_MODEL), f32),
        compiler_params=_cparams(("arbitrary",)),
        name="moe_experts",
    )(blk_e, nvalid, xs, w_gate, w_up, w_down)


def _combine_kernel(dest_ref, h_ref, wc_ref, ys_hbm, y_ref, gbuf, sem, *, tm):
    def issue(t, c):
        for k in range(2):
            pltpu.make_async_copy(ys_hbm.at[pl.ds(dest_ref[k, t], 1)], gbuf.at[k, pl.ds(t, 1)], sem).start()
        return c

    lax.fori_loop(0, tm, issue, 0, unroll=8)
    for k in range(2):
        pltpu.make_async_copy(ys_hbm.at[pl.ds(0, tm)], gbuf.at[k], sem).wait()
    w = wc_ref[...]
    y_ref[...] = h_ref[...] + w[:, 0:1] * gbuf[0] + w[:, 1:2] * gbuf[1]


def _combine(h, ys, dest, wc, *, tm):
    n = h.shape[0]
    rows = lambda w: pl.BlockSpec((tm, w), lambda i: (i, 0))
    return pl.pallas_call(
        functools.partial(_combine_kernel, tm=tm),
        grid=(n // tm,),
        in_specs=[pl.BlockSpec((8, tm), lambda i: (0, i), memory_space=pltpu.SMEM),
                  rows(D_MODEL), rows(LANES), pl.BlockSpec(memory_space=pl.ANY)],
        out_specs=rows(D_MODEL),
        out_shape=jax.ShapeDtypeStruct((n, D_MODEL), f32),
        scratch_shapes=[pltpu.VMEM((2, tm, D_MODEL), f32), pltpu.SemaphoreType.DMA],
        compiler_params=_cparams(("arbitrary",)),
        name="moe_combine",
    )(dest, h, wc, ys)


def _mix_and_moe(x, o, l, po, wts, *, tm, blk):
    n = x.shape[0] * x.shape[1]
    h, xn2, ri, wc, cnt = _merge(x, o, l, po, wts['ln1'], wts['w_gates'], wts['w_pa'], wts['w_pb'], wts['w_o'],
                                 wts['ln2'], wts['w_router'], wts['b_router'], tm=tm)
    n_blocks = (2 * n + N_EXPERTS * (blk - 1) + blk - 1) // blk
    counts = cnt[:, 0].astype(jnp.int32)
    padded = (counts + blk - 1) // blk * blk
    pad_ends = jnp.cumsum(padded)
    pad_starts = pad_ends - padded
    blk_start = jnp.arange(n_blocks, dtype=jnp.int32) * blk
    blk_e = jnp.minimum(jnp.sum(pad_ends[None, :] <= blk_start[:, None], axis=1), N_EXPERTS - 1).astype(jnp.int32)
    nvalid = jnp.clip(pad_starts[blk_e] + counts[blk_e] - blk_start, 0, blk).astype(jnp.int32)
    meta = jnp.concatenate([pad_starts, counts, padded]).astype(jnp.int32)
    dest = _dest(ri, pad_starts, tn=tm)
    xs = _dispatch(xn2, dest, meta, tm=tm, blk=blk, n_blocks=n_blocks)
    ys = _moe(xs, blk_e, nvalid, wts['w_gate'], wts['w_up'], wts['w_down'], blk=blk)
    return _combine(h, ys, dest, wc, tm=tm)


def kernel(x_prompt, x_sample, cache_k_w128, cache_v_w128, cache_k_w512, cache_v_w512, cache_k_w2048, cache_v_w2048, state_pool, ln1, w_in, q_gain, k_gain, pool_lin, pool_scale, w_pa, w_pb, w_o, ln2, w_rg, b_rg, w_re, b_re, w_gate, w_up, w_down):
    B, T, D = x_prompt.shape
    Bd, S, _ = x_sample.shape
    past_len = 8192
    caches = ((cache_k_w128, cache_v_w128), (cache_k_w512, cache_v_w512), (cache_k_w2048, cache_v_w2048))
    slopes = jnp.exp2(-8.0 * jnp.arange(1, N_HEADS + 1, dtype=f32) / N_HEADS).reshape(N_GROUPS, HEADS_PER_GROUP)

    w_qkvu = w_in[:, :QKVU_W].astype(bf16)
    plin_bd = jnp.zeros((POOL_W, POOL_W), f32)
    for g in range(len(POOL_WINDOWS)):
        plin_bd = plin_bd.at[g * POOL_GW:(g + 1) * POOL_GW, g * POOL_GW:(g + 1) * POOL_GW].set(pool_lin[g])
    w_router = jnp.zeros((D, ROUTER_W), f32).at[:, :N_EXPERT_GROUPS].set(w_rg)
    w_router = w_router.at[:, EXPERT_COL0:EXPERT_COL0 + N_EXPERTS].set(w_re)
    b_router = jnp.zeros((1, ROUTER_W), f32).at[0, :N_EXPERT_GROUPS].set(b_rg)
    b_router = b_router.at[0, EXPERT_COL0:EXPERT_COL0 + N_EXPERTS].set(b_re)
    wts = dict(ln1=ln1.reshape(1, D), w_gates=w_in[:, QKVU_W:].astype(bf16), w_pa=w_pa.astype(bf16),
               w_pb=w_pb.astype(bf16), w_o=w_o.astype(bf16), ln2=ln2.reshape(1, D),
               w_router=w_router.astype(bf16), b_router=b_router,
               w_gate=w_gate.astype(bf16), w_up=w_up.astype(bf16), w_down=w_down.astype(bf16))
    proj_w = (wts['ln1'], w_qkvu, q_gain.reshape(1, ATTN_W), k_gain.reshape(1, ATTN_W),
              plin_bd.astype(bf16), pool_scale.reshape(1, POOL_W))

    *qkv, po, kt, vt, ut = _proj_prompt(x_prompt, *proj_w, tm=512)
    o, l = zip(*[_band_attention(*qkv[3 * g:3 * g + 3], g, slopes[g]) for g in range(N_GROUPS)])
    y_prompt = _mix_and_moe(x_prompt, o, l, po, wts, tm=512, blk=512).reshape(B, T, D)
    tail = kt.shape[2]
    pkv = []
    for g, w in enumerate(WINDOWS):
        for a in (kt, vt):
            a = a.reshape(B, N_HEADS, HEAD_DIM, tail)[:, g * HEADS_PER_GROUP:(g + 1) * HEADS_PER_GROUP, :, tail - w:]
            pkv.append(a.transpose(0, 3, 1, 2))
    p_pool = ut[:, 1:]

    n_s = Bd * S
    xs = x_sample.transpose(1, 0, 2).reshape(n_s, D)
    qs, ks, vs, pos, st = _proj_sample(xs, *proj_w, state_pool.transpose(1, 0, 2),
                                       n_seq=Bd, n_new=S, past_len=past_len)
    so, sl, skv = [], [], []
    pad8 = lambda a: jnp.pad(a.reshape(S, Bd, GROUP_W).transpose(1, 0, 2), ((0, 0), (0, 8 - S), (0, 0)))
    for g, w in enumerate(WINDOWS):
        cols = slice(g * GROUP_W, (g + 1) * GROUP_W)
        kc, vc = caches[g]
        og, lg, ko, vo = _cached_attention(
            pad8(qs[:, cols]), pad8(ks[:, cols]), pad8(vs[:, cols]),
            kc.transpose(0, 2, 3, 1).reshape(Bd, GROUP_W, w), vc.transpose(0, 2, 3, 1).reshape(Bd, GROUP_W, w),
            g, slopes[g], n_new=S)
        so.append(og[:, :S].transpose(1, 0, 2).reshape(1, 1, n_s, GROUP_W).astype(bf16))
        sl.append(lg[:, :S].transpose(1, 0, 2).reshape(1, 1, n_s, GROUP_W))
        for a in (ko, vo):
            skv.append(a.reshape(Bd, HEADS_PER_GROUP, HEAD_DIM, w).transpose(0, 3, 1, 2))
    y_sample = _mix_and_moe(xs[None], so, sl, pos[None], wts, tm=n_s, blk=128).reshape(S, Bd, D).transpose(1, 0, 2)
    s_pool = st.transpose(1, 0, 2)

    return (y_prompt, y_sample, *pkv, p_pool, *skv, s_pool)
```

```python
import functools

import jax
import jax.numpy as jnp
from jax import lax
from jax.experimental import pallas as pl
from jax.experimental.pallas import tpu as pltpu

D_MODEL = 1024
HEAD_DIM = 64
HEADS_PER_GROUP = 4
WINDOWS = (128, 512, 2048)
DILATIONS = (1, 4, 16)
N_GROUPS = len(WINDOWS)
N_HEADS = HEADS_PER_GROUP * N_GROUPS
ATTN_W = N_HEADS * HEAD_DIM
GROUP_W = HEADS_PER_GROUP * HEAD_DIM
BAND = 128
POOL_WINDOWS = (2, 4, 8, 16)
POOL_GW = 128
POOL_W = len(POOL_WINDOWS) * POOL_GW
POOL_STATE = max(POOL_WINDOWS) - 1
POOL_HIST = 32
assert POOL_WINDOWS == (2, 4, 8, 16)
N_EXPERT_GROUPS = 4
EXPERTS_PER_GROUP = 8
N_EXPERTS = N_EXPERT_GROUPS * EXPERTS_PER_GROUP
D_EXPERT = 512
QKVU_W = 3 * ATTN_W + POOL_W
NEG = -1e30
EPS = 1e-6
LANES = 128
ROUTER_W = LANES
EXPERT_COL0 = 8
VMEM_LIMIT = 56 * 1024 * 1024
IBUF_SLOTS = 2 * (N_GROUPS - 1) * (GROUP_W // LANES)

assert all(w // d == BAND for w, d in zip(WINDOWS, DILATIONS))

f32 = jnp.float32
bf16 = jnp.bfloat16


def _cparams(sem):
    return pltpu.CompilerParams(dimension_semantics=sem, vmem_limit_bytes=VMEM_LIMIT)


def _rmsnorm_rows(x, g):
    ms = jnp.mean(x * x, axis=-1, keepdims=True)
    return x * lax.rsqrt(ms + EPS) * g


def _head_rmsnorm_chunk(ch, gain):
    lane = lax.broadcasted_iota(jnp.int32, ch.shape, 1)
    lo_mask = lane < HEAD_DIM
    sq = ch * ch
    lo = jnp.sum(jnp.where(lo_mask, sq, 0.0), axis=-1, keepdims=True)
    hi = jnp.sum(jnp.where(lo_mask, 0.0, sq), axis=-1, keepdims=True)
    ss = jnp.where(lo_mask, lo, hi)
    return ch * lax.rsqrt(ss * (1.0 / HEAD_DIM) + EPS) * gain


def _project(x, ln1, w_ref, qg, kg):
    xn = _rmsnorm_rows(x, ln1).astype(bf16)
    nch = ATTN_W // LANES
    qs = qg * (HEAD_DIM ** -0.5)
    zq = jnp.dot(xn, w_ref[:, 0:ATTN_W], preferred_element_type=f32)
    q = [_head_rmsnorm_chunk(zq[:, c * LANES:(c + 1) * LANES], qs[:, c * LANES:(c + 1) * LANES]) for c in range(nch)]
    zk = jnp.dot(xn, w_ref[:, ATTN_W:2 * ATTN_W], preferred_element_type=f32)
    k = [_head_rmsnorm_chunk(zk[:, c * LANES:(c + 1) * LANES], kg[:, c * LANES:(c + 1) * LANES]) for c in range(nch)]
    zv = jnp.dot(xn, w_ref[:, 2 * ATTN_W:3 * ATTN_W], preferred_element_type=f32)
    v = [zv[:, c * LANES:(c + 1) * LANES] for c in range(nch)]
    u = jnp.dot(xn, w_ref[:, 3 * ATTN_W:3 * ATTN_W + POOL_W], preferred_element_type=f32)
    return q, k, v, u


def _proj_prompt_kernel(x_ref, ln1_ref, w_ref, qg_ref, kg_ref, plin_ref, pscale_ref, *rest,
                        tm, n_tiles, tail_tiles):
    qkv_refs = rest[:3 * N_GROUPS]
    po_ref, kt_ref, vt_ref, ut_ref, ubuf, pa, pb, sbuf = rest[3 * N_GROUPS:]
    i = pl.program_id(1)
    hist = POOL_STATE + 1
    ph, rows = POOL_HIST, POOL_HIST + tm

    @pl.when(i == 0)
    def _():
        ubuf[0:ph, :] = jnp.zeros((ph, POOL_W), f32)

    @pl.when(i > 0)
    def _():
        ubuf[0:ph, :] = ubuf[tm:tm + ph, :]

    q, k, v, u = _project(x_ref[0], ln1_ref[...], w_ref, qg_ref[...], kg_ref[...])
    cpg = GROUP_W // LANES
    slot = 0
    for t, chunks in enumerate((q, k, v)):
        for g, dil in enumerate(DILATIONS):
            out_ref = qkv_refs[3 * g + t]
            for c in range(cpg):
                val = chunks[g * cpg + c]
                cols = slice(c * LANES, (c + 1) * LANES)
                if dil == 1:
                    out_ref[0, 0, :, cols] = val.astype(bf16)
                else:
                    sbuf[slot * tm:(slot + 1) * tm, :] = val
                    for r in range(dil):
                        out_ref[0, r, :, cols] = sbuf[pl.ds(slot * tm + r, tm // dil, stride=dil), :].astype(bf16)
                    slot += 1

    ubuf[ph:rows, :] = u
    gw = POOL_GW
    pa[8:rows, :] = ubuf[8:rows, :] + ubuf[7:rows - 1, :]
    pb[16:rows, gw:] = pa[16:rows, gw:] + pa[14:rows - 2, gw:]
    pa[24:rows, 2 * gw:] = pb[24:rows, 2 * gw:] + pb[20:rows - 4, 2 * gw:]
    pb[32:rows, 3 * gw:] = pa[32:rows, 3 * gw:] + pa[24:rows - 8, 3 * gw:]
    pos = i * tm + lax.broadcasted_iota(jnp.int32, (tm, POOL_GW), 0)
    zs = []
    for g, w in enumerate(POOL_WINDOWS):
        cols = slice(g * POOL_GW, (g + 1) * POOL_GW)
        wsum = (pa if g % 2 == 0 else pb)[ph:rows, cols]
        cnt = jnp.minimum(pos + 1, w).astype(f32)
        zs.append(wsum / cnt - u[:, cols])
    z = jnp.concatenate(zs, axis=-1).astype(bf16)
    po = jnp.dot(z, plin_ref[...], preferred_element_type=f32) * pscale_ref[...]
    po_ref[0] = po.astype(bf16)

    @pl.when(i >= n_tiles - tail_tiles)
    def _():
        kt_ref[0] = jnp.concatenate(k, axis=-1).T
        vt_ref[0] = jnp.concatenate(v, axis=-1).T

    @pl.when(i == n_tiles - 1)
    def _():
        ut_ref[0] = ubuf[rows - hist:rows, :]


def _proj_prompt(x, ln1, w_qkvu, qg, kg, plin_bd, pscale, *, tm):
    B, T, D = x.shape
    n_tiles = T // tm
    tail = max(WINDOWS)
    assert T % tm == 0 and tail % tm == 0 and T >= tail
    tail_tiles = tail // tm
    hist = POOL_STATE + 1
    kern = functools.partial(_proj_prompt_kernel, tm=tm, n_tiles=n_tiles, tail_tiles=tail_tiles)
    const = lambda b, i: (0, 0)
    assert all(tm % (16 * d) == 0 for d in DILATIONS)
    qkv_specs = [pl.BlockSpec((1, d, tm // d, GROUP_W), lambda b, i: (b, 0, i, 0)) for d in DILATIONS for _ in range(3)]
    qkv_shapes = [jax.ShapeDtypeStruct((B, d, T // d, GROUP_W), bf16) for d in DILATIONS for _ in range(3)]
    tail_spec = pl.BlockSpec((1, ATTN_W, tm), lambda b, i: (b, 0, jnp.maximum(i - (n_tiles - tail_tiles), 0)))
    return pl.pallas_call(
        kern,
        grid=(B, n_tiles),
        in_specs=[
            pl.BlockSpec((1, tm, D), lambda b, i: (b, i, 0)),
            pl.BlockSpec((1, D), const),
            pl.BlockSpec((D, QKVU_W), const),
            pl.BlockSpec((1, ATTN_W), const),
            pl.BlockSpec((1, ATTN_W), const),
            pl.BlockSpec((POOL_W, POOL_W), const),
            pl.BlockSpec((1, POOL_W), const),
        ],
        out_specs=qkv_specs
        + [pl.BlockSpec((1, tm, POOL_W), lambda b, i: (b, i, 0)),
           tail_spec, tail_spec,
           pl.BlockSpec((1, hist, POOL_W), lambda b, i: (b, 0, 0))],
        out_shape=qkv_shapes
        + [jax.ShapeDtypeStruct((B, T, POOL_W), bf16),
           jax.ShapeDtypeStruct((B, ATTN_W, tail), f32),
           jax.ShapeDtypeStruct((B, ATTN_W, tail), f32),
           jax.ShapeDtypeStruct((B, hist, POOL_W), f32)],
        scratch_shapes=[pltpu.VMEM((POOL_HIST + tm, POOL_W), f32)] * 3
                       + [pltpu.VMEM((3 * (N_GROUPS - 1) * (GROUP_W // LANES) * tm, LANES), f32)],
        compiler_params=_cparams(("arbitrary", "arbitrary")),
        name="proj_prompt",
    )(x, ln1, w_qkvu, qg, kg, plin_bd, pscale)


def _proj_sample_kernel(x_ref, ln1_ref, w_ref, qg_ref, kg_ref, plin_ref, pscale_ref, state_ref,
                        q_ref, k_ref, v_ref, po_ref, st_ref, *, n_seq, n_new, past_len):
    q, k, v, u = _project(x_ref[...], ln1_ref[...], w_ref, qg_ref[...], kg_ref[...])
    q_ref[...] = jnp.concatenate(q, axis=-1)
    k_ref[...] = jnp.concatenate(k, axis=-1)
    v_ref[...] = jnp.concatenate(v, axis=-1)
    ext = [state_ref[j] for j in range(POOL_STATE)] + [u[s * n_seq:(s + 1) * n_seq, :] for s in range(n_new)]
    for s in range(n_new):
        zs = []
        for g, w in enumerate(POOL_WINDOWS):
            cols = slice(g * POOL_GW, (g + 1) * POOL_GW)
            cur = ext[POOL_STATE + s][:, cols]
            acc = cur
            for j in range(1, w):
                acc = acc + ext[POOL_STATE + s - j][:, cols]
            cnt = float(min(past_len + s + 1, w))
            zs.append(acc / cnt - cur)
        z = jnp.concatenate(zs, axis=-1).astype(bf16)
        po = jnp.dot(z, plin_ref[...], preferred_element_type=f32) * pscale_ref[...]
        po_ref[s * n_seq:(s + 1) * n_seq, :] = po.astype(bf16)
    for j in range(POOL_STATE):
        st_ref[j] = ext[j + n_new]


def _proj_sample(x, ln1, w_qkvu, qg, kg, plin_bd, pscale, state, *, n_seq, n_new, past_len):
    n = n_seq * n_new
    kern = functools.partial(_proj_sample_kernel, n_seq=n_seq, n_new=n_new, past_len=past_len)
    return pl.pallas_call(
        kern,
        out_shape=[jax.ShapeDtypeStruct((n, ATTN_W), f32)] * 3
        + [jax.ShapeDtypeStruct((n, POOL_W), bf16),
           jax.ShapeDtypeStruct((POOL_STATE, n_seq, POOL_W), f32)],
        compiler_params=pltpu.CompilerParams(vmem_limit_bytes=VMEM_LIMIT),
        name="proj_sample",
    )(x, ln1, w_qkvu, qg, kg, plin_bd, pscale, state)


def _head_masks(shape):
    lane = lax.broadcasted_iota(jnp.int32, shape, len(shape) - 1)
    return [(lane >= h * HEAD_DIM) & (lane < (h + 1) * HEAD_DIM) for h in range(HEADS_PER_GROUP)]


def _band_attn_kernel(q_ref, kc_ref, kp_ref, vc_ref, vp_ref, bias_ref, o_ref, l_ref, kbuf, vbuf, *, tl, unroll):
    i = pl.program_id(2)
    kbuf[0:BAND, :] = kp_ref[0, 0]
    kbuf[BAND:2 * BAND, :] = kc_ref[0, 0, 0:BAND, :]
    vbuf[0:BAND, :] = vp_ref[0, 0]
    vbuf[BAND:2 * BAND, :] = vc_ref[0, 0, 0:BAND, :]
    masks = _head_masks((BAND, GROUP_W))

    def sub_block(j, kk, vv, var):
        r0 = j * BAND if isinstance(j, int) else pl.multiple_of(j * BAND, BAND)
        q = q_ref[0, 0, pl.ds(r0, BAND), :]
        qm = jnp.concatenate([jnp.where(m, q, jnp.zeros_like(q)) for m in masks], axis=0)
        s = lax.dot_general(qm, kk, (((1,), (1,)), ((), ())), preferred_element_type=f32)
        s = s + bias_ref[var]
        m = jnp.max(s, axis=-1, keepdims=True)
        p = jnp.exp(s - m)
        den = jnp.sum(p, axis=-1, keepdims=True)
        pv = jnp.dot(p.astype(bf16), vv, preferred_element_type=f32)
        o = jnp.zeros((BAND, GROUP_W), f32)
        ms = jnp.zeros((BAND, GROUP_W), f32)
        ds = jnp.ones((BAND, GROUP_W), f32)
        for h, msk in enumerate(masks):
            rows = slice(h * BAND, (h + 1) * BAND)
            o = jnp.where(msk, pv[rows], o)
            ms = jnp.where(msk, m[rows], ms)
            ds = jnp.where(msk, den[rows], ds)
        o_ref[0, 0, pl.ds(r0, BAND), :] = (o / ds).astype(bf16)
        l_ref[0, 0, pl.ds(r0, BAND), :] = ms + jnp.log(ds)

    sub_block(0, kbuf[...], vbuf[...], jnp.where(i == 0, 0, 1))

    def body(j, carry):
        k0 = pl.multiple_of((j - 1) * BAND, BAND)
        sub_block(j, kc_ref[0, 0, pl.ds(k0, 2 * BAND), :], vc_ref[0, 0, pl.ds(k0, 2 * BAND), :], 1)
        return carry

    if tl > BAND:
        lax.fori_loop(1, tl // BAND, body, 0, unroll=unroll)


def _band_bias(slopes_g, dil):
    qi = jnp.arange(BAND)[:, None]
    kb = jnp.arange(2 * BAND)[None, :]
    rel = qi + BAND - kb
    valid = (rel >= 0) & (rel <= BAND)
    alibi = -slopes_g[:, None, None] * (dil * rel)[None].astype(f32)
    variants = []
    for first in (True, False):
        ok = valid & (kb >= BAND) if first else valid
        variants.append(jnp.where(ok[None], alibi, NEG).reshape(HEADS_PER_GROUP * BAND, 2 * BAND))
    return jnp.stack(variants, axis=0)


def _band_attention(q, k, v, g, slopes_g, *, tl_max=1024, unroll=7):
    B, dil, L, _ = q.shape
    tl = min(tl_max, L)
    assert dil == DILATIONS[g] and L % tl == 0 and tl % BAND == 0
    nsub = tl // BAND
    bias = _band_bias(slopes_g, dil)
    cur = pl.BlockSpec((1, 1, tl, GROUP_W), lambda b, r, i: (b, r, i, 0))
    prev = pl.BlockSpec((1, 1, BAND, GROUP_W), lambda b, r, i: (b, r, jnp.maximum(i * nsub - 1, 0), 0))
    return pl.pallas_call(
        functools.partial(_band_attn_kernel, tl=tl, unroll=max(1, min(unroll, nsub - 1))),
        grid=(B, dil, L // tl),
        in_specs=[cur, cur, prev, cur, prev,
                  pl.BlockSpec((2, HEADS_PER_GROUP * BAND, 2 * BAND), lambda b, r, i: (0, 0, 0))],
        out_specs=[cur, cur],
        out_shape=[jax.ShapeDtypeStruct((B, dil, L, GROUP_W), bf16),
                   jax.ShapeDtypeStruct((B, dil, L, GROUP_W), f32)],
        scratch_shapes=[pltpu.VMEM((2 * BAND, GROUP_W), bf16), pltpu.VMEM((2 * BAND, GROUP_W), bf16)],
        compiler_params=_cparams(("arbitrary", "arbitrary", "arbitrary")),
        name="band_attn_g%d" % g,
    )(q, k, k, v, v, bias)


def _cached_attn_kernel(q_ref, kn_ref, vn_ref, knt_ref, vnt_ref, kc_ref, vc_ref, bc_ref, bn_ref,
                        o_ref, l_ref, ko_ref, vo_ref, *, nb, n_new, win):
    masks8 = _head_masks((nb, 8, GROUP_W))
    q8, kn8, vn8 = q_ref[...], kn_ref[...], vn_ref[...]
    kc, vc = kc_ref[...], vc_ref[...]
    qm = jnp.concatenate([jnp.where(m, q8, 0.0) for m in masks8], axis=1)
    sc = jnp.einsum('bqd,bdk->bqk', qm.astype(bf16), kc.astype(bf16), preferred_element_type=f32) + bc_ref[...]
    m = jnp.max(sc, axis=-1, keepdims=True)
    sn = []
    for t in range(n_new):
        col = jnp.sum(qm * kn8[:, t:t + 1, :], axis=-1, keepdims=True) + bn_ref[:, t:t + 1]
        sn.append(col)
        m = jnp.maximum(m, col)
    pc = jnp.exp(sc - m)
    den = jnp.sum(pc, axis=-1, keepdims=True)
    acc = jnp.einsum('bqk,bdk->bqd', pc.astype(bf16), vc.astype(bf16), preferred_element_type=f32)
    for t in range(n_new):
        pn = jnp.exp(sn[t] - m)
        den = den + pn
        acc = acc + pn * vn8[:, t:t + 1, :]
    acc = acc / den
    lse = m + jnp.log(den)
    o = jnp.zeros((nb, 8, GROUP_W), f32)
    l = jnp.zeros((nb, 8, GROUP_W), f32)
    for h, msk in enumerate(masks8):
        o = jnp.where(msk, acc[:, h * 8:(h + 1) * 8, :], o)
        l = jnp.where(msk, lse[:, h * 8:(h + 1) * 8, :], l)
    o_ref[...] = o
    l_ref[...] = l
    lane_t = lax.broadcasted_iota(jnp.int32, (nb, GROUP_W, LANES), 2)
    for cache, new_t_ref, out_ref in ((kc, knt_ref, ko_ref), (vc, vnt_ref, vo_ref)):
        rolled = pltpu.roll(cache, win - n_new, axis=2)
        out_ref[...] = rolled
        tail = rolled[:, :, win - LANES:win]
        new_t = new_t_ref[...]
        for t in range(n_new):
            tail = jnp.where(lane_t == LANES - n_new + t, new_t[:, :, t:t + 1], tail)
        out_ref[:, :, win - LANES:win] = tail


def _cached_bias(slopes_g, dil, win, n_new):
    s = jnp.arange(8)[:, None]
    i = jnp.arange(win)[None, :]
    dist = win + s - i
    ok = (dist % dil == 0) & (dist // dil <= BAND) & (s < n_new)
    bc = jnp.where(ok[None], -slopes_g[:, None, None] * dist[None].astype(f32), NEG)
    t = jnp.arange(8)[None, :]
    dn = s - t
    okn = (dn >= 0) & (dn % dil == 0) & (dn // dil <= BAND) & (s < n_new) & (t < n_new)
    bn = jnp.where(okn[None], -slopes_g[:, None, None] * dn[None].astype(f32), NEG)
    pad = (s >= n_new)
    bc = jnp.where(pad[None], 0.0, bc)
    bn = jnp.where(pad[None], 0.0, bn)
    return bc.reshape(HEADS_PER_GROUP * 8, win), bn.reshape(HEADS_PER_GROUP * 8, 8)


def _cached_attention(q8, kn8, vn8, knt, vnt, kc_t, vc_t, g, slopes_g, *, n_new):
    Bd, _, win = kc_t.shape
    dil = DILATIONS[g]
    assert win == WINDOWS[g] and win % LANES == 0
    nb = max(1, min(Bd, 2048 // win))
    assert Bd % nb == 0
    bc, bn = _cached_bias(slopes_g, dil, win, n_new)
    small = pl.BlockSpec((nb, 8, GROUP_W), lambda b: (b, 0, 0))
    small_t = pl.BlockSpec((nb, GROUP_W, 8), lambda b: (b, 0, 0))
    cache = pl.BlockSpec((nb, GROUP_W, win), lambda b: (b, 0, 0))
    return pl.pallas_call(
        functools.partial(_cached_attn_kernel, nb=nb, n_new=n_new, win=win),
        grid=(Bd // nb,),
        in_specs=[small, small, small, small_t, small_t, cache, cache,
                  pl.BlockSpec(bc.shape, lambda b: (0, 0)), pl.BlockSpec(bn.shape, lambda b: (0, 0))],
        out_specs=[small, small, cache, cache],
        out_shape=[jax.ShapeDtypeStruct((Bd, 8, GROUP_W), f32)] * 2
        + [jax.ShapeDtypeStruct((Bd, GROUP_W, win), f32)] * 2,
        compiler_params=_cparams(("arbitrary",)),
        name="cached_attn_g%d" % g,
    )(q8, kn8, vn8, knt, vnt, kc_t, vc_t, bc, bn)


def _merge_kernel(x_ref, o0_ref, o1_ref, o2_ref, l0_ref, l1_ref, l2_ref, po_ref,
                  ln1_ref, wg_ref, wpa_ref, wpb_ref, wo_ref, ln2_ref, wr_ref, br_ref, tri_ref,
                  h_ref, xn_ref, ri_ref, wc_ref, cnt_ref, carry, ibuf, *, tm, dils):
    step = pl.program_id(0) * pl.num_programs(1) + pl.program_id(1)

    @pl.when(step == 0)
    def _():
        carry[...] = jnp.zeros_like(carry)

    slots = iter(range(IBUF_SLOTS))

    def token_order(ref, dil):
        if dil == 1:
            return ref[0, 0].astype(f32)
        chunks = []
        for c in range(GROUP_W // LANES):
            base = next(slots) * tm
            for r in range(dil):
                ibuf[pl.ds(base + r, tm // dil, stride=dil), :] = ref[0, r, :, c * LANES:(c + 1) * LANES].astype(f32)
            chunks.append(ibuf[base:base + tm, :])
        return jnp.concatenate(chunks, axis=-1)

    x = x_ref[0]
    xn = _rmsnorm_rows(x, ln1_ref[...]).astype(bf16)
    gates = jnp.dot(xn, wg_ref[...], preferred_element_type=f32)
    l0, l1, l2 = (token_order(r, d) for r, d in zip((l0_ref, l1_ref, l2_ref), dils))
    lm = jnp.maximum(jnp.maximum(l0, l1), l2)
    e0, e1, e2 = jnp.exp(l0 - lm), jnp.exp(l1 - lm), jnp.exp(l2 - lm)
    o0, o1, o2 = (token_order(r, d) for r, d in zip((o0_ref, o1_ref, o2_ref), dils))
    attn = (e0 * o0 + e1 * o1 + e2 * o2) / (e0 + e1 + e2)
    ma = jnp.dot(attn.astype(bf16), wpa_ref[...], preferred_element_type=f32)
    mb = jnp.dot(po_ref[0], wpb_ref[...], preferred_element_type=f32)
    mix = jax.nn.sigmoid(gates[:, :D_MODEL]) * ma + jax.nn.sigmoid(gates[:, D_MODEL:]) * mb
    h = x + jnp.dot(mix.astype(bf16), wo_ref[...], preferred_element_type=f32)
    h_ref[...] = h
    xn2 = _rmsnorm_rows(h, ln2_ref[...])
    xn_ref[...] = xn2

    lt = (jnp.dot(xn2.astype(bf16), wr_ref[...], preferred_element_type=f32) + br_ref[...]).T
    row8 = lax.broadcasted_iota(jnp.int32, (8, tm), 0)
    gl = jnp.where(row8 < N_EXPERT_GROUPS, lt[0:8], -jnp.inf)
    gmax = jnp.max(gl, axis=0, keepdims=True)
    gidx = jnp.min(jnp.where(gl == gmax, row8, 8), axis=0, keepdims=True)
    pg = 1.0 / jnp.sum(jnp.exp(gl - gmax), axis=0, keepdims=True)
    sel = jnp.zeros((8, tm), f32)
    for g in range(N_EXPERT_GROUPS):
        lo = EXPERT_COL0 + g * EXPERTS_PER_GROUP
        sel = jnp.where(gidx == g, lt[lo:lo + EXPERTS_PER_GROUP], sel)
    v0 = jnp.max(sel, axis=0, keepdims=True)
    i0 = jnp.min(jnp.where(sel == v0, row8, 8), axis=0, keepdims=True)
    sel2 = jnp.where(row8 == i0, -jnp.inf, sel)
    v1 = jnp.max(sel2, axis=0, keepdims=True)
    i1 = jnp.min(jnp.where(sel2 == v1, row8, 8), axis=0, keepdims=True)
    t = jnp.exp(v1 - v0)
    w0 = pg / (1.0 + t)
    w1 = pg * t / (1.0 + t)
    eid0 = gidx * EXPERTS_PER_GROUP + i0
    eid1 = gidx * EXPERTS_PER_GROUP + i1
    erow = lax.broadcasted_iota(jnp.int32, (N_EXPERTS, tm), 0)
    oh0 = erow == eid0
    oh1 = erow == eid1
    cnt = jnp.where(oh0 | oh1, 1.0, 0.0)
    before = jnp.dot(cnt.astype(bf16), tri_ref[...], preferred_element_type=f32) + carry[:, 0:1]
    rank0 = jnp.sum(jnp.where(oh0, before, 0.0), axis=0, keepdims=True)
    rank1 = jnp.sum(jnp.where(oh1, before, 0.0), axis=0, keepdims=True)
    carry[...] = carry[...] + jnp.sum(cnt, axis=1, keepdims=True)
    cnt_ref[...] = carry[...]
    ri = jnp.zeros((8, tm), jnp.int32)
    for r, val in enumerate((eid0, eid1, rank0.astype(jnp.int32), rank1.astype(jnp.int32))):
        ri = jnp.where(row8 == r, val, ri)
    ri_ref[...] = ri
    rowl = lax.broadcasted_iota(jnp.int32, (LANES, tm), 0)
    wslab = jnp.where(rowl == 0, w0, jnp.where(rowl == 1, w1, 0.0))
    wc_ref[...] = wslab.T


def _merge(x, o, l, po, ln1, w_gates, w_pa, w_pb, w_o, ln2, w_router, b_router, *, tm):
    B, T, _ = x.shape
    assert T % tm == 0
    nt = T // tm
    n = B * T
    dils = tuple(a.shape[1] for a in o)
    assert all(tm % (8 * d) == 0 for d in dils)
    tri = (jnp.arange(tm)[:, None] < jnp.arange(tm)[None, :]).astype(bf16)
    rows3 = lambda w: pl.BlockSpec((1, tm, w), lambda b, i: (b, i, 0))
    flat = lambda w: pl.BlockSpec((tm, w), lambda b, i: (b * nt + i, 0))
    grp = [pl.BlockSpec((1, d, tm // d, GROUP_W), lambda b, i: (b, 0, i, 0)) for d in dils]
    full = lambda a: pl.BlockSpec(a.shape, lambda b, i: (0,) * a.ndim)
    weights = (ln1, w_gates, w_pa, w_pb, w_o, ln2, w_router, b_router, tri)
    return pl.pallas_call(
        functools.partial(_merge_kernel, tm=tm, dils=dils),
        grid=(B, nt),
        in_specs=[rows3(D_MODEL)] + grp + grp + [rows3(POOL_W)] + [full(a) for a in weights],
        out_specs=[flat(D_MODEL), flat(D_MODEL),
                   pl.BlockSpec((8, tm), lambda b, i: (0, b * nt + i)),
                   flat(LANES),
                   pl.BlockSpec((N_EXPERTS, LANES), lambda b, i: (0, 0))],
        out_shape=[jax.ShapeDtypeStruct((n, D_MODEL), f32), jax.ShapeDtypeStruct((n, D_MODEL), f32),
                   jax.ShapeDtypeStruct((8, n), jnp.int32),
                   jax.ShapeDtypeStruct((n, LANES), f32),
                   jax.ShapeDtypeStruct((N_EXPERTS, LANES), f32)],
        scratch_shapes=[pltpu.VMEM((N_EXPERTS, LANES), f32), pltpu.VMEM((IBUF_SLOTS * tm, LANES), f32)],
        compiler_params=_cparams(("arbitrary", "arbitrary")),
        name="merge_router",
    )(x, *o, *l, po, *weights)


def _dest_kernel(ri_ref, ps_ref, dest_ref):
    tn = ri_ref.shape[1]
    erow = lax.broadcasted_iota(jnp.int32, (N_EXPERTS, tn), 0)
    ps = ps_ref[:, 0:1]
    row8 = lax.broadcasted_iota(jnp.int32, (8, tn), 0)
    out = jnp.zeros((8, tn), jnp.int32)
    for k in range(2):
        start = jnp.sum(jnp.where(erow == ri_ref[k:k + 1, :], ps, 0.0), axis=0, keepdims=True)
        out = jnp.where(row8 == k, start.astype(jnp.int32) + ri_ref[2 + k:3 + k, :], out)
    dest_ref[...] = out


def _dest(ri, pad_starts, *, tn):
    n = ri.shape[1]
    ps = jnp.broadcast_to(pad_starts.astype(f32)[:, None], (N_EXPERTS, LANES))
    return pl.pallas_call(
        _dest_kernel,
        grid=(n // tn,),
        in_specs=[pl.BlockSpec((8, tn), lambda i: (0, i)), pl.BlockSpec((N_EXPERTS, LANES), lambda i: (0, 0))],
        out_specs=pl.BlockSpec((8, tn), lambda i: (0, i)),
        out_shape=jax.ShapeDtypeStruct((8, n), jnp.int32),
        compiler_params=_cparams(("arbitrary",)),
        name="moe_dest",
    )(ri, ps)


def _dispatch_kernel(meta_ref, dest_ref, x_ref, xs_hbm, zblk, sem, zsem, *, tm, blk, n_blocks):
    i = pl.program_id(0)

    @pl.when(i == 0)
    def _():
        zblk[...] = jnp.zeros_like(zblk)

        def zero_block(row0):
            return pltpu.make_async_copy(zblk, xs_hbm.at[pl.ds(pl.multiple_of(row0, blk), blk)], zsem)

        last = N_EXPERTS - 1
        first_tail = (meta_ref[last] + meta_ref[2 * N_EXPERTS + last]) // blk

        def tail_start(b, c):
            zero_block(b * blk).start()
            return c

        def tail_wait(b, c):
            zero_block(b * blk).wait()
            return c

        lax.fori_loop(first_tail, n_blocks, tail_start, 0)
        lax.fori_loop(first_tail, n_blocks, tail_wait, 0)

        def expert_copy(e, go):
            padded = meta_ref[2 * N_EXPERTS + e]

            @pl.when(padded > 0)
            def _():
                cp = zero_block(meta_ref[e] + padded - blk)
                cp.start() if go else cp.wait()

        def expert_start(e, c):
            expert_copy(e, True)
            return c

        def expert_wait(e, c):
            expert_copy(e, False)
            return c

        lax.fori_loop(0, N_EXPERTS, expert_start, 0)
        lax.fori_loop(0, N_EXPERTS, expert_wait, 0)

    def issue(t, c):
        for k in range(2):
            pltpu.make_async_copy(x_ref.at[pl.ds(t, 1)], xs_hbm.at[pl.ds(dest_ref[k, t], 1)], sem).start()
        return c

    lax.fori_loop(0, tm, issue, 0, unroll=8)
    for k in range(2):
        pltpu.make_async_copy(x_ref, xs_hbm.at[pl.ds(0, tm)], sem).wait()


def _dispatch(xn2, dest, meta, *, tm, blk, n_blocks):
    n = xn2.shape[0]
    n_rows = n_blocks * blk
    return pl.pallas_call(
        functools.partial(_dispatch_kernel, tm=tm, blk=blk, n_blocks=n_blocks),
        grid_spec=pltpu.PrefetchScalarGridSpec(
            num_scalar_prefetch=1,
            grid=(n // tm,),
            in_specs=[pl.BlockSpec((8, tm), lambda i, m: (0, i), memory_space=pltpu.SMEM),
                      pl.BlockSpec((tm, D_MODEL), lambda i, m: (i, 0))],
            out_specs=pl.BlockSpec(memory_space=pl.ANY),
            scratch_shapes=[pltpu.VMEM((blk, D_MODEL), f32), pltpu.SemaphoreType.DMA, pltpu.SemaphoreType.DMA],
        ),
        out_shape=jax.ShapeDtypeStruct((n_rows, D_MODEL), f32),
        compiler_params=_cparams(("arbitrary",)),
        name="moe_dispatch",
    )(meta, dest, xn2)


def _moe_kernel(blk_e_ref, nvalid_ref, x_ref, wg_ref, wu_ref, wd_ref, y_ref):
    b = pl.program_id(0)

    @pl.when(nvalid_ref[b] > 0)
    def _():
        x = x_ref[...].astype(bf16)
        hid = jax.nn.silu(jnp.dot(x, wg_ref[0], preferred_element_type=f32)) * jnp.dot(x, wu_ref[0], preferred_element_type=f32)
        y_ref[...] = jnp.dot(hid.astype(bf16), wd_ref[0], preferred_element_type=f32)

    @pl.when(nvalid_ref[b] == 0)
    def _():
        y_ref[...] = jnp.zeros_like(y_ref)


def _moe(xs, blk_e, nvalid, w_gate, w_up, w_down, *, blk):
    n_blocks = blk_e.shape[0]
    wspec = lambda shape: pl.BlockSpec((1,) + shape, lambda b, be, nv: (be[b], 0, 0))
    rows = pl.BlockSpec((blk, D_MODEL), lambda b, be, nv: (b, 0))
    return pl.pallas_call(
        _moe_kernel,
        grid_spec=pltpu.PrefetchScalarGridSpec(
            num_scalar_prefetch=2,
            grid=(n_blocks,),
            in_specs=[rows, wspec((D_MODEL, D_EXPERT)), wspec((D_MODEL, D_EXPERT)), wspec((D_EXPERT, D_MODEL))],
            out_specs=rows,
        ),
        out_shape=jax.ShapeDtypeStruct((n_blocks * blk, D_MODEL), f32),
        compiler_params=_cparams(("arbitrary",)),
        name="moe_experts",
    )(blk_e, nvalid, xs, w_gate, w_up, w_down)


def _combine_kernel(dest_ref, h_ref, wc_ref, ys_hbm, y_ref, gbuf, sem, *, tm):
    def issue(t, c):
        for k in range(2):
            pltpu.make_async_copy(ys_hbm.at[pl.ds(dest_ref[k, t], 1)], gbuf.at[k, pl.ds(t, 1)], sem).start()
        return c

    lax.fori_loop(0, tm, issue, 0, unroll=8)
    for k in range(2):
        pltpu.make_async_copy(ys_hbm.at[pl.ds(0, tm)], gbuf.at[k], sem).wait()
    w = wc_ref[...]
    y_ref[...] = h_ref[...] + w[:, 0:1] * gbuf[0] + w[:, 1:2] * gbuf[1]


def _combine(h, ys, dest, wc, *, tm):
    n = h.shape[0]
    rows = lambda w: pl.BlockSpec((tm, w), lambda i: (i, 0))
    return pl.pallas_call(
        functools.partial(_combine_kernel, tm=tm),
        grid=(n // tm,),
        in_specs=[pl.BlockSpec((8, tm), lambda i: (0, i), memory_space=pltpu.SMEM),
                  rows(D_MODEL), rows(LANES), pl.BlockSpec(memory_space=pl.ANY)],
        out_specs=rows(D_MODEL),
        out_shape=jax.ShapeDtypeStruct((n, D_MODEL), f32),
        scratch_shapes=[pltpu.VMEM((2, tm, D_MODEL), f32), pltpu.SemaphoreType.DMA],
        compiler_params=_cparams(("arbitrary",)),
        name="moe_combine",
    )(dest, h, wc, ys)


def _mix_and_moe(x, o, l, po, wts, *, tm, blk):
    n = x.shape[0] * x.shape[1]
    h, xn2, ri, wc, cnt = _merge(x, o, l, po, wts['ln1'], wts['w_gates'], wts['w_pa'], wts['w_pb'], wts['w_o'],
                                 wts['ln2'], wts['w_router'], wts['b_router'], tm=tm)
    n_blocks = (2 * n + N_EXPERTS * (blk - 1) + blk - 1) // blk
    counts = cnt[:, 0].astype(jnp.int32)
    padded = (counts + blk - 1) // blk * blk
    pad_ends = jnp.cumsum(padded)
    pad_starts = pad_ends - padded
    blk_start = jnp.arange(n_blocks, dtype=jnp.int32) * blk
    blk_e = jnp.minimum(jnp.sum(pad_ends[None, :] <= blk_start[:, None], axis=1), N_EXPERTS - 1).astype(jnp.int32)
    nvalid = jnp.clip(pad_starts[blk_e] + counts[blk_e] - blk_start, 0, blk).astype(jnp.int32)
    meta = jnp.concatenate([pad_starts, counts, padded]).astype(jnp.int32)
    dest = _dest(ri, pad_starts, tn=min(n, 8 * tm))
    xs = _dispatch(xn2, dest, meta, tm=tm, blk=blk, n_blocks=n_blocks)
    ys = _moe(xs, blk_e, nvalid, wts['w_gate'], wts['w_up'], wts['w_down'], blk=blk)
    return _combine(h, ys, dest, wc, tm=tm)


def kernel(x_prompt, x_sample, cache_k_w128, cache_v_w128, cache_k_w512, cache_v_w512, cache_k_w2048, cache_v_w2048, state_pool, ln1, w_in, q_gain, k_gain, pool_lin, pool_scale, w_pa, w_pb, w_o, ln2, w_rg, b_rg, w_re, b_re, w_gate, w_up, w_down):
    B, T, D = x_prompt.shape
    Bd, S, _ = x_sample.shape
    past_len = 8192
    caches = ((cache_k_w128, cache_v_w128), (cache_k_w512, cache_v_w512), (cache_k_w2048, cache_v_w2048))
    slopes = jnp.exp2(-8.0 * jnp.arange(1, N_HEADS + 1, dtype=f32) / N_HEADS).reshape(N_GROUPS, HEADS_PER_GROUP)

    w_qkvu = w_in[:, :QKVU_W].astype(bf16)
    plin_bd = jnp.zeros((POOL_W, POOL_W), f32)
    for g in range(len(POOL_WINDOWS)):
        plin_bd = plin_bd.at[g * POOL_GW:(g + 1) * POOL_GW, g * POOL_GW:(g + 1) * POOL_GW].set(pool_lin[g])
    w_router = jnp.zeros((D, ROUTER_W), f32).at[:, :N_EXPERT_GROUPS].set(w_rg)
    w_router = w_router.at[:, EXPERT_COL0:EXPERT_COL0 + N_EXPERTS].set(w_re)
    b_router = jnp.zeros((1, ROUTER_W), f32).at[0, :N_EXPERT_GROUPS].set(b_rg)
    b_router = b_router.at[0, EXPERT_COL0:EXPERT_COL0 + N_EXPERTS].set(b_re)
    wts = dict(ln1=ln1.reshape(1, D), w_gates=w_in[:, QKVU_W:].astype(bf16), w_pa=w_pa.astype(bf16),
               w_pb=w_pb.astype(bf16), w_o=w_o.astype(bf16), ln2=ln2.reshape(1, D),
               w_router=w_router.astype(bf16), b_router=b_router,
               w_gate=w_gate.astype(bf16), w_up=w_up.astype(bf16), w_down=w_down.astype(bf16))
    proj_w = (wts['ln1'], w_qkvu, q_gain.reshape(1, ATTN_W), k_gain.reshape(1, ATTN_W),
              plin_bd.astype(bf16), pool_scale.reshape(1, POOL_W))

    *qkv, po, kt, vt, ut = _proj_prompt(x_prompt, *proj_w, tm=512)
    o, l = zip(*[_band_attention(*qkv[3 * g:3 * g + 3], g, slopes[g]) for g in range(N_GROUPS)])
    y_prompt = _mix_and_moe(x_prompt, o, l, po, wts, tm=512, blk=512).reshape(B, T, D)
    tail = kt.shape[2]
    pkv = []
    for g, w in enumerate(WINDOWS):
        for a in (kt, vt):
            a = a.reshape(B, N_HEADS, HEAD_DIM, tail)[:, g * HEADS_PER_GROUP:(g + 1) * HEADS_PER_GROUP, :, tail - w:]
            pkv.append(a.transpose(0, 3, 1, 2))
    p_pool = ut[:, 1:]

    n_s = Bd * S
    xs = x_sample.transpose(1, 0, 2).reshape(n_s, D)
    qs, ks, vs, pos, st = _proj_sample(xs, *proj_w, state_pool.transpose(1, 0, 2),
                                       n_seq=Bd, n_new=S, past_len=past_len)
    so, sl, skv = [], [], []
    pad8 = lambda a: jnp.pad(a.reshape(S, Bd, GROUP_W).transpose(1, 0, 2), ((0, 0), (0, 8 - S), (0, 0)))
    pad8_t = lambda a: jnp.pad(a.reshape(S, Bd, GROUP_W).transpose(1, 2, 0), ((0, 0), (0, 0), (0, 8 - S)))
    for g, w in enumerate(WINDOWS):
        cols = slice(g * GROUP_W, (g + 1) * GROUP_W)
        kc, vc = caches[g]
        og, lg, ko, vo = _cached_attention(
            pad8(qs[:, cols]), pad8(ks[:, cols]), pad8(vs[:, cols]), pad8_t(ks[:, cols]), pad8_t(vs[:, cols]),
            kc.transpose(0, 2, 3, 1).reshape(Bd, GROUP_W, w), vc.transpose(0, 2, 3, 1).reshape(Bd, GROUP_W, w),
            g, slopes[g], n_new=S)
        so.append(og[:, :S].transpose(1, 0, 2).reshape(1, 1, n_s, GROUP_W).astype(bf16))
        sl.append(lg[:, :S].transpose(1, 0, 2).reshape(1, 1, n_s, GROUP_W))
        for a in (ko, vo):
            skv.append(a.reshape(Bd, HEADS_PER_GROUP, HEAD_DIM, w).transpose(0, 3, 1, 2))
    y_sample = _mix_and_moe(xs[None], so, sl, pos[None], wts, tm=n_s, blk=128).reshape(S, Bd, D).transpose(1, 0, 2)
    s_pool = st.transpose(1, 0, 2)

    return (y_prompt, y_sample, *pkv, p_pool, *skv, s_pool)
```

```python
import functools

import jax
import jax.numpy as jnp
from jax import lax
from jax.experimental import pallas as pl
from jax.experimental.pallas import tpu as pltpu

D_MODEL = 1024
HEAD_DIM = 64
HEADS_PER_GROUP = 4
WINDOWS = (128, 512, 2048)
DILATIONS = (1, 4, 16)
N_GROUPS = len(WINDOWS)
N_HEADS = HEADS_PER_GROUP * N_GROUPS
ATTN_W = N_HEADS * HEAD_DIM
GROUP_W = HEADS_PER_GROUP * HEAD_DIM
BAND = 128
POOL_WINDOWS = (2, 4, 8, 16)
POOL_GW = 128
POOL_W = len(POOL_WINDOWS) * POOL_GW
POOL_STATE = max(POOL_WINDOWS) - 1
POOL_HIST = 32
assert POOL_WINDOWS == (2, 4, 8, 16)
N_EXPERT_GROUPS = 4
EXPERTS_PER_GROUP = 8
N_EXPERTS = N_EXPERT_GROUPS * EXPERTS_PER_GROUP
D_EXPERT = 512
QKVU_W = 3 * ATTN_W + POOL_W
NEG = -1e30
EPS = 1e-6
LANES = 128
ROUTER_W = LANES
EXPERT_COL0 = 8
VMEM_LIMIT = 56 * 1024 * 1024
IBUF_SLOTS = 2 * (N_GROUPS - 1) * (GROUP_W // LANES)
SEG_ALIGN = 8

assert all(w // d == BAND for w, d in zip(WINDOWS, DILATIONS))

f32 = jnp.float32
bf16 = jnp.bfloat16


def _cparams(sem):
    return pltpu.CompilerParams(dimension_semantics=sem, vmem_limit_bytes=VMEM_LIMIT)


def _rmsnorm_rows(x, g):
    ms = jnp.mean(x * x, axis=-1, keepdims=True)
    return x * lax.rsqrt(ms + EPS) * g


def _head_rmsnorm_chunk(ch, gain):
    lane = lax.broadcasted_iota(jnp.int32, ch.shape, 1)
    lo_mask = lane < HEAD_DIM
    sq = ch * ch
    lo = jnp.sum(jnp.where(lo_mask, sq, 0.0), axis=-1, keepdims=True)
    hi = jnp.sum(jnp.where(lo_mask, 0.0, sq), axis=-1, keepdims=True)
    ss = jnp.where(lo_mask, lo, hi)
    return ch * lax.rsqrt(ss * (1.0 / HEAD_DIM) + EPS) * gain


def _project(x, ln1, w_ref, qg, kg):
    xn = _rmsnorm_rows(x, ln1).astype(bf16)
    nch = ATTN_W // LANES
    qs = qg * (HEAD_DIM ** -0.5)
    zq = jnp.dot(xn, w_ref[:, 0:ATTN_W], preferred_element_type=f32)
    q = [_head_rmsnorm_chunk(zq[:, c * LANES:(c + 1) * LANES], qs[:, c * LANES:(c + 1) * LANES]) for c in range(nch)]
    zk = jnp.dot(xn, w_ref[:, ATTN_W:2 * ATTN_W], preferred_element_type=f32)
    k = [_head_rmsnorm_chunk(zk[:, c * LANES:(c + 1) * LANES], kg[:, c * LANES:(c + 1) * LANES]) for c in range(nch)]
    zv = jnp.dot(xn, w_ref[:, 2 * ATTN_W:3 * ATTN_W], preferred_element_type=f32)
    v = [zv[:, c * LANES:(c + 1) * LANES] for c in range(nch)]
    u = jnp.dot(xn, w_ref[:, 3 * ATTN_W:3 * ATTN_W + POOL_W], preferred_element_type=f32)
    return q, k, v, u


def _proj_prompt_kernel(x_ref, ln1_ref, w_ref, qg_ref, kg_ref, plin_ref, pscale_ref, *rest,
                        tm, n_tiles, tail_tiles):
    qkv_refs = rest[:3 * N_GROUPS]
    po_ref, kt_ref, vt_ref, ut_ref, ubuf, pa, pb, sbuf = rest[3 * N_GROUPS:]
    i = pl.program_id(1)
    hist = POOL_STATE + 1
    ph, rows = POOL_HIST, POOL_HIST + tm

    @pl.when(i == 0)
    def _():
        ubuf[0:ph, :] = jnp.zeros((ph, POOL_W), f32)

    @pl.when(i > 0)
    def _():
        ubuf[0:ph, :] = ubuf[tm:tm + ph, :]

    q, k, v, u = _project(x_ref[0], ln1_ref[...], w_ref, qg_ref[...], kg_ref[...])
    cpg = GROUP_W // LANES
    slot = 0
    for t, chunks in enumerate((q, k, v)):
        for g, dil in enumerate(DILATIONS):
            out_ref = qkv_refs[3 * g + t]
            for c in range(cpg):
                val = chunks[g * cpg + c]
                cols = slice(c * LANES, (c + 1) * LANES)
                if dil == 1:
                    out_ref[0, 0, :, cols] = val.astype(bf16)
                else:
                    sbuf[slot * tm:(slot + 1) * tm, :] = val
                    for r in range(dil):
                        out_ref[0, r, :, cols] = sbuf[pl.ds(slot * tm + r, tm // dil, stride=dil), :].astype(bf16)
                    slot += 1

    ubuf[ph:rows, :] = u
    gw = POOL_GW
    pa[8:rows, :] = ubuf[8:rows, :] + ubuf[7:rows - 1, :]
    pb[16:rows, gw:] = pa[16:rows, gw:] + pa[14:rows - 2, gw:]
    pa[24:rows, 2 * gw:] = pb[24:rows, 2 * gw:] + pb[20:rows - 4, 2 * gw:]
    pb[32:rows, 3 * gw:] = pa[32:rows, 3 * gw:] + pa[24:rows - 8, 3 * gw:]
    pos = i * tm + lax.broadcasted_iota(jnp.int32, (tm, POOL_GW), 0)
    zs = []
    for g, w in enumerate(POOL_WINDOWS):
        cols = slice(g * POOL_GW, (g + 1) * POOL_GW)
        wsum = (pa if g % 2 == 0 else pb)[ph:rows, cols]
        cnt = jnp.minimum(pos + 1, w).astype(f32)
        zs.append(wsum / cnt - u[:, cols])
    z = jnp.concatenate(zs, axis=-1).astype(bf16)
    po = jnp.dot(z, plin_ref[...], preferred_element_type=f32) * pscale_ref[...]
    po_ref[0] = po.astype(bf16)

    @pl.when(i >= n_tiles - tail_tiles)
    def _():
        kt_ref[0] = jnp.concatenate(k, axis=-1).T
        vt_ref[0] = jnp.concatenate(v, axis=-1).T

    @pl.when(i == n_tiles - 1)
    def _():
        ut_ref[0] = ubuf[rows - hist:rows, :]


def _proj_prompt(x, ln1, w_qkvu, qg, kg, plin_bd, pscale, *, tm):
    B, T, D = x.shape
    n_tiles = T // tm
    tail = max(WINDOWS)
    assert T % tm == 0 and tail % tm == 0 and T >= tail
    tail_tiles = tail // tm
    hist = POOL_STATE + 1
    kern = functools.partial(_proj_prompt_kernel, tm=tm, n_tiles=n_tiles, tail_tiles=tail_tiles)
    const = lambda b, i: (0, 0)
    assert all(tm % (16 * d) == 0 for d in DILATIONS)
    qkv_specs = [pl.BlockSpec((1, d, tm // d, GROUP_W), lambda b, i: (b, 0, i, 0)) for d in DILATIONS for _ in range(3)]
    qkv_shapes = [jax.ShapeDtypeStruct((B, d, T // d, GROUP_W), bf16) for d in DILATIONS for _ in range(3)]
    tail_spec = pl.BlockSpec((1, ATTN_W, tm), lambda b, i: (b, 0, jnp.maximum(i - (n_tiles - tail_tiles), 0)))
    return pl.pallas_call(
        kern,
        grid=(B, n_tiles),
        in_specs=[
            pl.BlockSpec((1, tm, D), lambda b, i: (b, i, 0)),
            pl.BlockSpec((1, D), const),
            pl.BlockSpec((D, QKVU_W), const),
            pl.BlockSpec((1, ATTN_W), const),
            pl.BlockSpec((1, ATTN_W), const),
            pl.BlockSpec((POOL_W, POOL_W), const),
            pl.BlockSpec((1, POOL_W), const),
        ],
        out_specs=qkv_specs
        + [pl.BlockSpec((1, tm, POOL_W), lambda b, i: (b, i, 0)),
           tail_spec, tail_spec,
           pl.BlockSpec((1, hist, POOL_W), lambda b, i: (b, 0, 0))],
        out_shape=qkv_shapes
        + [jax.ShapeDtypeStruct((B, T, POOL_W), bf16),
           jax.ShapeDtypeStruct((B, ATTN_W, tail), f32),
           jax.ShapeDtypeStruct((B, ATTN_W, tail), f32),
           jax.ShapeDtypeStruct((B, hist, POOL_W), f32)],
        scratch_shapes=[pltpu.VMEM((POOL_HIST + tm, POOL_W), f32)] * 3
                       + [pltpu.VMEM((3 * (N_GROUPS - 1) * (GROUP_W // LANES) * tm, LANES), f32)],
        compiler_params=_cparams(("arbitrary", "arbitrary")),
        name="proj_prompt",
    )(x, ln1, w_qkvu, qg, kg, plin_bd, pscale)


def _proj_sample_kernel(x_ref, ln1_ref, w_ref, qg_ref, kg_ref, plin_ref, pscale_ref, state_ref,
                        q_ref, k_ref, v_ref, po_ref, st_ref, *, n_seq, n_new, past_len):
    q, k, v, u = _project(x_ref[...], ln1_ref[...], w_ref, qg_ref[...], kg_ref[...])
    q_ref[...] = jnp.concatenate(q, axis=-1)
    k_ref[...] = jnp.concatenate(k, axis=-1)
    v_ref[...] = jnp.concatenate(v, axis=-1)
    ext = [state_ref[j] for j in range(POOL_STATE)] + [u[s * n_seq:(s + 1) * n_seq, :] for s in range(n_new)]
    for s in range(n_new):
        zs = []
        for g, w in enumerate(POOL_WINDOWS):
            cols = slice(g * POOL_GW, (g + 1) * POOL_GW)
            cur = ext[POOL_STATE + s][:, cols]
            acc = cur
            for j in range(1, w):
                acc = acc + ext[POOL_STATE + s - j][:, cols]
            cnt = float(min(past_len + s + 1, w))
            zs.append(acc / cnt - cur)
        z = jnp.concatenate(zs, axis=-1).astype(bf16)
        po = jnp.dot(z, plin_ref[...], preferred_element_type=f32) * pscale_ref[...]
        po_ref[s * n_seq:(s + 1) * n_seq, :] = po.astype(bf16)
    for j in range(POOL_STATE):
        st_ref[j] = ext[j + n_new]


def _proj_sample(x, ln1, w_qkvu, qg, kg, plin_bd, pscale, state, *, n_seq, n_new, past_len):
    n = n_seq * n_new
    kern = functools.partial(_proj_sample_kernel, n_seq=n_seq, n_new=n_new, past_len=past_len)
    return pl.pallas_call(
        kern,
        out_shape=[jax.ShapeDtypeStruct((n, ATTN_W), f32)] * 3
        + [jax.ShapeDtypeStruct((n, POOL_W), bf16),
           jax.ShapeDtypeStruct((POOL_STATE, n_seq, POOL_W), f32)],
        compiler_params=pltpu.CompilerParams(vmem_limit_bytes=VMEM_LIMIT),
        name="proj_sample",
    )(x, ln1, w_qkvu, qg, kg, plin_bd, pscale, state)


def _head_masks(shape):
    lane = lax.broadcasted_iota(jnp.int32, shape, len(shape) - 1)
    return [(lane >= h * HEAD_DIM) & (lane < (h + 1) * HEAD_DIM) for h in range(HEADS_PER_GROUP)]


def _band_attn_kernel(q_ref, kc_ref, kp_ref, vc_ref, vp_ref, bias_ref, o_ref, l_ref, kbuf, vbuf, *, tl, unroll):
    i = pl.program_id(2)
    kbuf[0:BAND, :] = kp_ref[0, 0]
    kbuf[BAND:2 * BAND, :] = kc_ref[0, 0, 0:BAND, :]
    vbuf[0:BAND, :] = vp_ref[0, 0]
    vbuf[BAND:2 * BAND, :] = vc_ref[0, 0, 0:BAND, :]
    masks = _head_masks((BAND, GROUP_W))

    def sub_block(j, kk, vv, var):
        r0 = j * BAND if isinstance(j, int) else pl.multiple_of(j * BAND, BAND)
        q = q_ref[0, 0, pl.ds(r0, BAND), :]
        qm = jnp.concatenate([jnp.where(m, q, jnp.zeros_like(q)) for m in masks], axis=0)
        s = lax.dot_general(qm, kk, (((1,), (1,)), ((), ())), preferred_element_type=f32)
        s = s + bias_ref[var]
        m = jnp.max(s, axis=-1, keepdims=True)
        p = jnp.exp(s - m)
        den = jnp.sum(p, axis=-1, keepdims=True)
        pv = jnp.dot(p.astype(bf16), vv, preferred_element_type=f32)
        o = jnp.zeros((BAND, GROUP_W), f32)
        ms = jnp.zeros((BAND, GROUP_W), f32)
        ds = jnp.ones((BAND, GROUP_W), f32)
        for h, msk in enumerate(masks):
            rows = slice(h * BAND, (h + 1) * BAND)
            o = jnp.where(msk, pv[rows], o)
            ms = jnp.where(msk, m[rows], ms)
            ds = jnp.where(msk, den[rows], ds)
        o_ref[0, 0, pl.ds(r0, BAND), :] = (o / ds).astype(bf16)
        l_ref[0, 0, pl.ds(r0, BAND), :] = ms + jnp.log(ds)

    sub_block(0, kbuf[...], vbuf[...], jnp.where(i == 0, 0, 1))

    def body(j, carry):
        k0 = pl.multiple_of((j - 1) * BAND, BAND)
        sub_block(j, kc_ref[0, 0, pl.ds(k0, 2 * BAND), :], vc_ref[0, 0, pl.ds(k0, 2 * BAND), :], 1)
        return carry

    if tl > BAND:
        lax.fori_loop(1, tl // BAND, body, 0, unroll=unroll)


def _band_bias(slopes_g, dil):
    qi = jnp.arange(BAND)[:, None]
    kb = jnp.arange(2 * BAND)[None, :]
    rel = qi + BAND - kb
    valid = (rel >= 0) & (rel <= BAND)
    alibi = -slopes_g[:, None, None] * (dil * rel)[None].astype(f32)
    variants = []
    for first in (True, False):
        ok = valid & (kb >= BAND) if first else valid
        variants.append(jnp.where(ok[None], alibi, NEG).reshape(HEADS_PER_GROUP * BAND, 2 * BAND))
    return jnp.stack(variants, axis=0)


def _band_attention(q, k, v, g, slopes_g, *, tl_max=1024, unroll=7):
    B, dil, L, _ = q.shape
    tl = min(tl_max, L)
    assert dil == DILATIONS[g] and L % tl == 0 and tl % BAND == 0
    nsub = tl // BAND
    bias = _band_bias(slopes_g, dil)
    cur = pl.BlockSpec((1, 1, tl, GROUP_W), lambda b, r, i: (b, r, i, 0))
    prev = pl.BlockSpec((1, 1, BAND, GROUP_W), lambda b, r, i: (b, r, jnp.maximum(i * nsub - 1, 0), 0))
    return pl.pallas_call(
        functools.partial(_band_attn_kernel, tl=tl, unroll=max(1, min(unroll, nsub - 1))),
        grid=(B, dil, L // tl),
        in_specs=[cur, cur, prev, cur, prev,
                  pl.BlockSpec((2, HEADS_PER_GROUP * BAND, 2 * BAND), lambda b, r, i: (0, 0, 0))],
        out_specs=[cur, cur],
        out_shape=[jax.ShapeDtypeStruct((B, dil, L, GROUP_W), bf16),
                   jax.ShapeDtypeStruct((B, dil, L, GROUP_W), f32)],
        scratch_shapes=[pltpu.VMEM((2 * BAND, GROUP_W), bf16), pltpu.VMEM((2 * BAND, GROUP_W), bf16)],
        compiler_params=_cparams(("arbitrary", "arbitrary", "arbitrary")),
        name="band_attn_g%d" % g,
    )(q, k, k, v, v, bias)


def _cached_attn_kernel(q_ref, kn_ref, vn_ref, kc_ref, vc_ref, bc_ref, bn_ref,
                        o_ref, l_ref, ko_ref, vo_ref, *, nb, n_new, win):
    masks8 = _head_masks((nb, 8, GROUP_W))
    q8, kn8, vn8 = q_ref[...], kn_ref[...], vn_ref[...]
    kc, vc = kc_ref[...], vc_ref[...]
    qm = jnp.concatenate([jnp.where(m, q8, 0.0) for m in masks8], axis=1)
    sc = jnp.einsum('bqd,bdk->bqk', qm.astype(bf16), kc.astype(bf16), preferred_element_type=f32) + bc_ref[...]
    m = jnp.max(sc, axis=-1, keepdims=True)
    sn = []
    for t in range(n_new):
        col = jnp.sum(qm * kn8[:, t:t + 1, :], axis=-1, keepdims=True) + bn_ref[:, t:t + 1]
        sn.append(col)
        m = jnp.maximum(m, col)
    pc = jnp.exp(sc - m)
    den = jnp.sum(pc, axis=-1, keepdims=True)
    acc = jnp.einsum('bqk,bdk->bqd', pc.astype(bf16), vc.astype(bf16), preferred_element_type=f32)
    for t in range(n_new):
        pn = jnp.exp(sn[t] - m)
        den = den + pn
        acc = acc + pn * vn8[:, t:t + 1, :]
    acc = acc / den
    lse = m + jnp.log(den)
    o = jnp.zeros((nb, 8, GROUP_W), f32)
    l = jnp.zeros((nb, 8, GROUP_W), f32)
    for h, msk in enumerate(masks8):
        o = jnp.where(msk, acc[:, h * 8:(h + 1) * 8, :], o)
        l = jnp.where(msk, lse[:, h * 8:(h + 1) * 8, :], l)
    o_ref[...] = o
    l_ref[...] = l
    lane_t = lax.broadcasted_iota(jnp.int32, (nb, GROUP_W, LANES), 2)
    for cache, new8, out_ref in ((kc, kn8, ko_ref), (vc, vn8, vo_ref)):
        rolled = pltpu.roll(cache, win - n_new, axis=2)
        out_ref[...] = rolled
        new_t = jnp.swapaxes(jnp.concatenate([new8, jnp.zeros((nb, LANES - 8, GROUP_W), f32)], axis=1), 1, 2)
        new_t = pltpu.roll(new_t, LANES - n_new, axis=2)
        out_ref[:, :, win - LANES:win] = jnp.where(lane_t >= LANES - n_new, new_t, rolled[:, :, win - LANES:win])


def _cached_bias(slopes_g, dil, win, n_new):
    s = jnp.arange(8)[:, None]
    i = jnp.arange(win)[None, :]
    dist = win + s - i
    ok = (dist % dil == 0) & (dist // dil <= BAND) & (s < n_new)
    bc = jnp.where(ok[None], -slopes_g[:, None, None] * dist[None].astype(f32), NEG)
    t = jnp.arange(8)[None, :]
    dn = s - t
    okn = (dn >= 0) & (dn % dil == 0) & (dn // dil <= BAND) & (s < n_new) & (t < n_new)
    bn = jnp.where(okn[None], -slopes_g[:, None, None] * dn[None].astype(f32), NEG)
    pad = (s >= n_new)
    bc = jnp.where(pad[None], 0.0, bc)
    bn = jnp.where(pad[None], 0.0, bn)
    return bc.reshape(HEADS_PER_GROUP * 8, win), bn.reshape(HEADS_PER_GROUP * 8, 8)


def _cached_attention(q8, kn8, vn8, kc_t, vc_t, g, slopes_g, *, n_new):
    Bd, _, win = kc_t.shape
    dil = DILATIONS[g]
    assert win == WINDOWS[g] and win % LANES == 0
    nb = max(1, min(Bd, 2048 // win))
    assert Bd % nb == 0
    bc, bn = _cached_bias(slopes_g, dil, win, n_new)
    small = pl.BlockSpec((nb, 8, GROUP_W), lambda b: (b, 0, 0))
    cache = pl.BlockSpec((nb, GROUP_W, win), lambda b: (b, 0, 0))
    return pl.pallas_call(
        functools.partial(_cached_attn_kernel, nb=nb, n_new=n_new, win=win),
        grid=(Bd // nb,),
        in_specs=[small, small, small, cache, cache,
                  pl.BlockSpec(bc.shape, lambda b: (0, 0)), pl.BlockSpec(bn.shape, lambda b: (0, 0))],
        out_specs=[small, small, cache, cache],
        out_shape=[jax.ShapeDtypeStruct((Bd, 8, GROUP_W), f32)] * 2
        + [jax.ShapeDtypeStruct((Bd, GROUP_W, win), f32)] * 2,
        compiler_params=_cparams(("arbitrary",)),
        name="cached_attn_g%d" % g,
    )(q8, kn8, vn8, kc_t, vc_t, bc, bn)


def _merge_kernel(x_ref, o0_ref, o1_ref, o2_ref, l0_ref, l1_ref, l2_ref, po_ref,
                  ln1_ref, wg_ref, wpa_ref, wpb_ref, wo_ref, ln2_ref, wr_ref, br_ref, tri_ref, ltri_ref,
                  h_ref, xl_ref, wc_ref, tc_ref, ibuf, *, tm, ts, dils):
    slots = iter(range(IBUF_SLOTS))

    def token_order(ref, dil):
        if dil == 1:
            return ref[0, 0].astype(f32)
        chunks = []
        for c in range(GROUP_W // LANES):
            base = next(slots) * tm
            for r in range(dil):
                ibuf[pl.ds(base + r, tm // dil, stride=dil), :] = ref[0, r, :, c * LANES:(c + 1) * LANES].astype(f32)
            chunks.append(ibuf[base:base + tm, :])
        return jnp.concatenate(chunks, axis=-1)

    x = x_ref[0]
    xn = _rmsnorm_rows(x, ln1_ref[...]).astype(bf16)
    gates = jnp.dot(xn, wg_ref[...], preferred_element_type=f32)
    l0, l1, l2 = (token_order(r, d) for r, d in zip((l0_ref, l1_ref, l2_ref), dils))
    lm = jnp.maximum(jnp.maximum(l0, l1), l2)
    e0, e1, e2 = jnp.exp(l0 - lm), jnp.exp(l1 - lm), jnp.exp(l2 - lm)
    o0, o1, o2 = (token_order(r, d) for r, d in zip((o0_ref, o1_ref, o2_ref), dils))
    attn = (e0 * o0 + e1 * o1 + e2 * o2) / (e0 + e1 + e2)
    ma = jnp.dot(attn.astype(bf16), wpa_ref[...], preferred_element_type=f32)
    mb = jnp.dot(po_ref[0], wpb_ref[...], preferred_element_type=f32)
    mix = jax.nn.sigmoid(gates[:, :D_MODEL]) * ma + jax.nn.sigmoid(gates[:, D_MODEL:]) * mb
    h = x + jnp.dot(mix.astype(bf16), wo_ref[...], preferred_element_type=f32)
    h_ref[...] = h
    xn2 = _rmsnorm_rows(h, ln2_ref[...]).astype(bf16)

    lt = (jnp.dot(xn2, wr_ref[...], preferred_element_type=f32) + br_ref[...]).T
    row8 = lax.broadcasted_iota(jnp.int32, (8, tm), 0)
    gl = jnp.where(row8 < N_EXPERT_GROUPS, lt[0:8], -jnp.inf)
    gmax = jnp.max(gl, axis=0, keepdims=True)
    gidx = jnp.min(jnp.where(gl == gmax, row8, 8), axis=0, keepdims=True)
    pg = 1.0 / jnp.sum(jnp.exp(gl - gmax), axis=0, keepdims=True)
    sel = jnp.zeros((8, tm), f32)
    for g in range(N_EXPERT_GROUPS):
        lo = EXPERT_COL0 + g * EXPERTS_PER_GROUP
        sel = jnp.where(gidx == g, lt[lo:lo + EXPERTS_PER_GROUP], sel)
    v0 = jnp.max(sel, axis=0, keepdims=True)
    i0 = jnp.min(jnp.where(sel == v0, row8, 8), axis=0, keepdims=True)
    sel2 = jnp.where(row8 == i0, -jnp.inf, sel)
    v1 = jnp.max(sel2, axis=0, keepdims=True)
    i1 = jnp.min(jnp.where(sel2 == v1, row8, 8), axis=0, keepdims=True)
    t = jnp.exp(v1 - v0)
    w0 = pg / (1.0 + t)
    w1 = pg * t / (1.0 + t)
    eid0 = gidx * EXPERTS_PER_GROUP + i0
    eid1 = gidx * EXPERTS_PER_GROUP + i1
    erow = lax.broadcasted_iota(jnp.int32, (N_EXPERTS, tm), 0)
    oh0 = erow == eid0
    oh1 = erow == eid1
    cnt = jnp.where(oh0, 1.0, jnp.where(oh1, 1.0, 0.0))
    before = jnp.dot(cnt.astype(bf16), tri_ref[...], preferred_element_type=f32)
    tcount = jnp.sum(cnt, axis=1, keepdims=True)
    units = jnp.floor((tcount + (SEG_ALIGN - 1)) * (1.0 / SEG_ALIGN))
    ub = jnp.broadcast_to(units, (N_EXPERTS, LANES)).astype(bf16)
    seg0 = SEG_ALIGN * jnp.dot(ltri_ref[...], ub, preferred_element_type=f32)[:, 0:1]
    pos_e = seg0 + before
    lpos0 = jnp.sum(jnp.where(oh0, pos_e, 0.0), axis=0, keepdims=True)
    lpos1 = jnp.sum(jnp.where(oh1, pos_e, 0.0), axis=0, keepdims=True)
    prow = lax.broadcasted_iota(jnp.int32, (ts, tm), 0)
    perm = jnp.where(prow == lpos0.astype(jnp.int32), 1.0, jnp.where(prow == lpos1.astype(jnp.int32), 1.0, 0.0))
    xl_ref[...] = jnp.dot(perm.astype(bf16), xn2, preferred_element_type=f32)
    tc_ref[...] = jnp.broadcast_to(tcount, (N_EXPERTS, LANES))
    rowl = lax.broadcasted_iota(jnp.int32, (LANES, tm), 0)
    wslab = jnp.zeros((LANES, tm), f32)
    for r, val in enumerate((w0, w1, lpos0, lpos1)):
        wslab = jnp.where(rowl == r, val, wslab)
    wc_ref[...] = wslab.T


def _merge(x, o, l, po, ln1, w_gates, w_pa, w_pb, w_o, ln2, w_router, b_router, *, tm):
    B, T, _ = x.shape
    assert T % tm == 0
    nt = T // tm
    n = B * T
    dils = tuple(a.shape[1] for a in o)
    assert all(tm % (8 * d) == 0 for d in dils)
    assert 2 * tm // SEG_ALIGN <= 256
    ts = _sorted_tile_rows(tm)
    tri = (jnp.arange(tm)[:, None] < jnp.arange(tm)[None, :]).astype(bf16)
    ltri = (jnp.arange(N_EXPERTS)[None, :] < jnp.arange(N_EXPERTS)[:, None]).astype(bf16)
    rows3 = lambda w: pl.BlockSpec((1, tm, w), lambda b, i: (b, i, 0))
    flat = lambda r, w: pl.BlockSpec((r, w), lambda b, i: (b * nt + i, 0))
    grp = [pl.BlockSpec((1, d, tm // d, GROUP_W), lambda b, i: (b, 0, i, 0)) for d in dils]
    full = lambda a: pl.BlockSpec(a.shape, lambda b, i: (0,) * a.ndim)
    weights = (ln1, w_gates, w_pa, w_pb, w_o, ln2, w_router, b_router, tri, ltri)
    return pl.pallas_call(
        functools.partial(_merge_kernel, tm=tm, ts=ts, dils=dils),
        grid=(B, nt),
        in_specs=[rows3(D_MODEL)] + grp + grp + [rows3(POOL_W)] + [full(a) for a in weights],
        out_specs=[flat(tm, D_MODEL), flat(ts, D_MODEL), flat(tm, LANES),
                   pl.BlockSpec((N_EXPERTS, LANES), lambda b, i: (0, b * nt + i))],
        out_shape=[jax.ShapeDtypeStruct((n, D_MODEL), f32),
                   jax.ShapeDtypeStruct((B * nt * ts, D_MODEL), f32),
                   jax.ShapeDtypeStruct((n, LANES), f32),
                   jax.ShapeDtypeStruct((N_EXPERTS, B * nt * LANES), f32)],
        scratch_shapes=[pltpu.VMEM((IBUF_SLOTS * tm, LANES), f32)],
        compiler_params=_cparams(("arbitrary", "arbitrary")),
        name="merge_router",
    )(x, *o, *l, po, *weights)


def _sorted_tile_rows(tm):
    return -(-(2 * tm + N_EXPERTS * (SEG_ALIGN - 1)) // LANES) * LANES


def _moe_seg_kernel(blk_e_ref, blk_r0_ref, blk_n_ref, seg_g_ref, seg_c_ref, seg_src_ref,
                    xl_hbm, wg_ref, wu_ref, wd_ref, yl_init_hbm, yl_hbm,
                    xbuf, ybuf, sem_in, sem_out, ptr, *, blk, n_tiles, n_blocks):
    del yl_init_hbm
    b = pl.program_id(0)
    slot = lax.rem(b, 2)
    unit_bits = (blk // SEG_ALIGN).bit_length()

    def row_pieces(rows):
        units = lax.shift_right_logical(rows, jnp.int32(SEG_ALIGN.bit_length() - 1))
        off = jnp.int32(0)
        for bit in reversed(range(unit_bits)):
            on = lax.shift_right_logical(units, jnp.int32(bit)) & 1
            yield on == 1, off, SEG_ALIGN << bit
            off = off + on * (SEG_ALIGN << bit)

    def copy_rows(src, src_row, dst, dst_row, rows, sem):
        for on, off, size in row_pieces(rows):
            @pl.when(on)
            def _():
                pltpu.make_async_copy(src.at[pl.ds(pl.multiple_of(src_row + off, SEG_ALIGN), size)],
                                      dst.at[pl.ds(pl.multiple_of(dst_row + off, SEG_ALIGN), size)], sem).start()

    def wait_rows(src, dst, rows, sem):
        for on, _, size in row_pieces(rows):
            @pl.when(on)
            def _():
                pltpu.make_async_copy(src.at[pl.ds(0, size)], dst.at[pl.ds(0, size)], sem).wait()

    def for_pieces(bb, stream, fn):
        e, r0, n = blk_e_ref[bb], blk_r0_ref[bb], blk_n_ref[bb]

        @pl.when(n > 0)
        def _():
            def seg(i):
                return jnp.minimum(i, n_tiles - 1) * N_EXPERTS + e

            def cond(i):
                return (i < n_tiles) & (seg_g_ref[seg(i)] < r0 + n)

            def body(i):
                g = seg_g_ref[seg(i)]
                lo = jnp.maximum(g, r0)
                hi = jnp.minimum(g + seg_c_ref[seg(i)], r0 + n)

                @pl.when(hi > lo)
                def _():
                    fn(seg_src_ref[seg(i)] + (lo - g), lo - r0, hi - lo)

                return i + 1

            end = lax.while_loop(cond, body, jnp.where(r0 == 0, 0, ptr[stream]))
            ptr[stream] = jnp.maximum(end - 1, 0)

    def gather(bb):
        s = lax.rem(bb, 2)
        for_pieces(bb, 0, lambda lrow, brow, rows: copy_rows(xl_hbm, lrow, xbuf.at[s], brow, rows, sem_in.at[s]))

    def scatter(bb):
        s = lax.rem(bb, 2)
        for_pieces(bb, 1, lambda lrow, brow, rows: copy_rows(ybuf.at[s], brow, yl_hbm, lrow, rows, sem_out.at[s]))

    @pl.when(b == 0)
    def _():
        xbuf[...] = jnp.zeros_like(xbuf)
        ptr[0] = 0
        ptr[1] = 0
        gather(0)

    n_b = blk_n_ref[b]
    wait_rows(xl_hbm, xbuf.at[slot], n_b, sem_in.at[slot])

    @pl.when(b + 1 < n_blocks)
    def _():
        gather(b + 1)

    @pl.when(b >= 2)
    def _():
        wait_rows(ybuf.at[slot], yl_hbm, blk_n_ref[b - 2], sem_out.at[slot])

    @pl.when(n_b > 0)
    def _():
        x = xbuf[slot].astype(bf16)
        hid = jax.nn.silu(jnp.dot(x, wg_ref[0], preferred_element_type=f32)) * jnp.dot(x, wu_ref[0], preferred_element_type=f32)
        ybuf[slot] = jnp.dot(hid.astype(bf16), wd_ref[0], preferred_element_type=f32)
        scatter(b)

    @pl.when(b == n_blocks - 1)
    def _():
        @pl.when(b >= 1)
        def _():
            wait_rows(ybuf.at[1 - slot], yl_hbm, blk_n_ref[b - 1], sem_out.at[1 - slot])

        wait_rows(ybuf.at[slot], yl_hbm, n_b, sem_out.at[slot])


def _moe_segments(xl, tables, w_gate, w_up, w_down, *, blk, n_tiles):
    blk_e = tables[0]
    n_blocks = blk_e.shape[0]
    wspec = lambda shape: pl.BlockSpec((1,) + shape, lambda b, be, *_: (be[b], 0, 0))
    hbm = pl.BlockSpec(memory_space=pl.ANY)
    return pl.pallas_call(
        functools.partial(_moe_seg_kernel, blk=blk, n_tiles=n_tiles, n_blocks=n_blocks),
        grid_spec=pltpu.PrefetchScalarGridSpec(
            num_scalar_prefetch=len(tables),
            grid=(n_blocks,),
            in_specs=[hbm, wspec((D_MODEL, D_EXPERT)), wspec((D_MODEL, D_EXPERT)), wspec((D_EXPERT, D_MODEL)), hbm],
            out_specs=hbm,
            scratch_shapes=[pltpu.VMEM((2, blk, D_MODEL), f32), pltpu.VMEM((2, blk, D_MODEL), f32),
                            pltpu.SemaphoreType.DMA((2,)), pltpu.SemaphoreType.DMA((2,)),
                            pltpu.SMEM((2,), jnp.int32)],
        ),
        out_shape=jax.ShapeDtypeStruct(xl.shape, f32),
        input_output_aliases={len(tables) + 4: 0},
        compiler_params=_cparams(("arbitrary",)),
        name="moe_experts",
    )(*tables, xl, w_gate, w_up, w_down, jnp.zeros(xl.shape, f32))


def _unsort_kernel(h_ref, wc_ref, yl_ref, y_ref, *, ts):
    w = wc_ref[...]
    yl = yl_ref[...].astype(bf16)
    col = lax.broadcasted_iota(jnp.int32, (w.shape[0], ts), 1)
    y = h_ref[...]
    for k in range(2):
        pick = jnp.where(col == w[:, 2 + k:3 + k].astype(jnp.int32), 1.0, 0.0).astype(bf16)
        y = y + w[:, k:k + 1] * jnp.dot(pick, yl, preferred_element_type=f32)
    y_ref[...] = y


def _unsort(h, wc, yl, *, tm, ts):
    n = h.shape[0]
    rows = lambda r, w: pl.BlockSpec((r, w), lambda i: (i, 0))
    return pl.pallas_call(
        functools.partial(_unsort_kernel, ts=ts),
        grid=(n // tm,),
        in_specs=[rows(tm, D_MODEL), rows(tm, LANES), rows(ts, D_MODEL)],
        out_specs=rows(tm, D_MODEL),
        out_shape=jax.ShapeDtypeStruct((n, D_MODEL), f32),
        compiler_params=_cparams(("arbitrary",)),
        name="moe_unsort",
    )(h, wc, yl)


def _mix_and_moe(x, o, l, po, wts, *, tm, blk):
    n = x.shape[0] * x.shape[1]
    n_tiles = n // tm
    ts = _sorted_tile_rows(tm)
    h, xl, wc, tc = _merge(x, o, l, po, wts['ln1'], wts['w_gates'], wts['w_pa'], wts['w_pb'], wts['w_o'],
                           wts['ln2'], wts['w_router'], wts['b_router'], tm=tm)
    c8 = (tc[:, ::LANES].T.astype(jnp.int32) + (SEG_ALIGN - 1)) // SEG_ALIGN * SEG_ALIGN
    seg_src = jnp.arange(n_tiles, dtype=jnp.int32)[:, None] * ts + jnp.cumsum(c8, axis=1) - c8
    seg_g = jnp.cumsum(c8, axis=0) - c8
    tot = jnp.sum(c8, axis=0)
    padded = (tot + blk - 1) // blk * blk
    pad_ends = jnp.cumsum(padded)
    n_blocks = -(-(2 * n + n_tiles * N_EXPERTS * (SEG_ALIGN - 1) + N_EXPERTS * (blk - 1)) // blk)
    blk_start = jnp.arange(n_blocks, dtype=jnp.int32) * blk
    blk_e = jnp.minimum(jnp.sum(pad_ends[None, :] <= blk_start[:, None], axis=1), N_EXPERTS - 1).astype(jnp.int32)
    pick = blk_e[:, None] == jnp.arange(N_EXPERTS, dtype=jnp.int32)[None, :]
    blk_r0 = blk_start - jnp.sum(jnp.where(pick, (pad_ends - padded)[None, :], 0), axis=1)
    blk_n = jnp.clip(jnp.sum(jnp.where(pick, tot[None, :], 0), axis=1) - blk_r0, 0, blk)
    tables = tuple(a.astype(jnp.int32).reshape(-1) for a in (blk_e, blk_r0, blk_n, seg_g, c8, seg_src))
    yl = _moe_segments(xl, tables, wts['w_gate'], wts['w_up'], wts['w_down'], blk=blk, n_tiles=n_tiles)
    return _unsort(h, wc, yl, tm=tm, ts=ts)


def kernel(x_prompt, x_sample, cache_k_w128, cache_v_w128, cache_k_w512, cache_v_w512, cache_k_w2048, cache_v_w2048, state_pool, ln1, w_in, q_gain, k_gain, pool_lin, pool_scale, w_pa, w_pb, w_o, ln2, w_rg, b_rg, w_re, b_re, w_gate, w_up, w_down):
    B, T, D = x_prompt.shape
    Bd, S, _ = x_sample.shape
    past_len = 8192
    caches = ((cache_k_w128, cache_v_w128), (cache_k_w512, cache_v_w512), (cache_k_w2048, cache_v_w2048))
    slopes = jnp.exp2(-8.0 * jnp.arange(1, N_HEADS + 1, dtype=f32) / N_HEADS).reshape(N_GROUPS, HEADS_PER_GROUP)

    w_qkvu = w_in[:, :QKVU_W].astype(bf16)
    plin_bd = jnp.zeros((POOL_W, POOL_W), f32)
    for g in range(len(POOL_WINDOWS)):
        plin_bd = plin_bd.at[g * POOL_GW:(g + 1) * POOL_GW, g * POOL_GW:(g + 1) * POOL_GW].set(pool_lin[g])
    w_router = jnp.zeros((D, ROUTER_W), f32).at[:, :N_EXPERT_GROUPS].set(w_rg)
    w_router = w_router.at[:, EXPERT_COL0:EXPERT_COL0 + N_EXPERTS].set(w_re)
    b_router = jnp.zeros((1, ROUTER_W), f32).at[0, :N_EXPERT_GROUPS].set(b_rg)
    b_router = b_router.at[0, EXPERT_COL0:EXPERT_COL0 + N_EXPERTS].set(b_re)
    wts = dict(ln1=ln1.reshape(1, D), w_gates=w_in[:, QKVU_W:].astype(bf16), w_pa=w_pa.astype(bf16),
               w_pb=w_pb.astype(bf16), w_o=w_o.astype(bf16), ln2=ln2.reshape(1, D),
               w_router=w_router.astype(bf16), b_router=b_router,
               w_gate=w_gate.astype(bf16), w_up=w_up.astype(bf16), w_down=w_down.astype(bf16))
    proj_w = (wts['ln1'], w_qkvu, q_gain.reshape(1, ATTN_W), k_gain.reshape(1, ATTN_W),
              plin_bd.astype(bf16), pool_scale.reshape(1, POOL_W))

    *qkv, po, kt, vt, ut = _proj_prompt(x_prompt, *proj_w, tm=512)
    o, l = zip(*[_band_attention(*qkv[3 * g:3 * g + 3], g, slopes[g]) for g in range(N_GROUPS)])
    y_prompt = _mix_and_moe(x_prompt, o, l, po, wts, tm=512, blk=512).reshape(B, T, D)
    tail = kt.shape[2]
    pkv = []
    for g, w in enumerate(WINDOWS):
        for a in (kt, vt):
            a = a.reshape(B, N_HEADS, HEAD_DIM, tail)[:, g * HEADS_PER_GROUP:(g + 1) * HEADS_PER_GROUP, :, tail - w:]
            pkv.append(a.transpose(0, 3, 1, 2))
    p_pool = ut[:, 1:]

    n_s = Bd * S
    xs = x_sample.transpose(1, 0, 2).reshape(n_s, D)
    qs, ks, vs, pos, st = _proj_sample(xs, *proj_w, state_pool.transpose(1, 0, 2),
                                       n_seq=Bd, n_new=S, past_len=past_len)
    so, sl, skv = [], [], []
    pad8 = lambda a: jnp.pad(a.reshape(S, Bd, GROUP_W).transpose(1, 0, 2), ((0, 0), (0, 8 - S), (0, 0)))
    for g, w in enumerate(WINDOWS):
        cols = slice(g * GROUP_W, (g + 1) * GROUP_W)
        kc, vc = caches[g]
        og, lg, ko, vo = _cached_attention(
            pad8(qs[:, cols]), pad8(ks[:, cols]), pad8(vs[:, cols]),
            kc.transpose(0, 2, 3, 1).reshape(Bd, GROUP_W, w), vc.transpose(0, 2, 3, 1).reshape(Bd, GROUP_W, w),
            g, slopes[g], n_new=S)
        so.append(og[:, :S].transpose(1, 0, 2).reshape(1, 1, n_s, GROUP_W).astype(bf16))
        sl.append(lg[:, :S].transpose(1, 0, 2).reshape(1, 1, n_s, GROUP_W))
        for a in (ko, vo):
            skv.append(a.reshape(Bd, HEADS_PER_GROUP, HEAD_DIM, w).transpose(0, 3, 1, 2))
    y_sample = _mix_and_moe(xs[None], so, sl, pos[None], wts, tm=n_s, blk=128).reshape(S, Bd, D).transpose(1, 0, 2)
    s_pool = st.transpose(1, 0, 2)

    return (y_prompt, y_sample, *pkv, p_pool, *skv, s_pool)
```

```python
import functools

import jax
import jax.numpy as jnp
from jax import lax
from jax.experimental import pallas as pl
from jax.experimental.pallas import tpu as pltpu

D_MODEL = 1024
HEAD_DIM = 64
HEADS_PER_GROUP = 4
WINDOWS = (128, 512, 2048)
DILATIONS = (1, 4, 16)
N_GROUPS = len(WINDOWS)
N_HEADS = HEADS_PER_GROUP * N_GROUPS
ATTN_W = N_HEADS * HEAD_DIM
GROUP_W = HEADS_PER_GROUP * HEAD_DIM
BAND = 128
POOL_WINDOWS = (2, 4, 8, 16)
POOL_GW = 128
POOL_W = len(POOL_WINDOWS) * POOL_GW
POOL_STATE = max(POOL_WINDOWS) - 1
POOL_HIST = 32
assert POOL_WINDOWS == (2, 4, 8, 16)
N_EXPERT_GROUPS = 4
EXPERTS_PER_GROUP = 8
N_EXPERTS = N_EXPERT_GROUPS * EXPERTS_PER_GROUP
D_EXPERT = 512
QKVU_W = 3 * ATTN_W + POOL_W
NEG = -1e30
EPS = 1e-6
LANES = 128
ROUTER_W = LANES
EXPERT_COL0 = 8
VMEM_LIMIT = 56 * 1024 * 1024
IBUF_SLOTS = 2 * (N_GROUPS - 1) * (GROUP_W // LANES)
SEG_ALIGN = 8

assert all(w // d == BAND for w, d in zip(WINDOWS, DILATIONS))

f32 = jnp.float32
bf16 = jnp.bfloat16


def _cparams(sem):
    return pltpu.CompilerParams(dimension_semantics=sem, vmem_limit_bytes=VMEM_LIMIT)


def _rmsnorm_rows(x, g):
    ms = jnp.mean(x * x, axis=-1, keepdims=True)
    return x * lax.rsqrt(ms + EPS) * g


def _head_rmsnorm_chunk(ch, gain):
    lane = lax.broadcasted_iota(jnp.int32, ch.shape, 1)
    lo_mask = lane < HEAD_DIM
    sq = ch * ch
    lo = jnp.sum(jnp.where(lo_mask, sq, 0.0), axis=-1, keepdims=True)
    hi = jnp.sum(jnp.where(lo_mask, 0.0, sq), axis=-1, keepdims=True)
    ss = jnp.where(lo_mask, lo, hi)
    return ch * lax.rsqrt(ss * (1.0 / HEAD_DIM) + EPS) * gain


def _project(x, ln1, w_ref, qg, kg):
    xn = _rmsnorm_rows(x, ln1).astype(bf16)
    nch = ATTN_W // LANES
    qs = qg * (HEAD_DIM ** -0.5)
    zq = jnp.dot(xn, w_ref[:, 0:ATTN_W], preferred_element_type=f32)
    q = [_head_rmsnorm_chunk(zq[:, c * LANES:(c + 1) * LANES], qs[:, c * LANES:(c + 1) * LANES]) for c in range(nch)]
    zk = jnp.dot(xn, w_ref[:, ATTN_W:2 * ATTN_W], preferred_element_type=f32)
    k = [_head_rmsnorm_chunk(zk[:, c * LANES:(c + 1) * LANES], kg[:, c * LANES:(c + 1) * LANES]) for c in range(nch)]
    zv = jnp.dot(xn, w_ref[:, 2 * ATTN_W:3 * ATTN_W], preferred_element_type=f32)
    v = [zv[:, c * LANES:(c + 1) * LANES] for c in range(nch)]
    u = jnp.dot(xn, w_ref[:, 3 * ATTN_W:3 * ATTN_W + POOL_W], preferred_element_type=f32)
    return q, k, v, u


def _proj_prompt_kernel(x_ref, ln1_ref, w_ref, qg_ref, kg_ref, plin_ref, pscale_ref, *rest,
                        tm, n_tiles, tail_tiles):
    qkv_refs = rest[:3 * N_GROUPS]
    po_ref, kt_ref, vt_ref, ut_ref, ubuf, pa, pb, sbuf = rest[3 * N_GROUPS:]
    i = pl.program_id(1)
    hist = POOL_STATE + 1
    ph, rows = POOL_HIST, POOL_HIST + tm

    @pl.when(i == 0)
    def _():
        ubuf[0:ph, :] = jnp.zeros((ph, POOL_W), f32)

    @pl.when(i > 0)
    def _():
        ubuf[0:ph, :] = ubuf[tm:tm + ph, :]

    q, k, v, u = _project(x_ref[0], ln1_ref[...], w_ref, qg_ref[...], kg_ref[...])
    cpg = GROUP_W // LANES
    slot = 0
    for t, chunks in enumerate((q, k, v)):
        for g, dil in enumerate(DILATIONS):
            out_ref = qkv_refs[3 * g + t]
            for c in range(cpg):
                val = chunks[g * cpg + c]
                cols = slice(c * LANES, (c + 1) * LANES)
                if dil == 1:
                    out_ref[0, 0, :, cols] = val.astype(bf16)
                else:
                    sbuf[slot * tm:(slot + 1) * tm, :] = val
                    for r in range(dil):
                        out_ref[0, r, :, cols] = sbuf[pl.ds(slot * tm + r, tm // dil, stride=dil), :].astype(bf16)
                    slot += 1

    ubuf[ph:rows, :] = u
    gw = POOL_GW
    pa[8:rows, :] = ubuf[8:rows, :] + ubuf[7:rows - 1, :]
    pb[16:rows, gw:] = pa[16:rows, gw:] + pa[14:rows - 2, gw:]
    pa[24:rows, 2 * gw:] = pb[24:rows, 2 * gw:] + pb[20:rows - 4, 2 * gw:]
    pb[32:rows, 3 * gw:] = pa[32:rows, 3 * gw:] + pa[24:rows - 8, 3 * gw:]
    pos = i * tm + lax.broadcasted_iota(jnp.int32, (tm, POOL_GW), 0)
    zs = []
    for g, w in enumerate(POOL_WINDOWS):
        cols = slice(g * POOL_GW, (g + 1) * POOL_GW)
        wsum = (pa if g % 2 == 0 else pb)[ph:rows, cols]
        cnt = jnp.minimum(pos + 1, w).astype(f32)
        zs.append(wsum / cnt - u[:, cols])
    z = jnp.concatenate(zs, axis=-1).astype(bf16)
    po = jnp.dot(z, plin_ref[...], preferred_element_type=f32) * pscale_ref[...]
    po_ref[0] = po.astype(bf16)

    @pl.when(i >= n_tiles - tail_tiles)
    def _():
        kt_ref[0] = jnp.concatenate(k, axis=-1).T
        vt_ref[0] = jnp.concatenate(v, axis=-1).T

    @pl.when(i == n_tiles - 1)
    def _():
        ut_ref[0] = ubuf[rows - hist:rows, :]


def _proj_prompt(x, ln1, w_qkvu, qg, kg, plin_bd, pscale, *, tm):
    B, T, D = x.shape
    n_tiles = T // tm
    tail = max(WINDOWS)
    assert T % tm == 0 and tail % tm == 0 and T >= tail
    tail_tiles = tail // tm
    hist = POOL_STATE + 1
    kern = functools.partial(_proj_prompt_kernel, tm=tm, n_tiles=n_tiles, tail_tiles=tail_tiles)
    const = lambda b, i: (0, 0)
    assert all(tm % (16 * d) == 0 for d in DILATIONS)
    qkv_specs = [pl.BlockSpec((1, d, tm // d, GROUP_W), lambda b, i: (b, 0, i, 0)) for d in DILATIONS for _ in range(3)]
    qkv_shapes = [jax.ShapeDtypeStruct((B, d, T // d, GROUP_W), bf16) for d in DILATIONS for _ in range(3)]
    tail_spec = pl.BlockSpec((1, ATTN_W, tm), lambda b, i: (b, 0, jnp.maximum(i - (n_tiles - tail_tiles), 0)))
    return pl.pallas_call(
        kern,
        grid=(B, n_tiles),
        in_specs=[
            pl.BlockSpec((1, tm, D), lambda b, i: (b, i, 0)),
            pl.BlockSpec((1, D), const),
            pl.BlockSpec((D, QKVU_W), const),
            pl.BlockSpec((1, ATTN_W), const),
            pl.BlockSpec((1, ATTN_W), const),
            pl.BlockSpec((POOL_W, POOL_W), const),
            pl.BlockSpec((1, POOL_W), const),
        ],
        out_specs=qkv_specs
        + [pl.BlockSpec((1, tm, POOL_W), lambda b, i: (b, i, 0)),
           tail_spec, tail_spec,
           pl.BlockSpec((1, hist, POOL_W), lambda b, i: (b, 0, 0))],
        out_shape=qkv_shapes
        + [jax.ShapeDtypeStruct((B, T, POOL_W), bf16),
           jax.ShapeDtypeStruct((B, ATTN_W, tail), f32),
           jax.ShapeDtypeStruct((B, ATTN_W, tail), f32),
           jax.ShapeDtypeStruct((B, hist, POOL_W), f32)],
        scratch_shapes=[pltpu.VMEM((POOL_HIST + tm, POOL_W), f32)] * 3
                       + [pltpu.VMEM((3 * (N_GROUPS - 1) * (GROUP_W // LANES) * tm, LANES), f32)],
        compiler_params=_cparams(("arbitrary", "arbitrary")),
        name="proj_prompt",
    )(x, ln1, w_qkvu, qg, kg, plin_bd, pscale)


def _proj_sample_kernel(x_ref, ln1_ref, w_ref, qg_ref, kg_ref, plin_ref, pscale_ref, state_ref,
                        q_ref, k_ref, v_ref, po_ref, st_ref, *, n_seq, n_new, past_len):
    q, k, v, u = _project(x_ref[...], ln1_ref[...], w_ref, qg_ref[...], kg_ref[...])
    q_ref[...] = jnp.concatenate(q, axis=-1)
    k_ref[...] = jnp.concatenate(k, axis=-1)
    v_ref[...] = jnp.concatenate(v, axis=-1)
    ext = [state_ref[j] for j in range(POOL_STATE)] + [u[s * n_seq:(s + 1) * n_seq, :] for s in range(n_new)]
    for s in range(n_new):
        zs = []
        for g, w in enumerate(POOL_WINDOWS):
            cols = slice(g * POOL_GW, (g + 1) * POOL_GW)
            cur = ext[POOL_STATE + s][:, cols]
            acc = cur
            for j in range(1, w):
                acc = acc + ext[POOL_STATE + s - j][:, cols]
            cnt = float(min(past_len + s + 1, w))
            zs.append(acc / cnt - cur)
        z = jnp.concatenate(zs, axis=-1).astype(bf16)
        po = jnp.dot(z, plin_ref[...], preferred_element_type=f32) * pscale_ref[...]
        po_ref[s * n_seq:(s + 1) * n_seq, :] = po.astype(bf16)
    for j in range(POOL_STATE):
        st_ref[j] = ext[j + n_new]


def _proj_sample(x, ln1, w_qkvu, qg, kg, plin_bd, pscale, state, *, n_seq, n_new, past_len):
    n = n_seq * n_new
    kern = functools.partial(_proj_sample_kernel, n_seq=n_seq, n_new=n_new, past_len=past_len)
    return pl.pallas_call(
        kern,
        out_shape=[jax.ShapeDtypeStruct((n, ATTN_W), f32)] * 3
        + [jax.ShapeDtypeStruct((n, POOL_W), bf16),
           jax.ShapeDtypeStruct((POOL_STATE, n_seq, POOL_W), f32)],
        compiler_params=pltpu.CompilerParams(vmem_limit_bytes=VMEM_LIMIT),
        name="proj_sample",
    )(x, ln1, w_qkvu, qg, kg, plin_bd, pscale, state)


def _head_masks(shape):
    lane = lax.broadcasted_iota(jnp.int32, shape, len(shape) - 1)
    return [(lane >= h * HEAD_DIM) & (lane < (h + 1) * HEAD_DIM) for h in range(HEADS_PER_GROUP)]


def _band_attn_kernel(q_ref, kc_ref, kp_ref, vc_ref, vp_ref, bias_ref, o_ref, l_ref, kbuf, vbuf, *, tl, unroll):
    i = pl.program_id(2)
    kbuf[0:BAND, :] = kp_ref[0, 0]
    kbuf[BAND:2 * BAND, :] = kc_ref[0, 0, 0:BAND, :]
    vbuf[0:BAND, :] = vp_ref[0, 0]
    vbuf[BAND:2 * BAND, :] = vc_ref[0, 0, 0:BAND, :]
    masks = _head_masks((BAND, GROUP_W))

    def sub_block(j, kk, vv, var):
        r0 = j * BAND if isinstance(j, int) else pl.multiple_of(j * BAND, BAND)
        q = q_ref[0, 0, pl.ds(r0, BAND), :]
        qm = jnp.concatenate([jnp.where(m, q, jnp.zeros_like(q)) for m in masks], axis=0)
        s = lax.dot_general(qm, kk, (((1,), (1,)), ((), ())), preferred_element_type=f32)
        s = s + bias_ref[var]
        m = jnp.max(s, axis=-1, keepdims=True)
        p = jnp.exp(s - m)
        den = jnp.sum(p, axis=-1, keepdims=True)
        pv = jnp.dot(p.astype(bf16), vv, preferred_element_type=f32)
        o = jnp.zeros((BAND, GROUP_W), f32)
        ms = jnp.zeros((BAND, GROUP_W), f32)
        ds = jnp.ones((BAND, GROUP_W), f32)
        for h, msk in enumerate(masks):
            rows = slice(h * BAND, (h + 1) * BAND)
            o = jnp.where(msk, pv[rows], o)
            ms = jnp.where(msk, m[rows], ms)
            ds = jnp.where(msk, den[rows], ds)
        o_ref[0, 0, pl.ds(r0, BAND), :] = (o / ds).astype(bf16)
        l_ref[0, 0, pl.ds(r0, BAND), :] = ms + jnp.log(ds)

    sub_block(0, kbuf[...], vbuf[...], jnp.where(i == 0, 0, 1))

    def body(j, carry):
        k0 = pl.multiple_of((j - 1) * BAND, BAND)
        sub_block(j, kc_ref[0, 0, pl.ds(k0, 2 * BAND), :], vc_ref[0, 0, pl.ds(k0, 2 * BAND), :], 1)
        return carry

    if tl > BAND:
        lax.fori_loop(1, tl // BAND, body, 0, unroll=unroll)


def _band_bias(slopes_g, dil):
    qi = jnp.arange(BAND)[:, None]
    kb = jnp.arange(2 * BAND)[None, :]
    rel = qi + BAND - kb
    valid = (rel >= 0) & (rel <= BAND)
    alibi = -slopes_g[:, None, None] * (dil * rel)[None].astype(f32)
    variants = []
    for first in (True, False):
        ok = valid & (kb >= BAND) if first else valid
        variants.append(jnp.where(ok[None], alibi, NEG).reshape(HEADS_PER_GROUP * BAND, 2 * BAND))
    return jnp.stack(variants, axis=0)


def _band_attention(q, k, v, g, slopes_g, *, tl_max=1024, unroll=7):
    B, dil, L, _ = q.shape
    tl = min(tl_max, L)
    assert dil == DILATIONS[g] and L % tl == 0 and tl % BAND == 0
    nsub = tl // BAND
    bias = _band_bias(slopes_g, dil)
    cur = pl.BlockSpec((1, 1, tl, GROUP_W), lambda b, r, i: (b, r, i, 0))
    prev = pl.BlockSpec((1, 1, BAND, GROUP_W), lambda b, r, i: (b, r, jnp.maximum(i * nsub - 1, 0), 0))
    return pl.pallas_call(
        functools.partial(_band_attn_kernel, tl=tl, unroll=max(1, min(unroll, nsub - 1))),
        grid=(B, dil, L // tl),
        in_specs=[cur, cur, prev, cur, prev,
                  pl.BlockSpec((2, HEADS_PER_GROUP * BAND, 2 * BAND), lambda b, r, i: (0, 0, 0))],
        out_specs=[cur, cur],
        out_shape=[jax.ShapeDtypeStruct((B, dil, L, GROUP_W), bf16),
                   jax.ShapeDtypeStruct((B, dil, L, GROUP_W), f32)],
        scratch_shapes=[pltpu.VMEM((2 * BAND, GROUP_W), bf16), pltpu.VMEM((2 * BAND, GROUP_W), bf16)],
        compiler_params=_cparams(("arbitrary", "arbitrary", "arbitrary")),
        name="band_attn_g%d" % g,
    )(q, k, k, v, v, bias)


def _cached_attn_kernel(q_ref, kn_ref, vn_ref, kc_ref, vc_ref, bc_ref, bn_ref,
                        o_ref, l_ref, ko_ref, vo_ref, *, nb, n_new, win):
    masks8 = _head_masks((nb, 8, GROUP_W))
    q8, kn8, vn8 = q_ref[...], kn_ref[...], vn_ref[...]
    kc, vc = kc_ref[...], vc_ref[...]
    qm = jnp.concatenate([jnp.where(m, q8, 0.0) for m in masks8], axis=1)
    sc = jnp.einsum('bqd,bdk->bqk', qm.astype(bf16), kc.astype(bf16), preferred_element_type=f32) + bc_ref[...]
    m = jnp.max(sc, axis=-1, keepdims=True)
    sn = []
    for t in range(n_new):
        col = jnp.sum(qm * kn8[:, t:t + 1, :], axis=-1, keepdims=True) + bn_ref[:, t:t + 1]
        sn.append(col)
        m = jnp.maximum(m, col)
    pc = jnp.exp(sc - m)
    den = jnp.sum(pc, axis=-1, keepdims=True)
    acc = jnp.einsum('bqk,bdk->bqd', pc.astype(bf16), vc.astype(bf16), preferred_element_type=f32)
    for t in range(n_new):
        pn = jnp.exp(sn[t] - m)
        den = den + pn
        acc = acc + pn * vn8[:, t:t + 1, :]
    acc = acc / den
    lse = m + jnp.log(den)
    o = jnp.zeros((nb, 8, GROUP_W), f32)
    l = jnp.zeros((nb, 8, GROUP_W), f32)
    for h, msk in enumerate(masks8):
        o = jnp.where(msk, acc[:, h * 8:(h + 1) * 8, :], o)
        l = jnp.where(msk, lse[:, h * 8:(h + 1) * 8, :], l)
    o_ref[...] = o
    l_ref[...] = l
    lane_t = lax.broadcasted_iota(jnp.int32, (nb, GROUP_W, LANES), 2)
    for cache, new8, out_ref in ((kc, kn8, ko_ref), (vc, vn8, vo_ref)):
        rolled = pltpu.roll(cache, win - n_new, axis=2)
        out_ref[...] = rolled
        new_t = jnp.swapaxes(jnp.concatenate([new8, jnp.zeros((nb, LANES - 8, GROUP_W), f32)], axis=1), 1, 2)
        new_t = pltpu.roll(new_t, LANES - n_new, axis=2)
        out_ref[:, :, win - LANES:win] = jnp.where(lane_t >= LANES - n_new, new_t, rolled[:, :, win - LANES:win])


def _cached_bias(slopes_g, dil, win, n_new):
    s = jnp.arange(8)[:, None]
    i = jnp.arange(win)[None, :]
    dist = win + s - i
    ok = (dist % dil == 0) & (dist // dil <= BAND) & (s < n_new)
    bc = jnp.where(ok[None], -slopes_g[:, None, None] * dist[None].astype(f32), NEG)
    t = jnp.arange(8)[None, :]
    dn = s - t
    okn = (dn >= 0) & (dn % dil == 0) & (dn // dil <= BAND) & (s < n_new) & (t < n_new)
    bn = jnp.where(okn[None], -slopes_g[:, None, None] * dn[None].astype(f32), NEG)
    pad = (s >= n_new)
    bc = jnp.where(pad[None], 0.0, bc)
    bn = jnp.where(pad[None], 0.0, bn)
    return bc.reshape(HEADS_PER_GROUP * 8, win), bn.reshape(HEADS_PER_GROUP * 8, 8)


def _cached_attention(q8, kn8, vn8, kc_t, vc_t, g, slopes_g, *, n_new):
    Bd, _, win = kc_t.shape
    dil = DILATIONS[g]
    assert win == WINDOWS[g] and win % LANES == 0
    nb = max(1, min(Bd, 2048 // win))
    assert Bd % nb == 0
    bc, bn = _cached_bias(slopes_g, dil, win, n_new)
    small = pl.BlockSpec((nb, 8, GROUP_W), lambda b: (b, 0, 0))
    cache = pl.BlockSpec((nb, GROUP_W, win), lambda b: (b, 0, 0))
    return pl.pallas_call(
        functools.partial(_cached_attn_kernel, nb=nb, n_new=n_new, win=win),
        grid=(Bd // nb,),
        in_specs=[small, small, small, cache, cache,
                  pl.BlockSpec(bc.shape, lambda b: (0, 0)), pl.BlockSpec(bn.shape, lambda b: (0, 0))],
        out_specs=[small, small, cache, cache],
        out_shape=[jax.ShapeDtypeStruct((Bd, 8, GROUP_W), f32)] * 2
        + [jax.ShapeDtypeStruct((Bd, GROUP_W, win), f32)] * 2,
        compiler_params=_cparams(("arbitrary",)),
        name="cached_attn_g%d" % g,
    )(q8, kn8, vn8, kc_t, vc_t, bc, bn)


def _merge_kernel(x_ref, o0_ref, o1_ref, o2_ref, l0_ref, l1_ref, l2_ref, po_ref,
                  ln1_ref, wg_ref, wpa_ref, wpb_ref, wo_ref, ln2_ref, wr_ref, br_ref, tri_ref, ltri_ref,
                  h_ref, xl_ref, wc_ref, tc_ref, ibuf, *, tm, ts, dils):
    slots = iter(range(IBUF_SLOTS))

    def token_order(ref, dil):
        if dil == 1:
            return ref[0, 0].astype(f32)
        chunks = []
        for c in range(GROUP_W // LANES):
            base = next(slots) * tm
            for r in range(dil):
                ibuf[pl.ds(base + r, tm // dil, stride=dil), :] = ref[0, r, :, c * LANES:(c + 1) * LANES].astype(f32)
            chunks.append(ibuf[base:base + tm, :])
        return jnp.concatenate(chunks, axis=-1)

    x = x_ref[0]
    xn = _rmsnorm_rows(x, ln1_ref[...]).astype(bf16)
    gates = jnp.dot(xn, wg_ref[...], preferred_element_type=f32)
    l0, l1, l2 = (token_order(r, d) for r, d in zip((l0_ref, l1_ref, l2_ref), dils))
    lm = jnp.maximum(jnp.maximum(l0, l1), l2)
    e0, e1, e2 = jnp.exp(l0 - lm), jnp.exp(l1 - lm), jnp.exp(l2 - lm)
    o0, o1, o2 = (token_order(r, d) for r, d in zip((o0_ref, o1_ref, o2_ref), dils))
    attn = (e0 * o0 + e1 * o1 + e2 * o2) / (e0 + e1 + e2)
    ma = jnp.dot(attn.astype(bf16), wpa_ref[...], preferred_element_type=f32)
    mb = jnp.dot(po_ref[0], wpb_ref[...], preferred_element_type=f32)
    mix = jax.nn.sigmoid(gates[:, :D_MODEL]) * ma + jax.nn.sigmoid(gates[:, D_MODEL:]) * mb
    h = x + jnp.dot(mix.astype(bf16), wo_ref[...], preferred_element_type=f32)
    h_ref[...] = h
    xn2 = _rmsnorm_rows(h, ln2_ref[...]).astype(bf16)

    lt = (jnp.dot(xn2, wr_ref[...], preferred_element_type=f32) + br_ref[...]).T
    row8 = lax.broadcasted_iota(jnp.int32, (8, tm), 0)
    gl = jnp.where(row8 < N_EXPERT_GROUPS, lt[0:8], -jnp.inf)
    gmax = jnp.max(gl, axis=0, keepdims=True)
    gidx = jnp.min(jnp.where(gl == gmax, row8, 8), axis=0, keepdims=True)
    pg = 1.0 / jnp.sum(jnp.exp(gl - gmax), axis=0, keepdims=True)
    sel = jnp.zeros((8, tm), f32)
    for g in range(N_EXPERT_GROUPS):
        lo = EXPERT_COL0 + g * EXPERTS_PER_GROUP
        sel = jnp.where(gidx == g, lt[lo:lo + EXPERTS_PER_GROUP], sel)
    v0 = jnp.max(sel, axis=0, keepdims=True)
    i0 = jnp.min(jnp.where(sel == v0, row8, 8), axis=0, keepdims=True)
    sel2 = jnp.where(row8 == i0, -jnp.inf, sel)
    v1 = jnp.max(sel2, axis=0, keepdims=True)
    i1 = jnp.min(jnp.where(sel2 == v1, row8, 8), axis=0, keepdims=True)
    t = jnp.exp(v1 - v0)
    w0 = pg / (1.0 + t)
    w1 = pg * t / (1.0 + t)
    eid0 = gidx * EXPERTS_PER_GROUP + i0
    eid1 = gidx * EXPERTS_PER_GROUP + i1
    erow = lax.broadcasted_iota(jnp.int32, (N_EXPERTS, tm), 0)
    oh0 = erow == eid0
    oh1 = erow == eid1
    cnt = jnp.where(oh0, 1.0, jnp.where(oh1, 1.0, 0.0))
    before = jnp.dot(cnt.astype(bf16), tri_ref[...], preferred_element_type=f32)
    tcount = jnp.sum(cnt, axis=1, keepdims=True)
    units = jnp.floor((tcount + (SEG_ALIGN - 1)) * (1.0 / SEG_ALIGN))
    ub = jnp.broadcast_to(units, (N_EXPERTS, LANES)).astype(bf16)
    seg0 = SEG_ALIGN * jnp.dot(ltri_ref[...], ub, preferred_element_type=f32)[:, 0:1]
    pos_e = seg0 + before
    lpos0 = jnp.sum(jnp.where(oh0, pos_e, 0.0), axis=0, keepdims=True)
    lpos1 = jnp.sum(jnp.where(oh1, pos_e, 0.0), axis=0, keepdims=True)
    prow = lax.broadcasted_iota(jnp.int32, (ts, tm), 0)
    perm = jnp.where(prow == lpos0.astype(jnp.int32), 1.0, jnp.where(prow == lpos1.astype(jnp.int32), 1.0, 0.0))
    xl_ref[...] = jnp.dot(perm.astype(bf16), xn2, preferred_element_type=f32)
    tc_ref[...] = jnp.broadcast_to(tcount, (N_EXPERTS, LANES))
    rowl = lax.broadcasted_iota(jnp.int32, (LANES, tm), 0)
    wslab = jnp.zeros((LANES, tm), f32)
    for r, val in enumerate((w0, w1, lpos0, lpos1)):
        wslab = jnp.where(rowl == r, val, wslab)
    wc_ref[...] = wslab.T


def _merge(x, o, l, po, ln1, w_gates, w_pa, w_pb, w_o, ln2, w_router, b_router, *, tm):
    B, T, _ = x.shape
    assert T % tm == 0
    nt = T // tm
    n = B * T
    dils = tuple(a.shape[1] for a in o)
    assert all(tm % (8 * d) == 0 for d in dils)
    assert 2 * tm // SEG_ALIGN <= 256
    ts = _sorted_tile_rows(tm)
    tri = (jnp.arange(tm)[:, None] < jnp.arange(tm)[None, :]).astype(bf16)
    ltri = (jnp.arange(N_EXPERTS)[None, :] < jnp.arange(N_EXPERTS)[:, None]).astype(bf16)
    rows3 = lambda w: pl.BlockSpec((1, tm, w), lambda b, i: (b, i, 0))
    flat = lambda r, w: pl.BlockSpec((r, w), lambda b, i: (b * nt + i, 0))
    grp = [pl.BlockSpec((1, d, tm // d, GROUP_W), lambda b, i: (b, 0, i, 0)) for d in dils]
    full = lambda a: pl.BlockSpec(a.shape, lambda b, i: (0,) * a.ndim)
    weights = (ln1, w_gates, w_pa, w_pb, w_o, ln2, w_router, b_router, tri, ltri)
    return pl.pallas_call(
        functools.partial(_merge_kernel, tm=tm, ts=ts, dils=dils),
        grid=(B, nt),
        in_specs=[rows3(D_MODEL)] + grp + grp + [rows3(POOL_W)] + [full(a) for a in weights],
        out_specs=[flat(tm, D_MODEL), flat(ts, D_MODEL), flat(tm, LANES),
                   pl.BlockSpec((N_EXPERTS, LANES), lambda b, i: (0, b * nt + i))],
        out_shape=[jax.ShapeDtypeStruct((n, D_MODEL), f32),
                   jax.ShapeDtypeStruct((B * nt * ts, D_MODEL), f32),
                   jax.ShapeDtypeStruct((n, LANES), f32),
                   jax.ShapeDtypeStruct((N_EXPERTS, B * nt * LANES), f32)],
        scratch_shapes=[pltpu.VMEM((IBUF_SLOTS * tm, LANES), f32)],
        compiler_params=_cparams(("arbitrary", "arbitrary")),
        name="merge_router",
    )(x, *o, *l, po, *weights)


def _sorted_tile_rows(tm):
    return -(-(2 * tm + N_EXPERTS * (SEG_ALIGN - 1)) // LANES) * LANES


def _moe_seg_kernel(blk_e_ref, blk_r0_ref, blk_n_ref, seg_g_ref, seg_c_ref, seg_src_ref, used_ref,
                    xl_hbm, wg_ref, wu_ref, wd_ref, yl_hbm,
                    xbuf, ybuf, sem_in, sem_out, sem_zero, ptr, *, blk, ts, n_tiles, n_blocks):
    b = pl.program_id(0)
    slot = lax.rem(b, 2)
    unit_bits = (blk // SEG_ALIGN).bit_length()

    def row_pieces(rows):
        units = lax.shift_right_logical(rows, jnp.int32(SEG_ALIGN.bit_length() - 1))
        off = jnp.int32(0)
        for bit in reversed(range(unit_bits)):
            on = lax.shift_right_logical(units, jnp.int32(bit)) & 1
            yield on == 1, off, SEG_ALIGN << bit
            off = off + on * (SEG_ALIGN << bit)

    def copy_rows(src, src_row, dst, dst_row, rows, sem):
        for on, off, size in row_pieces(rows):
            @pl.when(on)
            def _():
                pltpu.make_async_copy(src.at[pl.ds(pl.multiple_of(src_row + off, SEG_ALIGN), size)],
                                      dst.at[pl.ds(pl.multiple_of(dst_row + off, SEG_ALIGN), size)], sem).start()

    def wait_rows(src, dst, rows, sem):
        for on, _, size in row_pieces(rows):
            @pl.when(on)
            def _():
                pltpu.make_async_copy(src.at[pl.ds(0, size)], dst.at[pl.ds(0, size)], sem).wait()

    def for_pieces(bb, stream, fn):
        e, r0, n = blk_e_ref[bb], blk_r0_ref[bb], blk_n_ref[bb]

        @pl.when(n > 0)
        def _():
            def seg(i):
                return jnp.minimum(i, n_tiles - 1) * N_EXPERTS + e

            def cond(i):
                return (i < n_tiles) & (seg_g_ref[seg(i)] < r0 + n)

            def body(i):
                g = seg_g_ref[seg(i)]
                lo = jnp.maximum(g, r0)
                hi = jnp.minimum(g + seg_c_ref[seg(i)], r0 + n)

                @pl.when(hi > lo)
                def _():
                    fn(seg_src_ref[seg(i)] + (lo - g), lo - r0, hi - lo)

                return i + 1

            end = lax.while_loop(cond, body, jnp.where(r0 == 0, 0, ptr[stream]))
            ptr[stream] = jnp.maximum(end - 1, 0)

    def gather(bb):
        s = lax.rem(bb, 2)
        for_pieces(bb, 0, lambda lrow, brow, rows: copy_rows(xl_hbm, lrow, xbuf.at[s], brow, rows, sem_in.at[s]))

    def scatter(bb):
        s = lax.rem(bb, 2)
        for_pieces(bb, 1, lambda lrow, brow, rows: copy_rows(ybuf.at[s], brow, yl_hbm, lrow, rows, sem_out.at[s]))

    def zero_tail(i, go):
        row0 = i * ts + used_ref[i]
        rows = ts - used_ref[i]
        whole = lax.shift_right_logical(rows, jnp.int32(blk.bit_length() - 1))
        zsrc = xbuf.at[1]

        def whole_block(j, c):
            cp = pltpu.make_async_copy(zsrc, yl_hbm.at[pl.ds(pl.multiple_of(row0 + j * blk, SEG_ALIGN), blk)], sem_zero)
            cp.start() if go else cp.wait()
            return c

        lax.fori_loop(0, whole, whole_block, 0)
        rest = rows - whole * blk
        if go:
            copy_rows(zsrc, 0, yl_hbm, row0 + whole * blk, rest, sem_zero)
        else:
            wait_rows(zsrc, yl_hbm, rest, sem_zero)

    @pl.when(b == 0)
    def _():
        xbuf[...] = jnp.zeros_like(xbuf)
        ptr[0] = 0
        ptr[1] = 0
        gather(0)
        for go in (True, False):
            def per_tile(i, c, go=go):
                zero_tail(i, go)
                return c

            lax.fori_loop(0, n_tiles, per_tile, 0)

    n_b = blk_n_ref[b]
    wait_rows(xl_hbm, xbuf.at[slot], n_b, sem_in.at[slot])

    @pl.when(b + 1 < n_blocks)
    def _():
        gather(b + 1)

    @pl.when(b >= 2)
    def _():
        wait_rows(ybuf.at[slot], yl_hbm, blk_n_ref[b - 2], sem_out.at[slot])

    @pl.when(n_b > 0)
    def _():
        x = xbuf[slot].astype(bf16)
        hid = jax.nn.silu(jnp.dot(x, wg_ref[0], preferred_element_type=f32)) * jnp.dot(x, wu_ref[0], preferred_element_type=f32)
        ybuf[slot] = jnp.dot(hid.astype(bf16), wd_ref[0], preferred_element_type=f32)
        scatter(b)

    @pl.when(b == n_blocks - 1)
    def _():
        @pl.when(b >= 1)
        def _():
            wait_rows(ybuf.at[1 - slot], yl_hbm, blk_n_ref[b - 1], sem_out.at[1 - slot])

        wait_rows(ybuf.at[slot], yl_hbm, n_b, sem_out.at[slot])


def _moe_segments(xl, tables, w_gate, w_up, w_down, *, blk, ts, n_tiles):
    blk_e = tables[0]
    n_blocks = blk_e.shape[0]
    assert blk & (blk - 1) == 0 and blk % SEG_ALIGN == 0
    wspec = lambda shape: pl.BlockSpec((1,) + shape, lambda b, be, *_: (be[b], 0, 0))
    hbm = pl.BlockSpec(memory_space=pl.ANY)
    return pl.pallas_call(
        functools.partial(_moe_seg_kernel, blk=blk, ts=ts, n_tiles=n_tiles, n_blocks=n_blocks),
        grid_spec=pltpu.PrefetchScalarGridSpec(
            num_scalar_prefetch=len(tables),
            grid=(n_blocks,),
            in_specs=[hbm, wspec((D_MODEL, D_EXPERT)), wspec((D_MODEL, D_EXPERT)), wspec((D_EXPERT, D_MODEL))],
            out_specs=hbm,
            scratch_shapes=[pltpu.VMEM((2, blk, D_MODEL), f32), pltpu.VMEM((2, blk, D_MODEL), f32),
                            pltpu.SemaphoreType.DMA((2,)), pltpu.SemaphoreType.DMA((2,)), pltpu.SemaphoreType.DMA,
                            pltpu.SMEM((2,), jnp.int32)],
        ),
        out_shape=jax.ShapeDtypeStruct(xl.shape, f32),
        compiler_params=_cparams(("arbitrary",)),
        name="moe_experts",
    )(*tables, xl, w_gate, w_up, w_down)


def _unsort_kernel(h_ref, wc_ref, yl_ref, y_ref, *, ts):
    w = wc_ref[...]
    yl = yl_ref[...].astype(bf16)
    col = lax.broadcasted_iota(jnp.int32, (w.shape[0], ts), 1)
    y = h_ref[...]
    for k in range(2):
        pick = jnp.where(col == w[:, 2 + k:3 + k].astype(jnp.int32), 1.0, 0.0).astype(bf16)
        y = y + w[:, k:k + 1] * jnp.dot(pick, yl, preferred_element_type=f32)
    y_ref[...] = y


def _unsort(h, wc, yl, *, tm, ts):
    n = h.shape[0]
    rows = lambda r, w: pl.BlockSpec((r, w), lambda i: (i, 0))
    return pl.pallas_call(
        functools.partial(_unsort_kernel, ts=ts),
        grid=(n // tm,),
        in_specs=[rows(tm, D_MODEL), rows(tm, LANES), rows(ts, D_MODEL)],
        out_specs=rows(tm, D_MODEL),
        out_shape=jax.ShapeDtypeStruct((n, D_MODEL), f32),
        compiler_params=_cparams(("arbitrary",)),
        name="moe_unsort",
    )(h, wc, yl)


def _mix_and_moe(x, o, l, po, wts, *, tm, blk):
    n = x.shape[0] * x.shape[1]
    n_tiles = n // tm
    ts = _sorted_tile_rows(tm)
    h, xl, wc, tc = _merge(x, o, l, po, wts['ln1'], wts['w_gates'], wts['w_pa'], wts['w_pb'], wts['w_o'],
                           wts['ln2'], wts['w_router'], wts['b_router'], tm=tm)
    c8 = (tc[:, ::LANES].T.astype(jnp.int32) + (SEG_ALIGN - 1)) // SEG_ALIGN * SEG_ALIGN
    seg_src = jnp.arange(n_tiles, dtype=jnp.int32)[:, None] * ts + jnp.cumsum(c8, axis=1) - c8
    seg_g = jnp.cumsum(c8, axis=0) - c8
    tot = jnp.sum(c8, axis=0)
    padded = (tot + blk - 1) // blk * blk
    pad_ends = jnp.cumsum(padded)
    n_blocks = -(-(2 * n + n_tiles * N_EXPERTS * (SEG_ALIGN - 1) + N_EXPERTS * (blk - 1)) // blk)
    blk_start = jnp.arange(n_blocks, dtype=jnp.int32) * blk
    blk_e = jnp.minimum(jnp.sum(pad_ends[None, :] <= blk_start[:, None], axis=1), N_EXPERTS - 1).astype(jnp.int32)
    pick = blk_e[:, None] == jnp.arange(N_EXPERTS, dtype=jnp.int32)[None, :]
    blk_r0 = blk_start - jnp.sum(jnp.where(pick, (pad_ends - padded)[None, :], 0), axis=1)
    blk_n = jnp.clip(jnp.sum(jnp.where(pick, tot[None, :], 0), axis=1) - blk_r0, 0, blk)
    used = jnp.sum(c8, axis=1)
    tables = tuple(a.astype(jnp.int32).reshape(-1) for a in (blk_e, blk_r0, blk_n, seg_g, c8, seg_src, used))
    yl = _moe_segments(xl, tables, wts['w_gate'], wts['w_up'], wts['w_down'], blk=blk, ts=ts, n_tiles=n_tiles)
    return _unsort(h, wc, yl, tm=tm, ts=ts)


def kernel(x_prompt, x_sample, cache_k_w128, cache_v_w128, cache_k_w512, cache_v_w512, cache_k_w2048, cache_v_w2048, state_pool, ln1, w_in, q_gain, k_gain, pool_lin, pool_scale, w_pa, w_pb, w_o, ln2, w_rg, b_rg, w_re, b_re, w_gate, w_up, w_down):
    B, T, D = x_prompt.shape
    Bd, S, _ = x_sample.shape
    past_len = 8192
    caches = ((cache_k_w128, cache_v_w128), (cache_k_w512, cache_v_w512), (cache_k_w2048, cache_v_w2048))
    slopes = jnp.exp2(-8.0 * jnp.arange(1, N_HEADS + 1, dtype=f32) / N_HEADS).reshape(N_GROUPS, HEADS_PER_GROUP)

    w_qkvu = w_in[:, :QKVU_W].astype(bf16)
    plin_bd = jnp.zeros((POOL_W, POOL_W), f32)
    for g in range(len(POOL_WINDOWS)):
        plin_bd = plin_bd.at[g * POOL_GW:(g + 1) * POOL_GW, g * POOL_GW:(g + 1) * POOL_GW].set(pool_lin[g])
    w_router = jnp.zeros((D, ROUTER_W), f32).at[:, :N_EXPERT_GROUPS].set(w_rg)
    w_router = w_router.at[:, EXPERT_COL0:EXPERT_COL0 + N_EXPERTS].set(w_re)
    b_router = jnp.zeros((1, ROUTER_W), f32).at[0, :N_EXPERT_GROUPS].set(b_rg)
    b_router = b_router.at[0, EXPERT_COL0:EXPERT_COL0 + N_EXPERTS].set(b_re)
    wts = dict(ln1=ln1.reshape(1, D), w_gates=w_in[:, QKVU_W:].astype(bf16), w_pa=w_pa.astype(bf16),
               w_pb=w_pb.astype(bf16), w_o=w_o.astype(bf16), ln2=ln2.reshape(1, D),
               w_router=w_router.astype(bf16), b_router=b_router,
               w_gate=w_gate.astype(bf16), w_up=w_up.astype(bf16), w_down=w_down.astype(bf16))
    proj_w = (wts['ln1'], w_qkvu, q_gain.reshape(1, ATTN_W), k_gain.reshape(1, ATTN_W),
              plin_bd.astype(bf16), pool_scale.reshape(1, POOL_W))

    *qkv, po, kt, vt, ut = _proj_prompt(x_prompt, *proj_w, tm=512)
    o, l = zip(*[_band_attention(*qkv[3 * g:3 * g + 3], g, slopes[g]) for g in range(N_GROUPS)])
    y_prompt = _mix_and_moe(x_prompt, o, l, po, wts, tm=512, blk=512).reshape(B, T, D)
    tail = kt.shape[2]
    pkv = []
    for g, w in enumerate(WINDOWS):
        for a in (kt, vt):
            a = a.reshape(B, N_HEADS, HEAD_DIM, tail)[:, g * HEADS_PER_GROUP:(g + 1) * HEADS_PER_GROUP, :, tail - w:]
            pkv.append(a.transpose(0, 3, 1, 2))
    p_pool = ut[:, 1:]

    n_s = Bd * S
    xs = x_sample.transpose(1, 0, 2).reshape(n_s, D)
    qs, ks, vs, pos, st = _proj_sample(xs, *proj_w, state_pool.transpose(1, 0, 2),
                                       n_seq=Bd, n_new=S, past_len=past_len)
    so, sl, skv = [], [], []
    pad8 = lambda a: jnp.pad(a.reshape(S, Bd, GROUP_W).transpose(1, 0, 2), ((0, 0), (0, 8 - S), (0, 0)))
    for g, w in enumerate(WINDOWS):
        cols = slice(g * GROUP_W, (g + 1) * GROUP_W)
        kc, vc = caches[g]
        og, lg, ko, vo = _cached_attention(
            pad8(qs[:, cols]), pad8(ks[:, cols]), pad8(vs[:, cols]),
            kc.transpose(0, 2, 3, 1).reshape(Bd, GROUP_W, w), vc.transpose(0, 2, 3, 1).reshape(Bd, GROUP_W, w),
            g, slopes[g], n_new=S)
        so.append(og[:, :S].transpose(1, 0, 2).reshape(1, 1, n_s, GROUP_W).astype(bf16))
        sl.append(lg[:, :S].transpose(1, 0, 2).reshape(1, 1, n_s, GROUP_W))
        for a in (ko, vo):
            skv.append(a.reshape(Bd, HEADS_PER_GROUP, HEAD_DIM, w).transpose(0, 3, 1, 2))
    y_sample = _mix_and_moe(xs[None], so, sl, pos[None], wts, tm=n_s, blk=128).reshape(S, Bd, D).transpose(1, 0, 2)
    s_pool = st.transpose(1, 0, 2)

    return (y_prompt, y_sample, *pkv, p_pool, *skv, s_pool)
```

```python
import functools

import jax
import jax.numpy as jnp
from jax import lax
from jax.experimental import pallas as pl
from jax.experimental.pallas import tpu as pltpu

D_MODEL = 1024
HEAD_DIM = 64
HEADS_PER_GROUP = 4
WINDOWS = (128, 512, 2048)
DILATIONS = (1, 4, 16)
N_GROUPS = len(WINDOWS)
N_HEADS = HEADS_PER_GROUP * N_GROUPS
ATTN_W = N_HEADS * HEAD_DIM
GROUP_W = HEADS_PER_GROUP * HEAD_DIM
BAND = 128
POOL_WINDOWS = (2, 4, 8, 16)
POOL_GW = 128
POOL_W = len(POOL_WINDOWS) * POOL_GW
POOL_STATE = max(POOL_WINDOWS) - 1
POOL_HIST = 32
assert POOL_WINDOWS == (2, 4, 8, 16)
N_EXPERT_GROUPS = 4
EXPERTS_PER_GROUP = 8
N_EXPERTS = N_EXPERT_GROUPS * EXPERTS_PER_GROUP
D_EXPERT = 512
QKVU_W = 3 * ATTN_W + POOL_W
NEG = -1e30
EPS = 1e-6
LANES = 128
ROUTER_W = LANES
EXPERT_COL0 = 8
VMEM_LIMIT = 56 * 1024 * 1024
IBUF_SLOTS = 2 * (N_GROUPS - 1) * (GROUP_W // LANES)
SEG_ALIGN = 8

assert all(w // d == BAND for w, d in zip(WINDOWS, DILATIONS))

f32 = jnp.float32
bf16 = jnp.bfloat16


def _cparams(sem):
    return pltpu.CompilerParams(dimension_semantics=sem, vmem_limit_bytes=VMEM_LIMIT)


def _rmsnorm_rows(x, g):
    ms = jnp.mean(x * x, axis=-1, keepdims=True)
    return x * lax.rsqrt(ms + EPS) * g


def _head_rmsnorm_chunk(ch, gain):
    lane = lax.broadcasted_iota(jnp.int32, ch.shape, 1)
    lo_mask = lane < HEAD_DIM
    sq = ch * ch
    lo = jnp.sum(jnp.where(lo_mask, sq, 0.0), axis=-1, keepdims=True)
    hi = jnp.sum(jnp.where(lo_mask, 0.0, sq), axis=-1, keepdims=True)
    ss = jnp.where(lo_mask, lo, hi)
    return ch * lax.rsqrt(ss * (1.0 / HEAD_DIM) + EPS) * gain


def _project(x, ln1, w_ref, qg, kg):
    xn = _rmsnorm_rows(x, ln1).astype(bf16)
    nch = ATTN_W // LANES
    qs = qg * (HEAD_DIM ** -0.5)
    zq = jnp.dot(xn, w_ref[:, 0:ATTN_W], preferred_element_type=f32)
    q = [_head_rmsnorm_chunk(zq[:, c * LANES:(c + 1) * LANES], qs[:, c * LANES:(c + 1) * LANES]) for c in range(nch)]
    zk = jnp.dot(xn, w_ref[:, ATTN_W:2 * ATTN_W], preferred_element_type=f32)
    k = [_head_rmsnorm_chunk(zk[:, c * LANES:(c + 1) * LANES], kg[:, c * LANES:(c + 1) * LANES]) for c in range(nch)]
    zv = jnp.dot(xn, w_ref[:, 2 * ATTN_W:3 * ATTN_W], preferred_element_type=f32)
    v = [zv[:, c * LANES:(c + 1) * LANES] for c in range(nch)]
    u = jnp.dot(xn, w_ref[:, 3 * ATTN_W:3 * ATTN_W + POOL_W], preferred_element_type=f32)
    return q, k, v, u


def _proj_prompt_kernel(x_ref, ln1_ref, w_ref, qg_ref, kg_ref, plin_ref, pscale_ref, *rest,
                        tm, n_tiles, tail_tiles, hosted):
    n_hosted_in, n_hosted_out = (7, 4) if hosted else (0, 0)
    hosted_in, rest = rest[:n_hosted_in], rest[n_hosted_in:]
    qkv_refs = rest[:3 * N_GROUPS]
    po_ref, kt_ref, vt_ref, ut_ref = rest[3 * N_GROUPS:3 * N_GROUPS + 4]
    rest = rest[3 * N_GROUPS + 4:]
    hosted_out, (ubuf, pa, pb, sbuf) = rest[:n_hosted_out], rest[n_hosted_out:]
    i = pl.program_id(1)
    hist = POOL_STATE + 1
    ph, rows = POOL_HIST, POOL_HIST + tm

    @pl.when(i == 0)
    def _():
        ubuf[0:ph, :] = jnp.zeros((ph, POOL_W), f32)

    @pl.when(i > 0)
    def _():
        ubuf[0:ph, :] = ubuf[tm:tm + ph, :]

    if hosted:
        _cached_attn_kernel(*hosted_in, *hosted_out, **hosted)
    q, k, v, u = _project(x_ref[0], ln1_ref[...], w_ref, qg_ref[...], kg_ref[...])
    cpg = GROUP_W // LANES
    slot = 0
    for t, chunks in enumerate((q, k, v)):
        for g, dil in enumerate(DILATIONS):
            out_ref = qkv_refs[3 * g + t]
            for c in range(cpg):
                val = chunks[g * cpg + c]
                cols = slice(c * LANES, (c + 1) * LANES)
                if dil == 1:
                    out_ref[0, 0, :, cols] = val.astype(bf16)
                else:
                    sbuf[slot * tm:(slot + 1) * tm, :] = val
                    for r in range(dil):
                        out_ref[0, r, :, cols] = sbuf[pl.ds(slot * tm + r, tm // dil, stride=dil), :].astype(bf16)
                    slot += 1

    ubuf[ph:rows, :] = u
    gw = POOL_GW
    pa[8:rows, :] = ubuf[8:rows, :] + ubuf[7:rows - 1, :]
    pb[16:rows, gw:] = pa[16:rows, gw:] + pa[14:rows - 2, gw:]
    pa[24:rows, 2 * gw:] = pb[24:rows, 2 * gw:] + pb[20:rows - 4, 2 * gw:]
    pb[32:rows, 3 * gw:] = pa[32:rows, 3 * gw:] + pa[24:rows - 8, 3 * gw:]
    pos = i * tm + lax.broadcasted_iota(jnp.int32, (tm, POOL_GW), 0)
    zs = []
    for g, w in enumerate(POOL_WINDOWS):
        cols = slice(g * POOL_GW, (g + 1) * POOL_GW)
        wsum = (pa if g % 2 == 0 else pb)[ph:rows, cols]
        cnt = jnp.minimum(pos + 1, w).astype(f32)
        zs.append(wsum / cnt - u[:, cols])
    z = jnp.concatenate(zs, axis=-1).astype(bf16)
    po = jnp.dot(z, plin_ref[...], preferred_element_type=f32) * pscale_ref[...]
    po_ref[0] = po.astype(bf16)

    @pl.when(i >= n_tiles - tail_tiles)
    def _():
        kt_ref[0] = jnp.concatenate(k, axis=-1).T
        vt_ref[0] = jnp.concatenate(v, axis=-1).T

    @pl.when(i == n_tiles - 1)
    def _():
        ut_ref[0] = ubuf[rows - hist:rows, :]


def _proj_prompt(x, ln1, w_qkvu, qg, kg, plin_bd, pscale, *, tm, hosted=None):
    B, T, D = x.shape
    n_tiles = T // tm
    tail = max(WINDOWS)
    assert T % tm == 0 and tail % tm == 0 and T >= tail
    tail_tiles = tail // tm
    hist = POOL_STATE + 1
    kern = functools.partial(_proj_prompt_kernel, tm=tm, n_tiles=n_tiles, tail_tiles=tail_tiles,
                             hosted=hosted['params'] if hosted else None)
    h_ops, h_in, h_out, h_shape = ((), [], [], []) if not hosted else (
        hosted['operands'], hosted['in_specs'], hosted['out_specs'], hosted['out_shape'])
    const = lambda b, i: (0, 0)
    assert all(tm % (16 * d) == 0 for d in DILATIONS)
    qkv_specs = [pl.BlockSpec((1, d, tm // d, GROUP_W), lambda b, i: (b, 0, i, 0)) for d in DILATIONS for _ in range(3)]
    qkv_shapes = [jax.ShapeDtypeStruct((B, d, T // d, GROUP_W), bf16) for d in DILATIONS for _ in range(3)]
    tail_spec = pl.BlockSpec((1, ATTN_W, tm), lambda b, i: (b, 0, jnp.maximum(i - (n_tiles - tail_tiles), 0)))
    return pl.pallas_call(
        kern,
        grid=(B, n_tiles),
        in_specs=[
            pl.BlockSpec((1, tm, D), lambda b, i: (b, i, 0)),
            pl.BlockSpec((1, D), const),
            pl.BlockSpec((D, QKVU_W), const),
            pl.BlockSpec((1, ATTN_W), const),
            pl.BlockSpec((1, ATTN_W), const),
            pl.BlockSpec((POOL_W, POOL_W), const),
            pl.BlockSpec((1, POOL_W), const),
        ] + h_in,
        out_specs=qkv_specs
        + [pl.BlockSpec((1, tm, POOL_W), lambda b, i: (b, i, 0)),
           tail_spec, tail_spec,
           pl.BlockSpec((1, hist, POOL_W), lambda b, i: (b, 0, 0))] + h_out,
        out_shape=qkv_shapes
        + [jax.ShapeDtypeStruct((B, T, POOL_W), bf16),
           jax.ShapeDtypeStruct((B, ATTN_W, tail), f32),
           jax.ShapeDtypeStruct((B, ATTN_W, tail), f32),
           jax.ShapeDtypeStruct((B, hist, POOL_W), f32)] + h_shape,
        scratch_shapes=[pltpu.VMEM((POOL_HIST + tm, POOL_W), f32)] * 3
                       + [pltpu.VMEM((3 * (N_GROUPS - 1) * (GROUP_W // LANES) * tm, LANES), f32)],
        compiler_params=_cparams(("arbitrary", "arbitrary")),
        name="proj_prompt",
    )(x, ln1, w_qkvu, qg, kg, plin_bd, pscale, *h_ops)


def _proj_sample_kernel(x_ref, ln1_ref, w_ref, qg_ref, kg_ref, plin_ref, pscale_ref, state_ref,
                        q_ref, k_ref, v_ref, po_ref, st_ref, *, n_seq, n_new, past_len):
    q, k, v, u = _project(x_ref[...], ln1_ref[...], w_ref, qg_ref[...], kg_ref[...])
    q_ref[...] = jnp.concatenate(q, axis=-1)
    k_ref[...] = jnp.concatenate(k, axis=-1)
    v_ref[...] = jnp.concatenate(v, axis=-1)
    ext = [state_ref[j] for j in range(POOL_STATE)] + [u[s * n_seq:(s + 1) * n_seq, :] for s in range(n_new)]
    for s in range(n_new):
        zs = []
        for g, w in enumerate(POOL_WINDOWS):
            cols = slice(g * POOL_GW, (g + 1) * POOL_GW)
            cur = ext[POOL_STATE + s][:, cols]
            acc = cur
            for j in range(1, w):
                acc = acc + ext[POOL_STATE + s - j][:, cols]
            cnt = float(min(past_len + s + 1, w))
            zs.append(acc / cnt - cur)
        z = jnp.concatenate(zs, axis=-1).astype(bf16)
        po = jnp.dot(z, plin_ref[...], preferred_element_type=f32) * pscale_ref[...]
        po_ref[s * n_seq:(s + 1) * n_seq, :] = po.astype(bf16)
    for j in range(POOL_STATE):
        st_ref[j] = ext[j + n_new]


def _proj_sample(x, ln1, w_qkvu, qg, kg, plin_bd, pscale, state, *, n_seq, n_new, past_len):
    n = n_seq * n_new
    kern = functools.partial(_proj_sample_kernel, n_seq=n_seq, n_new=n_new, past_len=past_len)
    return pl.pallas_call(
        kern,
        out_shape=[jax.ShapeDtypeStruct((n, ATTN_W), f32)] * 3
        + [jax.ShapeDtypeStruct((n, POOL_W), bf16),
           jax.ShapeDtypeStruct((POOL_STATE, n_seq, POOL_W), f32)],
        compiler_params=pltpu.CompilerParams(vmem_limit_bytes=VMEM_LIMIT),
        name="proj_sample",
    )(x, ln1, w_qkvu, qg, kg, plin_bd, pscale, state)


def _head_masks(shape):
    lane = lax.broadcasted_iota(jnp.int32, shape, len(shape) - 1)
    return [(lane >= h * HEAD_DIM) & (lane < (h + 1) * HEAD_DIM) for h in range(HEADS_PER_GROUP)]


def _band_attn_kernel(q_ref, kc_ref, kp_ref, vc_ref, vp_ref, bias_ref, o_ref, l_ref, kbuf, vbuf, *, tl, unroll):
    i = pl.program_id(2)
    kbuf[0:BAND, :] = kp_ref[0, 0]
    kbuf[BAND:2 * BAND, :] = kc_ref[0, 0, 0:BAND, :]
    vbuf[0:BAND, :] = vp_ref[0, 0]
    vbuf[BAND:2 * BAND, :] = vc_ref[0, 0, 0:BAND, :]
    masks = _head_masks((BAND, GROUP_W))

    def sub_block(j, kk, vv, var):
        r0 = j * BAND if isinstance(j, int) else pl.multiple_of(j * BAND, BAND)
        q = q_ref[0, 0, pl.ds(r0, BAND), :]
        qm = jnp.concatenate([jnp.where(m, q, jnp.zeros_like(q)) for m in masks], axis=0)
        s = lax.dot_general(qm, kk, (((1,), (1,)), ((), ())), preferred_element_type=f32)
        s = s + bias_ref[var]
        m = jnp.max(s, axis=-1, keepdims=True)
        p = jnp.exp(s - m)
        den = jnp.sum(p, axis=-1, keepdims=True)
        pv = jnp.dot(p.astype(bf16), vv, preferred_element_type=f32)
        o = jnp.zeros((BAND, GROUP_W), f32)
        ms = jnp.zeros((BAND, GROUP_W), f32)
        ds = jnp.ones((BAND, GROUP_W), f32)
        for h, msk in enumerate(masks):
            rows = slice(h * BAND, (h + 1) * BAND)
            o = jnp.where(msk, pv[rows], o)
            ms = jnp.where(msk, m[rows], ms)
            ds = jnp.where(msk, den[rows], ds)
        o_ref[0, 0, pl.ds(r0, BAND), :] = (o / ds).astype(bf16)
        l_ref[0, 0, pl.ds(r0, BAND), :] = ms + jnp.log(ds)

    sub_block(0, kbuf[...], vbuf[...], jnp.where(i == 0, 0, 1))

    def body(j, carry):
        k0 = pl.multiple_of((j - 1) * BAND, BAND)
        sub_block(j, kc_ref[0, 0, pl.ds(k0, 2 * BAND), :], vc_ref[0, 0, pl.ds(k0, 2 * BAND), :], 1)
        return carry

    if tl > BAND:
        lax.fori_loop(1, tl // BAND, body, 0, unroll=unroll)


def _band_bias(slopes_g, dil):
    qi = jnp.arange(BAND)[:, None]
    kb = jnp.arange(2 * BAND)[None, :]
    rel = qi + BAND - kb
    valid = (rel >= 0) & (rel <= BAND)
    alibi = -slopes_g[:, None, None] * (dil * rel)[None].astype(f32)
    variants = []
    for first in (True, False):
        ok = valid & (kb >= BAND) if first else valid
        variants.append(jnp.where(ok[None], alibi, NEG).reshape(HEADS_PER_GROUP * BAND, 2 * BAND))
    return jnp.stack(variants, axis=0)


def _band_attention(q, k, v, g, slopes_g, *, tl_max=1024, unroll=7):
    B, dil, L, _ = q.shape
    tl = min(tl_max, L)
    assert dil == DILATIONS[g] and L % tl == 0 and tl % BAND == 0
    nsub = tl // BAND
    bias = _band_bias(slopes_g, dil)
    cur = pl.BlockSpec((1, 1, tl, GROUP_W), lambda b, r, i: (b, r, i, 0))
    prev = pl.BlockSpec((1, 1, BAND, GROUP_W), lambda b, r, i: (b, r, jnp.maximum(i * nsub - 1, 0), 0))
    return pl.pallas_call(
        functools.partial(_band_attn_kernel, tl=tl, unroll=max(1, min(unroll, nsub - 1))),
        grid=(B, dil, L // tl),
        in_specs=[cur, cur, prev, cur, prev,
                  pl.BlockSpec((2, HEADS_PER_GROUP * BAND, 2 * BAND), lambda b, r, i: (0, 0, 0))],
        out_specs=[cur, cur],
        out_shape=[jax.ShapeDtypeStruct((B, dil, L, GROUP_W), bf16),
                   jax.ShapeDtypeStruct((B, dil, L, GROUP_W), f32)],
        scratch_shapes=[pltpu.VMEM((2 * BAND, GROUP_W), bf16), pltpu.VMEM((2 * BAND, GROUP_W), bf16)],
        compiler_params=_cparams(("arbitrary", "arbitrary", "arbitrary")),
        name="band_attn_g%d" % g,
    )(q, k, k, v, v, bias)


def _cached_attn_kernel(q_ref, kn_ref, vn_ref, kc_ref, vc_ref, bc_ref, bn_ref,
                        o_ref, l_ref, ko_ref, vo_ref, *, nb, n_new, win):
    masks8 = _head_masks((nb, 8, GROUP_W))
    q8, kn8, vn8 = q_ref[...], kn_ref[...], vn_ref[...]
    kc, vc = kc_ref[...], vc_ref[...]
    qm = jnp.concatenate([jnp.where(m, q8, 0.0) for m in masks8], axis=1)
    sc = jnp.einsum('bqd,bdk->bqk', qm.astype(bf16), kc.astype(bf16), preferred_element_type=f32) + bc_ref[...]
    m = jnp.max(sc, axis=-1, keepdims=True)
    sn = []
    for t in range(n_new):
        col = jnp.sum(qm * kn8[:, t:t + 1, :], axis=-1, keepdims=True) + bn_ref[:, t:t + 1]
        sn.append(col)
        m = jnp.maximum(m, col)
    pc = jnp.exp(sc - m)
    den = jnp.sum(pc, axis=-1, keepdims=True)
    acc = jnp.einsum('bqk,bdk->bqd', pc.astype(bf16), vc.astype(bf16), preferred_element_type=f32)
    for t in range(n_new):
        pn = jnp.exp(sn[t] - m)
        den = den + pn
        acc = acc + pn * vn8[:, t:t + 1, :]
    acc = acc / den
    lse = m + jnp.log(den)
    o = jnp.zeros((nb, 8, GROUP_W), f32)
    l = jnp.zeros((nb, 8, GROUP_W), f32)
    for h, msk in enumerate(masks8):
        o = jnp.where(msk, acc[:, h * 8:(h + 1) * 8, :], o)
        l = jnp.where(msk, lse[:, h * 8:(h + 1) * 8, :], l)
    o_ref[...] = o
    l_ref[...] = l
    lane_t = lax.broadcasted_iota(jnp.int32, (nb, GROUP_W, LANES), 2)
    for cache, new8, out_ref in ((kc, kn8, ko_ref), (vc, vn8, vo_ref)):
        rolled = pltpu.roll(cache, win - n_new, axis=2)
        out_ref[...] = rolled
        new_t = jnp.swapaxes(jnp.concatenate([new8, jnp.zeros((nb, LANES - 8, GROUP_W), f32)], axis=1), 1, 2)
        new_t = pltpu.roll(new_t, LANES - n_new, axis=2)
        out_ref[:, :, win - LANES:win] = jnp.where(lane_t >= LANES - n_new, new_t, rolled[:, :, win - LANES:win])


def _cached_bias(slopes_g, dil, win, n_new):
    s = jnp.arange(8)[:, None]
    i = jnp.arange(win)[None, :]
    dist = win + s - i
    ok = (dist % dil == 0) & (dist // dil <= BAND) & (s < n_new)
    bc = jnp.where(ok[None], -slopes_g[:, None, None] * dist[None].astype(f32), NEG)
    t = jnp.arange(8)[None, :]
    dn = s - t
    okn = (dn >= 0) & (dn % dil == 0) & (dn // dil <= BAND) & (s < n_new) & (t < n_new)
    bn = jnp.where(okn[None], -slopes_g[:, None, None] * dn[None].astype(f32), NEG)
    pad = (s >= n_new)
    bc = jnp.where(pad[None], 0.0, bc)
    bn = jnp.where(pad[None], 0.0, bn)
    return bc.reshape(HEADS_PER_GROUP * 8, win), bn.reshape(HEADS_PER_GROUP * 8, 8)


def _cached_call_parts(q8, kn8, vn8, kc_t, vc_t, g, slopes_g, *, n_new, nb, step_of):
    Bd, _, win = kc_t.shape
    assert win == WINDOWS[g] and win % LANES == 0 and Bd % nb == 0
    bc, bn = _cached_bias(slopes_g, DILATIONS[g], win, n_new)
    small = pl.BlockSpec((nb, 8, GROUP_W), lambda *idx: (step_of(*idx), 0, 0))
    cache = pl.BlockSpec((nb, GROUP_W, win), lambda *idx: (step_of(*idx), 0, 0))
    const = lambda a: pl.BlockSpec(a.shape, lambda *idx: (0, 0))
    return dict(
        operands=(q8, kn8, vn8, kc_t, vc_t, bc, bn),
        in_specs=[small, small, small, cache, cache, const(bc), const(bn)],
        out_specs=[small, small, cache, cache],
        out_shape=[jax.ShapeDtypeStruct((Bd, 8, GROUP_W), f32)] * 2 + [jax.ShapeDtypeStruct((Bd, GROUP_W, win), f32)] * 2,
        params=dict(nb=nb, n_new=n_new, win=win))


def _cached_attention(q8, kn8, vn8, kc_t, vc_t, g, slopes_g, *, n_new):
    Bd, _, win = kc_t.shape
    nb = max(1, min(Bd, 2048 // win))
    parts = _cached_call_parts(q8, kn8, vn8, kc_t, vc_t, g, slopes_g, n_new=n_new, nb=nb, step_of=lambda b: b)
    return pl.pallas_call(
        functools.partial(_cached_attn_kernel, **parts['params']),
        grid=(Bd // nb,),
        in_specs=parts['in_specs'],
        out_specs=parts['out_specs'],
        out_shape=parts['out_shape'],
        compiler_params=_cparams(("arbitrary",)),
        name="cached_attn_g%d" % g,
    )(*parts['operands'])


def _merge_kernel(x_ref, o0_ref, o1_ref, o2_ref, l0_ref, l1_ref, l2_ref, po_ref,
                  ln1_ref, wg_ref, wpa_ref, wpb_ref, wo_ref, ln2_ref, wr_ref, br_ref, tri_ref, ltri_ref,
                  h_ref, xl_ref, wc_ref, tc_ref, ibuf, *, tm, ts, dils):
    slots = iter(range(IBUF_SLOTS))

    def token_order(ref, dil):
        if dil == 1:
            return ref[0, 0].astype(f32)
        chunks = []
        for c in range(GROUP_W // LANES):
            base = next(slots) * tm
            for r in range(dil):
                ibuf[pl.ds(base + r, tm // dil, stride=dil), :] = ref[0, r, :, c * LANES:(c + 1) * LANES].astype(f32)
            chunks.append(ibuf[base:base + tm, :])
        return jnp.concatenate(chunks, axis=-1)

    x = x_ref[0]
    xn = _rmsnorm_rows(x, ln1_ref[...]).astype(bf16)
    gates = jnp.dot(xn, wg_ref[...], preferred_element_type=f32)
    l0, l1, l2 = (token_order(r, d) for r, d in zip((l0_ref, l1_ref, l2_ref), dils))
    lm = jnp.maximum(jnp.maximum(l0, l1), l2)
    e0, e1, e2 = jnp.exp(l0 - lm), jnp.exp(l1 - lm), jnp.exp(l2 - lm)
    o0, o1, o2 = (token_order(r, d) for r, d in zip((o0_ref, o1_ref, o2_ref), dils))
    attn = (e0 * o0 + e1 * o1 + e2 * o2) / (e0 + e1 + e2)
    ma = jnp.dot(attn.astype(bf16), wpa_ref[...], preferred_element_type=f32)
    mb = jnp.dot(po_ref[0], wpb_ref[...], preferred_element_type=f32)
    mix = jax.nn.sigmoid(gates[:, :D_MODEL]) * ma + jax.nn.sigmoid(gates[:, D_MODEL:]) * mb
    h = x + jnp.dot(mix.astype(bf16), wo_ref[...], preferred_element_type=f32)
    h_ref[...] = h
    xn2 = _rmsnorm_rows(h, ln2_ref[...]).astype(bf16)

    lt = (jnp.dot(xn2, wr_ref[...], preferred_element_type=f32) + br_ref[...]).T
    row8 = lax.broadcasted_iota(jnp.int32, (8, tm), 0)
    gl = jnp.where(row8 < N_EXPERT_GROUPS, lt[0:8], -jnp.inf)
    gmax = jnp.max(gl, axis=0, keepdims=True)
    gidx = jnp.min(jnp.where(gl == gmax, row8, 8), axis=0, keepdims=True)
    pg = 1.0 / jnp.sum(jnp.exp(gl - gmax), axis=0, keepdims=True)
    sel = jnp.zeros((8, tm), f32)
    for g in range(N_EXPERT_GROUPS):
        lo = EXPERT_COL0 + g * EXPERTS_PER_GROUP
        sel = jnp.where(gidx == g, lt[lo:lo + EXPERTS_PER_GROUP], sel)
    v0 = jnp.max(sel, axis=0, keepdims=True)
    i0 = jnp.min(jnp.where(sel == v0, row8, 8), axis=0, keepdims=True)
    sel2 = jnp.where(row8 == i0, -jnp.inf, sel)
    v1 = jnp.max(sel2, axis=0, keepdims=True)
    i1 = jnp.min(jnp.where(sel2 == v1, row8, 8), axis=0, keepdims=True)
    t = jnp.exp(v1 - v0)
    w0 = pg / (1.0 + t)
    w1 = pg * t / (1.0 + t)
    eid0 = gidx * EXPERTS_PER_GROUP + i0
    eid1 = gidx * EXPERTS_PER_GROUP + i1
    erow = lax.broadcasted_iota(jnp.int32, (N_EXPERTS, tm), 0)
    oh0 = erow == eid0
    oh1 = erow == eid1
    cnt = jnp.where(oh0, 1.0, jnp.where(oh1, 1.0, 0.0))
    before = jnp.dot(cnt.astype(bf16), tri_ref[...], preferred_element_type=f32)
    tcount = jnp.sum(cnt, axis=1, keepdims=True)
    units = jnp.floor((tcount + (SEG_ALIGN - 1)) * (1.0 / SEG_ALIGN))
    ub = jnp.broadcast_to(units, (N_EXPERTS, LANES)).astype(bf16)
    seg0 = SEG_ALIGN * jnp.dot(ltri_ref[...], ub, preferred_element_type=f32)[:, 0:1]
    pos_e = seg0 + before
    lpos0 = jnp.sum(jnp.where(oh0, pos_e, 0.0), axis=0, keepdims=True)
    lpos1 = jnp.sum(jnp.where(oh1, pos_e, 0.0), axis=0, keepdims=True)
    prow = lax.broadcasted_iota(jnp.int32, (ts, tm), 0)
    perm = jnp.where(prow == lpos0.astype(jnp.int32), 1.0, jnp.where(prow == lpos1.astype(jnp.int32), 1.0, 0.0))
    xl_ref[...] = jnp.dot(perm.astype(bf16), xn2, preferred_element_type=f32)
    tc_ref[...] = jnp.broadcast_to(tcount, (N_EXPERTS, LANES))
    rowl = lax.broadcasted_iota(jnp.int32, (LANES, tm), 0)
    wslab = jnp.zeros((LANES, tm), f32)
    for r, val in enumerate((w0, w1, lpos0, lpos1)):
        wslab = jnp.where(rowl == r, val, wslab)
    wc_ref[...] = wslab.T


def _merge(x, o, l, po, ln1, w_gates, w_pa, w_pb, w_o, ln2, w_router, b_router, *, tm):
    B, T, _ = x.shape
    assert T % tm == 0
    nt = T // tm
    n = B * T
    dils = tuple(a.shape[1] for a in o)
    assert all(tm % (8 * d) == 0 for d in dils)
    assert 2 * tm // SEG_ALIGN <= 256
    ts = _sorted_tile_rows(tm)
    tri = (jnp.arange(tm)[:, None] < jnp.arange(tm)[None, :]).astype(bf16)
    ltri = (jnp.arange(N_EXPERTS)[None, :] < jnp.arange(N_EXPERTS)[:, None]).astype(bf16)
    rows3 = lambda w: pl.BlockSpec((1, tm, w), lambda b, i: (b, i, 0))
    flat = lambda r, w: pl.BlockSpec((r, w), lambda b, i: (b * nt + i, 0))
    grp = [pl.BlockSpec((1, d, tm // d, GROUP_W), lambda b, i: (b, 0, i, 0)) for d in dils]
    full = lambda a: pl.BlockSpec(a.shape, lambda b, i: (0,) * a.ndim)
    weights = (ln1, w_gates, w_pa, w_pb, w_o, ln2, w_router, b_router, tri, ltri)
    return pl.pallas_call(
        functools.partial(_merge_kernel, tm=tm, ts=ts, dils=dils),
        grid=(B, nt),
        in_specs=[rows3(D_MODEL)] + grp + grp + [rows3(POOL_W)] + [full(a) for a in weights],
        out_specs=[flat(tm, D_MODEL), flat(ts, D_MODEL), flat(tm, LANES),
                   pl.BlockSpec((N_EXPERTS, LANES), lambda b, i: (0, b * nt + i))],
        out_shape=[jax.ShapeDtypeStruct((n, D_MODEL), f32),
                   jax.ShapeDtypeStruct((B * nt * ts, D_MODEL), f32),
                   jax.ShapeDtypeStruct((n, LANES), f32),
                   jax.ShapeDtypeStruct((N_EXPERTS, B * nt * LANES), f32)],
        scratch_shapes=[pltpu.VMEM((IBUF_SLOTS * tm, LANES), f32)],
        compiler_params=_cparams(("arbitrary", "arbitrary")),
        name="merge_router",
    )(x, *o, *l, po, *weights)


def _sorted_tile_rows(tm):
    return -(-(2 * tm + N_EXPERTS * (SEG_ALIGN - 1)) // LANES) * LANES


def _moe_seg_kernel(blk_e_ref, blk_r0_ref, blk_n_ref, seg_g_ref, seg_c_ref, seg_src_ref, used_ref,
                    xl_hbm, wg_ref, wu_ref, wd_ref, yl_hbm,
                    xbuf, ybuf, sem_in, sem_out, sem_zero, ptr, *, blk, ts, n_tiles, n_blocks):
    b = pl.program_id(0)
    slot = lax.rem(b, 2)
    unit_bits = (blk // SEG_ALIGN).bit_length()

    def row_pieces(rows):
        units = lax.shift_right_logical(rows, jnp.int32(SEG_ALIGN.bit_length() - 1))
        off = jnp.int32(0)
        for bit in reversed(range(unit_bits)):
            on = lax.shift_right_logical(units, jnp.int32(bit)) & 1
            yield on == 1, off, SEG_ALIGN << bit
            off = off + on * (SEG_ALIGN << bit)

    def copy_rows(src, src_row, dst, dst_row, rows, sem):
        for on, off, size in row_pieces(rows):
            @pl.when(on)
            def _():
                pltpu.make_async_copy(src.at[pl.ds(pl.multiple_of(src_row + off, SEG_ALIGN), size)],
                                      dst.at[pl.ds(pl.multiple_of(dst_row + off, SEG_ALIGN), size)], sem).start()

    def wait_rows(src, dst, rows, sem):
        for on, _, size in row_pieces(rows):
            @pl.when(on)
            def _():
                pltpu.make_async_copy(src.at[pl.ds(0, size)], dst.at[pl.ds(0, size)], sem).wait()

    def for_pieces(bb, stream, fn):
        e, r0, n = blk_e_ref[bb], blk_r0_ref[bb], blk_n_ref[bb]

        @pl.when(n > 0)
        def _():
            def seg(i):
                return jnp.minimum(i, n_tiles - 1) * N_EXPERTS + e

            def cond(i):
                return (i < n_tiles) & (seg_g_ref[seg(i)] < r0 + n)

            def body(i):
                g = seg_g_ref[seg(i)]
                lo = jnp.maximum(g, r0)
                hi = jnp.minimum(g + seg_c_ref[seg(i)], r0 + n)

                @pl.when(hi > lo)
                def _():
                    fn(seg_src_ref[seg(i)] + (lo - g), lo - r0, hi - lo)

                return i + 1

            end = lax.while_loop(cond, body, jnp.where(r0 == 0, 0, ptr[stream]))
            ptr[stream] = jnp.maximum(end - 1, 0)

    def gather(bb):
        s = lax.rem(bb, 2)
        for_pieces(bb, 0, lambda lrow, brow, rows: copy_rows(xl_hbm, lrow, xbuf.at[s], brow, rows, sem_in.at[s]))

    def scatter(bb):
        s = lax.rem(bb, 2)
        for_pieces(bb, 1, lambda lrow, brow, rows: copy_rows(ybuf.at[s], brow, yl_hbm, lrow, rows, sem_out.at[s]))

    def zero_tail(i, go):
        row0 = i * ts + used_ref[i]
        rows = ts - used_ref[i]
        whole = lax.shift_right_logical(rows, jnp.int32(blk.bit_length() - 1))
        zsrc = xbuf.at[1]

        def whole_block(j, c):
            cp = pltpu.make_async_copy(zsrc, yl_hbm.at[pl.ds(pl.multiple_of(row0 + j * blk, SEG_ALIGN), blk)], sem_zero)
            cp.start() if go else cp.wait()
            return c

        lax.fori_loop(0, whole, whole_block, 0)
        rest = rows - whole * blk
        if go:
            copy_rows(zsrc, 0, yl_hbm, row0 + whole * blk, rest, sem_zero)
        else:
            wait_rows(zsrc, yl_hbm, rest, sem_zero)

    @pl.when(b == 0)
    def _():
        xbuf[...] = jnp.zeros_like(xbuf)
        ptr[0] = 0
        ptr[1] = 0
        gather(0)
        for go in (True, False):
            def per_tile(i, c, go=go):
                zero_tail(i, go)
                return c

            lax.fori_loop(0, n_tiles, per_tile, 0)

    n_b = blk_n_ref[b]
    wait_rows(xl_hbm, xbuf.at[slot], n_b, sem_in.at[slot])

    @pl.when(b + 1 < n_blocks)
    def _():
        gather(b + 1)

    @pl.when(b >= 2)
    def _():
        wait_rows(ybuf.at[slot], yl_hbm, blk_n_ref[b - 2], sem_out.at[slot])

    @pl.when(n_b > 0)
    def _():
        x = xbuf[slot].astype(bf16)
        hid = jax.nn.silu(jnp.dot(x, wg_ref[0], preferred_element_type=f32)) * jnp.dot(x, wu_ref[0], preferred_element_type=f32)
        ybuf[slot] = jnp.dot(hid.astype(bf16), wd_ref[0], preferred_element_type=f32)
        scatter(b)

    @pl.when(b == n_blocks - 1)
    def _():
        @pl.when(b >= 1)
        def _():
            wait_rows(ybuf.at[1 - slot], yl_hbm, blk_n_ref[b - 1], sem_out.at[1 - slot])

        wait_rows(ybuf.at[slot], yl_hbm, n_b, sem_out.at[slot])


def _moe_segments(xl, tables, w_gate, w_up, w_down, *, blk, ts, n_tiles):
    blk_e = tables[0]
    n_blocks = blk_e.shape[0]
    assert blk & (blk - 1) == 0 and blk % SEG_ALIGN == 0
    wspec = lambda shape: pl.BlockSpec((1,) + shape, lambda b, be, *_: (be[b], 0, 0))
    hbm = pl.BlockSpec(memory_space=pl.ANY)
    return pl.pallas_call(
        functools.partial(_moe_seg_kernel, blk=blk, ts=ts, n_tiles=n_tiles, n_blocks=n_blocks),
        grid_spec=pltpu.PrefetchScalarGridSpec(
            num_scalar_prefetch=len(tables),
            grid=(n_blocks,),
            in_specs=[hbm, wspec((D_MODEL, D_EXPERT)), wspec((D_MODEL, D_EXPERT)), wspec((D_EXPERT, D_MODEL))],
            out_specs=hbm,
            scratch_shapes=[pltpu.VMEM((2, blk, D_MODEL), f32), pltpu.VMEM((2, blk, D_MODEL), f32),
                            pltpu.SemaphoreType.DMA((2,)), pltpu.SemaphoreType.DMA((2,)), pltpu.SemaphoreType.DMA,
                            pltpu.SMEM((2,), jnp.int32)],
        ),
        out_shape=jax.ShapeDtypeStruct(xl.shape, f32),
        compiler_params=_cparams(("arbitrary",)),
        name="moe_experts",
    )(*tables, xl, w_gate, w_up, w_down)


def _unsort_kernel(h_ref, wc_ref, yl_ref, y_ref, *, ts):
    w = wc_ref[...]
    yl = yl_ref[...].astype(bf16)
    col = lax.broadcasted_iota(jnp.int32, (w.shape[0], ts), 1)
    y = h_ref[...]
    for k in range(2):
        pick = jnp.where(col == w[:, 2 + k:3 + k].astype(jnp.int32), 1.0, 0.0).astype(bf16)
        y = y + w[:, k:k + 1] * jnp.dot(pick, yl, preferred_element_type=f32)
    y_ref[...] = y


def _unsort(h, wc, yl, *, tm, ts):
    n = h.shape[0]
    rows = lambda r, w: pl.BlockSpec((r, w), lambda i: (i, 0))
    return pl.pallas_call(
        functools.partial(_unsort_kernel, ts=ts),
        grid=(n // tm,),
        in_specs=[rows(tm, D_MODEL), rows(tm, LANES), rows(ts, D_MODEL)],
        out_specs=rows(tm, D_MODEL),
        out_shape=jax.ShapeDtypeStruct((n, D_MODEL), f32),
        compiler_params=_cparams(("arbitrary",)),
        name="moe_unsort",
    )(h, wc, yl)


def _mix_and_moe(x, o, l, po, wts, *, tm, blk):
    n = x.shape[0] * x.shape[1]
    n_tiles = n // tm
    ts = _sorted_tile_rows(tm)
    h, xl, wc, tc = _merge(x, o, l, po, wts['ln1'], wts['w_gates'], wts['w_pa'], wts['w_pb'], wts['w_o'],
                           wts['ln2'], wts['w_router'], wts['b_router'], tm=tm)
    c8 = (tc[:, ::LANES].T.astype(jnp.int32) + (SEG_ALIGN - 1)) // SEG_ALIGN * SEG_ALIGN
    seg_src = jnp.arange(n_tiles, dtype=jnp.int32)[:, None] * ts + jnp.cumsum(c8, axis=1) - c8
    seg_g = jnp.cumsum(c8, axis=0) - c8
    tot = jnp.sum(c8, axis=0)
    padded = (tot + blk - 1) // blk * blk
    pad_ends = jnp.cumsum(padded)
    n_blocks = -(-(2 * n + n_tiles * N_EXPERTS * (SEG_ALIGN - 1) + N_EXPERTS * (blk - 1)) // blk)
    blk_start = jnp.arange(n_blocks, dtype=jnp.int32) * blk
    blk_e = jnp.minimum(jnp.sum(pad_ends[None, :] <= blk_start[:, None], axis=1), N_EXPERTS - 1).astype(jnp.int32)
    pick = blk_e[:, None] == jnp.arange(N_EXPERTS, dtype=jnp.int32)[None, :]
    blk_r0 = blk_start - jnp.sum(jnp.where(pick, (pad_ends - padded)[None, :], 0), axis=1)
    blk_n = jnp.clip(jnp.sum(jnp.where(pick, tot[None, :], 0), axis=1) - blk_r0, 0, blk)
    used = jnp.sum(c8, axis=1)
    tables = tuple(a.astype(jnp.int32).reshape(-1) for a in (blk_e, blk_r0, blk_n, seg_g, c8, seg_src, used))
    yl = _moe_segments(xl, tables, wts['w_gate'], wts['w_up'], wts['w_down'], blk=blk, ts=ts, n_tiles=n_tiles)
    return _unsort(h, wc, yl, tm=tm, ts=ts)


def kernel(x_prompt, x_sample, cache_k_w128, cache_v_w128, cache_k_w512, cache_v_w512, cache_k_w2048, cache_v_w2048, state_pool, ln1, w_in, q_gain, k_gain, pool_lin, pool_scale, w_pa, w_pb, w_o, ln2, w_rg, b_rg, w_re, b_re, w_gate, w_up, w_down):
    B, T, D = x_prompt.shape
    Bd, S, _ = x_sample.shape
    past_len = 8192
    caches = ((cache_k_w128, cache_v_w128), (cache_k_w512, cache_v_w512), (cache_k_w2048, cache_v_w2048))
    slopes = jnp.exp2(-8.0 * jnp.arange(1, N_HEADS + 1, dtype=f32) / N_HEADS).reshape(N_GROUPS, HEADS_PER_GROUP)

    w_qkvu = w_in[:, :QKVU_W].astype(bf16)
    plin_bd = jnp.zeros((POOL_W, POOL_W), f32)
    for g in range(len(POOL_WINDOWS)):
        plin_bd = plin_bd.at[g * POOL_GW:(g + 1) * POOL_GW, g * POOL_GW:(g + 1) * POOL_GW].set(pool_lin[g])
    w_router = jnp.zeros((D, ROUTER_W), f32).at[:, :N_EXPERT_GROUPS].set(w_rg)
    w_router = w_router.at[:, EXPERT_COL0:EXPERT_COL0 + N_EXPERTS].set(w_re)
    b_router = jnp.zeros((1, ROUTER_W), f32).at[0, :N_EXPERT_GROUPS].set(b_rg)
    b_router = b_router.at[0, EXPERT_COL0:EXPERT_COL0 + N_EXPERTS].set(b_re)
    wts = dict(ln1=ln1.reshape(1, D), w_gates=w_in[:, QKVU_W:].astype(bf16), w_pa=w_pa.astype(bf16),
               w_pb=w_pb.astype(bf16), w_o=w_o.astype(bf16), ln2=ln2.reshape(1, D),
               w_router=w_router.astype(bf16), b_router=b_router,
               w_gate=w_gate.astype(bf16), w_up=w_up.astype(bf16), w_down=w_down.astype(bf16))
    proj_w = (wts['ln1'], w_qkvu, q_gain.reshape(1, ATTN_W), k_gain.reshape(1, ATTN_W),
              plin_bd.astype(bf16), pool_scale.reshape(1, POOL_W))

    n_s = Bd * S
    xs = x_sample.transpose(1, 0, 2).reshape(n_s, D)
    qs, ks, vs, pos, st = _proj_sample(xs, *proj_w, state_pool.transpose(1, 0, 2),
                                       n_seq=Bd, n_new=S, past_len=past_len)
    pad8 = lambda a: jnp.pad(a.reshape(S, Bd, GROUP_W).transpose(1, 0, 2), ((0, 0), (0, 8 - S), (0, 0)))

    def cached_operands(g):
        cols = slice(g * GROUP_W, (g + 1) * GROUP_W)
        kc, vc = caches[g]
        w = WINDOWS[g]
        return (pad8(qs[:, cols]), pad8(ks[:, cols]), pad8(vs[:, cols]),
                kc.transpose(0, 2, 3, 1).reshape(Bd, GROUP_W, w), vc.transpose(0, 2, 3, 1).reshape(Bd, GROUP_W, w))

    tm_proj = 256
    proj_steps = B * (T // tm_proj)
    host_g = N_GROUPS - 1
    hosted = None
    if Bd % proj_steps == 0:
        n_t = T // tm_proj
        hosted = _cached_call_parts(*cached_operands(host_g), host_g, slopes[host_g], n_new=S, nb=Bd // proj_steps,
                                    step_of=lambda b, i: b * n_t + i)
    *qkv, po, kt, vt, ut = _proj_prompt(x_prompt, *proj_w, tm=tm_proj, hosted=hosted)
    hosted_out = None
    if hosted:
        qkv, po, kt, vt, ut, hosted_out = qkv[:3 * N_GROUPS], qkv[3 * N_GROUPS], qkv[3 * N_GROUPS + 1], \
            qkv[3 * N_GROUPS + 2], qkv[3 * N_GROUPS + 3], [po, kt, vt, ut]
    o, l = zip(*[_band_attention(*qkv[3 * g:3 * g + 3], g, slopes[g]) for g in range(N_GROUPS)])
    y_prompt = _mix_and_moe(x_prompt, o, l, po, wts, tm=512, blk=512).reshape(B, T, D)
    tail = kt.shape[2]
    pkv = []
    for g, w in enumerate(WINDOWS):
        for a in (kt, vt):
            a = a.reshape(B, N_HEADS, HEAD_DIM, tail)[:, g * HEADS_PER_GROUP:(g + 1) * HEADS_PER_GROUP, :, tail - w:]
            pkv.append(a.transpose(0, 3, 1, 2))
    p_pool = ut[:, 1:]

    so, sl, skv = [], [], []
    for g, w in enumerate(WINDOWS):
        if hosted_out is not None and g == host_g:
            og, lg, ko, vo = hosted_out
        else:
            og, lg, ko, vo = _cached_attention(*cached_operands(g), g, slopes[g], n_new=S)
        so.append(og[:, :S].transpose(1, 0, 2).reshape(1, 1, n_s, GROUP_W).astype(bf16))
        sl.append(lg[:, :S].transpose(1, 0, 2).reshape(1, 1, n_s, GROUP_W))
        for a in (ko, vo):
            skv.append(a.reshape(Bd, HEADS_PER_GROUP, HEAD_DIM, w).transpose(0, 3, 1, 2))
    y_sample = _mix_and_moe(xs[None], so, sl, pos[None], wts, tm=n_s, blk=128).reshape(S, Bd, D).transpose(1, 0, 2)
    s_pool = st.transpose(1, 0, 2)

    return (y_prompt, y_sample, *pkv, p_pool, *skv, s_pool)
```

```python
import functools

import jax
import jax.numpy as jnp
from jax import lax
from jax.experimental import pallas as pl
from jax.experimental.pallas import tpu as pltpu

D_MODEL = 1024
HEAD_DIM = 64
HEADS_PER_GROUP = 4
WINDOWS = (128, 512, 2048)
DILATIONS = (1, 4, 16)
N_GROUPS = len(WINDOWS)
N_HEADS = HEADS_PER_GROUP * N_GROUPS
ATTN_W = N_HEADS * HEAD_DIM
GROUP_W = HEADS_PER_GROUP * HEAD_DIM
BAND = 128
POOL_WINDOWS = (2, 4, 8, 16)
POOL_GW = 128
POOL_W = len(POOL_WINDOWS) * POOL_GW
POOL_STATE = max(POOL_WINDOWS) - 1
POOL_HIST = 32
assert POOL_WINDOWS == (2, 4, 8, 16)
N_EXPERT_GROUPS = 4
EXPERTS_PER_GROUP = 8
N_EXPERTS = N_EXPERT_GROUPS * EXPERTS_PER_GROUP
D_EXPERT = 512
QKVU_W = 3 * ATTN_W + POOL_W
NEG = -1e30
EPS = 1e-6
LANES = 128
ROUTER_W = LANES
EXPERT_COL0 = 8
VMEM_LIMIT = 56 * 1024 * 1024
IBUF_SLOTS = 2 * (N_GROUPS - 1) * (GROUP_W // LANES)
SEG_ALIGN = 8

assert all(w // d == BAND for w, d in zip(WINDOWS, DILATIONS))

f32 = jnp.float32
bf16 = jnp.bfloat16


def _cparams(sem):
    return pltpu.CompilerParams(dimension_semantics=sem, vmem_limit_bytes=VMEM_LIMIT)


def _rmsnorm_rows(x, g):
    ms = jnp.mean(x * x, axis=-1, keepdims=True)
    return x * lax.rsqrt(ms + EPS) * g


def _head_rmsnorm_chunk(ch, gain):
    lane = lax.broadcasted_iota(jnp.int32, ch.shape, 1)
    lo_mask = lane < HEAD_DIM
    sq = ch * ch
    lo = jnp.sum(jnp.where(lo_mask, sq, 0.0), axis=-1, keepdims=True)
    hi = jnp.sum(jnp.where(lo_mask, 0.0, sq), axis=-1, keepdims=True)
    ss = jnp.where(lo_mask, lo, hi)
    return ch * lax.rsqrt(ss * (1.0 / HEAD_DIM) + EPS) * gain


def _project(x, ln1, w_ref, qg, kg):
    xn = _rmsnorm_rows(x, ln1).astype(bf16)
    nch = ATTN_W // LANES
    qs = qg * (HEAD_DIM ** -0.5)
    zq = jnp.dot(xn, w_ref[:, 0:ATTN_W], preferred_element_type=f32)
    q = [_head_rmsnorm_chunk(zq[:, c * LANES:(c + 1) * LANES], qs[:, c * LANES:(c + 1) * LANES]) for c in range(nch)]
    zk = jnp.dot(xn, w_ref[:, ATTN_W:2 * ATTN_W], preferred_element_type=f32)
    k = [_head_rmsnorm_chunk(zk[:, c * LANES:(c + 1) * LANES], kg[:, c * LANES:(c + 1) * LANES]) for c in range(nch)]
    zv = jnp.dot(xn, w_ref[:, 2 * ATTN_W:3 * ATTN_W], preferred_element_type=f32)
    v = [zv[:, c * LANES:(c + 1) * LANES] for c in range(nch)]
    u = jnp.dot(xn, w_ref[:, 3 * ATTN_W:3 * ATTN_W + POOL_W], preferred_element_type=f32)
    return q, k, v, u


def _proj_prompt_kernel(x_ref, ln1_ref, w_ref, qg_ref, kg_ref, plin_ref, pscale_ref, *rest,
                        tm, n_tiles, tail_tiles, hosted):
    n_hosted_in, n_hosted_out = (7, 4) if hosted else (0, 0)
    hosted_in, rest = rest[:n_hosted_in], rest[n_hosted_in:]
    qkv_refs = rest[:3 * N_GROUPS]
    po_ref, kt_ref, vt_ref, ut_ref = rest[3 * N_GROUPS:3 * N_GROUPS + 4]
    rest = rest[3 * N_GROUPS + 4:]
    hosted_out, (ubuf, pa, pb, sbuf) = rest[:n_hosted_out], rest[n_hosted_out:]
    i = pl.program_id(1)
    hist = POOL_STATE + 1
    ph, rows = POOL_HIST, POOL_HIST + tm

    @pl.when(i == 0)
    def _():
        ubuf[0:ph, :] = jnp.zeros((ph, POOL_W), f32)

    @pl.when(i > 0)
    def _():
        ubuf[0:ph, :] = ubuf[tm:tm + ph, :]

    if hosted:
        _cached_attn_kernel(*hosted_in, *hosted_out, **hosted)
    q, k, v, u = _project(x_ref[0], ln1_ref[...], w_ref, qg_ref[...], kg_ref[...])
    cpg = GROUP_W // LANES
    slot = 0
    for t, chunks in enumerate((q, k, v)):
        for g, dil in enumerate(DILATIONS):
            out_ref = qkv_refs[3 * g + t]
            for c in range(cpg):
                val = chunks[g * cpg + c]
                cols = slice(c * LANES, (c + 1) * LANES)
                if dil == 1:
                    out_ref[0, 0, :, cols] = val.astype(bf16)
                else:
                    sbuf[slot * tm:(slot + 1) * tm, :] = val
                    for r in range(dil):
                        out_ref[0, r, :, cols] = sbuf[pl.ds(slot * tm + r, tm // dil, stride=dil), :].astype(bf16)
                    slot += 1

    ubuf[ph:rows, :] = u
    gw = POOL_GW
    pa[8:rows, :] = ubuf[8:rows, :] + ubuf[7:rows - 1, :]
    pb[16:rows, gw:] = pa[16:rows, gw:] + pa[14:rows - 2, gw:]
    pa[24:rows, 2 * gw:] = pb[24:rows, 2 * gw:] + pb[20:rows - 4, 2 * gw:]
    pb[32:rows, 3 * gw:] = pa[32:rows, 3 * gw:] + pa[24:rows - 8, 3 * gw:]
    pos = i * tm + lax.broadcasted_iota(jnp.int32, (tm, POOL_GW), 0)
    zs = []
    for g, w in enumerate(POOL_WINDOWS):
        cols = slice(g * POOL_GW, (g + 1) * POOL_GW)
        wsum = (pa if g % 2 == 0 else pb)[ph:rows, cols]
        cnt = jnp.minimum(pos + 1, w).astype(f32)
        zs.append(wsum / cnt - u[:, cols])
    z = jnp.concatenate(zs, axis=-1).astype(bf16)
    po = jnp.dot(z, plin_ref[...], preferred_element_type=f32) * pscale_ref[...]
    po_ref[0] = po.astype(bf16)

    @pl.when(i >= n_tiles - tail_tiles)
    def _():
        kt_ref[0] = jnp.concatenate(k, axis=-1).T
        vt_ref[0] = jnp.concatenate(v, axis=-1).T

    @pl.when(i == n_tiles - 1)
    def _():
        ut_ref[0] = ubuf[rows - hist:rows, :]


def _proj_prompt(x, ln1, w_qkvu, qg, kg, plin_bd, pscale, *, tm, hosted=None):
    B, T, D = x.shape
    n_tiles = T // tm
    tail = max(WINDOWS)
    assert T % tm == 0 and tail % tm == 0 and T >= tail
    tail_tiles = tail // tm
    hist = POOL_STATE + 1
    kern = functools.partial(_proj_prompt_kernel, tm=tm, n_tiles=n_tiles, tail_tiles=tail_tiles,
                             hosted=hosted['params'] if hosted else None)
    h_ops, h_in, h_out, h_shape = ((), [], [], []) if not hosted else (
        hosted['operands'], hosted['in_specs'], hosted['out_specs'], hosted['out_shape'])
    const = lambda b, i: (0, 0)
    assert all(tm % (16 * d) == 0 for d in DILATIONS)
    qkv_specs = [pl.BlockSpec((1, d, tm // d, GROUP_W), lambda b, i: (b, 0, i, 0)) for d in DILATIONS for _ in range(3)]
    qkv_shapes = [jax.ShapeDtypeStruct((B, d, T // d, GROUP_W), bf16) for d in DILATIONS for _ in range(3)]
    tail_spec = pl.BlockSpec((1, ATTN_W, tm), lambda b, i: (b, 0, jnp.maximum(i - (n_tiles - tail_tiles), 0)))
    return pl.pallas_call(
        kern,
        grid=(B, n_tiles),
        in_specs=[
            pl.BlockSpec((1, tm, D), lambda b, i: (b, i, 0)),
            pl.BlockSpec((1, D), const),
            pl.BlockSpec((D, QKVU_W), const),
            pl.BlockSpec((1, ATTN_W), const),
            pl.BlockSpec((1, ATTN_W), const),
            pl.BlockSpec((POOL_W, POOL_W), const),
            pl.BlockSpec((1, POOL_W), const),
        ] + h_in,
        out_specs=qkv_specs
        + [pl.BlockSpec((1, tm, POOL_W), lambda b, i: (b, i, 0)),
           tail_spec, tail_spec,
           pl.BlockSpec((1, hist, POOL_W), lambda b, i: (b, 0, 0))] + h_out,
        out_shape=qkv_shapes
        + [jax.ShapeDtypeStruct((B, T, POOL_W), bf16),
           jax.ShapeDtypeStruct((B, ATTN_W, tail), f32),
           jax.ShapeDtypeStruct((B, ATTN_W, tail), f32),
           jax.ShapeDtypeStruct((B, hist, POOL_W), f32)] + h_shape,
        scratch_shapes=[pltpu.VMEM((POOL_HIST + tm, POOL_W), f32)] * 3
                       + [pltpu.VMEM((3 * (N_GROUPS - 1) * (GROUP_W // LANES) * tm, LANES), f32)],
        compiler_params=_cparams(("arbitrary", "arbitrary")),
        name="proj_prompt",
    )(x, ln1, w_qkvu, qg, kg, plin_bd, pscale, *h_ops)


def _proj_sample_kernel(x_ref, ln1_ref, w_ref, qg_ref, kg_ref, plin_ref, pscale_ref, state_ref,
                        q_ref, k_ref, v_ref, po_ref, st_ref, *, n_seq, n_new, past_len):
    q, k, v, u = _project(x_ref[...], ln1_ref[...], w_ref, qg_ref[...], kg_ref[...])
    q_ref[...] = jnp.concatenate(q, axis=-1)
    k_ref[...] = jnp.concatenate(k, axis=-1)
    v_ref[...] = jnp.concatenate(v, axis=-1)
    ext = [state_ref[j] for j in range(POOL_STATE)] + [u[s * n_seq:(s + 1) * n_seq, :] for s in range(n_new)]
    for s in range(n_new):
        zs = []
        for g, w in enumerate(POOL_WINDOWS):
            cols = slice(g * POOL_GW, (g + 1) * POOL_GW)
            cur = ext[POOL_STATE + s][:, cols]
            acc = cur
            for j in range(1, w):
                acc = acc + ext[POOL_STATE + s - j][:, cols]
            cnt = float(min(past_len + s + 1, w))
            zs.append(acc / cnt - cur)
        z = jnp.concatenate(zs, axis=-1).astype(bf16)
        po = jnp.dot(z, plin_ref[...], preferred_element_type=f32) * pscale_ref[...]
        po_ref[s * n_seq:(s + 1) * n_seq, :] = po.astype(bf16)
    for j in range(POOL_STATE):
        st_ref[j] = ext[j + n_new]


def _proj_sample(x, ln1, w_qkvu, qg, kg, plin_bd, pscale, state, *, n_seq, n_new, past_len):
    n = n_seq * n_new
    kern = functools.partial(_proj_sample_kernel, n_seq=n_seq, n_new=n_new, past_len=past_len)
    return pl.pallas_call(
        kern,
        out_shape=[jax.ShapeDtypeStruct((n, ATTN_W), f32)] * 3
        + [jax.ShapeDtypeStruct((n, POOL_W), bf16),
           jax.ShapeDtypeStruct((POOL_STATE, n_seq, POOL_W), f32)],
        compiler_params=pltpu.CompilerParams(vmem_limit_bytes=VMEM_LIMIT),
        name="proj_sample",
    )(x, ln1, w_qkvu, qg, kg, plin_bd, pscale, state)


def _head_masks(shape):
    lane = lax.broadcasted_iota(jnp.int32, shape, len(shape) - 1)
    return [(lane >= h * HEAD_DIM) & (lane < (h + 1) * HEAD_DIM) for h in range(HEADS_PER_GROUP)]


def _band_attn_kernel(q_ref, kc_ref, kp_ref, vc_ref, vp_ref, bias_ref, o_ref, l_ref, kbuf, vbuf, *, tl, unroll):
    i = pl.program_id(2)
    kbuf[0:BAND, :] = kp_ref[0, 0]
    kbuf[BAND:2 * BAND, :] = kc_ref[0, 0, 0:BAND, :]
    vbuf[0:BAND, :] = vp_ref[0, 0]
    vbuf[BAND:2 * BAND, :] = vc_ref[0, 0, 0:BAND, :]
    masks = _head_masks((BAND, GROUP_W))

    def sub_block(j, kk, vv, var):
        r0 = j * BAND if isinstance(j, int) else pl.multiple_of(j * BAND, BAND)
        q = q_ref[0, 0, pl.ds(r0, BAND), :]
        qm = jnp.concatenate([jnp.where(m, q, jnp.zeros_like(q)) for m in masks], axis=0)
        s = lax.dot_general(qm, kk, (((1,), (1,)), ((), ())), preferred_element_type=f32)
        s = s + bias_ref[var]
        m = jnp.max(s, axis=-1, keepdims=True)
        p = jnp.exp(s - m)
        den = jnp.sum(p, axis=-1, keepdims=True)
        pv = jnp.dot(p.astype(bf16), vv, preferred_element_type=f32)
        o = jnp.zeros((BAND, GROUP_W), f32)
        ms = jnp.zeros((BAND, GROUP_W), f32)
        ds = jnp.ones((BAND, GROUP_W), f32)
        for h, msk in enumerate(masks):
            rows = slice(h * BAND, (h + 1) * BAND)
            o = jnp.where(msk, pv[rows], o)
            ms = jnp.where(msk, m[rows], ms)
            ds = jnp.where(msk, den[rows], ds)
        o_ref[0, 0, pl.ds(r0, BAND), :] = (o / ds).astype(bf16)
        l_ref[0, 0, pl.ds(r0, BAND), :] = ms + jnp.log(ds)

    sub_block(0, kbuf[...], vbuf[...], jnp.where(i == 0, 0, 1))

    def body(j, carry):
        k0 = pl.multiple_of((j - 1) * BAND, BAND)
        sub_block(j, kc_ref[0, 0, pl.ds(k0, 2 * BAND), :], vc_ref[0, 0, pl.ds(k0, 2 * BAND), :], 1)
        return carry

    if tl > BAND:
        lax.fori_loop(1, tl // BAND, body, 0, unroll=unroll)


def _band_bias(slopes_g, dil):
    qi = jnp.arange(BAND)[:, None]
    kb = jnp.arange(2 * BAND)[None, :]
    rel = qi + BAND - kb
    valid = (rel >= 0) & (rel <= BAND)
    alibi = -slopes_g[:, None, None] * (dil * rel)[None].astype(f32)
    variants = []
    for first in (True, False):
        ok = valid & (kb >= BAND) if first else valid
        variants.append(jnp.where(ok[None], alibi, NEG).reshape(HEADS_PER_GROUP * BAND, 2 * BAND))
    return jnp.stack(variants, axis=0)


def _band_attention(q, k, v, g, slopes_g, *, tl_max=1024, unroll=7):
    B, dil, L, _ = q.shape
    tl = min(tl_max, L)
    assert dil == DILATIONS[g] and L % tl == 0 and tl % BAND == 0
    nsub = tl // BAND
    bias = _band_bias(slopes_g, dil)
    cur = pl.BlockSpec((1, 1, tl, GROUP_W), lambda b, r, i: (b, r, i, 0))
    prev = pl.BlockSpec((1, 1, BAND, GROUP_W), lambda b, r, i: (b, r, jnp.maximum(i * nsub - 1, 0), 0))
    return pl.pallas_call(
        functools.partial(_band_attn_kernel, tl=tl, unroll=max(1, min(unroll, nsub - 1))),
        grid=(B, dil, L // tl),
        in_specs=[cur, cur, prev, cur, prev,
                  pl.BlockSpec((2, HEADS_PER_GROUP * BAND, 2 * BAND), lambda b, r, i: (0, 0, 0))],
        out_specs=[cur, cur],
        out_shape=[jax.ShapeDtypeStruct((B, dil, L, GROUP_W), bf16),
                   jax.ShapeDtypeStruct((B, dil, L, GROUP_W), f32)],
        scratch_shapes=[pltpu.VMEM((2 * BAND, GROUP_W), bf16), pltpu.VMEM((2 * BAND, GROUP_W), bf16)],
        compiler_params=_cparams(("arbitrary", "arbitrary", "arbitrary")),
        name="band_attn_g%d" % g,
    )(q, k, k, v, v, bias)


def _cached_attn_kernel(q_ref, kn_ref, vn_ref, kc_ref, vc_ref, bc_ref, bn_ref,
                        o_ref, l_ref, ko_ref, vo_ref, *, nb, n_new, win):
    masks8 = _head_masks((nb, 8, GROUP_W))
    q8, kn8, vn8 = q_ref[...], kn_ref[...], vn_ref[...]
    kc, vc = kc_ref[...], vc_ref[...]
    qm = jnp.concatenate([jnp.where(m, q8, 0.0) for m in masks8], axis=1)
    sc = jnp.einsum('bqd,bdk->bqk', qm.astype(bf16), kc.astype(bf16), preferred_element_type=f32) + bc_ref[...]
    m = jnp.max(sc, axis=-1, keepdims=True)
    sn = []
    for t in range(n_new):
        col = jnp.sum(qm * kn8[:, t:t + 1, :], axis=-1, keepdims=True) + bn_ref[:, t:t + 1]
        sn.append(col)
        m = jnp.maximum(m, col)
    pc = jnp.exp(sc - m)
    den = jnp.sum(pc, axis=-1, keepdims=True)
    acc = jnp.einsum('bqk,bdk->bqd', pc.astype(bf16), vc.astype(bf16), preferred_element_type=f32)
    for t in range(n_new):
        pn = jnp.exp(sn[t] - m)
        den = den + pn
        acc = acc + pn * vn8[:, t:t + 1, :]
    acc = acc / den
    lse = m + jnp.log(den)
    o = jnp.zeros((nb, 8, GROUP_W), f32)
    l = jnp.zeros((nb, 8, GROUP_W), f32)
    for h, msk in enumerate(masks8):
        o = jnp.where(msk, acc[:, h * 8:(h + 1) * 8, :], o)
        l = jnp.where(msk, lse[:, h * 8:(h + 1) * 8, :], l)
    o_ref[...] = o
    l_ref[...] = l
    lane_t = lax.broadcasted_iota(jnp.int32, (nb, GROUP_W, LANES), 2)
    for cache, new8, out_ref in ((kc, kn8, ko_ref), (vc, vn8, vo_ref)):
        rolled = pltpu.roll(cache, win - n_new, axis=2)
        out_ref[...] = rolled
        new_t = jnp.swapaxes(jnp.concatenate([new8, jnp.zeros((nb, LANES - 8, GROUP_W), f32)], axis=1), 1, 2)
        new_t = pltpu.roll(new_t, LANES - n_new, axis=2)
        out_ref[:, :, win - LANES:win] = jnp.where(lane_t >= LANES - n_new, new_t, rolled[:, :, win - LANES:win])


def _cached_bias(slopes_g, dil, win, n_new):
    s = jnp.arange(8)[:, None]
    i = jnp.arange(win)[None, :]
    dist = win + s - i
    ok = (dist % dil == 0) & (dist // dil <= BAND) & (s < n_new)
    bc = jnp.where(ok[None], -slopes_g[:, None, None] * dist[None].astype(f32), NEG)
    t = jnp.arange(8)[None, :]
    dn = s - t
    okn = (dn >= 0) & (dn % dil == 0) & (dn // dil <= BAND) & (s < n_new) & (t < n_new)
    bn = jnp.where(okn[None], -slopes_g[:, None, None] * dn[None].astype(f32), NEG)
    pad = (s >= n_new)
    bc = jnp.where(pad[None], 0.0, bc)
    bn = jnp.where(pad[None], 0.0, bn)
    return bc.reshape(HEADS_PER_GROUP * 8, win), bn.reshape(HEADS_PER_GROUP * 8, 8)


def _cached_call_parts(q8, kn8, vn8, kc_t, vc_t, g, slopes_g, *, n_new, nb, step_of):
    Bd, _, win = kc_t.shape
    assert win == WINDOWS[g] and win % LANES == 0 and Bd % nb == 0
    bc, bn = _cached_bias(slopes_g, DILATIONS[g], win, n_new)
    small = pl.BlockSpec((nb, 8, GROUP_W), lambda *idx: (step_of(*idx), 0, 0))
    cache = pl.BlockSpec((nb, GROUP_W, win), lambda *idx: (step_of(*idx), 0, 0))
    const = lambda a: pl.BlockSpec(a.shape, lambda *idx: (0, 0))
    return dict(
        operands=(q8, kn8, vn8, kc_t, vc_t, bc, bn),
        in_specs=[small, small, small, cache, cache, const(bc), const(bn)],
        out_specs=[small, small, cache, cache],
        out_shape=[jax.ShapeDtypeStruct((Bd, 8, GROUP_W), f32)] * 2 + [jax.ShapeDtypeStruct((Bd, GROUP_W, win), f32)] * 2,
        params=dict(nb=nb, n_new=n_new, win=win))


def _cached_attention(q8, kn8, vn8, kc_t, vc_t, g, slopes_g, *, n_new):
    Bd, _, win = kc_t.shape
    nb = max(1, min(Bd, 2048 // win))
    parts = _cached_call_parts(q8, kn8, vn8, kc_t, vc_t, g, slopes_g, n_new=n_new, nb=nb, step_of=lambda b: b)
    return pl.pallas_call(
        functools.partial(_cached_attn_kernel, **parts['params']),
        grid=(Bd // nb,),
        in_specs=parts['in_specs'],
        out_specs=parts['out_specs'],
        out_shape=parts['out_shape'],
        compiler_params=_cparams(("arbitrary",)),
        name="cached_attn_g%d" % g,
    )(*parts['operands'])


def _split_hosted(rest, n_own_out, hosted):
    n_in, n_out = (7, 4) if hosted else (0, 0)
    hosted_in, rest = rest[:n_in], rest[n_in:]
    own_out, rest = rest[:n_own_out], rest[n_own_out:]
    return hosted_in, own_out, rest[:n_out], rest[n_out:]


def _merge_kernel(x_ref, o0_ref, o1_ref, o2_ref, l0_ref, l1_ref, l2_ref, po_ref,
                  ln1_ref, wg_ref, wpa_ref, wpb_ref, wo_ref, ln2_ref, wr_ref, br_ref, tri_ref, ltri_ref,
                  *rest, tm, ts, dils, hosted):
    hosted_in, (h_ref, xl_ref, wc_ref, tc_ref), hosted_out, (ibuf,) = _split_hosted(rest, 4, hosted)
    if hosted:
        _cached_attn_kernel(*hosted_in, *hosted_out, **hosted)
    slots = iter(range(IBUF_SLOTS))

    def token_order(ref, dil):
        if dil == 1:
            return ref[0, 0].astype(f32)
        chunks = []
        for c in range(GROUP_W // LANES):
            base = next(slots) * tm
            for r in range(dil):
                ibuf[pl.ds(base + r, tm // dil, stride=dil), :] = ref[0, r, :, c * LANES:(c + 1) * LANES].astype(f32)
            chunks.append(ibuf[base:base + tm, :])
        return jnp.concatenate(chunks, axis=-1)

    x = x_ref[0]
    xn = _rmsnorm_rows(x, ln1_ref[...]).astype(bf16)
    gates = jnp.dot(xn, wg_ref[...], preferred_element_type=f32)
    l0, l1, l2 = (token_order(r, d) for r, d in zip((l0_ref, l1_ref, l2_ref), dils))
    lm = jnp.maximum(jnp.maximum(l0, l1), l2)
    e0, e1, e2 = jnp.exp(l0 - lm), jnp.exp(l1 - lm), jnp.exp(l2 - lm)
    o0, o1, o2 = (token_order(r, d) for r, d in zip((o0_ref, o1_ref, o2_ref), dils))
    attn = (e0 * o0 + e1 * o1 + e2 * o2) / (e0 + e1 + e2)
    ma = jnp.dot(attn.astype(bf16), wpa_ref[...], preferred_element_type=f32)
    mb = jnp.dot(po_ref[0], wpb_ref[...], preferred_element_type=f32)
    mix = jax.nn.sigmoid(gates[:, :D_MODEL]) * ma + jax.nn.sigmoid(gates[:, D_MODEL:]) * mb
    h = x + jnp.dot(mix.astype(bf16), wo_ref[...], preferred_element_type=f32)
    h_ref[...] = h
    xn2 = _rmsnorm_rows(h, ln2_ref[...]).astype(bf16)

    lt = (jnp.dot(xn2, wr_ref[...], preferred_element_type=f32) + br_ref[...]).T
    row8 = lax.broadcasted_iota(jnp.int32, (8, tm), 0)
    gl = jnp.where(row8 < N_EXPERT_GROUPS, lt[0:8], -jnp.inf)
    gmax = jnp.max(gl, axis=0, keepdims=True)
    gidx = jnp.min(jnp.where(gl == gmax, row8, 8), axis=0, keepdims=True)
    pg = 1.0 / jnp.sum(jnp.exp(gl - gmax), axis=0, keepdims=True)
    sel = jnp.zeros((8, tm), f32)
    for g in range(N_EXPERT_GROUPS):
        lo = EXPERT_COL0 + g * EXPERTS_PER_GROUP
        sel = jnp.where(gidx == g, lt[lo:lo + EXPERTS_PER_GROUP], sel)
    v0 = jnp.max(sel, axis=0, keepdims=True)
    i0 = jnp.min(jnp.where(sel == v0, row8, 8), axis=0, keepdims=True)
    sel2 = jnp.where(row8 == i0, -jnp.inf, sel)
    v1 = jnp.max(sel2, axis=0, keepdims=True)
    i1 = jnp.min(jnp.where(sel2 == v1, row8, 8), axis=0, keepdims=True)
    t = jnp.exp(v1 - v0)
    w0 = pg / (1.0 + t)
    w1 = pg * t / (1.0 + t)
    eid0 = gidx * EXPERTS_PER_GROUP + i0
    eid1 = gidx * EXPERTS_PER_GROUP + i1
    erow = lax.broadcasted_iota(jnp.int32, (N_EXPERTS, tm), 0)
    oh0 = erow == eid0
    oh1 = erow == eid1
    cnt = jnp.where(oh0, 1.0, jnp.where(oh1, 1.0, 0.0))
    before = jnp.dot(cnt.astype(bf16), tri_ref[...], preferred_element_type=f32)
    tcount = jnp.sum(cnt, axis=1, keepdims=True)
    units = jnp.floor((tcount + (SEG_ALIGN - 1)) * (1.0 / SEG_ALIGN))
    ub = jnp.broadcast_to(units, (N_EXPERTS, LANES)).astype(bf16)
    seg0 = SEG_ALIGN * jnp.dot(ltri_ref[...], ub, preferred_element_type=f32)[:, 0:1]
    pos_e = seg0 + before
    lpos0 = jnp.sum(jnp.where(oh0, pos_e, 0.0), axis=0, keepdims=True)
    lpos1 = jnp.sum(jnp.where(oh1, pos_e, 0.0), axis=0, keepdims=True)
    prow = lax.broadcasted_iota(jnp.int32, (ts, tm), 0)
    perm = jnp.where(prow == lpos0.astype(jnp.int32), 1.0, jnp.where(prow == lpos1.astype(jnp.int32), 1.0, 0.0))
    xl_ref[...] = jnp.dot(perm.astype(bf16), xn2, preferred_element_type=f32)
    tc_ref[...] = jnp.broadcast_to(tcount, (N_EXPERTS, LANES))
    rowl = lax.broadcasted_iota(jnp.int32, (LANES, tm), 0)
    wslab = jnp.zeros((LANES, tm), f32)
    for r, val in enumerate((w0, w1, lpos0, lpos1)):
        wslab = jnp.where(rowl == r, val, wslab)
    wc_ref[...] = wslab.T


def _hosted_parts(hosted):
    if not hosted:
        return (), [], [], [], None
    return hosted['operands'], hosted['in_specs'], hosted['out_specs'], hosted['out_shape'], hosted['params']


def _merge(x, o, l, po, ln1, w_gates, w_pa, w_pb, w_o, ln2, w_router, b_router, *, tm, hosted=None):
    B, T, _ = x.shape
    h_ops, h_in, h_out, h_shape, h_params = _hosted_parts(hosted)
    assert T % tm == 0
    nt = T // tm
    n = B * T
    dils = tuple(a.shape[1] for a in o)
    assert all(tm % (8 * d) == 0 for d in dils)
    assert 2 * tm // SEG_ALIGN <= 256
    ts = _sorted_tile_rows(tm)
    tri = (jnp.arange(tm)[:, None] < jnp.arange(tm)[None, :]).astype(bf16)
    ltri = (jnp.arange(N_EXPERTS)[None, :] < jnp.arange(N_EXPERTS)[:, None]).astype(bf16)
    rows3 = lambda w: pl.BlockSpec((1, tm, w), lambda b, i: (b, i, 0))
    flat = lambda r, w: pl.BlockSpec((r, w), lambda b, i: (b * nt + i, 0))
    grp = [pl.BlockSpec((1, d, tm // d, GROUP_W), lambda b, i: (b, 0, i, 0)) for d in dils]
    full = lambda a: pl.BlockSpec(a.shape, lambda b, i: (0,) * a.ndim)
    weights = (ln1, w_gates, w_pa, w_pb, w_o, ln2, w_router, b_router, tri, ltri)
    return pl.pallas_call(
        functools.partial(_merge_kernel, tm=tm, ts=ts, dils=dils, hosted=h_params),
        grid=(B, nt),
        in_specs=[rows3(D_MODEL)] + grp + grp + [rows3(POOL_W)] + [full(a) for a in weights] + h_in,
        out_specs=[flat(tm, D_MODEL), flat(ts, D_MODEL), flat(tm, LANES),
                   pl.BlockSpec((N_EXPERTS, LANES), lambda b, i: (0, b * nt + i))] + h_out,
        out_shape=[jax.ShapeDtypeStruct((n, D_MODEL), f32),
                   jax.ShapeDtypeStruct((B * nt * ts, D_MODEL), f32),
                   jax.ShapeDtypeStruct((n, LANES), f32),
                   jax.ShapeDtypeStruct((N_EXPERTS, B * nt * LANES), f32)] + h_shape,
        scratch_shapes=[pltpu.VMEM((IBUF_SLOTS * tm, LANES), f32)],
        compiler_params=_cparams(("arbitrary", "arbitrary")),
        name="merge_router",
    )(x, *o, *l, po, *weights, *h_ops)


def _sorted_tile_rows(tm):
    return -(-(2 * tm + N_EXPERTS * (SEG_ALIGN - 1)) // LANES) * LANES


def _moe_seg_kernel(blk_e_ref, blk_r0_ref, blk_n_ref, seg_g_ref, seg_c_ref, seg_src_ref, used_ref,
                    xl_hbm, wg_ref, wu_ref, wd_ref, yl_hbm,
                    xbuf, ybuf, sem_in, sem_out, sem_zero, ptr, *, blk, ts, n_tiles, n_blocks):
    b = pl.program_id(0)
    slot = lax.rem(b, 2)
    unit_bits = (blk // SEG_ALIGN).bit_length()

    def row_pieces(rows):
        units = lax.shift_right_logical(rows, jnp.int32(SEG_ALIGN.bit_length() - 1))
        off = jnp.int32(0)
        for bit in reversed(range(unit_bits)):
            on = lax.shift_right_logical(units, jnp.int32(bit)) & 1
            yield on == 1, off, SEG_ALIGN << bit
            off = off + on * (SEG_ALIGN << bit)

    def copy_rows(src, src_row, dst, dst_row, rows, sem):
        def start(off, size):
            pltpu.make_async_copy(src.at[pl.ds(pl.multiple_of(src_row + off, SEG_ALIGN), size)],
                                  dst.at[pl.ds(pl.multiple_of(dst_row + off, SEG_ALIGN), size)], sem).start()

        chunk = SEG_ALIGN << 3
        n_chunks = lax.shift_right_logical(rows, jnp.int32(chunk.bit_length() - 1))

        def whole_chunk(j, c):
            start(j * chunk, chunk)
            return c

        lax.fori_loop(0, n_chunks, whole_chunk, 0)
        off = n_chunks * chunk
        rest = rows - off
        for bit in (2, 1, 0):
            size = SEG_ALIGN << bit
            on = (rest & size) != 0

            @pl.when(on)
            def _():
                start(off, size)

            off = off + jnp.where(on, size, 0)

    def wait_rows(src, dst, rows, sem):
        for on, _, size in row_pieces(rows):
            @pl.when(on)
            def _():
                pltpu.make_async_copy(src.at[pl.ds(0, size)], dst.at[pl.ds(0, size)], sem).wait()

    def for_pieces(bb, stream, fn):
        e, r0, n = blk_e_ref[bb], blk_r0_ref[bb], blk_n_ref[bb]

        @pl.when(n > 0)
        def _():
            def seg(i):
                return jnp.minimum(i, n_tiles - 1) * N_EXPERTS + e

            def cond(i):
                return (i < n_tiles) & (seg_g_ref[seg(i)] < r0 + n)

            def body(i):
                g = seg_g_ref[seg(i)]
                lo = jnp.maximum(g, r0)
                hi = jnp.minimum(g + seg_c_ref[seg(i)], r0 + n)

                @pl.when(hi > lo)
                def _():
                    fn(seg_src_ref[seg(i)] + (lo - g), lo - r0, hi - lo)

                return i + 1

            end = lax.while_loop(cond, body, jnp.where(r0 == 0, 0, ptr[stream]))
            ptr[stream] = jnp.maximum(end - 1, 0)

    def gather(bb):
        s = lax.rem(bb, 2)
        for_pieces(bb, 0, lambda lrow, brow, rows: copy_rows(xl_hbm, lrow, xbuf.at[s], brow, rows, sem_in.at[s]))

    def scatter(bb):
        s = lax.rem(bb, 2)
        for_pieces(bb, 1, lambda lrow, brow, rows: copy_rows(ybuf.at[s], brow, yl_hbm, lrow, rows, sem_out.at[s]))

    def zero_tail(i, go):
        row0 = i * ts + used_ref[i]
        rows = ts - used_ref[i]
        whole = lax.shift_right_logical(rows, jnp.int32(blk.bit_length() - 1))
        zsrc = xbuf.at[1]

        def whole_block(j, c):
            cp = pltpu.make_async_copy(zsrc, yl_hbm.at[pl.ds(pl.multiple_of(row0 + j * blk, SEG_ALIGN), blk)], sem_zero)
            cp.start() if go else cp.wait()
            return c

        lax.fori_loop(0, whole, whole_block, 0)
        rest = rows - whole * blk
        if go:
            copy_rows(zsrc, 0, yl_hbm, row0 + whole * blk, rest, sem_zero)
        else:
            wait_rows(zsrc, yl_hbm, rest, sem_zero)

    @pl.when(b == 0)
    def _():
        xbuf[...] = jnp.zeros_like(xbuf)
        ptr[0] = 0
        ptr[1] = 0
        gather(0)
        for go in (True, False):
            def per_tile(i, c, go=go):
                zero_tail(i, go)
                return c

            lax.fori_loop(0, n_tiles, per_tile, 0)

    n_b = blk_n_ref[b]
    wait_rows(xl_hbm, xbuf.at[slot], n_b, sem_in.at[slot])

    @pl.when(b + 1 < n_blocks)
    def _():
        gather(b + 1)

    @pl.when(b >= 2)
    def _():
        wait_rows(ybuf.at[slot], yl_hbm, blk_n_ref[b - 2], sem_out.at[slot])

    @pl.when(n_b > 0)
    def _():
        x = xbuf[slot].astype(bf16)
        hid = jax.nn.silu(jnp.dot(x, wg_ref[0], preferred_element_type=f32)) * jnp.dot(x, wu_ref[0], preferred_element_type=f32)
        ybuf[slot] = jnp.dot(hid.astype(bf16), wd_ref[0], preferred_element_type=f32)
        scatter(b)

    @pl.when(b == n_blocks - 1)
    def _():
        @pl.when(b >= 1)
        def _():
            wait_rows(ybuf.at[1 - slot], yl_hbm, blk_n_ref[b - 1], sem_out.at[1 - slot])

        wait_rows(ybuf.at[slot], yl_hbm, n_b, sem_out.at[slot])


def _moe_segments(xl, tables, w_gate, w_up, w_down, *, blk, ts, n_tiles):
    blk_e = tables[0]
    n_blocks = blk_e.shape[0]
    assert blk & (blk - 1) == 0 and blk % SEG_ALIGN == 0
    wspec = lambda shape: pl.BlockSpec((1,) + shape, lambda b, be, *_: (be[b], 0, 0))
    hbm = pl.BlockSpec(memory_space=pl.ANY)
    return pl.pallas_call(
        functools.partial(_moe_seg_kernel, blk=blk, ts=ts, n_tiles=n_tiles, n_blocks=n_blocks),
        grid_spec=pltpu.PrefetchScalarGridSpec(
            num_scalar_prefetch=len(tables),
            grid=(n_blocks,),
            in_specs=[hbm, wspec((D_MODEL, D_EXPERT)), wspec((D_MODEL, D_EXPERT)), wspec((D_EXPERT, D_MODEL))],
            out_specs=hbm,
            scratch_shapes=[pltpu.VMEM((2, blk, D_MODEL), f32), pltpu.VMEM((2, blk, D_MODEL), f32),
                            pltpu.SemaphoreType.DMA((2,)), pltpu.SemaphoreType.DMA((2,)), pltpu.SemaphoreType.DMA,
                            pltpu.SMEM((2,), jnp.int32)],
        ),
        out_shape=jax.ShapeDtypeStruct(xl.shape, f32),
        compiler_params=_cparams(("arbitrary",)),
        name="moe_experts",
    )(*tables, xl, w_gate, w_up, w_down)


def _unsort_kernel(h_ref, wc_ref, yl_ref, *rest, ts, hosted):
    hosted_in, (y_ref,), hosted_out, _ = _split_hosted(rest, 1, hosted)
    if hosted:
        _cached_attn_kernel(*hosted_in, *hosted_out, **hosted)
    w = wc_ref[...]
    yl = yl_ref[...].astype(bf16)
    col = lax.broadcasted_iota(jnp.int32, (w.shape[0], ts), 1)
    y = h_ref[...]
    for k in range(2):
        pick = jnp.where(col == w[:, 2 + k:3 + k].astype(jnp.int32), 1.0, 0.0).astype(bf16)
        y = y + w[:, k:k + 1] * jnp.dot(pick, yl, preferred_element_type=f32)
    y_ref[...] = y


def _unsort(h, wc, yl, *, tm, ts, hosted=None):
    n = h.shape[0]
    h_ops, h_in, h_out, h_shape, h_params = _hosted_parts(hosted)
    rows = lambda r, w: pl.BlockSpec((r, w), lambda i: (i, 0))
    return pl.pallas_call(
        functools.partial(_unsort_kernel, ts=ts, hosted=h_params),
        grid=(n // tm,),
        in_specs=[rows(tm, D_MODEL), rows(tm, LANES), rows(ts, D_MODEL)] + h_in,
        out_specs=[rows(tm, D_MODEL)] + h_out,
        out_shape=[jax.ShapeDtypeStruct((n, D_MODEL), f32)] + h_shape,
        compiler_params=_cparams(("arbitrary",)),
        name="moe_unsort",
    )(h, wc, yl, *h_ops)


def _mix_and_moe(x, o, l, po, wts, *, tm, blk, host_merge=None, host_unsort=None):
    n = x.shape[0] * x.shape[1]
    n_tiles = n // tm
    nt = x.shape[1] // tm
    ts = _sorted_tile_rows(tm)
    hosted = host_merge(n_tiles, lambda b, i: b * nt + i) if host_merge else None
    h, xl, wc, tc, *merge_hosted = _merge(x, o, l, po, wts['ln1'], wts['w_gates'], wts['w_pa'], wts['w_pb'],
                                          wts['w_o'], wts['ln2'], wts['w_router'], wts['b_router'], tm=tm,
                                          hosted=hosted)
    c8 = (tc[:, ::LANES].T.astype(jnp.int32) + (SEG_ALIGN - 1)) // SEG_ALIGN * SEG_ALIGN
    seg_src = jnp.arange(n_tiles, dtype=jnp.int32)[:, None] * ts + jnp.cumsum(c8, axis=1) - c8
    seg_g = jnp.cumsum(c8, axis=0) - c8
    tot = jnp.sum(c8, axis=0)
    padded = (tot + blk - 1) // blk * blk
    pad_ends = jnp.cumsum(padded)
    n_blocks = -(-(2 * n + n_tiles * N_EXPERTS * (SEG_ALIGN - 1) + N_EXPERTS * (blk - 1)) // blk)
    blk_start = jnp.arange(n_blocks, dtype=jnp.int32) * blk
    blk_e = jnp.minimum(jnp.sum(pad_ends[None, :] <= blk_start[:, None], axis=1), N_EXPERTS - 1).astype(jnp.int32)
    pick = blk_e[:, None] == jnp.arange(N_EXPERTS, dtype=jnp.int32)[None, :]
    blk_r0 = blk_start - jnp.sum(jnp.where(pick, (pad_ends - padded)[None, :], 0), axis=1)
    blk_n = jnp.clip(jnp.sum(jnp.where(pick, tot[None, :], 0), axis=1) - blk_r0, 0, blk)
    used = jnp.sum(c8, axis=1)
    tables = tuple(a.astype(jnp.int32).reshape(-1) for a in (blk_e, blk_r0, blk_n, seg_g, c8, seg_src, used))
    yl = _moe_segments(xl, tables, wts['w_gate'], wts['w_up'], wts['w_down'], blk=blk, ts=ts, n_tiles=n_tiles)
    hosted = host_unsort(n_tiles, lambda i: i) if host_unsort else None
    y, *unsort_hosted = _unsort(h, wc, yl, tm=tm, ts=ts, hosted=hosted)
    return y, merge_hosted, unsort_hosted


def kernel(x_prompt, x_sample, cache_k_w128, cache_v_w128, cache_k_w512, cache_v_w512, cache_k_w2048, cache_v_w2048, state_pool, ln1, w_in, q_gain, k_gain, pool_lin, pool_scale, w_pa, w_pb, w_o, ln2, w_rg, b_rg, w_re, b_re, w_gate, w_up, w_down):
    B, T, D = x_prompt.shape
    Bd, S, _ = x_sample.shape
    past_len = 8192
    caches = ((cache_k_w128, cache_v_w128), (cache_k_w512, cache_v_w512), (cache_k_w2048, cache_v_w2048))
    slopes = jnp.exp2(-8.0 * jnp.arange(1, N_HEADS + 1, dtype=f32) / N_HEADS).reshape(N_GROUPS, HEADS_PER_GROUP)

    w_qkvu = w_in[:, :QKVU_W].astype(bf16)
    plin_bd = jnp.zeros((POOL_W, POOL_W), f32)
    for g in range(len(POOL_WINDOWS)):
        plin_bd = plin_bd.at[g * POOL_GW:(g + 1) * POOL_GW, g * POOL_GW:(g + 1) * POOL_GW].set(pool_lin[g])
    w_router = jnp.zeros((D, ROUTER_W), f32).at[:, :N_EXPERT_GROUPS].set(w_rg)
    w_router = w_router.at[:, EXPERT_COL0:EXPERT_COL0 + N_EXPERTS].set(w_re)
    b_router = jnp.zeros((1, ROUTER_W), f32).at[0, :N_EXPERT_GROUPS].set(b_rg)
    b_router = b_router.at[0, EXPERT_COL0:EXPERT_COL0 + N_EXPERTS].set(b_re)
    wts = dict(ln1=ln1.reshape(1, D), w_gates=w_in[:, QKVU_W:].astype(bf16), w_pa=w_pa.astype(bf16),
               w_pb=w_pb.astype(bf16), w_o=w_o.astype(bf16), ln2=ln2.reshape(1, D),
               w_router=w_router.astype(bf16), b_router=b_router,
               w_gate=w_gate.astype(bf16), w_up=w_up.astype(bf16), w_down=w_down.astype(bf16))
    proj_w = (wts['ln1'], w_qkvu, q_gain.reshape(1, ATTN_W), k_gain.reshape(1, ATTN_W),
              plin_bd.astype(bf16), pool_scale.reshape(1, POOL_W))

    n_s = Bd * S
    xs = x_sample.transpose(1, 0, 2).reshape(n_s, D)
    qs, ks, vs, pos, st = _proj_sample(xs, *proj_w, state_pool.transpose(1, 0, 2),
                                       n_seq=Bd, n_new=S, past_len=past_len)
    pad8 = lambda a: jnp.pad(a.reshape(S, Bd, GROUP_W).transpose(1, 0, 2), ((0, 0), (0, 8 - S), (0, 0)))

    def cached_operands(g):
        cols = slice(g * GROUP_W, (g + 1) * GROUP_W)
        kc, vc = caches[g]
        w = WINDOWS[g]
        return (pad8(qs[:, cols]), pad8(ks[:, cols]), pad8(vs[:, cols]),
                kc.transpose(0, 2, 3, 1).reshape(Bd, GROUP_W, w), vc.transpose(0, 2, 3, 1).reshape(Bd, GROUP_W, w))

    def host(g):
        def parts(steps, step_of):
            if Bd % steps:
                return None
            return _cached_call_parts(*cached_operands(g), g, slopes[g], n_new=S, nb=Bd // steps, step_of=step_of)
        return parts

    tm_proj = 256
    n_t = T // tm_proj
    hosted = host(2)(B * n_t, lambda b, i: b * n_t + i)
    *qkv, po, kt, vt, ut = _proj_prompt(x_prompt, *proj_w, tm=tm_proj, hosted=hosted)
    cached_out = {}
    if hosted:
        qkv, po, kt, vt, ut, cached_out[2] = qkv[:3 * N_GROUPS], qkv[3 * N_GROUPS], qkv[3 * N_GROUPS + 1], \
            qkv[3 * N_GROUPS + 2], qkv[3 * N_GROUPS + 3], [po, kt, vt, ut]
    o, l = zip(*[_band_attention(*qkv[3 * g:3 * g + 3], g, slopes[g]) for g in range(N_GROUPS)])
    y_prompt, in_merge, in_unsort = _mix_and_moe(x_prompt, o, l, po, wts, tm=512, blk=512,
                                                 host_merge=host(1), host_unsort=host(0))
    y_prompt = y_prompt.reshape(B, T, D)
    if in_merge:
        cached_out[1] = in_merge
    if in_unsort:
        cached_out[0] = in_unsort
    tail = kt.shape[2]
    pkv = []
    for g, w in enumerate(WINDOWS):
        for a in (kt, vt):
            a = a.reshape(B, N_HEADS, HEAD_DIM, tail)[:, g * HEADS_PER_GROUP:(g + 1) * HEADS_PER_GROUP, :, tail - w:]
            pkv.append(a.transpose(0, 3, 1, 2))
    p_pool = ut[:, 1:]

    so, sl, skv = [], [], []
    for g, w in enumerate(WINDOWS):
        if g in cached_out:
            og, lg, ko, vo = cached_out[g]
        else:
            og, lg, ko, vo = _cached_attention(*cached_operands(g), g, slopes[g], n_new=S)
        so.append(og[:, :S].transpose(1, 0, 2).reshape(1, 1, n_s, GROUP_W).astype(bf16))
        sl.append(lg[:, :S].transpose(1, 0, 2).reshape(1, 1, n_s, GROUP_W))
        for a in (ko, vo):
            skv.append(a.reshape(Bd, HEADS_PER_GROUP, HEAD_DIM, w).transpose(0, 3, 1, 2))
    y_sample = _mix_and_moe(xs[None], so, sl, pos[None], wts, tm=n_s, blk=128)[0].reshape(S, Bd, D).transpose(1, 0, 2)
    s_pool = st.transpose(1, 0, 2)

    return (y_prompt, y_sample, *pkv, p_pool, *skv, s_pool)
```

```python
import functools

import jax
import jax.numpy as jnp
from jax import lax
from jax.experimental import pallas as pl
from jax.experimental.pallas import tpu as pltpu

D_MODEL = 1024
HEAD_DIM = 64
HEADS_PER_GROUP = 4
WINDOWS = (128, 512, 2048)
DILATIONS = (1, 4, 16)
N_GROUPS = len(WINDOWS)
N_HEADS = HEADS_PER_GROUP * N_GROUPS
ATTN_W = N_HEADS * HEAD_DIM
GROUP_W = HEADS_PER_GROUP * HEAD_DIM
BAND = 128
POOL_WINDOWS = (2, 4, 8, 16)
POOL_GW = 128
POOL_W = len(POOL_WINDOWS) * POOL_GW
POOL_STATE = max(POOL_WINDOWS) - 1
POOL_HIST = 32
assert POOL_WINDOWS == (2, 4, 8, 16)
N_EXPERT_GROUPS = 4
EXPERTS_PER_GROUP = 8
N_EXPERTS = N_EXPERT_GROUPS * EXPERTS_PER_GROUP
D_EXPERT = 512
QKVU_W = 3 * ATTN_W + POOL_W
IN_W = QKVU_W + 2 * D_MODEL
NEG = -1e30
EPS = 1e-6
LANES = 128
ROUTER_W = LANES
EXPERT_COL0 = 8
VMEM_LIMIT = 56 * 1024 * 1024
IBUF_SLOTS = 2 * (N_GROUPS - 1) * (GROUP_W // LANES)
SEG_ALIGN = 8

assert all(w // d == BAND for w, d in zip(WINDOWS, DILATIONS))

f32 = jnp.float32
bf16 = jnp.bfloat16


def _cparams(sem):
    return pltpu.CompilerParams(dimension_semantics=sem, vmem_limit_bytes=VMEM_LIMIT)


def _rmsnorm_rows(x, g):
    ms = jnp.mean(x * x, axis=-1, keepdims=True)
    return x * lax.rsqrt(ms + EPS) * g


def _head_rmsnorm_chunk(ch, gain):
    lane = lax.broadcasted_iota(jnp.int32, ch.shape, 1)
    lo_mask = lane < HEAD_DIM
    sq = ch * ch
    lo = jnp.sum(jnp.where(lo_mask, sq, 0.0), axis=-1, keepdims=True)
    hi = jnp.sum(jnp.where(lo_mask, 0.0, sq), axis=-1, keepdims=True)
    ss = jnp.where(lo_mask, lo, hi)
    return ch * lax.rsqrt(ss * (1.0 / HEAD_DIM) + EPS) * gain


def _project(x, ln1, w_ref, qg, kg):
    xn = _rmsnorm_rows(x, ln1).astype(bf16)
    nch = ATTN_W // LANES
    qs = qg * (HEAD_DIM ** -0.5)
    zq = jnp.dot(xn, w_ref[:, 0:ATTN_W], preferred_element_type=f32)
    q = [_head_rmsnorm_chunk(zq[:, c * LANES:(c + 1) * LANES], qs[:, c * LANES:(c + 1) * LANES]) for c in range(nch)]
    zk = jnp.dot(xn, w_ref[:, ATTN_W:2 * ATTN_W], preferred_element_type=f32)
    k = [_head_rmsnorm_chunk(zk[:, c * LANES:(c + 1) * LANES], kg[:, c * LANES:(c + 1) * LANES]) for c in range(nch)]
    zv = jnp.dot(xn, w_ref[:, 2 * ATTN_W:3 * ATTN_W], preferred_element_type=f32)
    v = [zv[:, c * LANES:(c + 1) * LANES] for c in range(nch)]
    u = jnp.dot(xn, w_ref[:, 3 * ATTN_W:3 * ATTN_W + POOL_W], preferred_element_type=f32)
    sg = jax.nn.sigmoid(jnp.dot(xn, w_ref[:, QKVU_W:QKVU_W + 2 * D_MODEL], preferred_element_type=f32)).astype(bf16)
    return q, k, v, u, sg


def _proj_prompt_kernel(x_ref, ln1_ref, w_ref, qg_ref, kg_ref, plin_ref, pscale_ref, *rest,
                        tm, n_tiles, tail_tiles, hosted):
    n_hosted_in, n_hosted_out = (7, 4) if hosted else (0, 0)
    hosted_in, rest = rest[:n_hosted_in], rest[n_hosted_in:]
    qkv_refs = rest[:3 * N_GROUPS]
    po_ref, kt_ref, vt_ref, ut_ref, sg_ref = rest[3 * N_GROUPS:3 * N_GROUPS + 5]
    rest = rest[3 * N_GROUPS + 5:]
    hosted_out, (ubuf, pa, pb, sbuf) = rest[:n_hosted_out], rest[n_hosted_out:]
    i = pl.program_id(1)
    hist = POOL_STATE + 1
    ph, rows = POOL_HIST, POOL_HIST + tm

    @pl.when(i == 0)
    def _():
        ubuf[0:ph, :] = jnp.zeros((ph, POOL_W), f32)

    @pl.when(i > 0)
    def _():
        ubuf[0:ph, :] = ubuf[tm:tm + ph, :]

    stages = _cached_attn_stages(*hosted_in, *hosted_out, **hosted) if hosted else ()
    for s in stages:
        s()
    q, k, v, u, sg = _project(x_ref[0], ln1_ref[...], w_ref, qg_ref[...], kg_ref[...])
    sg_ref[0] = sg
    cpg = GROUP_W // LANES
    slot = 0
    for t, chunks in enumerate((q, k, v)):
        for g, dil in enumerate(DILATIONS):
            out_ref = qkv_refs[3 * g + t]
            for c in range(cpg):
                val = chunks[g * cpg + c]
                cols = slice(c * LANES, (c + 1) * LANES)
                if dil == 1:
                    out_ref[0, 0, :, cols] = val.astype(bf16)
                else:
                    sbuf[slot * tm:(slot + 1) * tm, :] = val
                    for r in range(dil):
                        out_ref[0, r, :, cols] = sbuf[pl.ds(slot * tm + r, tm // dil, stride=dil), :].astype(bf16)
                    slot += 1

    ubuf[ph:rows, :] = u
    gw = POOL_GW
    pa[8:rows, :] = ubuf[8:rows, :] + ubuf[7:rows - 1, :]
    pb[16:rows, gw:] = pa[16:rows, gw:] + pa[14:rows - 2, gw:]
    pa[24:rows, 2 * gw:] = pb[24:rows, 2 * gw:] + pb[20:rows - 4, 2 * gw:]
    pb[32:rows, 3 * gw:] = pa[32:rows, 3 * gw:] + pa[24:rows - 8, 3 * gw:]
    pos = i * tm + lax.broadcasted_iota(jnp.int32, (tm, POOL_GW), 0)
    zs = []
    for g, w in enumerate(POOL_WINDOWS):
        cols = slice(g * POOL_GW, (g + 1) * POOL_GW)
        wsum = (pa if g % 2 == 0 else pb)[ph:rows, cols]
        cnt = jnp.minimum(pos + 1, w).astype(f32)
        zs.append(wsum / cnt - u[:, cols])
    z = jnp.concatenate(zs, axis=-1).astype(bf16)
    po = jnp.dot(z, plin_ref[...], preferred_element_type=f32) * pscale_ref[...]
    po_ref[0] = po.astype(bf16)

    @pl.when(i >= n_tiles - tail_tiles)
    def _():
        kt_ref[0] = jnp.concatenate(k, axis=-1).T
        vt_ref[0] = jnp.concatenate(v, axis=-1).T

    @pl.when(i == n_tiles - 1)
    def _():
        ut_ref[0] = ubuf[rows - hist:rows, :]


def _proj_prompt(x, ln1, w_qkvu, qg, kg, plin_bd, pscale, *, tm, hosted=None):
    B, T, D = x.shape
    n_tiles = T // tm
    tail = max(WINDOWS)
    assert T % tm == 0 and tail % tm == 0 and T >= tail
    tail_tiles = tail // tm
    hist = POOL_STATE + 1
    kern = functools.partial(_proj_prompt_kernel, tm=tm, n_tiles=n_tiles, tail_tiles=tail_tiles,
                             hosted=hosted['params'] if hosted else None)
    h_ops, h_in, h_out, h_shape = ((), [], [], []) if not hosted else (
        hosted['operands'], hosted['in_specs'], hosted['out_specs'], hosted['out_shape'])
    const = lambda b, i: (0, 0)
    assert all(tm % (16 * d) == 0 for d in DILATIONS)
    qkv_specs = [pl.BlockSpec((1, d, tm // d, GROUP_W), lambda b, i: (b, 0, i, 0)) for d in DILATIONS for _ in range(3)]
    qkv_shapes = [jax.ShapeDtypeStruct((B, d, T // d, GROUP_W), bf16) for d in DILATIONS for _ in range(3)]
    tail_spec = pl.BlockSpec((1, ATTN_W, tm), lambda b, i: (b, 0, jnp.maximum(i - (n_tiles - tail_tiles), 0)))
    return pl.pallas_call(
        kern,
        grid=(B, n_tiles),
        in_specs=[
            pl.BlockSpec((1, tm, D), lambda b, i: (b, i, 0)),
            pl.BlockSpec((1, D), const),
            pl.BlockSpec((D, IN_W), const),
            pl.BlockSpec((1, ATTN_W), const),
            pl.BlockSpec((1, ATTN_W), const),
            pl.BlockSpec((POOL_W, POOL_W), const),
            pl.BlockSpec((1, POOL_W), const),
        ] + h_in,
        out_specs=qkv_specs
        + [pl.BlockSpec((1, tm, POOL_W), lambda b, i: (b, i, 0)),
           tail_spec, tail_spec,
           pl.BlockSpec((1, hist, POOL_W), lambda b, i: (b, 0, 0)),
           pl.BlockSpec((1, tm, 2 * D), lambda b, i: (b, i, 0))] + h_out,
        out_shape=qkv_shapes
        + [jax.ShapeDtypeStruct((B, T, POOL_W), bf16),
           jax.ShapeDtypeStruct((B, ATTN_W, tail), f32),
           jax.ShapeDtypeStruct((B, ATTN_W, tail), f32),
           jax.ShapeDtypeStruct((B, hist, POOL_W), f32),
           jax.ShapeDtypeStruct((B, T, 2 * D), bf16)] + h_shape,
        scratch_shapes=[pltpu.VMEM((POOL_HIST + tm, POOL_W), f32)] * 3
                       + [pltpu.VMEM((3 * (N_GROUPS - 1) * (GROUP_W // LANES) * tm, LANES), f32)],
        compiler_params=_cparams(("arbitrary", "arbitrary")),
        name="proj_prompt",
    )(x, ln1, w_qkvu, qg, kg, plin_bd, pscale, *h_ops)


def _proj_sample_kernel(x_ref, ln1_ref, w_ref, qg_ref, kg_ref, plin_ref, pscale_ref, state_ref,
                        q_ref, k_ref, v_ref, po_ref, st_ref, sg_ref, *, n_seq, n_new, past_len):
    q, k, v, u, sg = _project(x_ref[...], ln1_ref[...], w_ref, qg_ref[...], kg_ref[...])
    sg_ref[...] = sg
    q_ref[...] = jnp.concatenate(q, axis=-1)
    k_ref[...] = jnp.concatenate(k, axis=-1)
    v_ref[...] = jnp.concatenate(v, axis=-1)
    ext = [state_ref[j] for j in range(POOL_STATE)] + [u[s * n_seq:(s + 1) * n_seq, :] for s in range(n_new)]
    for s in range(n_new):
        zs = []
        for g, w in enumerate(POOL_WINDOWS):
            cols = slice(g * POOL_GW, (g + 1) * POOL_GW)
            cur = ext[POOL_STATE + s][:, cols]
            acc = cur
            for j in range(1, w):
                acc = acc + ext[POOL_STATE + s - j][:, cols]
            cnt = float(min(past_len + s + 1, w))
            zs.append(acc / cnt - cur)
        z = jnp.concatenate(zs, axis=-1).astype(bf16)
        po = jnp.dot(z, plin_ref[...], preferred_element_type=f32) * pscale_ref[...]
        po_ref[s * n_seq:(s + 1) * n_seq, :] = po.astype(bf16)
    for j in range(POOL_STATE):
        st_ref[j] = ext[j + n_new]


def _proj_sample(x, ln1, w_qkvu, qg, kg, plin_bd, pscale, state, *, n_seq, n_new, past_len):
    n = n_seq * n_new
    kern = functools.partial(_proj_sample_kernel, n_seq=n_seq, n_new=n_new, past_len=past_len)
    return pl.pallas_call(
        kern,
        out_shape=[jax.ShapeDtypeStruct((n, ATTN_W), f32)] * 3
        + [jax.ShapeDtypeStruct((n, POOL_W), bf16),
           jax.ShapeDtypeStruct((POOL_STATE, n_seq, POOL_W), f32),
           jax.ShapeDtypeStruct((n, 2 * D_MODEL), bf16)],
        compiler_params=pltpu.CompilerParams(vmem_limit_bytes=VMEM_LIMIT),
        name="proj_sample",
    )(x, ln1, w_qkvu, qg, kg, plin_bd, pscale, state)


def _head_masks(shape):
    lane = lax.broadcasted_iota(jnp.int32, shape, len(shape) - 1)
    return [(lane >= h * HEAD_DIM) & (lane < (h + 1) * HEAD_DIM) for h in range(HEADS_PER_GROUP)]


def _band_attn_kernel(q_ref, kc_ref, kp_ref, vc_ref, vp_ref, bias_ref, o_ref, l_ref, kbuf, vbuf, *, tl, unroll):
    i = pl.program_id(2)
    kbuf[0:BAND, :] = kp_ref[0, 0]
    kbuf[BAND:2 * BAND, :] = kc_ref[0, 0, 0:BAND, :]
    vbuf[0:BAND, :] = vp_ref[0, 0]
    vbuf[BAND:2 * BAND, :] = vc_ref[0, 0, 0:BAND, :]
    masks = _head_masks((BAND, GROUP_W))

    def sub_block(j, kk, vv, var):
        r0 = j * BAND if isinstance(j, int) else pl.multiple_of(j * BAND, BAND)
        q = q_ref[0, 0, pl.ds(r0, BAND), :]
        qm = jnp.concatenate([jnp.where(m, q, jnp.zeros_like(q)) for m in masks], axis=0)
        s = lax.dot_general(qm, kk, (((1,), (1,)), ((), ())), preferred_element_type=f32)
        s = s + bias_ref[var]
        m = jnp.max(s, axis=-1, keepdims=True)
        p = jnp.exp(s - m)
        den = jnp.sum(p, axis=-1, keepdims=True)
        pv = jnp.dot(p.astype(bf16), vv, preferred_element_type=f32)
        o = jnp.zeros((BAND, GROUP_W), f32)
        ms = jnp.zeros((BAND, GROUP_W), f32)
        ds = jnp.ones((BAND, GROUP_W), f32)
        for h, msk in enumerate(masks):
            rows = slice(h * BAND, (h + 1) * BAND)
            o = jnp.where(msk, pv[rows], o)
            ms = jnp.where(msk, m[rows], ms)
            ds = jnp.where(msk, den[rows], ds)
        o_ref[0, 0, pl.ds(r0, BAND), :] = (o / ds).astype(bf16)
        l_ref[0, 0, pl.ds(r0, BAND), :] = ms + jnp.log(ds)

    sub_block(0, kbuf[...], vbuf[...], jnp.where(i == 0, 0, 1))

    def body(j, carry):
        k0 = pl.multiple_of((j - 1) * BAND, BAND)
        sub_block(j, kc_ref[0, 0, pl.ds(k0, 2 * BAND), :], vc_ref[0, 0, pl.ds(k0, 2 * BAND), :], 1)
        return carry

    if tl > BAND:
        lax.fori_loop(1, tl // BAND, body, 0, unroll=unroll)


def _band_bias(slopes_g, dil):
    qi = jnp.arange(BAND)[:, None]
    kb = jnp.arange(2 * BAND)[None, :]
    rel = qi + BAND - kb
    valid = (rel >= 0) & (rel <= BAND)
    alibi = -slopes_g[:, None, None] * (dil * rel)[None].astype(f32)
    variants = []
    for first in (True, False):
        ok = valid & (kb >= BAND) if first else valid
        variants.append(jnp.where(ok[None], alibi, NEG).reshape(HEADS_PER_GROUP * BAND, 2 * BAND))
    return jnp.stack(variants, axis=0)


def _band_attention(q, k, v, g, slopes_g, *, tl_max=1024, unroll=7):
    B, dil, L, _ = q.shape
    tl = min(tl_max, L)
    assert dil == DILATIONS[g] and L % tl == 0 and tl % BAND == 0
    nsub = tl // BAND
    bias = _band_bias(slopes_g, dil)
    cur = pl.BlockSpec((1, 1, tl, GROUP_W), lambda b, r, i: (b, r, i, 0))
    prev = pl.BlockSpec((1, 1, BAND, GROUP_W), lambda b, r, i: (b, r, jnp.maximum(i * nsub - 1, 0), 0))
    return pl.pallas_call(
        functools.partial(_band_attn_kernel, tl=tl, unroll=max(1, min(unroll, nsub - 1))),
        grid=(B, dil, L // tl),
        in_specs=[cur, cur, prev, cur, prev,
                  pl.BlockSpec((2, HEADS_PER_GROUP * BAND, 2 * BAND), lambda b, r, i: (0, 0, 0))],
        out_specs=[cur, cur],
        out_shape=[jax.ShapeDtypeStruct((B, dil, L, GROUP_W), bf16),
                   jax.ShapeDtypeStruct((B, dil, L, GROUP_W), f32)],
        scratch_shapes=[pltpu.VMEM((2 * BAND, GROUP_W), bf16), pltpu.VMEM((2 * BAND, GROUP_W), bf16)],
        compiler_params=_cparams(("arbitrary", "arbitrary", "arbitrary")),
        name="band_attn_g%d" % g,
    )(q, k, k, v, v, bias)


def _cached_attn_kernel(*refs, **params):
    for stage in _cached_attn_stages(*refs, **params):
        stage()


def _cached_attn_stages(q_ref, kn_ref, vn_ref, kc_ref, vc_ref, bc_ref, bn_ref,
                        o_ref, l_ref, ko_ref, vo_ref, *, nb, n_new, win):
    def roll(c_ref, new_ref, out_ref):
        lane_t = lax.broadcasted_iota(jnp.int32, (nb, GROUP_W, LANES), 2)
        rolled = pltpu.roll(c_ref[...], win - n_new, axis=2)
        out_ref[...] = rolled
        new_t = jnp.swapaxes(jnp.concatenate([new_ref[...], jnp.zeros((nb, LANES - 8, GROUP_W), f32)], axis=1), 1, 2)
        new_t = pltpu.roll(new_t, LANES - n_new, axis=2)
        out_ref[:, :, win - LANES:win] = jnp.where(lane_t >= LANES - n_new, new_t, rolled[:, :, win - LANES:win])

    return (functools.partial(_cached_attn_scores, q_ref, kn_ref, vn_ref, kc_ref, vc_ref, bc_ref, bn_ref, o_ref, l_ref,
                              nb=nb, n_new=n_new),
            functools.partial(roll, kc_ref, kn_ref, ko_ref),
            functools.partial(roll, vc_ref, vn_ref, vo_ref))


def _cached_attn_scores(q_ref, kn_ref, vn_ref, kc_ref, vc_ref, bc_ref, bn_ref, o_ref, l_ref, *, nb, n_new):
    masks8 = _head_masks((nb, 8, GROUP_W))
    q8, kn8, vn8 = q_ref[...], kn_ref[...], vn_ref[...]
    kc, vc = kc_ref[...], vc_ref[...]
    qm = jnp.concatenate([jnp.where(m, q8, 0.0) for m in masks8], axis=1)
    sc = jnp.einsum('bqd,bdk->bqk', qm.astype(bf16), kc.astype(bf16), preferred_element_type=f32) + bc_ref[...]
    m = jnp.max(sc, axis=-1, keepdims=True)
    sn = []
    for t in range(n_new):
        col = jnp.sum(qm * kn8[:, t:t + 1, :], axis=-1, keepdims=True) + bn_ref[:, t:t + 1]
        sn.append(col)
        m = jnp.maximum(m, col)
    pc = jnp.exp(sc - m)
    den = jnp.sum(pc, axis=-1, keepdims=True)
    acc = jnp.einsum('bqk,bdk->bqd', pc.astype(bf16), vc.astype(bf16), preferred_element_type=f32)
    for t in range(n_new):
        pn = jnp.exp(sn[t] - m)
        den = den + pn
        acc = acc + pn * vn8[:, t:t + 1, :]
    acc = acc / den
    lse = m + jnp.log(den)
    o = jnp.zeros((nb, 8, GROUP_W), f32)
    l = jnp.zeros((nb, 8, GROUP_W), f32)
    for h, msk in enumerate(masks8):
        o = jnp.where(msk, acc[:, h * 8:(h + 1) * 8, :], o)
        l = jnp.where(msk, lse[:, h * 8:(h + 1) * 8, :], l)
    o_ref[...] = o
    l_ref[...] = l


def _cached_bias(slopes_g, dil, win, n_new):
    s = jnp.arange(8)[:, None]
    i = jnp.arange(win)[None, :]
    dist = win + s - i
    ok = (dist % dil == 0) & (dist // dil <= BAND) & (s < n_new)
    bc = jnp.where(ok[None], -slopes_g[:, None, None] * dist[None].astype(f32), NEG)
    t = jnp.arange(8)[None, :]
    dn = s - t
    okn = (dn >= 0) & (dn % dil == 0) & (dn // dil <= BAND) & (s < n_new) & (t < n_new)
    bn = jnp.where(okn[None], -slopes_g[:, None, None] * dn[None].astype(f32), NEG)
    pad = (s >= n_new)
    bc = jnp.where(pad[None], 0.0, bc)
    bn = jnp.where(pad[None], 0.0, bn)
    return bc.reshape(HEADS_PER_GROUP * 8, win), bn.reshape(HEADS_PER_GROUP * 8, 8)


def _cached_call_parts(q8, kn8, vn8, kc_t, vc_t, g, slopes_g, *, n_new, nb, step_of):
    Bd, _, win = kc_t.shape
    assert win == WINDOWS[g] and win % LANES == 0 and Bd % nb == 0
    bc, bn = _cached_bias(slopes_g, DILATIONS[g], win, n_new)
    small = pl.BlockSpec((nb, 8, GROUP_W), lambda *idx: (step_of(*idx), 0, 0))
    cache = pl.BlockSpec((nb, GROUP_W, win), lambda *idx: (step_of(*idx), 0, 0))
    const = lambda a: pl.BlockSpec(a.shape, lambda *idx: (0, 0))
    return dict(
        operands=(q8, kn8, vn8, kc_t, vc_t, bc, bn),
        in_specs=[small, small, small, cache, cache, const(bc), const(bn)],
        out_specs=[small, small, cache, cache],
        out_shape=[jax.ShapeDtypeStruct((Bd, 8, GROUP_W), f32)] * 2 + [jax.ShapeDtypeStruct((Bd, GROUP_W, win), f32)] * 2,
        params=dict(nb=nb, n_new=n_new, win=win))


def _cached_attention(q8, kn8, vn8, kc_t, vc_t, g, slopes_g, *, n_new):
    Bd, _, win = kc_t.shape
    nb = max(1, min(Bd, 2048 // win))
    parts = _cached_call_parts(q8, kn8, vn8, kc_t, vc_t, g, slopes_g, n_new=n_new, nb=nb, step_of=lambda b: b)
    return pl.pallas_call(
        functools.partial(_cached_attn_kernel, **parts['params']),
        grid=(Bd // nb,),
        in_specs=parts['in_specs'],
        out_specs=parts['out_specs'],
        out_shape=parts['out_shape'],
        compiler_params=_cparams(("arbitrary",)),
        name="cached_attn_g%d" % g,
    )(*parts['operands'])


def _split_hosted(rest, n_own_out, hosted):
    n_in, n_out = (7, 4) if hosted else (0, 0)
    hosted_in, rest = rest[:n_in], rest[n_in:]
    own_out, rest = rest[:n_own_out], rest[n_own_out:]
    return hosted_in, own_out, rest[:n_out], rest[n_out:]


def _merge_kernel(x_ref, o0_ref, o1_ref, o2_ref, l0_ref, l1_ref, l2_ref, po_ref, sg_ref,
                  wpa_ref, wpb_ref, wo_ref, ln2_ref, wr_ref, br_ref, tri_ref, ltri_ref,
                  *rest, tm, ts, dils, hosted):
    hosted_in, (h_ref, xl_ref, wc_ref, tc_ref), hosted_out, (ibuf,) = _split_hosted(rest, 4, hosted)
    if hosted:
        _cached_attn_kernel(*hosted_in, *hosted_out, **hosted)
    slots = iter(range(IBUF_SLOTS))

    def token_order(ref, dil):
        if dil == 1:
            return ref[0, 0].astype(f32)
        chunks = []
        for c in range(GROUP_W // LANES):
            base = next(slots) * tm
            for r in range(dil):
                ibuf[pl.ds(base + r, tm // dil, stride=dil), :] = ref[0, r, :, c * LANES:(c + 1) * LANES].astype(f32)
            chunks.append(ibuf[base:base + tm, :])
        return jnp.concatenate(chunks, axis=-1)

    x = x_ref[0]
    l0, l1, l2 = (token_order(r, d) for r, d in zip((l0_ref, l1_ref, l2_ref), dils))
    lm = jnp.maximum(jnp.maximum(l0, l1), l2)
    e0, e1, e2 = jnp.exp(l0 - lm), jnp.exp(l1 - lm), jnp.exp(l2 - lm)
    o0, o1, o2 = (token_order(r, d) for r, d in zip((o0_ref, o1_ref, o2_ref), dils))
    attn = (e0 * o0 + e1 * o1 + e2 * o2) / (e0 + e1 + e2)
    ma = jnp.dot(attn.astype(bf16), wpa_ref[...], preferred_element_type=f32)
    mb = jnp.dot(po_ref[0], wpb_ref[...], preferred_element_type=f32)
    mix = sg_ref[0, :, :D_MODEL].astype(f32) * ma + sg_ref[0, :, D_MODEL:].astype(f32) * mb
    h = x + jnp.dot(mix.astype(bf16), wo_ref[...], preferred_element_type=f32)
    h_ref[...] = h
    xn2 = _rmsnorm_rows(h, ln2_ref[...]).astype(bf16)

    lt = (jnp.dot(xn2, wr_ref[...], preferred_element_type=f32) + br_ref[...]).T
    row8 = lax.broadcasted_iota(jnp.int32, (8, tm), 0)
    gl = jnp.where(row8 < N_EXPERT_GROUPS, lt[0:8], -jnp.inf)
    gmax = jnp.max(gl, axis=0, keepdims=True)
    gidx = jnp.min(jnp.where(gl == gmax, row8, 8), axis=0, keepdims=True)
    pg = 1.0 / jnp.sum(jnp.exp(gl - gmax), axis=0, keepdims=True)
    sel = jnp.zeros((8, tm), f32)
    for g in range(N_EXPERT_GROUPS):
        lo = EXPERT_COL0 + g * EXPERTS_PER_GROUP
        sel = jnp.where(gidx == g, lt[lo:lo + EXPERTS_PER_GROUP], sel)
    v0 = jnp.max(sel, axis=0, keepdims=True)
    i0 = jnp.min(jnp.where(sel == v0, row8, 8), axis=0, keepdims=True)
    sel2 = jnp.where(row8 == i0, -jnp.inf, sel)
    v1 = jnp.max(sel2, axis=0, keepdims=True)
    i1 = jnp.min(jnp.where(sel2 == v1, row8, 8), axis=0, keepdims=True)
    t = jnp.exp(v1 - v0)
    w0 = pg / (1.0 + t)
    w1 = pg * t / (1.0 + t)
    eid0 = gidx * EXPERTS_PER_GROUP + i0
    eid1 = gidx * EXPERTS_PER_GROUP + i1
    erow = lax.broadcasted_iota(jnp.int32, (N_EXPERTS, tm), 0)
    oh0 = erow == eid0
    oh1 = erow == eid1
    cnt = jnp.where(oh0, 1.0, jnp.where(oh1, 1.0, 0.0))
    before = jnp.dot(cnt.astype(bf16), tri_ref[...], preferred_element_type=f32)
    tcount = jnp.sum(cnt, axis=1, keepdims=True)
    units = jnp.floor((tcount + (SEG_ALIGN - 1)) * (1.0 / SEG_ALIGN))
    ub = jnp.broadcast_to(units, (N_EXPERTS, LANES)).astype(bf16)
    seg0 = SEG_ALIGN * jnp.dot(ltri_ref[...], ub, preferred_element_type=f32)[:, 0:1]
    pos_e = seg0 + before
    lpos0 = jnp.sum(jnp.where(oh0, pos_e, 0.0), axis=0, keepdims=True)
    lpos1 = jnp.sum(jnp.where(oh1, pos_e, 0.0), axis=0, keepdims=True)
    prow = lax.broadcasted_iota(jnp.int32, (ts, tm), 0)
    perm = jnp.where(prow == lpos0.astype(jnp.int32), 1.0, jnp.where(prow == lpos1.astype(jnp.int32), 1.0, 0.0))
    xl_ref[...] = jnp.dot(perm.astype(bf16), xn2, preferred_element_type=f32)
    tc_ref[...] = jnp.broadcast_to(tcount, (N_EXPERTS, LANES))
    rowl = lax.broadcasted_iota(jnp.int32, (LANES, tm), 0)
    wslab = jnp.zeros((LANES, tm), f32)
    for r, val in enumerate((w0, w1, lpos0, lpos1)):
        wslab = jnp.where(rowl == r, val, wslab)
    wc_ref[...] = wslab.T


def _hosted_parts(hosted):
    if not hosted:
        return (), [], [], [], None
    return hosted['operands'], hosted['in_specs'], hosted['out_specs'], hosted['out_shape'], hosted['params']


def _merge(x, o, l, po, sg, w_pa, w_pb, w_o, ln2, w_router, b_router, *, tm, hosted=None):
    B, T, _ = x.shape
    h_ops, h_in, h_out, h_shape, h_params = _hosted_parts(hosted)
    assert T % tm == 0
    nt = T // tm
    n = B * T
    dils = tuple(a.shape[1] for a in o)
    assert all(tm % (8 * d) == 0 for d in dils)
    assert 2 * tm // SEG_ALIGN <= 256
    ts = _sorted_tile_rows(tm)
    tri = (jnp.arange(tm)[:, None] < jnp.arange(tm)[None, :]).astype(bf16)
    ltri = (jnp.arange(N_EXPERTS)[None, :] < jnp.arange(N_EXPERTS)[:, None]).astype(bf16)
    rows3 = lambda w: pl.BlockSpec((1, tm, w), lambda b, i: (b, i, 0))
    flat = lambda r, w: pl.BlockSpec((r, w), lambda b, i: (b * nt + i, 0))
    grp = [pl.BlockSpec((1, d, tm // d, GROUP_W), lambda b, i: (b, 0, i, 0)) for d in dils]
    full = lambda a: pl.BlockSpec(a.shape, lambda b, i: (0,) * a.ndim)
    weights = (w_pa, w_pb, w_o, ln2, w_router, b_router, tri, ltri)
    return pl.pallas_call(
        functools.partial(_merge_kernel, tm=tm, ts=ts, dils=dils, hosted=h_params),
        grid=(B, nt),
        in_specs=[rows3(D_MODEL)] + grp + grp + [rows3(POOL_W), rows3(2 * D_MODEL)]
        + [full(a) for a in weights] + h_in,
        out_specs=[flat(tm, D_MODEL), flat(ts, D_MODEL), flat(tm, LANES),
                   pl.BlockSpec((N_EXPERTS, LANES), lambda b, i: (0, b * nt + i))] + h_out,
        out_shape=[jax.ShapeDtypeStruct((n, D_MODEL), f32),
                   jax.ShapeDtypeStruct((B * nt * ts, D_MODEL), f32),
                   jax.ShapeDtypeStruct((n, LANES), f32),
                   jax.ShapeDtypeStruct((N_EXPERTS, B * nt * LANES), f32)] + h_shape,
        scratch_shapes=[pltpu.VMEM((IBUF_SLOTS * tm, LANES), f32)],
        compiler_params=_cparams(("arbitrary", "arbitrary")),
        name="merge_router",
    )(x, *o, *l, po, sg, *weights, *h_ops)


def _sorted_tile_rows(tm):
    return -(-(2 * tm + N_EXPERTS * (SEG_ALIGN - 1)) // LANES) * LANES


def _moe_seg_kernel(blk_e_ref, blk_r0_ref, blk_n_ref, seg_g_ref, seg_c_ref, seg_src_ref, used_ref,
                    xl_hbm, wg_ref, wu_ref, wd_ref, yl_hbm,
                    xbuf, ybuf, sem_in, sem_out, sem_zero, ptr, *, blk, ts, n_tiles, n_blocks):
    b = pl.program_id(0)
    slot = lax.rem(b, 2)
    unit_bits = (blk // SEG_ALIGN).bit_length()

    def row_pieces(rows):
        units = lax.shift_right_logical(rows, jnp.int32(SEG_ALIGN.bit_length() - 1))
        off = jnp.int32(0)
        for bit in reversed(range(unit_bits)):
            on = lax.shift_right_logical(units, jnp.int32(bit)) & 1
            yield on == 1, off, SEG_ALIGN << bit
            off = off + on * (SEG_ALIGN << bit)

    def copy_rows(src, src_row, dst, dst_row, rows, sem):
        def start(off, size):
            pltpu.make_async_copy(src.at[pl.ds(pl.multiple_of(src_row + off, SEG_ALIGN), size)],
                                  dst.at[pl.ds(pl.multiple_of(dst_row + off, SEG_ALIGN), size)], sem).start()

        chunk = SEG_ALIGN << 3
        n_chunks = lax.shift_right_logical(rows, jnp.int32(chunk.bit_length() - 1))

        def whole_chunk(j, c):
            start(j * chunk, chunk)
            return c

        lax.fori_loop(0, n_chunks, whole_chunk, 0)
        off = n_chunks * chunk
        rest = rows - off
        for bit in (2, 1, 0):
            size = SEG_ALIGN << bit
            on = (rest & size) != 0

            @pl.when(on)
            def _():
                start(off, size)

            off = off + jnp.where(on, size, 0)

    def wait_rows(src, dst, rows, sem):
        for on, _, size in row_pieces(rows):
            @pl.when(on)
            def _():
                pltpu.make_async_copy(src.at[pl.ds(0, size)], dst.at[pl.ds(0, size)], sem).wait()

    def for_pieces(bb, stream, fn):
        e, r0, n = blk_e_ref[bb], blk_r0_ref[bb], blk_n_ref[bb]

        @pl.when(n > 0)
        def _():
            def seg(i):
                return jnp.minimum(i, n_tiles - 1) * N_EXPERTS + e

            def cond(i):
                return (i < n_tiles) & (seg_g_ref[seg(i)] < r0 + n)

            def body(i):
                g = seg_g_ref[seg(i)]
                lo = jnp.maximum(g, r0)
                hi = jnp.minimum(g + seg_c_ref[seg(i)], r0 + n)

                @pl.when(hi > lo)
                def _():
                    fn(seg_src_ref[seg(i)] + (lo - g), lo - r0, hi - lo)

                return i + 1

            end = lax.while_loop(cond, body, jnp.where(r0 == 0, 0, ptr[stream]))
            ptr[stream] = jnp.maximum(end - 1, 0)

    def gather(bb):
        s = lax.rem(bb, 2)
        for_pieces(bb, 0, lambda lrow, brow, rows: copy_rows(xl_hbm, lrow, xbuf.at[s], brow, rows, sem_in.at[s]))

    def scatter(bb):
        s = lax.rem(bb, 2)
        for_pieces(bb, 1, lambda lrow, brow, rows: copy_rows(ybuf.at[s], brow, yl_hbm, lrow, rows, sem_out.at[s]))

    def zero_tail(i, go):
        row0 = i * ts + used_ref[i]
        rows = ts - used_ref[i]
        whole = lax.shift_right_logical(rows, jnp.int32(blk.bit_length() - 1))
        zsrc = xbuf.at[1]

        def whole_block(j, c):
            cp = pltpu.make_async_copy(zsrc, yl_hbm.at[pl.ds(pl.multiple_of(row0 + j * blk, SEG_ALIGN), blk)], sem_zero)
            cp.start() if go else cp.wait()
            return c

        lax.fori_loop(0, whole, whole_block, 0)
        rest = rows - whole * blk
        if go:
            copy_rows(zsrc, 0, yl_hbm, row0 + whole * blk, rest, sem_zero)
        else:
            wait_rows(zsrc, yl_hbm, rest, sem_zero)

    @pl.when(b == 0)
    def _():
        xbuf[...] = jnp.zeros_like(xbuf)
        ptr[0] = 0
        ptr[1] = 0
        gather(0)
        for go in (True, False):
            def per_tile(i, c, go=go):
                zero_tail(i, go)
                return c

            lax.fori_loop(0, n_tiles, per_tile, 0)

    n_b = blk_n_ref[b]
    wait_rows(xl_hbm, xbuf.at[slot], n_b, sem_in.at[slot])

    @pl.when(b + 1 < n_blocks)
    def _():
        gather(b + 1)

    @pl.when(b >= 2)
    def _():
        wait_rows(ybuf.at[slot], yl_hbm, blk_n_ref[b - 2], sem_out.at[slot])

    @pl.when(n_b > 0)
    def _():
        x = xbuf[slot].astype(bf16)
        hid = jax.nn.silu(jnp.dot(x, wg_ref[0], preferred_element_type=f32)) * jnp.dot(x, wu_ref[0], preferred_element_type=f32)
        ybuf[slot] = jnp.dot(hid.astype(bf16), wd_ref[0], preferred_element_type=f32)
        scatter(b)

    @pl.when(b == n_blocks - 1)
    def _():
        @pl.when(b >= 1)
        def _():
            wait_rows(ybuf.at[1 - slot], yl_hbm, blk_n_ref[b - 1], sem_out.at[1 - slot])

        wait_rows(ybuf.at[slot], yl_hbm, n_b, sem_out.at[slot])


def _moe_segments(xl, tables, w_gate, w_up, w_down, *, blk, ts, n_tiles):
    blk_e = tables[0]
    n_blocks = blk_e.shape[0]
    assert blk & (blk - 1) == 0 and blk % SEG_ALIGN == 0
    wspec = lambda shape: pl.BlockSpec((1,) + shape, lambda b, be, *_: (be[b], 0, 0))
    hbm = pl.BlockSpec(memory_space=pl.ANY)
    return pl.pallas_call(
        functools.partial(_moe_seg_kernel, blk=blk, ts=ts, n_tiles=n_tiles, n_blocks=n_blocks),
        grid_spec=pltpu.PrefetchScalarGridSpec(
            num_scalar_prefetch=len(tables),
            grid=(n_blocks,),
            in_specs=[hbm, wspec((D_MODEL, D_EXPERT)), wspec((D_MODEL, D_EXPERT)), wspec((D_EXPERT, D_MODEL))],
            out_specs=hbm,
            scratch_shapes=[pltpu.VMEM((2, blk, D_MODEL), f32), pltpu.VMEM((2, blk, D_MODEL), f32),
                            pltpu.SemaphoreType.DMA((2,)), pltpu.SemaphoreType.DMA((2,)), pltpu.SemaphoreType.DMA,
                            pltpu.SMEM((2,), jnp.int32)],
        ),
        out_shape=jax.ShapeDtypeStruct(xl.shape, f32),
        compiler_params=_cparams(("arbitrary",)),
        name="moe_experts",
    )(*tables, xl, w_gate, w_up, w_down)


def _unsort_kernel(h_ref, wc_ref, yl_ref, *rest, ts, hosted):
    hosted_in, (y_ref,), hosted_out, _ = _split_hosted(rest, 1, hosted)
    if hosted:
        _cached_attn_kernel(*hosted_in, *hosted_out, **hosted)
    w = wc_ref[...]
    yl = yl_ref[...].astype(bf16)
    col = lax.broadcasted_iota(jnp.int32, (w.shape[0], ts), 1)
    y = h_ref[...]
    for k in range(2):
        pick = jnp.where(col == w[:, 2 + k:3 + k].astype(jnp.int32), 1.0, 0.0).astype(bf16)
        y = y + w[:, k:k + 1] * jnp.dot(pick, yl, preferred_element_type=f32)
    y_ref[...] = y


def _unsort(h, wc, yl, *, tm, ts, hosted=None):
    n = h.shape[0]
    h_ops, h_in, h_out, h_shape, h_params = _hosted_parts(hosted)
    rows = lambda r, w: pl.BlockSpec((r, w), lambda i: (i, 0))
    return pl.pallas_call(
        functools.partial(_unsort_kernel, ts=ts, hosted=h_params),
        grid=(n // tm,),
        in_specs=[rows(tm, D_MODEL), rows(tm, LANES), rows(ts, D_MODEL)] + h_in,
        out_specs=[rows(tm, D_MODEL)] + h_out,
        out_shape=[jax.ShapeDtypeStruct((n, D_MODEL), f32)] + h_shape,
        compiler_params=_cparams(("arbitrary",)),
        name="moe_unsort",
    )(h, wc, yl, *h_ops)


def _mix_and_moe(x, o, l, po, sg, wts, *, tm, blk, host_merge=None, host_unsort=None):
    n = x.shape[0] * x.shape[1]
    n_tiles = n // tm
    nt = x.shape[1] // tm
    ts = _sorted_tile_rows(tm)
    hosted = host_merge(n_tiles, lambda b, i: b * nt + i) if host_merge else None
    h, xl, wc, tc, *merge_hosted = _merge(x, o, l, po, sg, wts['w_pa'], wts['w_pb'], wts['w_o'], wts['ln2'],
                                          wts['w_router'], wts['b_router'], tm=tm, hosted=hosted)
    c8 = (tc[:, ::LANES].T.astype(jnp.int32) + (SEG_ALIGN - 1)) // SEG_ALIGN * SEG_ALIGN
    seg_src = jnp.arange(n_tiles, dtype=jnp.int32)[:, None] * ts + jnp.cumsum(c8, axis=1) - c8
    seg_g = jnp.cumsum(c8, axis=0) - c8
    tot = jnp.sum(c8, axis=0)
    padded = (tot + blk - 1) // blk * blk
    pad_ends = jnp.cumsum(padded)
    n_blocks = -(-(2 * n + n_tiles * N_EXPERTS * (SEG_ALIGN - 1) + N_EXPERTS * (blk - 1)) // blk)
    blk_start = jnp.arange(n_blocks, dtype=jnp.int32) * blk
    blk_e = jnp.minimum(jnp.sum(pad_ends[None, :] <= blk_start[:, None], axis=1), N_EXPERTS - 1).astype(jnp.int32)
    pick = blk_e[:, None] == jnp.arange(N_EXPERTS, dtype=jnp.int32)[None, :]
    blk_r0 = blk_start - jnp.sum(jnp.where(pick, (pad_ends - padded)[None, :], 0), axis=1)
    blk_n = jnp.clip(jnp.sum(jnp.where(pick, tot[None, :], 0), axis=1) - blk_r0, 0, blk)
    used = jnp.sum(c8, axis=1)
    tables = tuple(a.astype(jnp.int32).reshape(-1) for a in (blk_e, blk_r0, blk_n, seg_g, c8, seg_src, used))
    yl = _moe_segments(xl, tables, wts['w_gate'], wts['w_up'], wts['w_down'], blk=blk, ts=ts, n_tiles=n_tiles)
    hosted = host_unsort(n_tiles, lambda i: i) if host_unsort else None
    y, *unsort_hosted = _unsort(h, wc, yl, tm=tm, ts=ts, hosted=hosted)
    return y, merge_hosted, unsort_hosted


def kernel(x_prompt, x_sample, cache_k_w128, cache_v_w128, cache_k_w512, cache_v_w512, cache_k_w2048, cache_v_w2048, state_pool, ln1, w_in, q_gain, k_gain, pool_lin, pool_scale, w_pa, w_pb, w_o, ln2, w_rg, b_rg, w_re, b_re, w_gate, w_up, w_down):
    B, T, D = x_prompt.shape
    Bd, S, _ = x_sample.shape
    past_len = 8192
    caches = ((cache_k_w128, cache_v_w128), (cache_k_w512, cache_v_w512), (cache_k_w2048, cache_v_w2048))
    slopes = jnp.exp2(-8.0 * jnp.arange(1, N_HEADS + 1, dtype=f32) / N_HEADS).reshape(N_GROUPS, HEADS_PER_GROUP)

    plin_bd = jnp.zeros((POOL_W, POOL_W), f32)
    for g in range(len(POOL_WINDOWS)):
        plin_bd = plin_bd.at[g * POOL_GW:(g + 1) * POOL_GW, g * POOL_GW:(g + 1) * POOL_GW].set(pool_lin[g])
    w_router = jnp.zeros((D, ROUTER_W), f32).at[:, :N_EXPERT_GROUPS].set(w_rg)
    w_router = w_router.at[:, EXPERT_COL0:EXPERT_COL0 + N_EXPERTS].set(w_re)
    b_router = jnp.zeros((1, ROUTER_W), f32).at[0, :N_EXPERT_GROUPS].set(b_rg)
    b_router = b_router.at[0, EXPERT_COL0:EXPERT_COL0 + N_EXPERTS].set(b_re)
    wts = dict(w_pa=w_pa.astype(bf16), w_pb=w_pb.astype(bf16), w_o=w_o.astype(bf16), ln2=ln2.reshape(1, D),
               w_router=w_router.astype(bf16), b_router=b_router,
               w_gate=w_gate.astype(bf16), w_up=w_up.astype(bf16), w_down=w_down.astype(bf16))
    proj_w = (ln1.reshape(1, D), w_in.astype(bf16), q_gain.reshape(1, ATTN_W), k_gain.reshape(1, ATTN_W),
              plin_bd.astype(bf16), pool_scale.reshape(1, POOL_W))

    n_s = Bd * S
    xs = x_sample.transpose(1, 0, 2).reshape(n_s, D)
    qs, ks, vs, pos, st, sgs = _proj_sample(xs, *proj_w, state_pool.transpose(1, 0, 2),
                                            n_seq=Bd, n_new=S, past_len=past_len)
    pad8 = lambda a: jnp.pad(a.reshape(S, Bd, GROUP_W).transpose(1, 0, 2), ((0, 0), (0, 8 - S), (0, 0)))

    def cached_operands(g):
        cols = slice(g * GROUP_W, (g + 1) * GROUP_W)
        kc, vc = caches[g]
        w = WINDOWS[g]
        return (pad8(qs[:, cols]), pad8(ks[:, cols]), pad8(vs[:, cols]),
                kc.transpose(0, 2, 3, 1).reshape(Bd, GROUP_W, w), vc.transpose(0, 2, 3, 1).reshape(Bd, GROUP_W, w))

    def host(g):
        def parts(steps, step_of):
            if Bd % steps:
                return None
            return _cached_call_parts(*cached_operands(g), g, slopes[g], n_new=S, nb=Bd // steps, step_of=step_of)
        return parts

    tm_proj = 256
    n_t = T // tm_proj
    hosted = host(2)(B * n_t, lambda b, i: b * n_t + i)
    outs = _proj_prompt(x_prompt, *proj_w, tm=tm_proj, hosted=hosted)
    qkv, (po, kt, vt, ut, sg) = outs[:3 * N_GROUPS], outs[3 * N_GROUPS:3 * N_GROUPS + 5]
    cached_out = {}
    if hosted:
        cached_out[2] = outs[3 * N_GROUPS + 5:]
    o, l = zip(*[_band_attention(*qkv[3 * g:3 * g + 3], g, slopes[g]) for g in range(N_GROUPS)])
    y_prompt, in_merge, in_unsort = _mix_and_moe(x_prompt, o, l, po, sg, wts, tm=512, blk=512,
                                                 host_merge=host(1), host_unsort=host(0))
    y_prompt = y_prompt.reshape(B, T, D)
    if in_merge:
        cached_out[1] = in_merge
    if in_unsort:
        cached_out[0] = in_unsort
    tail = kt.shape[2]
    pkv = []
    for g, w in enumerate(WINDOWS):
        for a in (kt, vt):
            a = a.reshape(B, N_HEADS, HEAD_DIM, tail)[:, g * HEADS_PER_GROUP:(g + 1) * HEADS_PER_GROUP, :, tail - w:]
            pkv.append(a.transpose(0, 3, 1, 2))
    p_pool = ut[:, 1:]

    so, sl, skv = [], [], []
    for g, w in enumerate(WINDOWS):
        if g in cached_out:
            og, lg, ko, vo = cached_out[g]
        else:
            og, lg, ko, vo = _cached_attention(*cached_operands(g), g, slopes[g], n_new=S)
        so.append(og[:, :S].transpose(1, 0, 2).reshape(1, 1, n_s, GROUP_W).astype(bf16))
        sl.append(lg[:, :S].transpose(1, 0, 2).reshape(1, 1, n_s, GROUP_W))
        for a in (ko, vo):
            skv.append(a.reshape(Bd, HEADS_PER_GROUP, HEAD_DIM, w).transpose(0, 3, 1, 2))
    y_sample = _mix_and_moe(xs[None], so, sl, pos[None], sgs[None], wts, tm=n_s, blk=128)[0]
    y_sample = y_sample.reshape(S, Bd, D).transpose(1, 0, 2)
    s_pool = st.transpose(1, 0, 2)

    return (y_prompt, y_sample, *pkv, p_pool, *skv, s_pool)
```

```python
import functools

import jax
import jax.numpy as jnp
from jax import lax
from jax.experimental import pallas as pl
from jax.experimental.pallas import tpu as pltpu

D_MODEL = 1024
HEAD_DIM = 64
HEADS_PER_GROUP = 4
WINDOWS = (128, 512, 2048)
DILATIONS = (1, 4, 16)
N_GROUPS = len(WINDOWS)
N_HEADS = HEADS_PER_GROUP * N_GROUPS
ATTN_W = N_HEADS * HEAD_DIM
GROUP_W = HEADS_PER_GROUP * HEAD_DIM
BAND = 128
POOL_WINDOWS = (2, 4, 8, 16)
POOL_GW = 128
POOL_W = len(POOL_WINDOWS) * POOL_GW
POOL_STATE = max(POOL_WINDOWS) - 1
POOL_HIST = 32
assert POOL_WINDOWS == (2, 4, 8, 16)
N_EXPERT_GROUPS = 4
EXPERTS_PER_GROUP = 8
N_EXPERTS = N_EXPERT_GROUPS * EXPERTS_PER_GROUP
D_EXPERT = 512
QKVU_W = 3 * ATTN_W + POOL_W
IN_W = QKVU_W + 2 * D_MODEL
NEG = -1e30
EPS = 1e-6
LANES = 128
ROUTER_W = LANES
EXPERT_COL0 = 8
VMEM_LIMIT = 56 * 1024 * 1024
IBUF_SLOTS = 2 * (N_GROUPS - 1) * (GROUP_W // LANES)
SEG_ALIGN = 8

assert all(w // d == BAND for w, d in zip(WINDOWS, DILATIONS))

f32 = jnp.float32
bf16 = jnp.bfloat16


def _cparams(sem):
    return pltpu.CompilerParams(dimension_semantics=sem, vmem_limit_bytes=VMEM_LIMIT)


def _rmsnorm_rows(x, g):
    ms = jnp.mean(x * x, axis=-1, keepdims=True)
    return x * lax.rsqrt(ms + EPS) * g


def _head_rmsnorm_chunk(ch, gain):
    lane = lax.broadcasted_iota(jnp.int32, ch.shape, 1)
    lo_mask = lane < HEAD_DIM
    sq = ch * ch
    lo = jnp.sum(jnp.where(lo_mask, sq, 0.0), axis=-1, keepdims=True)
    hi = jnp.sum(jnp.where(lo_mask, 0.0, sq), axis=-1, keepdims=True)
    ss = jnp.where(lo_mask, lo, hi)
    return ch * lax.rsqrt(ss * (1.0 / HEAD_DIM) + EPS) * gain


def _project(x, ln1, w_ref, qg, kg):
    xn = _rmsnorm_rows(x, ln1).astype(bf16)
    nch = ATTN_W // LANES
    qs = qg * (HEAD_DIM ** -0.5)
    zq = jnp.dot(xn, w_ref[:, 0:ATTN_W], preferred_element_type=f32)
    q = [_head_rmsnorm_chunk(zq[:, c * LANES:(c + 1) * LANES], qs[:, c * LANES:(c + 1) * LANES]) for c in range(nch)]
    zk = jnp.dot(xn, w_ref[:, ATTN_W:2 * ATTN_W], preferred_element_type=f32)
    k = [_head_rmsnorm_chunk(zk[:, c * LANES:(c + 1) * LANES], kg[:, c * LANES:(c + 1) * LANES]) for c in range(nch)]
    zv = jnp.dot(xn, w_ref[:, 2 * ATTN_W:3 * ATTN_W], preferred_element_type=f32)
    v = [zv[:, c * LANES:(c + 1) * LANES] for c in range(nch)]
    u = jnp.dot(xn, w_ref[:, 3 * ATTN_W:3 * ATTN_W + POOL_W], preferred_element_type=f32)
    sg = jax.nn.sigmoid(jnp.dot(xn, w_ref[:, QKVU_W:QKVU_W + 2 * D_MODEL], preferred_element_type=f32)).astype(bf16)
    return q, k, v, u, sg


def _proj_prompt_kernel(x_ref, ln1_ref, w_ref, qg_ref, kg_ref, plin_ref, pscale_ref, *rest,
                        tm, n_tiles, tail_tiles, hosted):
    n_hosted_in, n_hosted_out = (7, 4) if hosted else (0, 0)
    hosted_in, rest = rest[:n_hosted_in], rest[n_hosted_in:]
    qkv_refs = rest[:3 * N_GROUPS]
    po_ref, kt_ref, vt_ref, ut_ref, sg_ref = rest[3 * N_GROUPS:3 * N_GROUPS + 5]
    rest = rest[3 * N_GROUPS + 5:]
    hosted_out, (ubuf, pa, pb, sbuf) = rest[:n_hosted_out], rest[n_hosted_out:]
    i = pl.program_id(1)
    hist = POOL_STATE + 1
    ph, rows = POOL_HIST, POOL_HIST + tm

    @pl.when(i == 0)
    def _():
        ubuf[0:ph, :] = jnp.zeros((ph, POOL_W), f32)

    @pl.when(i > 0)
    def _():
        ubuf[0:ph, :] = ubuf[tm:tm + ph, :]

    stages = _cached_attn_stages(*hosted_in, *hosted_out, **hosted) if hosted else ()
    for s in stages:
        s()
    q, k, v, u, sg = _project(x_ref[0], ln1_ref[...], w_ref, qg_ref[...], kg_ref[...])
    sg_ref[0] = sg
    cpg = GROUP_W // LANES
    slot = 0
    for t, chunks in enumerate((q, k, v)):
        for g, dil in enumerate(DILATIONS):
            out_ref = qkv_refs[3 * g + t]
            for c in range(cpg):
                val = chunks[g * cpg + c]
                cols = slice(c * LANES, (c + 1) * LANES)
                if dil == 1:
                    out_ref[0, 0, :, cols] = val.astype(bf16)
                else:
                    sbuf[slot * tm:(slot + 1) * tm, :] = val
                    for r in range(dil):
                        out_ref[0, r, :, cols] = sbuf[pl.ds(slot * tm + r, tm // dil, stride=dil), :].astype(bf16)
                    slot += 1

    ubuf[ph:rows, :] = u
    gw = POOL_GW
    pa[8:rows, :] = ubuf[8:rows, :] + ubuf[7:rows - 1, :]
    pb[16:rows, gw:] = pa[16:rows, gw:] + pa[14:rows - 2, gw:]
    pa[24:rows, 2 * gw:] = pb[24:rows, 2 * gw:] + pb[20:rows - 4, 2 * gw:]
    pb[32:rows, 3 * gw:] = pa[32:rows, 3 * gw:] + pa[24:rows - 8, 3 * gw:]
    pos = i * tm + lax.broadcasted_iota(jnp.int32, (tm, POOL_GW), 0)
    zs = []
    for g, w in enumerate(POOL_WINDOWS):
        cols = slice(g * POOL_GW, (g + 1) * POOL_GW)
        wsum = (pa if g % 2 == 0 else pb)[ph:rows, cols]
        cnt = jnp.minimum(pos + 1, w).astype(f32)
        zs.append(wsum / cnt - u[:, cols])
    z = jnp.concatenate(zs, axis=-1).astype(bf16)
    po = jnp.dot(z, plin_ref[...], preferred_element_type=f32) * pscale_ref[...]
    po_ref[0] = po.astype(bf16)

    @pl.when(i >= n_tiles - tail_tiles)
    def _():
        kt_ref[0] = jnp.concatenate(k, axis=-1).T
        vt_ref[0] = jnp.concatenate(v, axis=-1).T

    @pl.when(i == n_tiles - 1)
    def _():
        ut_ref[0] = ubuf[rows - hist:rows, :]


def _proj_prompt(x, ln1, w_qkvu, qg, kg, plin_bd, pscale, *, tm, hosted=None):
    B, T, D = x.shape
    n_tiles = T // tm
    tail = max(WINDOWS)
    assert T % tm == 0 and tail % tm == 0 and T >= tail
    tail_tiles = tail // tm
    hist = POOL_STATE + 1
    kern = functools.partial(_proj_prompt_kernel, tm=tm, n_tiles=n_tiles, tail_tiles=tail_tiles,
                             hosted=hosted['params'] if hosted else None)
    h_ops, h_in, h_out, h_shape = ((), [], [], []) if not hosted else (
        hosted['operands'], hosted['in_specs'], hosted['out_specs'], hosted['out_shape'])
    const = lambda b, i: (0, 0)
    assert all(tm % (16 * d) == 0 for d in DILATIONS)
    qkv_specs = [pl.BlockSpec((1, d, tm // d, GROUP_W), lambda b, i: (b, 0, i, 0)) for d in DILATIONS for _ in range(3)]
    qkv_shapes = [jax.ShapeDtypeStruct((B, d, T // d, GROUP_W), bf16) for d in DILATIONS for _ in range(3)]
    tail_spec = pl.BlockSpec((1, ATTN_W, tm), lambda b, i: (b, 0, jnp.maximum(i - (n_tiles - tail_tiles), 0)))
    return pl.pallas_call(
        kern,
        grid=(B, n_tiles),
        in_specs=[
            pl.BlockSpec((1, tm, D), lambda b, i: (b, i, 0)),
            pl.BlockSpec((1, D), const),
            pl.BlockSpec((D, IN_W), const),
            pl.BlockSpec((1, ATTN_W), const),
            pl.BlockSpec((1, ATTN_W), const),
            pl.BlockSpec((POOL_W, POOL_W), const),
            pl.BlockSpec((1, POOL_W), const),
        ] + h_in,
        out_specs=qkv_specs
        + [pl.BlockSpec((1, tm, POOL_W), lambda b, i: (b, i, 0)),
           tail_spec, tail_spec,
           pl.BlockSpec((1, hist, POOL_W), lambda b, i: (b, 0, 0)),
           pl.BlockSpec((1, tm, 2 * D), lambda b, i: (b, i, 0))] + h_out,
        out_shape=qkv_shapes
        + [jax.ShapeDtypeStruct((B, T, POOL_W), bf16),
           jax.ShapeDtypeStruct((B, ATTN_W, tail), f32),
           jax.ShapeDtypeStruct((B, ATTN_W, tail), f32),
           jax.ShapeDtypeStruct((B, hist, POOL_W), f32),
           jax.ShapeDtypeStruct((B, T, 2 * D), bf16)] + h_shape,
        scratch_shapes=[pltpu.VMEM((POOL_HIST + tm, POOL_W), f32)] * 3
                       + [pltpu.VMEM((3 * (N_GROUPS - 1) * (GROUP_W // LANES) * tm, LANES), f32)],
        compiler_params=_cparams(("arbitrary", "arbitrary")),
        name="proj_prompt",
    )(x, ln1, w_qkvu, qg, kg, plin_bd, pscale, *h_ops)


def _proj_sample_kernel(x_ref, ln1_ref, w_ref, qg_ref, kg_ref, plin_ref, pscale_ref, state_ref,
                        q_ref, k_ref, v_ref, po_ref, st_ref, sg_ref, *, n_seq, n_new, past_len):
    q, k, v, u, sg = _project(x_ref[...], ln1_ref[...], w_ref, qg_ref[...], kg_ref[...])
    sg_ref[...] = sg
    q_ref[...] = jnp.concatenate(q, axis=-1)
    k_ref[...] = jnp.concatenate(k, axis=-1)
    v_ref[...] = jnp.concatenate(v, axis=-1)
    ext = [state_ref[j] for j in range(POOL_STATE)] + [u[s * n_seq:(s + 1) * n_seq, :] for s in range(n_new)]
    for s in range(n_new):
        zs = []
        for g, w in enumerate(POOL_WINDOWS):
            cols = slice(g * POOL_GW, (g + 1) * POOL_GW)
            cur = ext[POOL_STATE + s][:, cols]
            acc = cur
            for j in range(1, w):
                acc = acc + ext[POOL_STATE + s - j][:, cols]
            cnt = float(min(past_len + s + 1, w))
            zs.append(acc / cnt - cur)
        z = jnp.concatenate(zs, axis=-1).astype(bf16)
        po = jnp.dot(z, plin_ref[...], preferred_element_type=f32) * pscale_ref[...]
        po_ref[s * n_seq:(s + 1) * n_seq, :] = po.astype(bf16)
    for j in range(POOL_STATE):
        st_ref[j] = ext[j + n_new]


def _proj_sample(x, ln1, w_qkvu, qg, kg, plin_bd, pscale, state, *, n_seq, n_new, past_len):
    n = n_seq * n_new
    kern = functools.partial(_proj_sample_kernel, n_seq=n_seq, n_new=n_new, past_len=past_len)
    return pl.pallas_call(
        kern,
        out_shape=[jax.ShapeDtypeStruct((n, ATTN_W), f32)] * 3
        + [jax.ShapeDtypeStruct((n, POOL_W), bf16),
           jax.ShapeDtypeStruct((POOL_STATE, n_seq, POOL_W), f32),
           jax.ShapeDtypeStruct((n, 2 * D_MODEL), bf16)],
        compiler_params=pltpu.CompilerParams(vmem_limit_bytes=VMEM_LIMIT),
        name="proj_sample",
    )(x, ln1, w_qkvu, qg, kg, plin_bd, pscale, state)


def _head_masks(shape):
    lane = lax.broadcasted_iota(jnp.int32, shape, len(shape) - 1)
    return [(lane >= h * HEAD_DIM) & (lane < (h + 1) * HEAD_DIM) for h in range(HEADS_PER_GROUP)]


def _band_attn_kernel(q_ref, kc_ref, kp_ref, vc_ref, vp_ref, bias_ref, *rest, tl, unroll, cast):
    if cast:
        w_ref, o_ref, l_ref, wb_ref, kbuf, vbuf = rest
        wb_ref[...] = w_ref[...].astype(bf16)
    else:
        o_ref, l_ref, kbuf, vbuf = rest
    i = pl.program_id(2)
    kbuf[0:BAND, :] = kp_ref[0, 0]
    kbuf[BAND:2 * BAND, :] = kc_ref[0, 0, 0:BAND, :]
    vbuf[0:BAND, :] = vp_ref[0, 0]
    vbuf[BAND:2 * BAND, :] = vc_ref[0, 0, 0:BAND, :]
    masks = _head_masks((BAND, GROUP_W))

    def sub_block(j, kk, vv, var):
        r0 = j * BAND if isinstance(j, int) else pl.multiple_of(j * BAND, BAND)
        q = q_ref[0, 0, pl.ds(r0, BAND), :]
        qm = jnp.concatenate([jnp.where(m, q, jnp.zeros_like(q)) for m in masks], axis=0)
        s = lax.dot_general(qm, kk, (((1,), (1,)), ((), ())), preferred_element_type=f32)
        s = s + bias_ref[var]
        m = jnp.max(s, axis=-1, keepdims=True)
        p = jnp.exp(s - m)
        den = jnp.sum(p, axis=-1, keepdims=True)
        pv = jnp.dot(p.astype(bf16), vv, preferred_element_type=f32)
        o = jnp.zeros((BAND, GROUP_W), f32)
        ms = jnp.zeros((BAND, GROUP_W), f32)
        ds = jnp.ones((BAND, GROUP_W), f32)
        for h, msk in enumerate(masks):
            rows = slice(h * BAND, (h + 1) * BAND)
            o = jnp.where(msk, pv[rows], o)
            ms = jnp.where(msk, m[rows], ms)
            ds = jnp.where(msk, den[rows], ds)
        o_ref[0, 0, pl.ds(r0, BAND), :] = (o / ds).astype(bf16)
        l_ref[0, 0, pl.ds(r0, BAND), :] = ms + jnp.log(ds)

    sub_block(0, kbuf[...], vbuf[...], jnp.where(i == 0, 0, 1))

    def body(j, carry):
        k0 = pl.multiple_of((j - 1) * BAND, BAND)
        sub_block(j, kc_ref[0, 0, pl.ds(k0, 2 * BAND), :], vc_ref[0, 0, pl.ds(k0, 2 * BAND), :], 1)
        return carry

    if tl > BAND:
        lax.fori_loop(1, tl // BAND, body, 0, unroll=unroll)


def _band_bias(slopes_g, dil):
    qi = jnp.arange(BAND)[:, None]
    kb = jnp.arange(2 * BAND)[None, :]
    rel = qi + BAND - kb
    valid = (rel >= 0) & (rel <= BAND)
    alibi = -slopes_g[:, None, None] * (dil * rel)[None].astype(f32)
    variants = []
    for first in (True, False):
        ok = valid & (kb >= BAND) if first else valid
        variants.append(jnp.where(ok[None], alibi, NEG).reshape(HEADS_PER_GROUP * BAND, 2 * BAND))
    return jnp.stack(variants, axis=0)


def _band_attention(q, k, v, g, slopes_g, *, tl_max=1024, unroll=7, cast_w=None):
    B, dil, L, _ = q.shape
    tl = min(tl_max, L)
    assert dil == DILATIONS[g] and L % tl == 0 and tl % BAND == 0
    nsub = tl // BAND
    nl = L // tl
    bias = _band_bias(slopes_g, dil)
    cur = pl.BlockSpec((1, 1, tl, GROUP_W), lambda b, r, i: (b, r, i, 0))
    prev = pl.BlockSpec((1, 1, BAND, GROUP_W), lambda b, r, i: (b, r, jnp.maximum(i * nsub - 1, 0), 0))
    host_cast = cast_w is not None and cast_w.shape[0] == B * dil * nl
    w_spec, w_ops, w_shape = [], (), []
    if host_cast:
        w_spec = [pl.BlockSpec((1,) + cast_w.shape[1:], lambda b, r, i: ((b * dil + r) * nl + i, 0, 0))]
        w_ops, w_shape = (cast_w,), [jax.ShapeDtypeStruct(cast_w.shape, bf16)]
    res = pl.pallas_call(
        functools.partial(_band_attn_kernel, tl=tl, unroll=max(1, min(unroll, nsub - 1)), cast=host_cast),
        grid=(B, dil, nl),
        in_specs=[cur, cur, prev, cur, prev,
                  pl.BlockSpec((2, HEADS_PER_GROUP * BAND, 2 * BAND), lambda b, r, i: (0, 0, 0))] + w_spec,
        out_specs=[cur, cur] + w_spec,
        out_shape=[jax.ShapeDtypeStruct((B, dil, L, GROUP_W), bf16),
                   jax.ShapeDtypeStruct((B, dil, L, GROUP_W), f32)] + w_shape,
        scratch_shapes=[pltpu.VMEM((2 * BAND, GROUP_W), bf16), pltpu.VMEM((2 * BAND, GROUP_W), bf16)],
        compiler_params=_cparams(("arbitrary", "arbitrary", "arbitrary")),
        name="band_attn_g%d" % g,
    )(q, k, k, v, v, bias, *w_ops)
    if cast_w is None:
        return res
    return res[0], res[1], (res[2] if host_cast else cast_w.astype(bf16))


def _cached_attn_kernel(*refs, **params):
    for stage in _cached_attn_stages(*refs, **params):
        stage()


def _cached_attn_stages(q_ref, kn_ref, vn_ref, kc_ref, vc_ref, bc_ref, bn_ref,
                        o_ref, l_ref, ko_ref, vo_ref, *, nb, n_new, win):
    def roll(c_ref, new_ref, out_ref):
        lane_t = lax.broadcasted_iota(jnp.int32, (nb, GROUP_W, LANES), 2)
        rolled = pltpu.roll(c_ref[...], win - n_new, axis=2)
        out_ref[...] = rolled
        new_t = jnp.swapaxes(jnp.concatenate([new_ref[...], jnp.zeros((nb, LANES - 8, GROUP_W), f32)], axis=1), 1, 2)
        new_t = pltpu.roll(new_t, LANES - n_new, axis=2)
        out_ref[:, :, win - LANES:win] = jnp.where(lane_t >= LANES - n_new, new_t, rolled[:, :, win - LANES:win])

    return (functools.partial(_cached_attn_scores, q_ref, kn_ref, vn_ref, kc_ref, vc_ref, bc_ref, bn_ref, o_ref, l_ref,
                              nb=nb, n_new=n_new),
            functools.partial(roll, kc_ref, kn_ref, ko_ref),
            functools.partial(roll, vc_ref, vn_ref, vo_ref))


def _cached_attn_scores(q_ref, kn_ref, vn_ref, kc_ref, vc_ref, bc_ref, bn_ref, o_ref, l_ref, *, nb, n_new):
    masks8 = _head_masks((nb, 8, GROUP_W))
    q8, kn8, vn8 = q_ref[...], kn_ref[...], vn_ref[...]
    kc, vc = kc_ref[...], vc_ref[...]
    qm = jnp.concatenate([jnp.where(m, q8, 0.0) for m in masks8], axis=1)
    sc = jnp.einsum('bqd,bdk->bqk', qm.astype(bf16), kc.astype(bf16), preferred_element_type=f32) + bc_ref[...]
    m = jnp.max(sc, axis=-1, keepdims=True)
    sn = []
    for t in range(n_new):
        col = jnp.sum(qm * kn8[:, t:t + 1, :], axis=-1, keepdims=True) + bn_ref[:, t:t + 1]
        sn.append(col)
        m = jnp.maximum(m, col)
    pc = jnp.exp(sc - m)
    den = jnp.sum(pc, axis=-1, keepdims=True)
    acc = jnp.einsum('bqk,bdk->bqd', pc.astype(bf16), vc.astype(bf16), preferred_element_type=f32)
    for t in range(n_new):
        pn = jnp.exp(sn[t] - m)
        den = den + pn
        acc = acc + pn * vn8[:, t:t + 1, :]
    acc = acc / den
    lse = m + jnp.log(den)
    o = jnp.zeros((nb, 8, GROUP_W), f32)
    l = jnp.zeros((nb, 8, GROUP_W), f32)
    for h, msk in enumerate(masks8):
        o = jnp.where(msk, acc[:, h * 8:(h + 1) * 8, :], o)
        l = jnp.where(msk, lse[:, h * 8:(h + 1) * 8, :], l)
    o_ref[...] = o
    l_ref[...] = l


def _cached_bias(slopes_g, dil, win, n_new):
    s = jnp.arange(8)[:, None]
    i = jnp.arange(win)[None, :]
    dist = win + s - i
    ok = (dist % dil == 0) & (dist // dil <= BAND) & (s < n_new)
    bc = jnp.where(ok[None], -slopes_g[:, None, None] * dist[None].astype(f32), NEG)
    t = jnp.arange(8)[None, :]
    dn = s - t
    okn = (dn >= 0) & (dn % dil == 0) & (dn // dil <= BAND) & (s < n_new) & (t < n_new)
    bn = jnp.where(okn[None], -slopes_g[:, None, None] * dn[None].astype(f32), NEG)
    pad = (s >= n_new)
    bc = jnp.where(pad[None], 0.0, bc)
    bn = jnp.where(pad[None], 0.0, bn)
    return bc.reshape(HEADS_PER_GROUP * 8, win), bn.reshape(HEADS_PER_GROUP * 8, 8)


def _cached_call_parts(q8, kn8, vn8, kc_t, vc_t, g, slopes_g, *, n_new, nb, step_of):
    Bd, _, win = kc_t.shape
    assert win == WINDOWS[g] and win % LANES == 0 and Bd % nb == 0
    bc, bn = _cached_bias(slopes_g, DILATIONS[g], win, n_new)
    small = pl.BlockSpec((nb, 8, GROUP_W), lambda *idx: (step_of(*idx), 0, 0))
    cache = pl.BlockSpec((nb, GROUP_W, win), lambda *idx: (step_of(*idx), 0, 0))
    const = lambda a: pl.BlockSpec(a.shape, lambda *idx: (0, 0))
    return dict(
        operands=(q8, kn8, vn8, kc_t, vc_t, bc, bn),
        in_specs=[small, small, small, cache, cache, const(bc), const(bn)],
        out_specs=[small, small, cache, cache],
        out_shape=[jax.ShapeDtypeStruct((Bd, 8, GROUP_W), f32)] * 2 + [jax.ShapeDtypeStruct((Bd, GROUP_W, win), f32)] * 2,
        params=dict(nb=nb, n_new=n_new, win=win))


def _cached_attention(q8, kn8, vn8, kc_t, vc_t, g, slopes_g, *, n_new):
    Bd, _, win = kc_t.shape
    nb = max(1, min(Bd, 2048 // win))
    parts = _cached_call_parts(q8, kn8, vn8, kc_t, vc_t, g, slopes_g, n_new=n_new, nb=nb, step_of=lambda b: b)
    return pl.pallas_call(
        functools.partial(_cached_attn_kernel, **parts['params']),
        grid=(Bd // nb,),
        in_specs=parts['in_specs'],
        out_specs=parts['out_specs'],
        out_shape=parts['out_shape'],
        compiler_params=_cparams(("arbitrary",)),
        name="cached_attn_g%d" % g,
    )(*parts['operands'])


def _split_hosted(rest, n_own_out, hosted):
    n_in, n_out = (7, 4) if hosted else (0, 0)
    hosted_in, rest = rest[:n_in], rest[n_in:]
    own_out, rest = rest[:n_own_out], rest[n_own_out:]
    return hosted_in, own_out, rest[:n_out], rest[n_out:]


def _merge_kernel(x_ref, o0_ref, o1_ref, o2_ref, l0_ref, l1_ref, l2_ref, po_ref, sg_ref,
                  wpa_ref, wpb_ref, wo_ref, ln2_ref, wr_ref, br_ref, tri_ref, ltri_ref,
                  *rest, tm, ts, dils, hosted):
    hosted_in, (h_ref, xl_ref, wc_ref, tc_ref), hosted_out, (ibuf,) = _split_hosted(rest, 4, hosted)
    if hosted:
        _cached_attn_kernel(*hosted_in, *hosted_out, **hosted)
    slots = iter(range(IBUF_SLOTS))

    def token_order(ref, dil):
        if dil == 1:
            return ref[0, 0].astype(f32)
        chunks = []
        for c in range(GROUP_W // LANES):
            base = next(slots) * tm
            for r in range(dil):
                ibuf[pl.ds(base + r, tm // dil, stride=dil), :] = ref[0, r, :, c * LANES:(c + 1) * LANES].astype(f32)
            chunks.append(ibuf[base:base + tm, :])
        return jnp.concatenate(chunks, axis=-1)

    x = x_ref[0]
    l0, l1, l2 = (token_order(r, d) for r, d in zip((l0_ref, l1_ref, l2_ref), dils))
    lm = jnp.maximum(jnp.maximum(l0, l1), l2)
    e0, e1, e2 = jnp.exp(l0 - lm), jnp.exp(l1 - lm), jnp.exp(l2 - lm)
    o0, o1, o2 = (token_order(r, d) for r, d in zip((o0_ref, o1_ref, o2_ref), dils))
    attn = (e0 * o0 + e1 * o1 + e2 * o2) / (e0 + e1 + e2)
    ma = jnp.dot(attn.astype(bf16), wpa_ref[...], preferred_element_type=f32)
    mb = jnp.dot(po_ref[0], wpb_ref[...], preferred_element_type=f32)
    mix = sg_ref[0, :, :D_MODEL].astype(f32) * ma + sg_ref[0, :, D_MODEL:].astype(f32) * mb
    h = x + jnp.dot(mix.astype(bf16), wo_ref[...], preferred_element_type=f32)
    h_ref[...] = h
    xn2 = _rmsnorm_rows(h, ln2_ref[...]).astype(bf16)

    lt = (jnp.dot(xn2, wr_ref[...], preferred_element_type=f32) + br_ref[...]).T
    row8 = lax.broadcasted_iota(jnp.int32, (8, tm), 0)
    gl = jnp.where(row8 < N_EXPERT_GROUPS, lt[0:8], -jnp.inf)
    gmax = jnp.max(gl, axis=0, keepdims=True)
    gidx = jnp.min(jnp.where(gl == gmax, row8, 8), axis=0, keepdims=True)
    pg = 1.0 / jnp.sum(jnp.exp(gl - gmax), axis=0, keepdims=True)
    sel = jnp.zeros((8, tm), f32)
    for g in range(N_EXPERT_GROUPS):
        lo = EXPERT_COL0 + g * EXPERTS_PER_GROUP
        sel = jnp.where(gidx == g, lt[lo:lo + EXPERTS_PER_GROUP], sel)
    v0 = jnp.max(sel, axis=0, keepdims=True)
    i0 = jnp.min(jnp.where(sel == v0, row8, 8), axis=0, keepdims=True)
    sel2 = jnp.where(row8 == i0, -jnp.inf, sel)
    v1 = jnp.max(sel2, axis=0, keepdims=True)
    i1 = jnp.min(jnp.where(sel2 == v1, row8, 8), axis=0, keepdims=True)
    t = jnp.exp(v1 - v0)
    w0 = pg / (1.0 + t)
    w1 = pg * t / (1.0 + t)
    eid0 = gidx * EXPERTS_PER_GROUP + i0
    eid1 = gidx * EXPERTS_PER_GROUP + i1
    erow = lax.broadcasted_iota(jnp.int32, (N_EXPERTS, tm), 0)
    oh0 = erow == eid0
    oh1 = erow == eid1
    cnt = jnp.where(oh0, 1.0, jnp.where(oh1, 1.0, 0.0))
    before = jnp.dot(cnt.astype(bf16), tri_ref[...], preferred_element_type=f32)
    tcount = jnp.sum(cnt, axis=1, keepdims=True)
    units = jnp.floor((tcount + (SEG_ALIGN - 1)) * (1.0 / SEG_ALIGN))
    ub = jnp.broadcast_to(units, (N_EXPERTS, LANES)).astype(bf16)
    seg0 = SEG_ALIGN * jnp.dot(ltri_ref[...], ub, preferred_element_type=f32)[:, 0:1]
    pos_e = seg0 + before
    lpos0 = jnp.sum(jnp.where(oh0, pos_e, 0.0), axis=0, keepdims=True)
    lpos1 = jnp.sum(jnp.where(oh1, pos_e, 0.0), axis=0, keepdims=True)
    prow = lax.broadcasted_iota(jnp.int32, (ts, tm), 0)
    perm = jnp.where(prow == lpos0.astype(jnp.int32), 1.0, jnp.where(prow == lpos1.astype(jnp.int32), 1.0, 0.0))
    xl_ref[...] = jnp.dot(perm.astype(bf16), xn2, preferred_element_type=f32)
    tc_ref[...] = jnp.broadcast_to(tcount, (N_EXPERTS, LANES))
    rowl = lax.broadcasted_iota(jnp.int32, (LANES, tm), 0)
    wslab = jnp.zeros((LANES, tm), f32)
    for r, val in enumerate((w0, w1, lpos0, lpos1)):
        wslab = jnp.where(rowl == r, val, wslab)
    wc_ref[...] = wslab.T


def _hosted_parts(hosted):
    if not hosted:
        return (), [], [], [], None
    return hosted['operands'], hosted['in_specs'], hosted['out_specs'], hosted['out_shape'], hosted['params']


def _merge(x, o, l, po, sg, w_pa, w_pb, w_o, ln2, w_router, b_router, *, tm, hosted=None):
    B, T, _ = x.shape
    h_ops, h_in, h_out, h_shape, h_params = _hosted_parts(hosted)
    assert T % tm == 0
    nt = T // tm
    n = B * T
    dils = tuple(a.shape[1] for a in o)
    assert all(tm % (8 * d) == 0 for d in dils)
    assert 2 * tm // SEG_ALIGN <= 256
    ts = _sorted_tile_rows(tm)
    tri = (jnp.arange(tm)[:, None] < jnp.arange(tm)[None, :]).astype(bf16)
    ltri = (jnp.arange(N_EXPERTS)[None, :] < jnp.arange(N_EXPERTS)[:, None]).astype(bf16)
    rows3 = lambda w: pl.BlockSpec((1, tm, w), lambda b, i: (b, i, 0))
    flat = lambda r, w: pl.BlockSpec((r, w), lambda b, i: (b * nt + i, 0))
    grp = [pl.BlockSpec((1, d, tm // d, GROUP_W), lambda b, i: (b, 0, i, 0)) for d in dils]
    full = lambda a: pl.BlockSpec(a.shape, lambda b, i: (0,) * a.ndim)
    weights = (w_pa, w_pb, w_o, ln2, w_router, b_router, tri, ltri)
    return pl.pallas_call(
        functools.partial(_merge_kernel, tm=tm, ts=ts, dils=dils, hosted=h_params),
        grid=(B, nt),
        in_specs=[rows3(D_MODEL)] + grp + grp + [rows3(POOL_W), rows3(2 * D_MODEL)]
        + [full(a) for a in weights] + h_in,
        out_specs=[flat(tm, D_MODEL), flat(ts, D_MODEL), flat(tm, LANES),
                   pl.BlockSpec((N_EXPERTS, LANES), lambda b, i: (0, b * nt + i))] + h_out,
        out_shape=[jax.ShapeDtypeStruct((n, D_MODEL), f32),
                   jax.ShapeDtypeStruct((B * nt * ts, D_MODEL), f32),
                   jax.ShapeDtypeStruct((n, LANES), f32),
                   jax.ShapeDtypeStruct((N_EXPERTS, B * nt * LANES), f32)] + h_shape,
        scratch_shapes=[pltpu.VMEM((IBUF_SLOTS * tm, LANES), f32)],
        compiler_params=_cparams(("arbitrary", "arbitrary")),
        name="merge_router",
    )(x, *o, *l, po, sg, *weights, *h_ops)


def _sorted_tile_rows(tm):
    return -(-(2 * tm + N_EXPERTS * (SEG_ALIGN - 1)) // LANES) * LANES


def _moe_seg_kernel(blk_e_ref, blk_r0_ref, blk_n_ref, seg_g_ref, seg_c_ref, seg_src_ref, used_ref,
                    xl_hbm, wg_ref, wu_ref, wd_ref, yl_hbm,
                    xbuf, ybuf, sem_in, sem_out, sem_zero, ptr, *, blk, ts, n_tiles, n_blocks):
    b = pl.program_id(0)
    slot = lax.rem(b, 2)
    unit_bits = (blk // SEG_ALIGN).bit_length()

    def row_pieces(rows):
        units = lax.shift_right_logical(rows, jnp.int32(SEG_ALIGN.bit_length() - 1))
        off = jnp.int32(0)
        for bit in reversed(range(unit_bits)):
            on = lax.shift_right_logical(units, jnp.int32(bit)) & 1
            yield on == 1, off, SEG_ALIGN << bit
            off = off + on * (SEG_ALIGN << bit)

    def copy_rows(src, src_row, dst, dst_row, rows, sem):
        def start(off, size):
            pltpu.make_async_copy(src.at[pl.ds(pl.multiple_of(src_row + off, SEG_ALIGN), size)],
                                  dst.at[pl.ds(pl.multiple_of(dst_row + off, SEG_ALIGN), size)], sem).start()

        chunk = SEG_ALIGN << 3
        n_chunks = lax.shift_right_logical(rows, jnp.int32(chunk.bit_length() - 1))

        def whole_chunk(j, c):
            start(j * chunk, chunk)
            return c

        lax.fori_loop(0, n_chunks, whole_chunk, 0)
        off = n_chunks * chunk
        rest = rows - off
        for bit in (2, 1, 0):
            size = SEG_ALIGN << bit
            on = (rest & size) != 0

            @pl.when(on)
            def _():
                start(off, size)

            off = off + jnp.where(on, size, 0)

    def wait_rows(src, dst, rows, sem):
        for on, _, size in row_pieces(rows):
            @pl.when(on)
            def _():
                pltpu.make_async_copy(src.at[pl.ds(0, size)], dst.at[pl.ds(0, size)], sem).wait()

    def for_pieces(bb, stream, fn):
        e, r0, n = blk_e_ref[bb], blk_r0_ref[bb], blk_n_ref[bb]

        @pl.when(n > 0)
        def _():
            def seg(i):
                return jnp.minimum(i, n_tiles - 1) * N_EXPERTS + e

            def cond(i):
                return (i < n_tiles) & (seg_g_ref[seg(i)] < r0 + n)

            def body(i):
                g = seg_g_ref[seg(i)]
                lo = jnp.maximum(g, r0)
                hi = jnp.minimum(g + seg_c_ref[seg(i)], r0 + n)

                @pl.when(hi > lo)
                def _():
                    fn(seg_src_ref[seg(i)] + (lo - g), lo - r0, hi - lo)

                return i + 1

            end = lax.while_loop(cond, body, jnp.where(r0 == 0, 0, ptr[stream]))
            ptr[stream] = jnp.maximum(end - 1, 0)

    def gather(bb):
        s = lax.rem(bb, 2)
        for_pieces(bb, 0, lambda lrow, brow, rows: copy_rows(xl_hbm, lrow, xbuf.at[s], brow, rows, sem_in.at[s]))

    def scatter(bb):
        s = lax.rem(bb, 2)
        for_pieces(bb, 1, lambda lrow, brow, rows: copy_rows(ybuf.at[s], brow, yl_hbm, lrow, rows, sem_out.at[s]))

    def zero_tail(i, go):
        row0 = i * ts + used_ref[i]
        rows = ts - used_ref[i]
        whole = lax.shift_right_logical(rows, jnp.int32(blk.bit_length() - 1))
        zsrc = xbuf.at[1]

        def whole_block(j, c):
            cp = pltpu.make_async_copy(zsrc, yl_hbm.at[pl.ds(pl.multiple_of(row0 + j * blk, SEG_ALIGN), blk)], sem_zero)
            cp.start() if go else cp.wait()
            return c

        lax.fori_loop(0, whole, whole_block, 0)
        rest = rows - whole * blk
        if go:
            copy_rows(zsrc, 0, yl_hbm, row0 + whole * blk, rest, sem_zero)
        else:
            wait_rows(zsrc, yl_hbm, rest, sem_zero)

    @pl.when(b == 0)
    def _():
        xbuf[...] = jnp.zeros_like(xbuf)
        ptr[0] = 0
        ptr[1] = 0
        gather(0)
        for go in (True, False):
            def per_tile(i, c, go=go):
                zero_tail(i, go)
                return c

            lax.fori_loop(0, n_tiles, per_tile, 0)

    n_b = blk_n_ref[b]
    wait_rows(xl_hbm, xbuf.at[slot], n_b, sem_in.at[slot])

    @pl.when(b + 1 < n_blocks)
    def _():
        gather(b + 1)

    @pl.when(b >= 2)
    def _():
        wait_rows(ybuf.at[slot], yl_hbm, blk_n_ref[b - 2], sem_out.at[slot])

    @pl.when(n_b > 0)
    def _():
        x = xbuf[slot].astype(bf16)
        hid = jax.nn.silu(jnp.dot(x, wg_ref[0], preferred_element_type=f32)) * jnp.dot(x, wu_ref[0], preferred_element_type=f32)
        ybuf[slot] = jnp.dot(hid.astype(bf16), wd_ref[0], preferred_element_type=f32)
        scatter(b)

    @pl.when(b == n_blocks - 1)
    def _():
        @pl.when(b >= 1)
        def _():
            wait_rows(ybuf.at[1 - slot], yl_hbm, blk_n_ref[b - 1], sem_out.at[1 - slot])

        wait_rows(ybuf.at[slot], yl_hbm, n_b, sem_out.at[slot])


def _moe_segments(xl, tables, w_gate, w_up, w_down, *, blk, ts, n_tiles):
    blk_e = tables[0]
    n_blocks = blk_e.shape[0]
    assert blk & (blk - 1) == 0 and blk % SEG_ALIGN == 0
    wspec = lambda shape: pl.BlockSpec((1,) + shape, lambda b, be, *_: (be[b], 0, 0))
    hbm = pl.BlockSpec(memory_space=pl.ANY)
    return pl.pallas_call(
        functools.partial(_moe_seg_kernel, blk=blk, ts=ts, n_tiles=n_tiles, n_blocks=n_blocks),
        grid_spec=pltpu.PrefetchScalarGridSpec(
            num_scalar_prefetch=len(tables),
            grid=(n_blocks,),
            in_specs=[hbm, wspec((D_MODEL, D_EXPERT)), wspec((D_MODEL, D_EXPERT)), wspec((D_EXPERT, D_MODEL))],
            out_specs=hbm,
            scratch_shapes=[pltpu.VMEM((2, blk, D_MODEL), f32), pltpu.VMEM((2, blk, D_MODEL), f32),
                            pltpu.SemaphoreType.DMA((2,)), pltpu.SemaphoreType.DMA((2,)), pltpu.SemaphoreType.DMA,
                            pltpu.SMEM((2,), jnp.int32)],
        ),
        out_shape=jax.ShapeDtypeStruct(xl.shape, f32),
        compiler_params=_cparams(("arbitrary",)),
        name="moe_experts",
    )(*tables, xl, w_gate, w_up, w_down)


def _unsort_kernel(h_ref, wc_ref, yl_ref, *rest, ts, hosted):
    hosted_in, (y_ref,), hosted_out, _ = _split_hosted(rest, 1, hosted)
    if hosted:
        _cached_attn_kernel(*hosted_in, *hosted_out, **hosted)
    w = wc_ref[...]
    yl = yl_ref[...].astype(bf16)
    col = lax.broadcasted_iota(jnp.int32, (w.shape[0], ts), 1)
    y = h_ref[...]
    for k in range(2):
        pick = jnp.where(col == w[:, 2 + k:3 + k].astype(jnp.int32), 1.0, 0.0).astype(bf16)
        y = y + w[:, k:k + 1] * jnp.dot(pick, yl, preferred_element_type=f32)
    y_ref[...] = y


def _unsort(h, wc, yl, *, tm, ts, hosted=None):
    n = h.shape[0]
    h_ops, h_in, h_out, h_shape, h_params = _hosted_parts(hosted)
    rows = lambda r, w: pl.BlockSpec((r, w), lambda i: (i, 0))
    return pl.pallas_call(
        functools.partial(_unsort_kernel, ts=ts, hosted=h_params),
        grid=(n // tm,),
        in_specs=[rows(tm, D_MODEL), rows(tm, LANES), rows(ts, D_MODEL)] + h_in,
        out_specs=[rows(tm, D_MODEL)] + h_out,
        out_shape=[jax.ShapeDtypeStruct((n, D_MODEL), f32)] + h_shape,
        compiler_params=_cparams(("arbitrary",)),
        name="moe_unsort",
    )(h, wc, yl, *h_ops)


def _mix_and_moe(x, o, l, po, sg, wts, *, tm, blk, host_merge=None, host_unsort=None):
    n = x.shape[0] * x.shape[1]
    n_tiles = n // tm
    nt = x.shape[1] // tm
    ts = _sorted_tile_rows(tm)
    hosted = host_merge(n_tiles, lambda b, i: b * nt + i) if host_merge else None
    h, xl, wc, tc, *merge_hosted = _merge(x, o, l, po, sg, wts['w_pa'], wts['w_pb'], wts['w_o'], wts['ln2'],
                                          wts['w_router'], wts['b_router'], tm=tm, hosted=hosted)
    c8 = (tc[:, ::LANES].T.astype(jnp.int32) + (SEG_ALIGN - 1)) // SEG_ALIGN * SEG_ALIGN
    seg_src = jnp.arange(n_tiles, dtype=jnp.int32)[:, None] * ts + jnp.cumsum(c8, axis=1) - c8
    seg_g = jnp.cumsum(c8, axis=0) - c8
    tot = jnp.sum(c8, axis=0)
    padded = (tot + blk - 1) // blk * blk
    pad_ends = jnp.cumsum(padded)
    n_blocks = -(-(2 * n + n_tiles * N_EXPERTS * (SEG_ALIGN - 1) + N_EXPERTS * (blk - 1)) // blk)
    blk_start = jnp.arange(n_blocks, dtype=jnp.int32) * blk
    blk_e = jnp.minimum(jnp.sum(pad_ends[None, :] <= blk_start[:, None], axis=1), N_EXPERTS - 1).astype(jnp.int32)
    pick = blk_e[:, None] == jnp.arange(N_EXPERTS, dtype=jnp.int32)[None, :]
    blk_r0 = blk_start - jnp.sum(jnp.where(pick, (pad_ends - padded)[None, :], 0), axis=1)
    blk_n = jnp.clip(jnp.sum(jnp.where(pick, tot[None, :], 0), axis=1) - blk_r0, 0, blk)
    used = jnp.sum(c8, axis=1)
    tables = tuple(a.astype(jnp.int32).reshape(-1) for a in (blk_e, blk_r0, blk_n, seg_g, c8, seg_src, used))
    yl = _moe_segments(xl, tables, wts['w_gate'], wts['w_up'], wts['w_down'], blk=blk, ts=ts, n_tiles=n_tiles)
    hosted = host_unsort(n_tiles, lambda i: i) if host_unsort else None
    y, *unsort_hosted = _unsort(h, wc, yl, tm=tm, ts=ts, hosted=hosted)
    return y, merge_hosted, unsort_hosted


def kernel(x_prompt, x_sample, cache_k_w128, cache_v_w128, cache_k_w512, cache_v_w512, cache_k_w2048, cache_v_w2048, state_pool, ln1, w_in, q_gain, k_gain, pool_lin, pool_scale, w_pa, w_pb, w_o, ln2, w_rg, b_rg, w_re, b_re, w_gate, w_up, w_down):
    B, T, D = x_prompt.shape
    Bd, S, _ = x_sample.shape
    past_len = 8192
    caches = ((cache_k_w128, cache_v_w128), (cache_k_w512, cache_v_w512), (cache_k_w2048, cache_v_w2048))
    slopes = jnp.exp2(-8.0 * jnp.arange(1, N_HEADS + 1, dtype=f32) / N_HEADS).reshape(N_GROUPS, HEADS_PER_GROUP)

    plin_bd = jnp.zeros((POOL_W, POOL_W), f32)
    for g in range(len(POOL_WINDOWS)):
        plin_bd = plin_bd.at[g * POOL_GW:(g + 1) * POOL_GW, g * POOL_GW:(g + 1) * POOL_GW].set(pool_lin[g])
    w_router = jnp.zeros((D, ROUTER_W), f32).at[:, :N_EXPERT_GROUPS].set(w_rg)
    w_router = w_router.at[:, EXPERT_COL0:EXPERT_COL0 + N_EXPERTS].set(w_re)
    b_router = jnp.zeros((1, ROUTER_W), f32).at[0, :N_EXPERT_GROUPS].set(b_rg)
    b_router = b_router.at[0, EXPERT_COL0:EXPERT_COL0 + N_EXPERTS].set(b_re)
    wts = dict(w_pa=w_pa.astype(bf16), w_pb=w_pb.astype(bf16), w_o=w_o.astype(bf16), ln2=ln2.reshape(1, D),
               w_router=w_router.astype(bf16), b_router=b_router)
    proj_w = (ln1.reshape(1, D), w_in.astype(bf16), q_gain.reshape(1, ATTN_W), k_gain.reshape(1, ATTN_W),
              plin_bd.astype(bf16), pool_scale.reshape(1, POOL_W))

    n_s = Bd * S
    xs = x_sample.transpose(1, 0, 2).reshape(n_s, D)
    qs, ks, vs, pos, st, sgs = _proj_sample(xs, *proj_w, state_pool.transpose(1, 0, 2),
                                            n_seq=Bd, n_new=S, past_len=past_len)
    pad8 = lambda a: jnp.pad(a.reshape(S, Bd, GROUP_W).transpose(1, 0, 2), ((0, 0), (0, 8 - S), (0, 0)))

    def cached_operands(g):
        cols = slice(g * GROUP_W, (g + 1) * GROUP_W)
        kc, vc = caches[g]
        w = WINDOWS[g]
        return (pad8(qs[:, cols]), pad8(ks[:, cols]), pad8(vs[:, cols]),
                kc.transpose(0, 2, 3, 1).reshape(Bd, GROUP_W, w), vc.transpose(0, 2, 3, 1).reshape(Bd, GROUP_W, w))

    def host(g):
        def parts(steps, step_of):
            if Bd % steps:
                return None
            return _cached_call_parts(*cached_operands(g), g, slopes[g], n_new=S, nb=Bd // steps, step_of=step_of)
        return parts

    tm_proj = 256
    n_t = T // tm_proj
    hosted = host(2)(B * n_t, lambda b, i: b * n_t + i)
    outs = _proj_prompt(x_prompt, *proj_w, tm=tm_proj, hosted=hosted)
    qkv, (po, kt, vt, ut, sg) = outs[:3 * N_GROUPS], outs[3 * N_GROUPS:3 * N_GROUPS + 5]
    cached_out = {}
    if hosted:
        cached_out[2] = outs[3 * N_GROUPS + 5:]
    o, l, (wts['w_gate'], wts['w_up'], wts['w_down']) = zip(*[
        _band_attention(*qkv[3 * g:3 * g + 3], g, slopes[g], cast_w=w) for g, w in enumerate((w_gate, w_up, w_down))])
    y_prompt, in_merge, in_unsort = _mix_and_moe(x_prompt, o, l, po, sg, wts, tm=512, blk=512,
                                                 host_merge=host(1), host_unsort=host(0))
    y_prompt = y_prompt.reshape(B, T, D)
    if in_merge:
        cached_out[1] = in_merge
    if in_unsort:
        cached_out[0] = in_unsort
    tail = kt.shape[2]
    pkv = []
    for g, w in enumerate(WINDOWS):
        for a in (kt, vt):
            a = a.reshape(B, N_HEADS, HEAD_DIM, tail)[:, g * HEADS_PER_GROUP:(g + 1) * HEADS_PER_GROUP, :, tail - w:]
            pkv.append(a.transpose(0, 3, 1, 2))
    p_pool = ut[:, 1:]

    so, sl, skv = [], [], []
    for g, w in enumerate(WINDOWS):
        if g in cached_out:
            og, lg, ko, vo = cached_out[g]
        else:
            og, lg, ko, vo = _cached_attention(*cached_operands(g), g, slopes[g], n_new=S)
        so.append(og[:, :S].transpose(1, 0, 2).reshape(1, 1, n_s, GROUP_W).astype(bf16))
        sl.append(lg[:, :S].transpose(1, 0, 2).reshape(1, 1, n_s, GROUP_W))
        for a in (ko, vo):
            skv.append(a.reshape(Bd, HEADS_PER_GROUP, HEAD_DIM, w).transpose(0, 3, 1, 2))
    y_sample = _mix_and_moe(xs[None], so, sl, pos[None], sgs[None], wts, tm=n_s, blk=128)[0]
    y_sample = y_sample.reshape(S, Bd, D).transpose(1, 0, 2)
    s_pool = st.transpose(1, 0, 2)

    return (y_prompt, y_sample, *pkv, p_pool, *skv, s_pool)
```

```python
import functools

import jax
import jax.numpy as jnp
from jax import lax
from jax.experimental import pallas as pl
from jax.experimental.pallas import tpu as pltpu

D_MODEL = 1024
HEAD_DIM = 64
HEADS_PER_GROUP = 4
WINDOWS = (128, 512, 2048)
DILATIONS = (1, 4, 16)
N_GROUPS = len(WINDOWS)
N_HEADS = HEADS_PER_GROUP * N_GROUPS
ATTN_W = N_HEADS * HEAD_DIM
GROUP_W = HEADS_PER_GROUP * HEAD_DIM
BAND = 128
POOL_WINDOWS = (2, 4, 8, 16)
POOL_GW = 128
POOL_W = len(POOL_WINDOWS) * POOL_GW
POOL_STATE = max(POOL_WINDOWS) - 1
POOL_HIST = 32
assert POOL_WINDOWS == (2, 4, 8, 16)
N_EXPERT_GROUPS = 4
EXPERTS_PER_GROUP = 8
N_EXPERTS = N_EXPERT_GROUPS * EXPERTS_PER_GROUP
D_EXPERT = 512
QKVU_W = 3 * ATTN_W + POOL_W
IN_W = QKVU_W + 2 * D_MODEL
NEG = -1e30
EPS = 1e-6
LANES = 128
ROUTER_W = LANES
EXPERT_COL0 = 8
VMEM_LIMIT = 56 * 1024 * 1024
IBUF_SLOTS = 2 * (N_GROUPS - 1) * (GROUP_W // LANES)
SEG_ALIGN = 16

assert all(w // d == BAND for w, d in zip(WINDOWS, DILATIONS))

f32 = jnp.float32
bf16 = jnp.bfloat16


def _cparams(sem):
    return pltpu.CompilerParams(dimension_semantics=sem, vmem_limit_bytes=VMEM_LIMIT)


def _rmsnorm_rows(x, g):
    ms = jnp.mean(x * x, axis=-1, keepdims=True)
    return x * lax.rsqrt(ms + EPS) * g


def _head_rmsnorm_chunk(ch, gain):
    lane = lax.broadcasted_iota(jnp.int32, ch.shape, 1)
    lo_mask = lane < HEAD_DIM
    sq = ch * ch
    lo = jnp.sum(jnp.where(lo_mask, sq, 0.0), axis=-1, keepdims=True)
    hi = jnp.sum(jnp.where(lo_mask, 0.0, sq), axis=-1, keepdims=True)
    ss = jnp.where(lo_mask, lo, hi)
    return ch * lax.rsqrt(ss * (1.0 / HEAD_DIM) + EPS) * gain


def _project(x, ln1, w_ref, qg, kg):
    xn = _rmsnorm_rows(x, ln1).astype(bf16)
    nch = ATTN_W // LANES
    qs = qg * (HEAD_DIM ** -0.5)
    zq = jnp.dot(xn, w_ref[:, 0:ATTN_W], preferred_element_type=f32)
    q = [_head_rmsnorm_chunk(zq[:, c * LANES:(c + 1) * LANES], qs[:, c * LANES:(c + 1) * LANES]) for c in range(nch)]
    zk = jnp.dot(xn, w_ref[:, ATTN_W:2 * ATTN_W], preferred_element_type=f32)
    k = [_head_rmsnorm_chunk(zk[:, c * LANES:(c + 1) * LANES], kg[:, c * LANES:(c + 1) * LANES]) for c in range(nch)]
    zv = jnp.dot(xn, w_ref[:, 2 * ATTN_W:3 * ATTN_W], preferred_element_type=f32)
    v = [zv[:, c * LANES:(c + 1) * LANES] for c in range(nch)]
    u = jnp.dot(xn, w_ref[:, 3 * ATTN_W:3 * ATTN_W + POOL_W], preferred_element_type=f32)
    sg = jax.nn.sigmoid(jnp.dot(xn, w_ref[:, QKVU_W:QKVU_W + 2 * D_MODEL], preferred_element_type=f32)).astype(bf16)
    return q, k, v, u, sg


def _proj_prompt_kernel(x_ref, ln1_ref, w_ref, qg_ref, kg_ref, plin_ref, pscale_ref, *rest,
                        tm, n_tiles, tail_tiles, hosted):
    n_hosted_in, n_hosted_out = (7, 4) if hosted else (0, 0)
    hosted_in, rest = rest[:n_hosted_in], rest[n_hosted_in:]
    qkv_refs = rest[:3 * N_GROUPS]
    po_ref, kt_ref, vt_ref, ut_ref, sg_ref = rest[3 * N_GROUPS:3 * N_GROUPS + 5]
    rest = rest[3 * N_GROUPS + 5:]
    hosted_out, (ubuf, pa, pb, sbuf) = rest[:n_hosted_out], rest[n_hosted_out:]
    i = pl.program_id(1)
    hist = POOL_STATE + 1
    ph, rows = POOL_HIST, POOL_HIST + tm

    @pl.when(i == 0)
    def _():
        ubuf[0:ph, :] = jnp.zeros((ph, POOL_W), f32)

    @pl.when(i > 0)
    def _():
        ubuf[0:ph, :] = ubuf[tm:tm + ph, :]

    stages = _cached_attn_stages(*hosted_in, *hosted_out, **hosted) if hosted else ()
    for s in stages:
        s()
    q, k, v, u, sg = _project(x_ref[0], ln1_ref[...], w_ref, qg_ref[...], kg_ref[...])
    sg_ref[0] = sg
    cpg = GROUP_W // LANES
    slot = 0
    for t, chunks in enumerate((q, k, v)):
        for g, dil in enumerate(DILATIONS):
            out_ref = qkv_refs[3 * g + t]
            for c in range(cpg):
                val = chunks[g * cpg + c]
                cols = slice(c * LANES, (c + 1) * LANES)
                if dil == 1:
                    out_ref[0, 0, :, cols] = val.astype(bf16)
                else:
                    sbuf[slot * tm:(slot + 1) * tm, :] = val
                    for r in range(dil):
                        out_ref[0, r, :, cols] = sbuf[pl.ds(slot * tm + r, tm // dil, stride=dil), :].astype(bf16)
                    slot += 1

    ubuf[ph:rows, :] = u
    gw = POOL_GW
    pa[8:rows, :] = ubuf[8:rows, :] + ubuf[7:rows - 1, :]
    pb[16:rows, gw:] = pa[16:rows, gw:] + pa[14:rows - 2, gw:]
    pa[24:rows, 2 * gw:] = pb[24:rows, 2 * gw:] + pb[20:rows - 4, 2 * gw:]
    pb[32:rows, 3 * gw:] = pa[32:rows, 3 * gw:] + pa[24:rows - 8, 3 * gw:]
    pos = i * tm + lax.broadcasted_iota(jnp.int32, (tm, POOL_GW), 0)
    zs = []
    for g, w in enumerate(POOL_WINDOWS):
        cols = slice(g * POOL_GW, (g + 1) * POOL_GW)
        wsum = (pa if g % 2 == 0 else pb)[ph:rows, cols]
        cnt = jnp.minimum(pos + 1, w).astype(f32)
        zs.append(wsum / cnt - u[:, cols])
    z = jnp.concatenate(zs, axis=-1).astype(bf16)
    po = jnp.dot(z, plin_ref[...], preferred_element_type=f32) * pscale_ref[...]
    po_ref[0] = po.astype(bf16)

    @pl.when(i >= n_tiles - tail_tiles)
    def _():
        kt_ref[0] = jnp.concatenate(k, axis=-1).T
        vt_ref[0] = jnp.concatenate(v, axis=-1).T

    @pl.when(i == n_tiles - 1)
    def _():
        ut_ref[0] = ubuf[rows - hist:rows, :]


def _proj_prompt(x, ln1, w_qkvu, qg, kg, plin_bd, pscale, *, tm, hosted=None):
    B, T, D = x.shape
    n_tiles = T // tm
    tail = max(WINDOWS)
    assert T % tm == 0 and tail % tm == 0 and T >= tail
    tail_tiles = tail // tm
    hist = POOL_STATE + 1
    kern = functools.partial(_proj_prompt_kernel, tm=tm, n_tiles=n_tiles, tail_tiles=tail_tiles,
                             hosted=hosted['params'] if hosted else None)
    h_ops, h_in, h_out, h_shape = ((), [], [], []) if not hosted else (
        hosted['operands'], hosted['in_specs'], hosted['out_specs'], hosted['out_shape'])
    const = lambda b, i: (0, 0)
    assert all(tm % (16 * d) == 0 for d in DILATIONS)
    qkv_specs = [pl.BlockSpec((1, d, tm // d, GROUP_W), lambda b, i: (b, 0, i, 0)) for d in DILATIONS for _ in range(3)]
    qkv_shapes = [jax.ShapeDtypeStruct((B, d, T // d, GROUP_W), bf16) for d in DILATIONS for _ in range(3)]
    tail_spec = pl.BlockSpec((1, ATTN_W, tm), lambda b, i: (b, 0, jnp.maximum(i - (n_tiles - tail_tiles), 0)))
    return pl.pallas_call(
        kern,
        grid=(B, n_tiles),
        in_specs=[
            pl.BlockSpec((1, tm, D), lambda b, i: (b, i, 0)),
            pl.BlockSpec((1, D), const),
            pl.BlockSpec((D, IN_W), const),
            pl.BlockSpec((1, ATTN_W), const),
            pl.BlockSpec((1, ATTN_W), const),
            pl.BlockSpec((POOL_W, POOL_W), const),
            pl.BlockSpec((1, POOL_W), const),
        ] + h_in,
        out_specs=qkv_specs
        + [pl.BlockSpec((1, tm, POOL_W), lambda b, i: (b, i, 0)),
           tail_spec, tail_spec,
           pl.BlockSpec((1, hist, POOL_W), lambda b, i: (b, 0, 0)),
           pl.BlockSpec((1, tm, 2 * D), lambda b, i: (b, i, 0))] + h_out,
        out_shape=qkv_shapes
        + [jax.ShapeDtypeStruct((B, T, POOL_W), bf16),
           jax.ShapeDtypeStruct((B, ATTN_W, tail), f32),
           jax.ShapeDtypeStruct((B, ATTN_W, tail), f32),
           jax.ShapeDtypeStruct((B, hist, POOL_W), f32),
           jax.ShapeDtypeStruct((B, T, 2 * D), bf16)] + h_shape,
        scratch_shapes=[pltpu.VMEM((POOL_HIST + tm, POOL_W), f32)] * 3
                       + [pltpu.VMEM((3 * (N_GROUPS - 1) * (GROUP_W // LANES) * tm, LANES), f32)],
        compiler_params=_cparams(("arbitrary", "arbitrary")),
        name="proj_prompt",
    )(x, ln1, w_qkvu, qg, kg, plin_bd, pscale, *h_ops)


def _proj_sample_kernel(x_ref, ln1_ref, w_ref, qg_ref, kg_ref, plin_ref, pscale_ref, state_ref,
                        q_ref, k_ref, v_ref, po_ref, st_ref, sg_ref, *, n_seq, n_new, past_len):
    q, k, v, u, sg = _project(x_ref[...], ln1_ref[...], w_ref, qg_ref[...], kg_ref[...])
    sg_ref[...] = sg
    q_ref[...] = jnp.concatenate(q, axis=-1)
    k_ref[...] = jnp.concatenate(k, axis=-1)
    v_ref[...] = jnp.concatenate(v, axis=-1)
    ext = [state_ref[j] for j in range(POOL_STATE)] + [u[s * n_seq:(s + 1) * n_seq, :] for s in range(n_new)]
    for s in range(n_new):
        zs = []
        for g, w in enumerate(POOL_WINDOWS):
            cols = slice(g * POOL_GW, (g + 1) * POOL_GW)
            cur = ext[POOL_STATE + s][:, cols]
            acc = cur
            for j in range(1, w):
                acc = acc + ext[POOL_STATE + s - j][:, cols]
            cnt = float(min(past_len + s + 1, w))
            zs.append(acc / cnt - cur)
        z = jnp.concatenate(zs, axis=-1).astype(bf16)
        po = jnp.dot(z, plin_ref[...], preferred_element_type=f32) * pscale_ref[...]
        po_ref[s * n_seq:(s + 1) * n_seq, :] = po.astype(bf16)
    for j in range(POOL_STATE):
        st_ref[j] = ext[j + n_new]


def _proj_sample(x, ln1, w_qkvu, qg, kg, plin_bd, pscale, state, *, n_seq, n_new, past_len):
    n = n_seq * n_new
    kern = functools.partial(_proj_sample_kernel, n_seq=n_seq, n_new=n_new, past_len=past_len)
    return pl.pallas_call(
        kern,
        out_shape=[jax.ShapeDtypeStruct((n, ATTN_W), f32)] * 3
        + [jax.ShapeDtypeStruct((n, POOL_W), bf16),
           jax.ShapeDtypeStruct((POOL_STATE, n_seq, POOL_W), f32),
           jax.ShapeDtypeStruct((n, 2 * D_MODEL), bf16)],
        compiler_params=pltpu.CompilerParams(vmem_limit_bytes=VMEM_LIMIT),
        name="proj_sample",
    )(x, ln1, w_qkvu, qg, kg, plin_bd, pscale, state)


def _head_masks(shape):
    lane = lax.broadcasted_iota(jnp.int32, shape, len(shape) - 1)
    return [(lane >= h * HEAD_DIM) & (lane < (h + 1) * HEAD_DIM) for h in range(HEADS_PER_GROUP)]


def _band_attn_kernel(q_ref, kc_ref, kp_ref, vc_ref, vp_ref, bias_ref, *rest, tl, unroll, cast):
    if cast:
        w_ref, o_ref, l_ref, wb_ref, kbuf, vbuf = rest
        wb_ref[...] = w_ref[...].astype(bf16)
    else:
        o_ref, l_ref, kbuf, vbuf = rest
    i = pl.program_id(2)
    kbuf[0:BAND, :] = kp_ref[0, 0]
    kbuf[BAND:2 * BAND, :] = kc_ref[0, 0, 0:BAND, :]
    vbuf[0:BAND, :] = vp_ref[0, 0]
    vbuf[BAND:2 * BAND, :] = vc_ref[0, 0, 0:BAND, :]
    masks = _head_masks((BAND, GROUP_W))

    def sub_block(j, kk, vv, var):
        r0 = j * BAND if isinstance(j, int) else pl.multiple_of(j * BAND, BAND)
        q = q_ref[0, 0, pl.ds(r0, BAND), :]
        qm = jnp.concatenate([jnp.where(m, q, jnp.zeros_like(q)) for m in masks], axis=0)
        s = lax.dot_general(qm, kk, (((1,), (1,)), ((), ())), preferred_element_type=f32)
        s = s + bias_ref[var]
        m = jnp.max(s, axis=-1, keepdims=True)
        p = jnp.exp(s - m)
        den = jnp.sum(p, axis=-1, keepdims=True)
        pv = jnp.dot(p.astype(bf16), vv, preferred_element_type=f32)
        o = jnp.zeros((BAND, GROUP_W), f32)
        ms = jnp.zeros((BAND, GROUP_W), f32)
        ds = jnp.ones((BAND, GROUP_W), f32)
        for h, msk in enumerate(masks):
            rows = slice(h * BAND, (h + 1) * BAND)
            o = jnp.where(msk, pv[rows], o)
            ms = jnp.where(msk, m[rows], ms)
            ds = jnp.where(msk, den[rows], ds)
        o_ref[0, 0, pl.ds(r0, BAND), :] = (o / ds).astype(bf16)
        l_ref[0, 0, pl.ds(r0, BAND), :] = ms + jnp.log(ds)

    sub_block(0, kbuf[...], vbuf[...], jnp.where(i == 0, 0, 1))

    def body(j, carry):
        k0 = pl.multiple_of((j - 1) * BAND, BAND)
        sub_block(j, kc_ref[0, 0, pl.ds(k0, 2 * BAND), :], vc_ref[0, 0, pl.ds(k0, 2 * BAND), :], 1)
        return carry

    if tl > BAND:
        lax.fori_loop(1, tl // BAND, body, 0, unroll=unroll)


def _band_bias(slopes_g, dil):
    qi = jnp.arange(BAND)[:, None]
    kb = jnp.arange(2 * BAND)[None, :]
    rel = qi + BAND - kb
    valid = (rel >= 0) & (rel <= BAND)
    alibi = -slopes_g[:, None, None] * (dil * rel)[None].astype(f32)
    variants = []
    for first in (True, False):
        ok = valid & (kb >= BAND) if first else valid
        variants.append(jnp.where(ok[None], alibi, NEG).reshape(HEADS_PER_GROUP * BAND, 2 * BAND))
    return jnp.stack(variants, axis=0)


def _band_attention(q, k, v, g, slopes_g, *, tl_max=1024, unroll=7, cast_w=None):
    B, dil, L, _ = q.shape
    tl = min(tl_max, L)
    assert dil == DILATIONS[g] and L % tl == 0 and tl % BAND == 0
    nsub = tl // BAND
    nl = L // tl
    bias = _band_bias(slopes_g, dil)
    cur = pl.BlockSpec((1, 1, tl, GROUP_W), lambda b, r, i: (b, r, i, 0))
    prev = pl.BlockSpec((1, 1, BAND, GROUP_W), lambda b, r, i: (b, r, jnp.maximum(i * nsub - 1, 0), 0))
    host_cast = cast_w is not None and cast_w.shape[0] == B * dil * nl
    w_spec, w_ops, w_shape = [], (), []
    if host_cast:
        w_spec = [pl.BlockSpec((1,) + cast_w.shape[1:], lambda b, r, i: ((b * dil + r) * nl + i, 0, 0))]
        w_ops, w_shape = (cast_w,), [jax.ShapeDtypeStruct(cast_w.shape, bf16)]
    res = pl.pallas_call(
        functools.partial(_band_attn_kernel, tl=tl, unroll=max(1, min(unroll, nsub - 1)), cast=host_cast),
        grid=(B, dil, nl),
        in_specs=[cur, cur, prev, cur, prev,
                  pl.BlockSpec((2, HEADS_PER_GROUP * BAND, 2 * BAND), lambda b, r, i: (0, 0, 0))] + w_spec,
        out_specs=[cur, cur] + w_spec,
        out_shape=[jax.ShapeDtypeStruct((B, dil, L, GROUP_W), bf16),
                   jax.ShapeDtypeStruct((B, dil, L, GROUP_W), f32)] + w_shape,
        scratch_shapes=[pltpu.VMEM((2 * BAND, GROUP_W), bf16), pltpu.VMEM((2 * BAND, GROUP_W), bf16)],
        compiler_params=_cparams(("arbitrary", "arbitrary", "arbitrary")),
        name="band_attn_g%d" % g,
    )(q, k, k, v, v, bias, *w_ops)
    if cast_w is None:
        return res
    return res[0], res[1], (res[2] if host_cast else cast_w.astype(bf16))


def _cached_attn_kernel(*refs, **params):
    for stage in _cached_attn_stages(*refs, **params):
        stage()


def _cached_attn_stages(q_ref, kn_ref, vn_ref, kc_ref, vc_ref, bc_ref, bn_ref,
                        o_ref, l_ref, ko_ref, vo_ref, *, nb, n_new, win):
    def roll(c_ref, new_ref, out_ref):
        lane_t = lax.broadcasted_iota(jnp.int32, (nb, GROUP_W, LANES), 2)
        rolled = pltpu.roll(c_ref[...], win - n_new, axis=2)
        out_ref[...] = rolled
        new_t = jnp.swapaxes(jnp.concatenate([new_ref[...], jnp.zeros((nb, LANES - 8, GROUP_W), f32)], axis=1), 1, 2)
        new_t = pltpu.roll(new_t, LANES - n_new, axis=2)
        out_ref[:, :, win - LANES:win] = jnp.where(lane_t >= LANES - n_new, new_t, rolled[:, :, win - LANES:win])

    return (functools.partial(_cached_attn_scores, q_ref, kn_ref, vn_ref, kc_ref, vc_ref, bc_ref, bn_ref, o_ref, l_ref,
                              nb=nb, n_new=n_new),
            functools.partial(roll, kc_ref, kn_ref, ko_ref),
            functools.partial(roll, vc_ref, vn_ref, vo_ref))


def _cached_attn_scores(q_ref, kn_ref, vn_ref, kc_ref, vc_ref, bc_ref, bn_ref, o_ref, l_ref, *, nb, n_new):
    masks8 = _head_masks((nb, 8, GROUP_W))
    q8, kn8, vn8 = q_ref[...], kn_ref[...], vn_ref[...]
    kc, vc = kc_ref[...], vc_ref[...]
    qm = jnp.concatenate([jnp.where(m, q8, 0.0) for m in masks8], axis=1)
    sc = jnp.einsum('bqd,bdk->bqk', qm.astype(bf16), kc.astype(bf16), preferred_element_type=f32) + bc_ref[...]
    m = jnp.max(sc, axis=-1, keepdims=True)
    sn = []
    for t in range(n_new):
        col = jnp.sum(qm * kn8[:, t:t + 1, :], axis=-1, keepdims=True) + bn_ref[:, t:t + 1]
        sn.append(col)
        m = jnp.maximum(m, col)
    pc = jnp.exp(sc - m)
    den = jnp.sum(pc, axis=-1, keepdims=True)
    acc = jnp.einsum('bqk,bdk->bqd', pc.astype(bf16), vc.astype(bf16), preferred_element_type=f32)
    for t in range(n_new):
        pn = jnp.exp(sn[t] - m)
        den = den + pn
        acc = acc + pn * vn8[:, t:t + 1, :]
    acc = acc / den
    lse = m + jnp.log(den)
    o = jnp.zeros((nb, 8, GROUP_W), f32)
    l = jnp.zeros((nb, 8, GROUP_W), f32)
    for h, msk in enumerate(masks8):
        o = jnp.where(msk, acc[:, h * 8:(h + 1) * 8, :], o)
        l = jnp.where(msk, lse[:, h * 8:(h + 1) * 8, :], l)
    o_ref[...] = o
    l_ref[...] = l


def _cached_bias(slopes_g, dil, win, n_new):
    s = jnp.arange(8)[:, None]
    i = jnp.arange(win)[None, :]
    dist = win + s - i
    ok = (dist % dil == 0) & (dist // dil <= BAND) & (s < n_new)
    bc = jnp.where(ok[None], -slopes_g[:, None, None] * dist[None].astype(f32), NEG)
    t = jnp.arange(8)[None, :]
    dn = s - t
    okn = (dn >= 0) & (dn % dil == 0) & (dn // dil <= BAND) & (s < n_new) & (t < n_new)
    bn = jnp.where(okn[None], -slopes_g[:, None, None] * dn[None].astype(f32), NEG)
    pad = (s >= n_new)
    bc = jnp.where(pad[None], 0.0, bc)
    bn = jnp.where(pad[None], 0.0, bn)
    return bc.reshape(HEADS_PER_GROUP * 8, win), bn.reshape(HEADS_PER_GROUP * 8, 8)


def _cached_call_parts(q8, kn8, vn8, kc_t, vc_t, g, slopes_g, *, n_new, nb, step_of):
    Bd, _, win = kc_t.shape
    assert win == WINDOWS[g] and win % LANES == 0 and Bd % nb == 0
    bc, bn = _cached_bias(slopes_g, DILATIONS[g], win, n_new)
    small = pl.BlockSpec((nb, 8, GROUP_W), lambda *idx: (step_of(*idx), 0, 0))
    cache = pl.BlockSpec((nb, GROUP_W, win), lambda *idx: (step_of(*idx), 0, 0))
    const = lambda a: pl.BlockSpec(a.shape, lambda *idx: (0, 0))
    return dict(
        operands=(q8, kn8, vn8, kc_t, vc_t, bc, bn),
        in_specs=[small, small, small, cache, cache, const(bc), const(bn)],
        out_specs=[small, small, cache, cache],
        out_shape=[jax.ShapeDtypeStruct((Bd, 8, GROUP_W), f32)] * 2 + [jax.ShapeDtypeStruct((Bd, GROUP_W, win), f32)] * 2,
        params=dict(nb=nb, n_new=n_new, win=win))


def _cached_attention(q8, kn8, vn8, kc_t, vc_t, g, slopes_g, *, n_new):
    Bd, _, win = kc_t.shape
    nb = max(1, min(Bd, 2048 // win))
    parts = _cached_call_parts(q8, kn8, vn8, kc_t, vc_t, g, slopes_g, n_new=n_new, nb=nb, step_of=lambda b: b)
    return pl.pallas_call(
        functools.partial(_cached_attn_kernel, **parts['params']),
        grid=(Bd // nb,),
        in_specs=parts['in_specs'],
        out_specs=parts['out_specs'],
        out_shape=parts['out_shape'],
        compiler_params=_cparams(("arbitrary",)),
        name="cached_attn_g%d" % g,
    )(*parts['operands'])


def _split_hosted(rest, n_own_out, hosted):
    n_in, n_out = (7, 4) if hosted else (0, 0)
    hosted_in, rest = rest[:n_in], rest[n_in:]
    own_out, rest = rest[:n_own_out], rest[n_own_out:]
    return hosted_in, own_out, rest[:n_out], rest[n_out:]


def _merge_kernel(x_ref, o0_ref, o1_ref, o2_ref, l0_ref, l1_ref, l2_ref, po_ref, sg_ref,
                  wpa_ref, wpb_ref, wo_ref, ln2_ref, wr_ref, br_ref, tri_ref, ltri_ref,
                  *rest, tm, ts, dils, hosted):
    hosted_in, (h_ref, xl_ref, wc_ref, tc_ref), hosted_out, (ibuf,) = _split_hosted(rest, 4, hosted)
    if hosted:
        _cached_attn_kernel(*hosted_in, *hosted_out, **hosted)
    slots = iter(range(IBUF_SLOTS))

    def token_order(ref, dil):
        if dil == 1:
            return ref[0, 0].astype(f32)
        chunks = []
        for c in range(GROUP_W // LANES):
            base = next(slots) * tm
            for r in range(dil):
                ibuf[pl.ds(base + r, tm // dil, stride=dil), :] = ref[0, r, :, c * LANES:(c + 1) * LANES].astype(f32)
            chunks.append(ibuf[base:base + tm, :])
        return jnp.concatenate(chunks, axis=-1)

    x = x_ref[0]
    l0, l1, l2 = (token_order(r, d) for r, d in zip((l0_ref, l1_ref, l2_ref), dils))
    lm = jnp.maximum(jnp.maximum(l0, l1), l2)
    e0, e1, e2 = jnp.exp(l0 - lm), jnp.exp(l1 - lm), jnp.exp(l2 - lm)
    o0, o1, o2 = (token_order(r, d) for r, d in zip((o0_ref, o1_ref, o2_ref), dils))
    attn = (e0 * o0 + e1 * o1 + e2 * o2) / (e0 + e1 + e2)
    ma = jnp.dot(attn.astype(bf16), wpa_ref[...], preferred_element_type=f32)
    mb = jnp.dot(po_ref[0], wpb_ref[...], preferred_element_type=f32)
    mix = sg_ref[0, :, :D_MODEL].astype(f32) * ma + sg_ref[0, :, D_MODEL:].astype(f32) * mb
    h = x + jnp.dot(mix.astype(bf16), wo_ref[...], preferred_element_type=f32)
    h_ref[...] = h
    xn2 = _rmsnorm_rows(h, ln2_ref[...]).astype(bf16)

    lt = (jnp.dot(xn2, wr_ref[...], preferred_element_type=f32) + br_ref[...]).T
    row8 = lax.broadcasted_iota(jnp.int32, (8, tm), 0)
    gl = jnp.where(row8 < N_EXPERT_GROUPS, lt[0:8], -jnp.inf)
    gmax = jnp.max(gl, axis=0, keepdims=True)
    gidx = jnp.min(jnp.where(gl == gmax, row8, 8), axis=0, keepdims=True)
    pg = 1.0 / jnp.sum(jnp.exp(gl - gmax), axis=0, keepdims=True)
    sel = jnp.zeros((8, tm), f32)
    for g in range(N_EXPERT_GROUPS):
        lo = EXPERT_COL0 + g * EXPERTS_PER_GROUP
        sel = jnp.where(gidx == g, lt[lo:lo + EXPERTS_PER_GROUP], sel)
    v0 = jnp.max(sel, axis=0, keepdims=True)
    i0 = jnp.min(jnp.where(sel == v0, row8, 8), axis=0, keepdims=True)
    sel2 = jnp.where(row8 == i0, -jnp.inf, sel)
    v1 = jnp.max(sel2, axis=0, keepdims=True)
    i1 = jnp.min(jnp.where(sel2 == v1, row8, 8), axis=0, keepdims=True)
    t = jnp.exp(v1 - v0)
    w0 = pg / (1.0 + t)
    w1 = pg * t / (1.0 + t)
    eid0 = gidx * EXPERTS_PER_GROUP + i0
    eid1 = gidx * EXPERTS_PER_GROUP + i1
    erow = lax.broadcasted_iota(jnp.int32, (N_EXPERTS, tm), 0)
    oh0 = erow == eid0
    oh1 = erow == eid1
    cnt = jnp.where(oh0, 1.0, jnp.where(oh1, 1.0, 0.0))
    before = jnp.dot(cnt.astype(bf16), tri_ref[...], preferred_element_type=f32)
    tcount = jnp.sum(cnt, axis=1, keepdims=True)
    units = jnp.floor((tcount + (SEG_ALIGN - 1)) * (1.0 / SEG_ALIGN))
    ub = jnp.broadcast_to(units, (N_EXPERTS, LANES)).astype(bf16)
    seg0 = SEG_ALIGN * jnp.dot(ltri_ref[...], ub, preferred_element_type=f32)[:, 0:1]
    pos_e = seg0 + before
    lpos0 = jnp.sum(jnp.where(oh0, pos_e, 0.0), axis=0, keepdims=True)
    lpos1 = jnp.sum(jnp.where(oh1, pos_e, 0.0), axis=0, keepdims=True)
    prow = lax.broadcasted_iota(jnp.int32, (ts, tm), 0)
    perm = jnp.where(prow == lpos0.astype(jnp.int32), 1.0, jnp.where(prow == lpos1.astype(jnp.int32), 1.0, 0.0))
    xl_ref[...] = jnp.dot(perm.astype(bf16), xn2, preferred_element_type=f32).astype(bf16)
    tc_ref[...] = jnp.broadcast_to(tcount, (N_EXPERTS, LANES))
    rowl = lax.broadcasted_iota(jnp.int32, (LANES, tm), 0)
    wslab = jnp.zeros((LANES, tm), f32)
    for r, val in enumerate((w0, w1, lpos0, lpos1)):
        wslab = jnp.where(rowl == r, val, wslab)
    wc_ref[...] = wslab.T


def _hosted_parts(hosted):
    if not hosted:
        return (), [], [], [], None
    return hosted['operands'], hosted['in_specs'], hosted['out_specs'], hosted['out_shape'], hosted['params']


def _merge(x, o, l, po, sg, w_pa, w_pb, w_o, ln2, w_router, b_router, *, tm, hosted=None):
    B, T, _ = x.shape
    h_ops, h_in, h_out, h_shape, h_params = _hosted_parts(hosted)
    assert T % tm == 0
    nt = T // tm
    n = B * T
    dils = tuple(a.shape[1] for a in o)
    assert all(tm % (8 * d) == 0 for d in dils)
    assert 2 * tm // SEG_ALIGN <= 256
    ts = _sorted_tile_rows(tm)
    tri = (jnp.arange(tm)[:, None] < jnp.arange(tm)[None, :]).astype(bf16)
    ltri = (jnp.arange(N_EXPERTS)[None, :] < jnp.arange(N_EXPERTS)[:, None]).astype(bf16)
    rows3 = lambda w: pl.BlockSpec((1, tm, w), lambda b, i: (b, i, 0))
    flat = lambda r, w: pl.BlockSpec((r, w), lambda b, i: (b * nt + i, 0))
    grp = [pl.BlockSpec((1, d, tm // d, GROUP_W), lambda b, i: (b, 0, i, 0)) for d in dils]
    full = lambda a: pl.BlockSpec(a.shape, lambda b, i: (0,) * a.ndim)
    weights = (w_pa, w_pb, w_o, ln2, w_router, b_router, tri, ltri)
    return pl.pallas_call(
        functools.partial(_merge_kernel, tm=tm, ts=ts, dils=dils, hosted=h_params),
        grid=(B, nt),
        in_specs=[rows3(D_MODEL)] + grp + grp + [rows3(POOL_W), rows3(2 * D_MODEL)]
        + [full(a) for a in weights] + h_in,
        out_specs=[flat(tm, D_MODEL), flat(ts, D_MODEL), flat(tm, LANES),
                   pl.BlockSpec((N_EXPERTS, LANES), lambda b, i: (0, b * nt + i))] + h_out,
        out_shape=[jax.ShapeDtypeStruct((n, D_MODEL), f32),
                   jax.ShapeDtypeStruct((B * nt * ts, D_MODEL), bf16),
                   jax.ShapeDtypeStruct((n, LANES), f32),
                   jax.ShapeDtypeStruct((N_EXPERTS, B * nt * LANES), f32)] + h_shape,
        scratch_shapes=[pltpu.VMEM((IBUF_SLOTS * tm, LANES), f32)],
        compiler_params=_cparams(("arbitrary", "arbitrary")),
        name="merge_router",
    )(x, *o, *l, po, sg, *weights, *h_ops)


def _sorted_tile_rows(tm):
    return -(-(2 * tm + N_EXPERTS * (SEG_ALIGN - 1)) // LANES) * LANES


def _moe_seg_kernel(blk_e_ref, blk_r0_ref, blk_n_ref, seg_g_ref, seg_c_ref, seg_src_ref, used_ref,
                    xl_hbm, wg_ref, wu_ref, wd_ref, yl_hbm,
                    xbuf, ybuf, sem_in, sem_out, sem_zero, ptr, *, blk, ts, n_tiles, n_blocks):
    b = pl.program_id(0)
    slot = lax.rem(b, 2)
    unit_bits = (blk // SEG_ALIGN).bit_length()

    def row_pieces(rows):
        units = lax.shift_right_logical(rows, jnp.int32(SEG_ALIGN.bit_length() - 1))
        off = jnp.int32(0)
        for bit in reversed(range(unit_bits)):
            on = lax.shift_right_logical(units, jnp.int32(bit)) & 1
            yield on == 1, off, SEG_ALIGN << bit
            off = off + on * (SEG_ALIGN << bit)

    def copy_rows(src, src_row, dst, dst_row, rows, sem):
        def start(off, size):
            pltpu.make_async_copy(src.at[pl.ds(pl.multiple_of(src_row + off, SEG_ALIGN), size)],
                                  dst.at[pl.ds(pl.multiple_of(dst_row + off, SEG_ALIGN), size)], sem).start()

        chunk = SEG_ALIGN << 3
        n_chunks = lax.shift_right_logical(rows, jnp.int32(chunk.bit_length() - 1))

        def whole_chunk(j, c):
            start(j * chunk, chunk)
            return c

        lax.fori_loop(0, n_chunks, whole_chunk, 0)
        off = n_chunks * chunk
        rest = rows - off
        for bit in (2, 1, 0):
            size = SEG_ALIGN << bit
            on = (rest & size) != 0

            @pl.when(on)
            def _():
                start(off, size)

            off = off + jnp.where(on, size, 0)

    def wait_rows(src, dst, rows, sem):
        for on, _, size in row_pieces(rows):
            @pl.when(on)
            def _():
                pltpu.make_async_copy(src.at[pl.ds(0, size)], dst.at[pl.ds(0, size)], sem).wait()

    def for_pieces(bb, stream, fn):
        e, r0, n = blk_e_ref[bb], blk_r0_ref[bb], blk_n_ref[bb]

        @pl.when(n > 0)
        def _():
            def seg(i):
                return jnp.minimum(i, n_tiles - 1) * N_EXPERTS + e

            def cond(i):
                return (i < n_tiles) & (seg_g_ref[seg(i)] < r0 + n)

            def body(i):
                g = seg_g_ref[seg(i)]
                lo = jnp.maximum(g, r0)
                hi = jnp.minimum(g + seg_c_ref[seg(i)], r0 + n)

                @pl.when(hi > lo)
                def _():
                    fn(seg_src_ref[seg(i)] + (lo - g), lo - r0, hi - lo)

                return i + 1

            end = lax.while_loop(cond, body, jnp.where(r0 == 0, 0, ptr[stream]))
            ptr[stream] = jnp.maximum(end - 1, 0)

    def gather(bb):
        s = lax.rem(bb, 2)
        for_pieces(bb, 0, lambda lrow, brow, rows: copy_rows(xl_hbm, lrow, xbuf.at[s], brow, rows, sem_in.at[s]))

    def scatter(bb):
        s = lax.rem(bb, 2)
        for_pieces(bb, 1, lambda lrow, brow, rows: copy_rows(ybuf.at[s], brow, yl_hbm, lrow, rows, sem_out.at[s]))

    def zero_tail(i, go):
        row0 = i * ts + used_ref[i]
        rows = ts - used_ref[i]
        whole = lax.shift_right_logical(rows, jnp.int32(blk.bit_length() - 1))
        zsrc = xbuf.at[1]

        def whole_block(j, c):
            cp = pltpu.make_async_copy(zsrc, yl_hbm.at[pl.ds(pl.multiple_of(row0 + j * blk, SEG_ALIGN), blk)], sem_zero)
            cp.start() if go else cp.wait()
            return c

        lax.fori_loop(0, whole, whole_block, 0)
        rest = rows - whole * blk
        if go:
            copy_rows(zsrc, 0, yl_hbm, row0 + whole * blk, rest, sem_zero)
        else:
            wait_rows(zsrc, yl_hbm, rest, sem_zero)

    @pl.when(b == 0)
    def _():
        xbuf[...] = jnp.zeros_like(xbuf)
        ptr[0] = 0
        ptr[1] = 0
        gather(0)
        for go in (True, False):
            def per_tile(i, c, go=go):
                zero_tail(i, go)
                return c

            lax.fori_loop(0, n_tiles, per_tile, 0)

    n_b = blk_n_ref[b]
    wait_rows(xl_hbm, xbuf.at[slot], n_b, sem_in.at[slot])

    @pl.when(b + 1 < n_blocks)
    def _():
        gather(b + 1)

    @pl.when(b >= 2)
    def _():
        wait_rows(ybuf.at[slot], yl_hbm, blk_n_ref[b - 2], sem_out.at[slot])

    @pl.when(n_b > 0)
    def _():
        x = xbuf[slot]
        hid = jax.nn.silu(jnp.dot(x, wg_ref[0], preferred_element_type=f32)) * jnp.dot(x, wu_ref[0], preferred_element_type=f32)
        ybuf[slot] = jnp.dot(hid.astype(bf16), wd_ref[0], preferred_element_type=f32).astype(bf16)
        scatter(b)

    @pl.when(b == n_blocks - 1)
    def _():
        @pl.when(b >= 1)
        def _():
            wait_rows(ybuf.at[1 - slot], yl_hbm, blk_n_ref[b - 1], sem_out.at[1 - slot])

        wait_rows(ybuf.at[slot], yl_hbm, n_b, sem_out.at[slot])


def _moe_segments(xl, tables, w_gate, w_up, w_down, *, blk, ts, n_tiles):
    blk_e = tables[0]
    n_blocks = blk_e.shape[0]
    assert blk & (blk - 1) == 0 and blk % SEG_ALIGN == 0
    wspec = lambda shape: pl.BlockSpec((1,) + shape, lambda b, be, *_: (be[b], 0, 0))
    hbm = pl.BlockSpec(memory_space=pl.ANY)
    return pl.pallas_call(
        functools.partial(_moe_seg_kernel, blk=blk, ts=ts, n_tiles=n_tiles, n_blocks=n_blocks),
        grid_spec=pltpu.PrefetchScalarGridSpec(
            num_scalar_prefetch=len(tables),
            grid=(n_blocks,),
            in_specs=[hbm, wspec((D_MODEL, D_EXPERT)), wspec((D_MODEL, D_EXPERT)), wspec((D_EXPERT, D_MODEL))],
            out_specs=hbm,
            scratch_shapes=[pltpu.VMEM((2, blk, D_MODEL), bf16), pltpu.VMEM((2, blk, D_MODEL), bf16),
                            pltpu.SemaphoreType.DMA((2,)), pltpu.SemaphoreType.DMA((2,)), pltpu.SemaphoreType.DMA,
                            pltpu.SMEM((2,), jnp.int32)],
        ),
        out_shape=jax.ShapeDtypeStruct(xl.shape, bf16),
        compiler_params=_cparams(("arbitrary",)),
        name="moe_experts",
    )(*tables, xl, w_gate, w_up, w_down)


def _unsort_kernel(h_ref, wc_ref, yl_ref, *rest, ts, hosted):
    hosted_in, (y_ref,), hosted_out, _ = _split_hosted(rest, 1, hosted)
    if hosted:
        _cached_attn_kernel(*hosted_in, *hosted_out, **hosted)
    w = wc_ref[...]
    yl = yl_ref[...]
    col = lax.broadcasted_iota(jnp.int32, (w.shape[0], ts), 1)
    y = h_ref[...]
    for k in range(2):
        pick = jnp.where(col == w[:, 2 + k:3 + k].astype(jnp.int32), 1.0, 0.0).astype(bf16)
        y = y + w[:, k:k + 1] * jnp.dot(pick, yl, preferred_element_type=f32)
    y_ref[...] = y


def _unsort(h, wc, yl, *, tm, ts, hosted=None):
    n = h.shape[0]
    h_ops, h_in, h_out, h_shape, h_params = _hosted_parts(hosted)
    rows = lambda r, w: pl.BlockSpec((r, w), lambda i: (i, 0))
    return pl.pallas_call(
        functools.partial(_unsort_kernel, ts=ts, hosted=h_params),
        grid=(n // tm,),
        in_specs=[rows(tm, D_MODEL), rows(tm, LANES), rows(ts, D_MODEL)] + h_in,
        out_specs=[rows(tm, D_MODEL)] + h_out,
        out_shape=[jax.ShapeDtypeStruct((n, D_MODEL), f32)] + h_shape,
        compiler_params=_cparams(("arbitrary",)),
        name="moe_unsort",
    )(h, wc, yl, *h_ops)


def _mix_and_moe(x, o, l, po, sg, wts, *, tm, blk, host_merge=None, host_unsort=None):
    n = x.shape[0] * x.shape[1]
    n_tiles = n // tm
    nt = x.shape[1] // tm
    ts = _sorted_tile_rows(tm)
    hosted = host_merge(n_tiles, lambda b, i: b * nt + i) if host_merge else None
    h, xl, wc, tc, *merge_hosted = _merge(x, o, l, po, sg, wts['w_pa'], wts['w_pb'], wts['w_o'], wts['ln2'],
                                          wts['w_router'], wts['b_router'], tm=tm, hosted=hosted)
    c8 = (tc[:, ::LANES].T.astype(jnp.int32) + (SEG_ALIGN - 1)) // SEG_ALIGN * SEG_ALIGN
    seg_src = jnp.arange(n_tiles, dtype=jnp.int32)[:, None] * ts + jnp.cumsum(c8, axis=1) - c8
    seg_g = jnp.cumsum(c8, axis=0) - c8
    tot = jnp.sum(c8, axis=0)
    padded = (tot + blk - 1) // blk * blk
    pad_ends = jnp.cumsum(padded)
    n_blocks = -(-(2 * n + n_tiles * N_EXPERTS * (SEG_ALIGN - 1) + N_EXPERTS * (blk - 1)) // blk)
    blk_start = jnp.arange(n_blocks, dtype=jnp.int32) * blk
    blk_e = jnp.minimum(jnp.sum(pad_ends[None, :] <= blk_start[:, None], axis=1), N_EXPERTS - 1).astype(jnp.int32)
    pick = blk_e[:, None] == jnp.arange(N_EXPERTS, dtype=jnp.int32)[None, :]
    blk_r0 = blk_start - jnp.sum(jnp.where(pick, (pad_ends - padded)[None, :], 0), axis=1)
    blk_n = jnp.clip(jnp.sum(jnp.where(pick, tot[None, :], 0), axis=1) - blk_r0, 0, blk)
    used = jnp.sum(c8, axis=1)
    tables = tuple(a.astype(jnp.int32).reshape(-1) for a in (blk_e, blk_r0, blk_n, seg_g, c8, seg_src, used))
    yl = _moe_segments(xl, tables, wts['w_gate'], wts['w_up'], wts['w_down'], blk=blk, ts=ts, n_tiles=n_tiles)
    hosted = host_unsort(n_tiles, lambda i: i) if host_unsort else None
    y, *unsort_hosted = _unsort(h, wc, yl, tm=tm, ts=ts, hosted=hosted)
    return y, merge_hosted, unsort_hosted


def kernel(x_prompt, x_sample, cache_k_w128, cache_v_w128, cache_k_w512, cache_v_w512, cache_k_w2048, cache_v_w2048, state_pool, ln1, w_in, q_gain, k_gain, pool_lin, pool_scale, w_pa, w_pb, w_o, ln2, w_rg, b_rg, w_re, b_re, w_gate, w_up, w_down):
    B, T, D = x_prompt.shape
    Bd, S, _ = x_sample.shape
    past_len = 8192
    caches = ((cache_k_w128, cache_v_w128), (cache_k_w512, cache_v_w512), (cache_k_w2048, cache_v_w2048))
    slopes = jnp.exp2(-8.0 * jnp.arange(1, N_HEADS + 1, dtype=f32) / N_HEADS).reshape(N_GROUPS, HEADS_PER_GROUP)

    plin_bd = jnp.zeros((POOL_W, POOL_W), f32)
    for g in range(len(POOL_WINDOWS)):
        plin_bd = plin_bd.at[g * POOL_GW:(g + 1) * POOL_GW, g * POOL_GW:(g + 1) * POOL_GW].set(pool_lin[g])
    w_router = jnp.zeros((D, ROUTER_W), f32).at[:, :N_EXPERT_GROUPS].set(w_rg)
    w_router = w_router.at[:, EXPERT_COL0:EXPERT_COL0 + N_EXPERTS].set(w_re)
    b_router = jnp.zeros((1, ROUTER_W), f32).at[0, :N_EXPERT_GROUPS].set(b_rg)
    b_router = b_router.at[0, EXPERT_COL0:EXPERT_COL0 + N_EXPERTS].set(b_re)
    wts = dict(w_pa=w_pa.astype(bf16), w_pb=w_pb.astype(bf16), w_o=w_o.astype(bf16), ln2=ln2.reshape(1, D),
               w_router=w_router.astype(bf16), b_router=b_router)
    proj_w = (ln1.reshape(1, D), w_in.astype(bf16), q_gain.reshape(1, ATTN_W), k_gain.reshape(1, ATTN_W),
              plin_bd.astype(bf16), pool_scale.reshape(1, POOL_W))

    n_s = Bd * S
    xs = x_sample.transpose(1, 0, 2).reshape(n_s, D)
    qs, ks, vs, pos, st, sgs = _proj_sample(xs, *proj_w, state_pool.transpose(1, 0, 2),
                                            n_seq=Bd, n_new=S, past_len=past_len)
    pad8 = lambda a: jnp.pad(a.reshape(S, Bd, GROUP_W).transpose(1, 0, 2), ((0, 0), (0, 8 - S), (0, 0)))

    def cached_operands(g):
        cols = slice(g * GROUP_W, (g + 1) * GROUP_W)
        kc, vc = caches[g]
        w = WINDOWS[g]
        return (pad8(qs[:, cols]), pad8(ks[:, cols]), pad8(vs[:, cols]),
                kc.transpose(0, 2, 3, 1).reshape(Bd, GROUP_W, w), vc.transpose(0, 2, 3, 1).reshape(Bd, GROUP_W, w))

    def host(g):
        def parts(steps, step_of):
            if Bd % steps:
                return None
            return _cached_call_parts(*cached_operands(g), g, slopes[g], n_new=S, nb=Bd // steps, step_of=step_of)
        return parts

    tm_proj = 256
    n_t = T // tm_proj
    hosted = host(2)(B * n_t, lambda b, i: b * n_t + i)
    outs = _proj_prompt(x_prompt, *proj_w, tm=tm_proj, hosted=hosted)
    qkv, (po, kt, vt, ut, sg) = outs[:3 * N_GROUPS], outs[3 * N_GROUPS:3 * N_GROUPS + 5]
    cached_out = {}
    if hosted:
        cached_out[2] = outs[3 * N_GROUPS + 5:]
    o, l, (wts['w_gate'], wts['w_up'], wts['w_down']) = zip(*[
        _band_attention(*qkv[3 * g:3 * g + 3], g, slopes[g], cast_w=w) for g, w in enumerate((w_gate, w_up, w_down))])
    y_prompt, in_merge, in_unsort = _mix_and_moe(x_prompt, o, l, po, sg, wts, tm=512, blk=512,
                                                 host_merge=host(1), host_unsort=host(0))
    y_prompt = y_prompt.reshape(B, T, D)
    if in_merge:
        cached_out[1] = in_merge
    if in_unsort:
        cached_out[0] = in_unsort
    tail = kt.shape[2]
    pkv = []
    for g, w in enumerate(WINDOWS):
        for a in (kt, vt):
            a = a.reshape(B, N_HEADS, HEAD_DIM, tail)[:, g * HEADS_PER_GROUP:(g + 1) * HEADS_PER_GROUP, :, tail - w:]
            pkv.append(a.transpose(0, 3, 1, 2))
    p_pool = ut[:, 1:]

    so, sl, skv = [], [], []
    for g, w in enumerate(WINDOWS):
        if g in cached_out:
            og, lg, ko, vo = cached_out[g]
        else:
            og, lg, ko, vo = _cached_attention(*cached_operands(g), g, slopes[g], n_new=S)
        so.append(og[:, :S].transpose(1, 0, 2).reshape(1, 1, n_s, GROUP_W).astype(bf16))
        sl.append(lg[:, :S].transpose(1, 0, 2).reshape(1, 1, n_s, GROUP_W))
        for a in (ko, vo):
            skv.append(a.reshape(Bd, HEADS_PER_GROUP, HEAD_DIM, w).transpose(0, 3, 1, 2))
    y_sample = _mix_and_moe(xs[None], so, sl, pos[None], sgs[None], wts, tm=n_s, blk=128)[0]
    y_sample = y_sample.reshape(S, Bd, D).transpose(1, 0, 2)
    s_pool = st.transpose(1, 0, 2)

    return (y_prompt, y_sample, *pkv, p_pool, *skv, s_pool)
```

```python
import functools

import jax
import jax.numpy as jnp
from jax import lax
from jax.experimental import pallas as pl
from jax.experimental.pallas import tpu as pltpu

D_MODEL = 1024
HEAD_DIM = 64
HEADS_PER_GROUP = 4
WINDOWS = (128, 512, 2048)
DILATIONS = (1, 4, 16)
N_GROUPS = len(WINDOWS)
N_HEADS = HEADS_PER_GROUP * N_GROUPS
ATTN_W = N_HEADS * HEAD_DIM
GROUP_W = HEADS_PER_GROUP * HEAD_DIM
BAND = 128
POOL_WINDOWS = (2, 4, 8, 16)
POOL_GW = 128
POOL_W = len(POOL_WINDOWS) * POOL_GW
POOL_STATE = max(POOL_WINDOWS) - 1
POOL_HIST = 32
assert POOL_WINDOWS == (2, 4, 8, 16)
N_EXPERT_GROUPS = 4
EXPERTS_PER_GROUP = 8
N_EXPERTS = N_EXPERT_GROUPS * EXPERTS_PER_GROUP
D_EXPERT = 512
QKVU_W = 3 * ATTN_W + POOL_W
IN_W = QKVU_W + 2 * D_MODEL
NEG = -1e30
EPS = 1e-6
LANES = 128
ROUTER_W = LANES
EXPERT_COL0 = 8
VMEM_LIMIT = 56 * 1024 * 1024
IBUF_SLOTS = 2 * (N_GROUPS - 1) * (GROUP_W // LANES)
SEG_ALIGN = 8

assert all(w // d == BAND for w, d in zip(WINDOWS, DILATIONS))

f32 = jnp.float32
bf16 = jnp.bfloat16


def _cparams(sem):
    return pltpu.CompilerParams(dimension_semantics=sem, vmem_limit_bytes=VMEM_LIMIT)


def _pack_rows(v):
    k = v.shape[1] // 2
    hi = lax.bitcast_convert_type(v[:, :k], jnp.uint32)
    lo = lax.bitcast_convert_type(v[:, k:], jnp.uint32)
    return hi | lax.shift_right_logical(lo, jnp.uint32(16))


def _unpack_rows(w):
    hi = lax.bitcast_convert_type(w & jnp.uint32(0xFFFF0000), f32)
    lo = lax.bitcast_convert_type(lax.shift_left(w, jnp.uint32(16)), f32)
    return jnp.concatenate([hi, lo], axis=-1).astype(bf16)


def _rmsnorm_rows(x, g):
    ms = jnp.mean(x * x, axis=-1, keepdims=True)
    return x * lax.rsqrt(ms + EPS) * g


def _head_rmsnorm_chunk(ch, gain):
    lane = lax.broadcasted_iota(jnp.int32, ch.shape, 1)
    lo_mask = lane < HEAD_DIM
    sq = ch * ch
    lo = jnp.sum(jnp.where(lo_mask, sq, 0.0), axis=-1, keepdims=True)
    hi = jnp.sum(jnp.where(lo_mask, 0.0, sq), axis=-1, keepdims=True)
    ss = jnp.where(lo_mask, lo, hi)
    return ch * lax.rsqrt(ss * (1.0 / HEAD_DIM) + EPS) * gain


def _project(x, ln1, w_ref, qg, kg):
    xn = _rmsnorm_rows(x, ln1).astype(bf16)
    nch = ATTN_W // LANES
    qs = qg * (HEAD_DIM ** -0.5)
    zq = jnp.dot(xn, w_ref[:, 0:ATTN_W], preferred_element_type=f32)
    q = [_head_rmsnorm_chunk(zq[:, c * LANES:(c + 1) * LANES], qs[:, c * LANES:(c + 1) * LANES]) for c in range(nch)]
    zk = jnp.dot(xn, w_ref[:, ATTN_W:2 * ATTN_W], preferred_element_type=f32)
    k = [_head_rmsnorm_chunk(zk[:, c * LANES:(c + 1) * LANES], kg[:, c * LANES:(c + 1) * LANES]) for c in range(nch)]
    zv = jnp.dot(xn, w_ref[:, 2 * ATTN_W:3 * ATTN_W], preferred_element_type=f32)
    v = [zv[:, c * LANES:(c + 1) * LANES] for c in range(nch)]
    u = jnp.dot(xn, w_ref[:, 3 * ATTN_W:3 * ATTN_W + POOL_W], preferred_element_type=f32)
    sg = jax.nn.sigmoid(jnp.dot(xn, w_ref[:, QKVU_W:QKVU_W + 2 * D_MODEL], preferred_element_type=f32)).astype(bf16)
    return q, k, v, u, sg


def _proj_prompt_kernel(x_ref, ln1_ref, w_ref, qg_ref, kg_ref, plin_ref, pscale_ref, *rest,
                        tm, n_tiles, tail_tiles, hosted):
    n_hosted_in, n_hosted_out = (7, 4) if hosted else (0, 0)
    hosted_in, rest = rest[:n_hosted_in], rest[n_hosted_in:]
    qkv_refs = rest[:3 * N_GROUPS]
    po_ref, kt_ref, vt_ref, ut_ref, sg_ref = rest[3 * N_GROUPS:3 * N_GROUPS + 5]
    rest = rest[3 * N_GROUPS + 5:]
    hosted_out, (ubuf, pa, pb, sbuf) = rest[:n_hosted_out], rest[n_hosted_out:]
    i = pl.program_id(1)
    hist = POOL_STATE + 1
    ph, rows = POOL_HIST, POOL_HIST + tm

    @pl.when(i == 0)
    def _():
        ubuf[0:ph, :] = jnp.zeros((ph, POOL_W), f32)

    @pl.when(i > 0)
    def _():
        ubuf[0:ph, :] = ubuf[tm:tm + ph, :]

    stages = _cached_attn_stages(*hosted_in, *hosted_out, **hosted) if hosted else ()
    for s in stages:
        s()
    q, k, v, u, sg = _project(x_ref[0], ln1_ref[...], w_ref, qg_ref[...], kg_ref[...])
    sg_ref[0] = sg
    cpg = GROUP_W // LANES
    slot = 0
    for t, chunks in enumerate((q, k, v)):
        for g, dil in enumerate(DILATIONS):
            out_ref = qkv_refs[3 * g + t]
            for c in range(cpg):
                val = chunks[g * cpg + c]
                cols = slice(c * LANES, (c + 1) * LANES)
                if dil == 1:
                    out_ref[0, 0, :, cols] = val.astype(bf16)
                else:
                    sbuf[slot * tm:(slot + 1) * tm, :] = val
                    for r in range(dil):
                        out_ref[0, r, :, cols] = sbuf[pl.ds(slot * tm + r, tm // dil, stride=dil), :].astype(bf16)
                    slot += 1

    ubuf[ph:rows, :] = u
    gw = POOL_GW
    pa[8:rows, :] = ubuf[8:rows, :] + ubuf[7:rows - 1, :]
    pb[16:rows, gw:] = pa[16:rows, gw:] + pa[14:rows - 2, gw:]
    pa[24:rows, 2 * gw:] = pb[24:rows, 2 * gw:] + pb[20:rows - 4, 2 * gw:]
    pb[32:rows, 3 * gw:] = pa[32:rows, 3 * gw:] + pa[24:rows - 8, 3 * gw:]
    pos = i * tm + lax.broadcasted_iota(jnp.int32, (tm, POOL_GW), 0)
    zs = []
    for g, w in enumerate(POOL_WINDOWS):
        cols = slice(g * POOL_GW, (g + 1) * POOL_GW)
        wsum = (pa if g % 2 == 0 else pb)[ph:rows, cols]
        cnt = jnp.minimum(pos + 1, w).astype(f32)
        zs.append(wsum / cnt - u[:, cols])
    z = jnp.concatenate(zs, axis=-1).astype(bf16)
    po = jnp.dot(z, plin_ref[...], preferred_element_type=f32) * pscale_ref[...]
    po_ref[0] = po.astype(bf16)

    @pl.when(i >= n_tiles - tail_tiles)
    def _():
        kt_ref[0] = jnp.concatenate(k, axis=-1).T
        vt_ref[0] = jnp.concatenate(v, axis=-1).T

    @pl.when(i == n_tiles - 1)
    def _():
        ut_ref[0] = ubuf[rows - hist:rows, :]


def _proj_prompt(x, ln1, w_qkvu, qg, kg, plin_bd, pscale, *, tm, hosted=None):
    B, T, D = x.shape
    n_tiles = T // tm
    tail = max(WINDOWS)
    assert T % tm == 0 and tail % tm == 0 and T >= tail
    tail_tiles = tail // tm
    hist = POOL_STATE + 1
    kern = functools.partial(_proj_prompt_kernel, tm=tm, n_tiles=n_tiles, tail_tiles=tail_tiles,
                             hosted=hosted['params'] if hosted else None)
    h_ops, h_in, h_out, h_shape = ((), [], [], []) if not hosted else (
        hosted['operands'], hosted['in_specs'], hosted['out_specs'], hosted['out_shape'])
    const = lambda b, i: (0, 0)
    assert all(tm % (16 * d) == 0 for d in DILATIONS)
    qkv_specs = [pl.BlockSpec((1, d, tm // d, GROUP_W), lambda b, i: (b, 0, i, 0)) for d in DILATIONS for _ in range(3)]
    qkv_shapes = [jax.ShapeDtypeStruct((B, d, T // d, GROUP_W), bf16) for d in DILATIONS for _ in range(3)]
    tail_spec = pl.BlockSpec((1, ATTN_W, tm), lambda b, i: (b, 0, jnp.maximum(i - (n_tiles - tail_tiles), 0)))
    return pl.pallas_call(
        kern,
        grid=(B, n_tiles),
        in_specs=[
            pl.BlockSpec((1, tm, D), lambda b, i: (b, i, 0)),
            pl.BlockSpec((1, D), const),
            pl.BlockSpec((D, IN_W), const),
            pl.BlockSpec((1, ATTN_W), const),
            pl.BlockSpec((1, ATTN_W), const),
            pl.BlockSpec((POOL_W, POOL_W), const),
            pl.BlockSpec((1, POOL_W), const),
        ] + h_in,
        out_specs=qkv_specs
        + [pl.BlockSpec((1, tm, POOL_W), lambda b, i: (b, i, 0)),
           tail_spec, tail_spec,
           pl.BlockSpec((1, hist, POOL_W), lambda b, i: (b, 0, 0)),
           pl.BlockSpec((1, tm, 2 * D), lambda b, i: (b, i, 0))] + h_out,
        out_shape=qkv_shapes
        + [jax.ShapeDtypeStruct((B, T, POOL_W), bf16),
           jax.ShapeDtypeStruct((B, ATTN_W, tail), f32),
           jax.ShapeDtypeStruct((B, ATTN_W, tail), f32),
           jax.ShapeDtypeStruct((B, hist, POOL_W), f32),
           jax.ShapeDtypeStruct((B, T, 2 * D), bf16)] + h_shape,
        scratch_shapes=[pltpu.VMEM((POOL_HIST + tm, POOL_W), f32)] * 3
                       + [pltpu.VMEM((3 * (N_GROUPS - 1) * (GROUP_W // LANES) * tm, LANES), f32)],
        compiler_params=_cparams(("arbitrary", "arbitrary")),
        name="proj_prompt",
    )(x, ln1, w_qkvu, qg, kg, plin_bd, pscale, *h_ops)


def _proj_sample_kernel(x_ref, ln1_ref, w_ref, qg_ref, kg_ref, plin_ref, pscale_ref, state_ref,
                        q_ref, k_ref, v_ref, po_ref, st_ref, sg_ref, *, n_seq, n_new, past_len):
    q, k, v, u, sg = _project(x_ref[...], ln1_ref[...], w_ref, qg_ref[...], kg_ref[...])
    sg_ref[...] = sg
    q_ref[...] = jnp.concatenate(q, axis=-1)
    k_ref[...] = jnp.concatenate(k, axis=-1)
    v_ref[...] = jnp.concatenate(v, axis=-1)
    ext = [state_ref[j] for j in range(POOL_STATE)] + [u[s * n_seq:(s + 1) * n_seq, :] for s in range(n_new)]
    for s in range(n_new):
        zs = []
        for g, w in enumerate(POOL_WINDOWS):
            cols = slice(g * POOL_GW, (g + 1) * POOL_GW)
            cur = ext[POOL_STATE + s][:, cols]
            acc = cur
            for j in range(1, w):
                acc = acc + ext[POOL_STATE + s - j][:, cols]
            cnt = float(min(past_len + s + 1, w))
            zs.append(acc / cnt - cur)
        z = jnp.concatenate(zs, axis=-1).astype(bf16)
        po = jnp.dot(z, plin_ref[...], preferred_element_type=f32) * pscale_ref[...]
        po_ref[s * n_seq:(s + 1) * n_seq, :] = po.astype(bf16)
    for j in range(POOL_STATE):
        st_ref[j] = ext[j + n_new]


def _proj_sample(x, ln1, w_qkvu, qg, kg, plin_bd, pscale, state, *, n_seq, n_new, past_len):
    n = n_seq * n_new
    kern = functools.partial(_proj_sample_kernel, n_seq=n_seq, n_new=n_new, past_len=past_len)
    return pl.pallas_call(
        kern,
        out_shape=[jax.ShapeDtypeStruct((n, ATTN_W), f32)] * 3
        + [jax.ShapeDtypeStruct((n, POOL_W), bf16),
           jax.ShapeDtypeStruct((POOL_STATE, n_seq, POOL_W), f32),
           jax.ShapeDtypeStruct((n, 2 * D_MODEL), bf16)],
        compiler_params=pltpu.CompilerParams(vmem_limit_bytes=VMEM_LIMIT),
        name="proj_sample",
    )(x, ln1, w_qkvu, qg, kg, plin_bd, pscale, state)


def _head_masks(shape):
    lane = lax.broadcasted_iota(jnp.int32, shape, len(shape) - 1)
    return [(lane >= h * HEAD_DIM) & (lane < (h + 1) * HEAD_DIM) for h in range(HEADS_PER_GROUP)]


def _band_attn_kernel(q_ref, kc_ref, kp_ref, vc_ref, vp_ref, bias_ref, *rest, tl, unroll, cast):
    if cast:
        w_ref, o_ref, l_ref, wb_ref, kbuf, vbuf = rest
        wb_ref[...] = w_ref[...].astype(bf16)
    else:
        o_ref, l_ref, kbuf, vbuf = rest
    i = pl.program_id(2)
    kbuf[0:BAND, :] = kp_ref[0, 0]
    kbuf[BAND:2 * BAND, :] = kc_ref[0, 0, 0:BAND, :]
    vbuf[0:BAND, :] = vp_ref[0, 0]
    vbuf[BAND:2 * BAND, :] = vc_ref[0, 0, 0:BAND, :]
    masks = _head_masks((BAND, GROUP_W))

    def sub_block(j, kk, vv, var):
        r0 = j * BAND if isinstance(j, int) else pl.multiple_of(j * BAND, BAND)
        q = q_ref[0, 0, pl.ds(r0, BAND), :]
        qm = jnp.concatenate([jnp.where(m, q, jnp.zeros_like(q)) for m in masks], axis=0)
        s = lax.dot_general(qm, kk, (((1,), (1,)), ((), ())), preferred_element_type=f32)
        s = s + bias_ref[var]
        m = jnp.max(s, axis=-1, keepdims=True)
        p = jnp.exp(s - m)
        den = jnp.sum(p, axis=-1, keepdims=True)
        pv = jnp.dot(p.astype(bf16), vv, preferred_element_type=f32)
        o = jnp.zeros((BAND, GROUP_W), f32)
        ms = jnp.zeros((BAND, GROUP_W), f32)
        ds = jnp.ones((BAND, GROUP_W), f32)
        for h, msk in enumerate(masks):
            rows = slice(h * BAND, (h + 1) * BAND)
            o = jnp.where(msk, pv[rows], o)
            ms = jnp.where(msk, m[rows], ms)
            ds = jnp.where(msk, den[rows], ds)
        o_ref[0, 0, pl.ds(r0, BAND), :] = (o / ds).astype(bf16)
        l_ref[0, 0, pl.ds(r0, BAND), :] = ms + jnp.log(ds)

    sub_block(0, kbuf[...], vbuf[...], jnp.where(i == 0, 0, 1))

    def body(j, carry):
        k0 = pl.multiple_of((j - 1) * BAND, BAND)
        sub_block(j, kc_ref[0, 0, pl.ds(k0, 2 * BAND), :], vc_ref[0, 0, pl.ds(k0, 2 * BAND), :], 1)
        return carry

    if tl > BAND:
        lax.fori_loop(1, tl // BAND, body, 0, unroll=unroll)


def _band_bias(slopes_g, dil):
    qi = jnp.arange(BAND)[:, None]
    kb = jnp.arange(2 * BAND)[None, :]
    rel = qi + BAND - kb
    valid = (rel >= 0) & (rel <= BAND)
    alibi = -slopes_g[:, None, None] * (dil * rel)[None].astype(f32)
    variants = []
    for first in (True, False):
        ok = valid & (kb >= BAND) if first else valid
        variants.append(jnp.where(ok[None], alibi, NEG).reshape(HEADS_PER_GROUP * BAND, 2 * BAND))
    return jnp.stack(variants, axis=0)


def _band_attention(q, k, v, g, slopes_g, *, tl_max=1024, unroll=7, cast_w=None):
    B, dil, L, _ = q.shape
    tl = min(tl_max, L)
    assert dil == DILATIONS[g] and L % tl == 0 and tl % BAND == 0
    nsub = tl // BAND
    nl = L // tl
    bias = _band_bias(slopes_g, dil)
    cur = pl.BlockSpec((1, 1, tl, GROUP_W), lambda b, r, i: (b, r, i, 0))
    prev = pl.BlockSpec((1, 1, BAND, GROUP_W), lambda b, r, i: (b, r, jnp.maximum(i * nsub - 1, 0), 0))
    host_cast = cast_w is not None and cast_w.shape[0] == B * dil * nl
    w_spec, w_ops, w_shape = [], (), []
    if host_cast:
        w_spec = [pl.BlockSpec((1,) + cast_w.shape[1:], lambda b, r, i: ((b * dil + r) * nl + i, 0, 0))]
        w_ops, w_shape = (cast_w,), [jax.ShapeDtypeStruct(cast_w.shape, bf16)]
    res = pl.pallas_call(
        functools.partial(_band_attn_kernel, tl=tl, unroll=max(1, min(unroll, nsub - 1)), cast=host_cast),
        grid=(B, dil, nl),
        in_specs=[cur, cur, prev, cur, prev,
                  pl.BlockSpec((2, HEADS_PER_GROUP * BAND, 2 * BAND), lambda b, r, i: (0, 0, 0))] + w_spec,
        out_specs=[cur, cur] + w_spec,
        out_shape=[jax.ShapeDtypeStruct((B, dil, L, GROUP_W), bf16),
                   jax.ShapeDtypeStruct((B, dil, L, GROUP_W), f32)] + w_shape,
        scratch_shapes=[pltpu.VMEM((2 * BAND, GROUP_W), bf16), pltpu.VMEM((2 * BAND, GROUP_W), bf16)],
        compiler_params=_cparams(("arbitrary", "arbitrary", "arbitrary")),
        name="band_attn_g%d" % g,
    )(q, k, k, v, v, bias, *w_ops)
    if cast_w is None:
        return res
    return res[0], res[1], (res[2] if host_cast else cast_w.astype(bf16))


def _cached_attn_kernel(*refs, **params):
    for stage in _cached_attn_stages(*refs, **params):
        stage()


def _cached_attn_stages(q_ref, kn_ref, vn_ref, kc_ref, vc_ref, bc_ref, bn_ref,
                        o_ref, l_ref, ko_ref, vo_ref, *, nb, n_new, win):
    def roll(c_ref, new_ref, out_ref):
        lane_t = lax.broadcasted_iota(jnp.int32, (nb, GROUP_W, LANES), 2)
        rolled = pltpu.roll(c_ref[...], win - n_new, axis=2)
        out_ref[...] = rolled
        new_t = jnp.swapaxes(jnp.concatenate([new_ref[...], jnp.zeros((nb, LANES - 8, GROUP_W), f32)], axis=1), 1, 2)
        new_t = pltpu.roll(new_t, LANES - n_new, axis=2)
        out_ref[:, :, win - LANES:win] = jnp.where(lane_t >= LANES - n_new, new_t, rolled[:, :, win - LANES:win])

    return (functools.partial(_cached_attn_scores, q_ref, kn_ref, vn_ref, kc_ref, vc_ref, bc_ref, bn_ref, o_ref, l_ref,
                              nb=nb, n_new=n_new),
            functools.partial(roll, kc_ref, kn_ref, ko_ref),
            functools.partial(roll, vc_ref, vn_ref, vo_ref))


def _cached_attn_scores(q_ref, kn_ref, vn_ref, kc_ref, vc_ref, bc_ref, bn_ref, o_ref, l_ref, *, nb, n_new):
    masks8 = _head_masks((nb, 8, GROUP_W))
    q8, kn8, vn8 = q_ref[...], kn_ref[...], vn_ref[...]
    kc, vc = kc_ref[...], vc_ref[...]
    qm = jnp.concatenate([jnp.where(m, q8, 0.0) for m in masks8], axis=1)
    sc = jnp.einsum('bqd,bdk->bqk', qm.astype(bf16), kc.astype(bf16), preferred_element_type=f32) + bc_ref[...]
    m = jnp.max(sc, axis=-1, keepdims=True)
    sn = []
    for t in range(n_new):
        col = jnp.sum(qm * kn8[:, t:t + 1, :], axis=-1, keepdims=True) + bn_ref[:, t:t + 1]
        sn.append(col)
        m = jnp.maximum(m, col)
    pc = jnp.exp(sc - m)
    den = jnp.sum(pc, axis=-1, keepdims=True)
    acc = jnp.einsum('bqk,bdk->bqd', pc.astype(bf16), vc.astype(bf16), preferred_element_type=f32)
    for t in range(n_new):
        pn = jnp.exp(sn[t] - m)
        den = den + pn
        acc = acc + pn * vn8[:, t:t + 1, :]
    acc = acc / den
    lse = m + jnp.log(den)
    o = jnp.zeros((nb, 8, GROUP_W), f32)
    l = jnp.zeros((nb, 8, GROUP_W), f32)
    for h, msk in enumerate(masks8):
        o = jnp.where(msk, acc[:, h * 8:(h + 1) * 8, :], o)
        l = jnp.where(msk, lse[:, h * 8:(h + 1) * 8, :], l)
    o_ref[...] = o
    l_ref[...] = l


def _cached_bias(slopes_g, dil, win, n_new):
    s = jnp.arange(8)[:, None]
    i = jnp.arange(win)[None, :]
    dist = win + s - i
    ok = (dist % dil == 0) & (dist // dil <= BAND) & (s < n_new)
    bc = jnp.where(ok[None], -slopes_g[:, None, None] * dist[None].astype(f32), NEG)
    t = jnp.arange(8)[None, :]
    dn = s - t
    okn = (dn >= 0) & (dn % dil == 0) & (dn // dil <= BAND) & (s < n_new) & (t < n_new)
    bn = jnp.where(okn[None], -slopes_g[:, None, None] * dn[None].astype(f32), NEG)
    pad = (s >= n_new)
    bc = jnp.where(pad[None], 0.0, bc)
    bn = jnp.where(pad[None], 0.0, bn)
    return bc.reshape(HEADS_PER_GROUP * 8, win), bn.reshape(HEADS_PER_GROUP * 8, 8)


def _cached_call_parts(q8, kn8, vn8, kc_t, vc_t, g, slopes_g, *, n_new, nb, step_of):
    Bd, _, win = kc_t.shape
    assert win == WINDOWS[g] and win % LANES == 0 and Bd % nb == 0
    bc, bn = _cached_bias(slopes_g, DILATIONS[g], win, n_new)
    small = pl.BlockSpec((nb, 8, GROUP_W), lambda *idx: (step_of(*idx), 0, 0))
    cache = pl.BlockSpec((nb, GROUP_W, win), lambda *idx: (step_of(*idx), 0, 0))
    const = lambda a: pl.BlockSpec(a.shape, lambda *idx: (0, 0))
    return dict(
        operands=(q8, kn8, vn8, kc_t, vc_t, bc, bn),
        in_specs=[small, small, small, cache, cache, const(bc), const(bn)],
        out_specs=[small, small, cache, cache],
        out_shape=[jax.ShapeDtypeStruct((Bd, 8, GROUP_W), f32)] * 2 + [jax.ShapeDtypeStruct((Bd, GROUP_W, win), f32)] * 2,
        params=dict(nb=nb, n_new=n_new, win=win))


def _cached_attention(q8, kn8, vn8, kc_t, vc_t, g, slopes_g, *, n_new):
    Bd, _, win = kc_t.shape
    nb = max(1, min(Bd, 2048 // win))
    parts = _cached_call_parts(q8, kn8, vn8, kc_t, vc_t, g, slopes_g, n_new=n_new, nb=nb, step_of=lambda b: b)
    return pl.pallas_call(
        functools.partial(_cached_attn_kernel, **parts['params']),
        grid=(Bd // nb,),
        in_specs=parts['in_specs'],
        out_specs=parts['out_specs'],
        out_shape=parts['out_shape'],
        compiler_params=_cparams(("arbitrary",)),
        name="cached_attn_g%d" % g,
    )(*parts['operands'])


def _split_hosted(rest, n_own_out, hosted):
    n_in, n_out = (7, 4) if hosted else (0, 0)
    hosted_in, rest = rest[:n_in], rest[n_in:]
    own_out, rest = rest[:n_own_out], rest[n_own_out:]
    return hosted_in, own_out, rest[:n_out], rest[n_out:]


def _merge_kernel(x_ref, o0_ref, o1_ref, o2_ref, l0_ref, l1_ref, l2_ref, po_ref, sg_ref,
                  wpa_ref, wpb_ref, wo_ref, ln2_ref, wr_ref, br_ref, tri_ref, ltri_ref,
                  *rest, tm, ts, dils, hosted):
    hosted_in, (h_ref, xl_ref, wc_ref, tc_ref), hosted_out, (ibuf,) = _split_hosted(rest, 4, hosted)
    if hosted:
        _cached_attn_kernel(*hosted_in, *hosted_out, **hosted)
    slots = iter(range(IBUF_SLOTS))

    def token_order(ref, dil):
        if dil == 1:
            return ref[0, 0].astype(f32)
        chunks = []
        for c in range(GROUP_W // LANES):
            base = next(slots) * tm
            for r in range(dil):
                ibuf[pl.ds(base + r, tm // dil, stride=dil), :] = ref[0, r, :, c * LANES:(c + 1) * LANES].astype(f32)
            chunks.append(ibuf[base:base + tm, :])
        return jnp.concatenate(chunks, axis=-1)

    x = x_ref[0]
    l0, l1, l2 = (token_order(r, d) for r, d in zip((l0_ref, l1_ref, l2_ref), dils))
    lm = jnp.maximum(jnp.maximum(l0, l1), l2)
    e0, e1, e2 = jnp.exp(l0 - lm), jnp.exp(l1 - lm), jnp.exp(l2 - lm)
    o0, o1, o2 = (token_order(r, d) for r, d in zip((o0_ref, o1_ref, o2_ref), dils))
    attn = (e0 * o0 + e1 * o1 + e2 * o2) / (e0 + e1 + e2)
    ma = jnp.dot(attn.astype(bf16), wpa_ref[...], preferred_element_type=f32)
    mb = jnp.dot(po_ref[0], wpb_ref[...], preferred_element_type=f32)
    mix = sg_ref[0, :, :D_MODEL].astype(f32) * ma + sg_ref[0, :, D_MODEL:].astype(f32) * mb
    h = x + jnp.dot(mix.astype(bf16), wo_ref[...], preferred_element_type=f32)
    h_ref[...] = h
    xn2 = _rmsnorm_rows(h, ln2_ref[...]).astype(bf16)

    lt = (jnp.dot(xn2, wr_ref[...], preferred_element_type=f32) + br_ref[...]).T
    row8 = lax.broadcasted_iota(jnp.int32, (8, tm), 0)
    gl = jnp.where(row8 < N_EXPERT_GROUPS, lt[0:8], -jnp.inf)
    gmax = jnp.max(gl, axis=0, keepdims=True)
    gidx = jnp.min(jnp.where(gl == gmax, row8, 8), axis=0, keepdims=True)
    pg = 1.0 / jnp.sum(jnp.exp(gl - gmax), axis=0, keepdims=True)
    sel = jnp.zeros((8, tm), f32)
    for g in range(N_EXPERT_GROUPS):
        lo = EXPERT_COL0 + g * EXPERTS_PER_GROUP
        sel = jnp.where(gidx == g, lt[lo:lo + EXPERTS_PER_GROUP], sel)
    v0 = jnp.max(sel, axis=0, keepdims=True)
    i0 = jnp.min(jnp.where(sel == v0, row8, 8), axis=0, keepdims=True)
    sel2 = jnp.where(row8 == i0, -jnp.inf, sel)
    v1 = jnp.max(sel2, axis=0, keepdims=True)
    i1 = jnp.min(jnp.where(sel2 == v1, row8, 8), axis=0, keepdims=True)
    t = jnp.exp(v1 - v0)
    w0 = pg / (1.0 + t)
    w1 = pg * t / (1.0 + t)
    eid0 = gidx * EXPERTS_PER_GROUP + i0
    eid1 = gidx * EXPERTS_PER_GROUP + i1
    erow = lax.broadcasted_iota(jnp.int32, (N_EXPERTS, tm), 0)
    oh0 = erow == eid0
    oh1 = erow == eid1
    cnt = jnp.where(oh0, 1.0, jnp.where(oh1, 1.0, 0.0))
    before = jnp.dot(cnt.astype(bf16), tri_ref[...], preferred_element_type=f32)
    tcount = jnp.sum(cnt, axis=1, keepdims=True)
    units = jnp.floor((tcount + (SEG_ALIGN - 1)) * (1.0 / SEG_ALIGN))
    ub = jnp.broadcast_to(units, (N_EXPERTS, LANES)).astype(bf16)
    seg0 = SEG_ALIGN * jnp.dot(ltri_ref[...], ub, preferred_element_type=f32)[:, 0:1]
    pos_e = seg0 + before
    lpos0 = jnp.sum(jnp.where(oh0, pos_e, 0.0), axis=0, keepdims=True)
    lpos1 = jnp.sum(jnp.where(oh1, pos_e, 0.0), axis=0, keepdims=True)
    prow = lax.broadcasted_iota(jnp.int32, (ts, tm), 0)
    perm = jnp.where(prow == lpos0.astype(jnp.int32), 1.0, jnp.where(prow == lpos1.astype(jnp.int32), 1.0, 0.0))
    xl_ref[...] = _pack_rows(jnp.dot(perm.astype(bf16), xn2, preferred_element_type=f32))
    tc_ref[...] = jnp.broadcast_to(tcount, (N_EXPERTS, LANES))
    rowl = lax.broadcasted_iota(jnp.int32, (LANES, tm), 0)
    wslab = jnp.zeros((LANES, tm), f32)
    for r, val in enumerate((w0, w1, lpos0, lpos1)):
        wslab = jnp.where(rowl == r, val, wslab)
    wc_ref[...] = wslab.T


def _hosted_parts(hosted):
    if not hosted:
        return (), [], [], [], None
    return hosted['operands'], hosted['in_specs'], hosted['out_specs'], hosted['out_shape'], hosted['params']


def _merge(x, o, l, po, sg, w_pa, w_pb, w_o, ln2, w_router, b_router, *, tm, hosted=None):
    B, T, _ = x.shape
    h_ops, h_in, h_out, h_shape, h_params = _hosted_parts(hosted)
    assert T % tm == 0
    nt = T // tm
    n = B * T
    dils = tuple(a.shape[1] for a in o)
    assert all(tm % (8 * d) == 0 for d in dils)
    assert 2 * tm // SEG_ALIGN <= 256
    ts = _sorted_tile_rows(tm)
    tri = (jnp.arange(tm)[:, None] < jnp.arange(tm)[None, :]).astype(bf16)
    ltri = (jnp.arange(N_EXPERTS)[None, :] < jnp.arange(N_EXPERTS)[:, None]).astype(bf16)
    rows3 = lambda w: pl.BlockSpec((1, tm, w), lambda b, i: (b, i, 0))
    flat = lambda r, w: pl.BlockSpec((r, w), lambda b, i: (b * nt + i, 0))
    grp = [pl.BlockSpec((1, d, tm // d, GROUP_W), lambda b, i: (b, 0, i, 0)) for d in dils]
    full = lambda a: pl.BlockSpec(a.shape, lambda b, i: (0,) * a.ndim)
    weights = (w_pa, w_pb, w_o, ln2, w_router, b_router, tri, ltri)
    return pl.pallas_call(
        functools.partial(_merge_kernel, tm=tm, ts=ts, dils=dils, hosted=h_params),
        grid=(B, nt),
        in_specs=[rows3(D_MODEL)] + grp + grp + [rows3(POOL_W), rows3(2 * D_MODEL)]
        + [full(a) for a in weights] + h_in,
        out_specs=[flat(tm, D_MODEL), flat(ts, D_MODEL // 2), flat(tm, LANES),
                   pl.BlockSpec((N_EXPERTS, LANES), lambda b, i: (0, b * nt + i))] + h_out,
        out_shape=[jax.ShapeDtypeStruct((n, D_MODEL), f32),
                   jax.ShapeDtypeStruct((B * nt * ts, D_MODEL // 2), jnp.uint32),
                   jax.ShapeDtypeStruct((n, LANES), f32),
                   jax.ShapeDtypeStruct((N_EXPERTS, B * nt * LANES), f32)] + h_shape,
        scratch_shapes=[pltpu.VMEM((IBUF_SLOTS * tm, LANES), f32)],
        compiler_params=_cparams(("arbitrary", "arbitrary")),
        name="merge_router",
    )(x, *o, *l, po, sg, *weights, *h_ops)


def _sorted_tile_rows(tm):
    return -(-(2 * tm + N_EXPERTS * (SEG_ALIGN - 1)) // LANES) * LANES


def _moe_seg_kernel(blk_e_ref, blk_r0_ref, blk_n_ref, seg_g_ref, seg_c_ref, seg_src_ref, used_ref,
                    xl_hbm, wg_ref, wu_ref, wd_ref, yl_hbm,
                    xbuf, ybuf, sem_in, sem_out, sem_zero, ptr, *, blk, ts, n_tiles, n_blocks):
    b = pl.program_id(0)
    slot = lax.rem(b, 2)
    unit_bits = (blk // SEG_ALIGN).bit_length()

    def row_pieces(rows):
        units = lax.shift_right_logical(rows, jnp.int32(SEG_ALIGN.bit_length() - 1))
        off = jnp.int32(0)
        for bit in reversed(range(unit_bits)):
            on = lax.shift_right_logical(units, jnp.int32(bit)) & 1
            yield on == 1, off, SEG_ALIGN << bit
            off = off + on * (SEG_ALIGN << bit)

    def copy_rows(src, src_row, dst, dst_row, rows, sem):
        def start(off, size):
            pltpu.make_async_copy(src.at[pl.ds(pl.multiple_of(src_row + off, SEG_ALIGN), size)],
                                  dst.at[pl.ds(pl.multiple_of(dst_row + off, SEG_ALIGN), size)], sem).start()

        chunk = SEG_ALIGN << 3
        n_chunks = lax.shift_right_logical(rows, jnp.int32(chunk.bit_length() - 1))

        def whole_chunk(j, c):
            start(j * chunk, chunk)
            return c

        lax.fori_loop(0, n_chunks, whole_chunk, 0)
        off = n_chunks * chunk
        rest = rows - off
        for bit in (2, 1, 0):
            size = SEG_ALIGN << bit
            on = (rest & size) != 0

            @pl.when(on)
            def _():
                start(off, size)

            off = off + jnp.where(on, size, 0)

    def wait_rows(src, dst, rows, sem):
        for on, _, size in row_pieces(rows):
            @pl.when(on)
            def _():
                pltpu.make_async_copy(src.at[pl.ds(0, size)], dst.at[pl.ds(0, size)], sem).wait()

    def for_pieces(bb, stream, fn):
        e, r0, n = blk_e_ref[bb], blk_r0_ref[bb], blk_n_ref[bb]

        @pl.when(n > 0)
        def _():
            def seg(i):
                return jnp.minimum(i, n_tiles - 1) * N_EXPERTS + e

            def cond(i):
                return (i < n_tiles) & (seg_g_ref[seg(i)] < r0 + n)

            def body(i):
                g = seg_g_ref[seg(i)]
                lo = jnp.maximum(g, r0)
                hi = jnp.minimum(g + seg_c_ref[seg(i)], r0 + n)

                @pl.when(hi > lo)
                def _():
                    fn(seg_src_ref[seg(i)] + (lo - g), lo - r0, hi - lo)

                return i + 1

            end = lax.while_loop(cond, body, jnp.where(r0 == 0, 0, ptr[stream]))
            ptr[stream] = jnp.maximum(end - 1, 0)

    def gather(bb):
        s = lax.rem(bb, 2)
        for_pieces(bb, 0, lambda lrow, brow, rows: copy_rows(xl_hbm, lrow, xbuf.at[s], brow, rows, sem_in.at[s]))

    def scatter(bb):
        s = lax.rem(bb, 2)
        for_pieces(bb, 1, lambda lrow, brow, rows: copy_rows(ybuf.at[s], brow, yl_hbm, lrow, rows, sem_out.at[s]))

    def zero_tail(i, go):
        row0 = i * ts + used_ref[i]
        rows = ts - used_ref[i]
        whole = lax.shift_right_logical(rows, jnp.int32(blk.bit_length() - 1))
        zsrc = xbuf.at[1]

        def whole_block(j, c):
            cp = pltpu.make_async_copy(zsrc, yl_hbm.at[pl.ds(pl.multiple_of(row0 + j * blk, SEG_ALIGN), blk)], sem_zero)
            cp.start() if go else cp.wait()
            return c

        lax.fori_loop(0, whole, whole_block, 0)
        rest = rows - whole * blk
        if go:
            copy_rows(zsrc, 0, yl_hbm, row0 + whole * blk, rest, sem_zero)
        else:
            wait_rows(zsrc, yl_hbm, rest, sem_zero)

    @pl.when(b == 0)
    def _():
        xbuf[...] = jnp.zeros_like(xbuf)
        ptr[0] = 0
        ptr[1] = 0
        gather(0)
        for go in (True, False):
            def per_tile(i, c, go=go):
                zero_tail(i, go)
                return c

            lax.fori_loop(0, n_tiles, per_tile, 0)

    n_b = blk_n_ref[b]
    wait_rows(xl_hbm, xbuf.at[slot], n_b, sem_in.at[slot])

    @pl.when(b + 1 < n_blocks)
    def _():
        gather(b + 1)

    @pl.when(b >= 2)
    def _():
        wait_rows(ybuf.at[slot], yl_hbm, blk_n_ref[b - 2], sem_out.at[slot])

    @pl.when(n_b > 0)
    def _():
        x = _unpack_rows(xbuf[slot])
        hid = jax.nn.silu(jnp.dot(x, wg_ref[0], preferred_element_type=f32)) * jnp.dot(x, wu_ref[0], preferred_element_type=f32)
        y = jnp.dot(hid.astype(bf16), wd_ref[0], preferred_element_type=f32)
        ybuf[slot] = _pack_rows(y.astype(bf16).astype(f32))
        scatter(b)

    @pl.when(b == n_blocks - 1)
    def _():
        @pl.when(b >= 1)
        def _():
            wait_rows(ybuf.at[1 - slot], yl_hbm, blk_n_ref[b - 1], sem_out.at[1 - slot])

        wait_rows(ybuf.at[slot], yl_hbm, n_b, sem_out.at[slot])


def _moe_segments(xl, tables, w_gate, w_up, w_down, *, blk, ts, n_tiles):
    blk_e = tables[0]
    n_blocks = blk_e.shape[0]
    assert blk & (blk - 1) == 0 and blk % SEG_ALIGN == 0
    wspec = lambda shape: pl.BlockSpec((1,) + shape, lambda b, be, *_: (be[b], 0, 0))
    hbm = pl.BlockSpec(memory_space=pl.ANY)
    return pl.pallas_call(
        functools.partial(_moe_seg_kernel, blk=blk, ts=ts, n_tiles=n_tiles, n_blocks=n_blocks),
        grid_spec=pltpu.PrefetchScalarGridSpec(
            num_scalar_prefetch=len(tables),
            grid=(n_blocks,),
            in_specs=[hbm, wspec((D_MODEL, D_EXPERT)), wspec((D_MODEL, D_EXPERT)), wspec((D_EXPERT, D_MODEL))],
            out_specs=hbm,
            scratch_shapes=[pltpu.VMEM((2, blk, D_MODEL // 2), jnp.uint32), pltpu.VMEM((2, blk, D_MODEL // 2), jnp.uint32),
                            pltpu.SemaphoreType.DMA((2,)), pltpu.SemaphoreType.DMA((2,)), pltpu.SemaphoreType.DMA,
                            pltpu.SMEM((2,), jnp.int32)],
        ),
        out_shape=jax.ShapeDtypeStruct(xl.shape, jnp.uint32),
        compiler_params=_cparams(("arbitrary",)),
        name="moe_experts",
    )(*tables, xl, w_gate, w_up, w_down)


def _unsort_kernel(h_ref, wc_ref, yl_ref, *rest, ts, hosted):
    hosted_in, (y_ref,), hosted_out, _ = _split_hosted(rest, 1, hosted)
    if hosted:
        _cached_attn_kernel(*hosted_in, *hosted_out, **hosted)
    w = wc_ref[...]
    yl = _unpack_rows(yl_ref[...])
    col = lax.broadcasted_iota(jnp.int32, (w.shape[0], ts), 1)
    y = h_ref[...]
    for k in range(2):
        pick = jnp.where(col == w[:, 2 + k:3 + k].astype(jnp.int32), 1.0, 0.0).astype(bf16)
        y = y + w[:, k:k + 1] * jnp.dot(pick, yl, preferred_element_type=f32)
    y_ref[...] = y


def _unsort(h, wc, yl, *, tm, ts, hosted=None):
    n = h.shape[0]
    h_ops, h_in, h_out, h_shape, h_params = _hosted_parts(hosted)
    rows = lambda r, w: pl.BlockSpec((r, w), lambda i: (i, 0))
    return pl.pallas_call(
        functools.partial(_unsort_kernel, ts=ts, hosted=h_params),
        grid=(n // tm,),
        in_specs=[rows(tm, D_MODEL), rows(tm, LANES), rows(ts, D_MODEL // 2)] + h_in,
        out_specs=[rows(tm, D_MODEL)] + h_out,
        out_shape=[jax.ShapeDtypeStruct((n, D_MODEL), f32)] + h_shape,
        compiler_params=_cparams(("arbitrary",)),
        name="moe_unsort",
    )(h, wc, yl, *h_ops)


def _mix_and_moe(x, o, l, po, sg, wts, *, tm, blk, host_merge=None, host_unsort=None):
    n = x.shape[0] * x.shape[1]
    n_tiles = n // tm
    nt = x.shape[1] // tm
    ts = _sorted_tile_rows(tm)
    hosted = host_merge(n_tiles, lambda b, i: b * nt + i) if host_merge else None
    h, xl, wc, tc, *merge_hosted = _merge(x, o, l, po, sg, wts['w_pa'], wts['w_pb'], wts['w_o'], wts['ln2'],
                                          wts['w_router'], wts['b_router'], tm=tm, hosted=hosted)
    c8 = (tc[:, ::LANES].T.astype(jnp.int32) + (SEG_ALIGN - 1)) // SEG_ALIGN * SEG_ALIGN
    seg_src = jnp.arange(n_tiles, dtype=jnp.int32)[:, None] * ts + jnp.cumsum(c8, axis=1) - c8
    seg_g = jnp.cumsum(c8, axis=0) - c8
    tot = jnp.sum(c8, axis=0)
    padded = (tot + blk - 1) // blk * blk
    pad_ends = jnp.cumsum(padded)
    n_blocks = -(-(2 * n + n_tiles * N_EXPERTS * (SEG_ALIGN - 1) + N_EXPERTS * (blk - 1)) // blk)
    blk_start = jnp.arange(n_blocks, dtype=jnp.int32) * blk
    blk_e = jnp.minimum(jnp.sum(pad_ends[None, :] <= blk_start[:, None], axis=1), N_EXPERTS - 1).astype(jnp.int32)
    pick = blk_e[:, None] == jnp.arange(N_EXPERTS, dtype=jnp.int32)[None, :]
    blk_r0 = blk_start - jnp.sum(jnp.where(pick, (pad_ends - padded)[None, :], 0), axis=1)
    blk_n = jnp.clip(jnp.sum(jnp.where(pick, tot[None, :], 0), axis=1) - blk_r0, 0, blk)
    used = jnp.sum(c8, axis=1)
    tables = tuple(a.astype(jnp.int32).reshape(-1) for a in (blk_e, blk_r0, blk_n, seg_g, c8, seg_src, used))
    yl = _moe_segments(xl, tables, wts['w_gate'], wts['w_up'], wts['w_down'], blk=blk, ts=ts, n_tiles=n_tiles)
    hosted = host_unsort(n_tiles, lambda i: i) if host_unsort else None
    y, *unsort_hosted = _unsort(h, wc, yl, tm=tm, ts=ts, hosted=hosted)
    return y, merge_hosted, unsort_hosted


def kernel(x_prompt, x_sample, cache_k_w128, cache_v_w128, cache_k_w512, cache_v_w512, cache_k_w2048, cache_v_w2048, state_pool, ln1, w_in, q_gain, k_gain, pool_lin, pool_scale, w_pa, w_pb, w_o, ln2, w_rg, b_rg, w_re, b_re, w_gate, w_up, w_down):
    B, T, D = x_prompt.shape
    Bd, S, _ = x_sample.shape
    past_len = 8192
    caches = ((cache_k_w128, cache_v_w128), (cache_k_w512, cache_v_w512), (cache_k_w2048, cache_v_w2048))
    slopes = jnp.exp2(-8.0 * jnp.arange(1, N_HEADS + 1, dtype=f32) / N_HEADS).reshape(N_GROUPS, HEADS_PER_GROUP)

    plin_bd = jnp.zeros((POOL_W, POOL_W), f32)
    for g in range(len(POOL_WINDOWS)):
        plin_bd = plin_bd.at[g * POOL_GW:(g + 1) * POOL_GW, g * POOL_GW:(g + 1) * POOL_GW].set(pool_lin[g])
    w_router = jnp.zeros((D, ROUTER_W), f32).at[:, :N_EXPERT_GROUPS].set(w_rg)
    w_router = w_router.at[:, EXPERT_COL0:EXPERT_COL0 + N_EXPERTS].set(w_re)
    b_router = jnp.zeros((1, ROUTER_W), f32).at[0, :N_EXPERT_GROUPS].set(b_rg)
    b_router = b_router.at[0, EXPERT_COL0:EXPERT_COL0 + N_EXPERTS].set(b_re)
    wts = dict(w_pa=w_pa.astype(bf16), w_pb=w_pb.astype(bf16), w_o=w_o.astype(bf16), ln2=ln2.reshape(1, D),
               w_router=w_router.astype(bf16), b_router=b_router)
    proj_w = (ln1.reshape(1, D), w_in.astype(bf16), q_gain.reshape(1, ATTN_W), k_gain.reshape(1, ATTN_W),
              plin_bd.astype(bf16), pool_scale.reshape(1, POOL_W))

    n_s = Bd * S
    xs = x_sample.transpose(1, 0, 2).reshape(n_s, D)
    qs, ks, vs, pos, st, sgs = _proj_sample(xs, *proj_w, state_pool.transpose(1, 0, 2),
                                            n_seq=Bd, n_new=S, past_len=past_len)
    pad8 = lambda a: jnp.pad(a.reshape(S, Bd, GROUP_W).transpose(1, 0, 2), ((0, 0), (0, 8 - S), (0, 0)))

    def cached_operands(g):
        cols = slice(g * GROUP_W, (g + 1) * GROUP_W)
        kc, vc = caches[g]
        w = WINDOWS[g]
        return (pad8(qs[:, cols]), pad8(ks[:, cols]), pad8(vs[:, cols]),
                kc.transpose(0, 2, 3, 1).reshape(Bd, GROUP_W, w), vc.transpose(0, 2, 3, 1).reshape(Bd, GROUP_W, w))

    def host(g):
        def parts(steps, step_of):
            if Bd % steps:
                return None
            return _cached_call_parts(*cached_operands(g), g, slopes[g], n_new=S, nb=Bd // steps, step_of=step_of)
        return parts

    tm_proj = 256
    n_t = T // tm_proj
    hosted = host(2)(B * n_t, lambda b, i: b * n_t + i)
    outs = _proj_prompt(x_prompt, *proj_w, tm=tm_proj, hosted=hosted)
    qkv, (po, kt, vt, ut, sg) = outs[:3 * N_GROUPS], outs[3 * N_GROUPS:3 * N_GROUPS + 5]
    cached_out = {}
    if hosted:
        cached_out[2] = outs[3 * N_GROUPS + 5:]
    o, l, (wts['w_gate'], wts['w_up'], wts['w_down']) = zip(*[
        _band_attention(*qkv[3 * g:3 * g + 3], g, slopes[g], cast_w=w) for g, w in enumerate((w_gate, w_up, w_down))])
    y_prompt, in_merge, in_unsort = _mix_and_moe(x_prompt, o, l, po, sg, wts, tm=512, blk=512,
                                                 host_merge=host(1), host_unsort=host(0))
    y_prompt = y_prompt.reshape(B, T, D)
    if in_merge:
        cached_out[1] = in_merge
    if in_unsort:
        cached_out[0] = in_unsort
    tail = kt.shape[2]
    pkv = []
    for g, w in enumerate(WINDOWS):
        for a in (kt, vt):
            a = a.reshape(B, N_HEADS, HEAD_DIM, tail)[:, g * HEADS_PER_GROUP:(g + 1) * HEADS_PER_GROUP, :, tail - w:]
            pkv.append(a.transpose(0, 3, 1, 2))
    p_pool = ut[:, 1:]

    so, sl, skv = [], [], []
    for g, w in enumerate(WINDOWS):
        if g in cached_out:
            og, lg, ko, vo = cached_out[g]
        else:
            og, lg, ko, vo = _cached_attention(*cached_operands(g), g, slopes[g], n_new=S)
        so.append(og[:, :S].transpose(1, 0, 2).reshape(1, 1, n_s, GROUP_W).astype(bf16))
        sl.append(lg[:, :S].transpose(1, 0, 2).reshape(1, 1, n_s, GROUP_W))
        for a in (ko, vo):
            skv.append(a.reshape(Bd, HEADS_PER_GROUP, HEAD_DIM, w).transpose(0, 3, 1, 2))
    y_sample = _mix_and_moe(xs[None], so, sl, pos[None], sgs[None], wts, tm=n_s, blk=128)[0]
    y_sample = y_sample.reshape(S, Bd, D).transpose(1, 0, 2)
    s_pool = st.transpose(1, 0, 2)

    return (y_prompt, y_sample, *pkv, p_pool, *skv, s_pool)
```

```python
import functools

import jax
import jax.numpy as jnp
from jax import lax
from jax.experimental import pallas as pl
from jax.experimental.pallas import tpu as pltpu

D_MODEL = 1024
HEAD_DIM = 64
HEADS_PER_GROUP = 4
WINDOWS = (128, 512, 2048)
DILATIONS = (1, 4, 16)
N_GROUPS = len(WINDOWS)
N_HEADS = HEADS_PER_GROUP * N_GROUPS
ATTN_W = N_HEADS * HEAD_DIM
GROUP_W = HEADS_PER_GROUP * HEAD_DIM
BAND = 128
POOL_WINDOWS = (2, 4, 8, 16)
POOL_GW = 128
POOL_W = len(POOL_WINDOWS) * POOL_GW
POOL_STATE = max(POOL_WINDOWS) - 1
POOL_HIST = 32
assert POOL_WINDOWS == (2, 4, 8, 16)
N_EXPERT_GROUPS = 4
EXPERTS_PER_GROUP = 8
N_EXPERTS = N_EXPERT_GROUPS * EXPERTS_PER_GROUP
D_EXPERT = 512
QKVU_W = 3 * ATTN_W + POOL_W
IN_W = QKVU_W + 2 * D_MODEL
NEG = -1e30
EPS = 1e-6
LANES = 128
ROUTER_W = LANES
EXPERT_COL0 = 8
VMEM_LIMIT = 56 * 1024 * 1024
IBUF_SLOTS = 2 * (N_GROUPS - 1) * (GROUP_W // LANES)
SEG_ALIGN = 8

PAST_LEN = 8192
TM_PROJ = 256
TM_TOKENS = 512
MOE_BLOCK_PROMPT = 512
MOE_BLOCK_SAMPLE = 128

assert all(w // d == BAND for w, d in zip(WINDOWS, DILATIONS))

f32 = jnp.float32
bf16 = jnp.bfloat16


def _cparams(sem):
    return pltpu.CompilerParams(dimension_semantics=sem, vmem_limit_bytes=VMEM_LIMIT)


def _pack_rows(v):
    k = v.shape[1] // 2
    hi = lax.bitcast_convert_type(v[:, :k], jnp.uint32)
    lo = lax.bitcast_convert_type(v[:, k:], jnp.uint32)
    return hi | lax.shift_right_logical(lo, jnp.uint32(16))


def _unpack_rows(w):
    hi = lax.bitcast_convert_type(w & jnp.uint32(0xFFFF0000), f32)
    lo = lax.bitcast_convert_type(lax.shift_left(w, jnp.uint32(16)), f32)
    return jnp.concatenate([hi, lo], axis=-1).astype(bf16)


def _rmsnorm_rows(x, g):
    ms = jnp.mean(x * x, axis=-1, keepdims=True)
    return x * lax.rsqrt(ms + EPS) * g


def _head_rmsnorm_chunk(ch, gain):
    lane = lax.broadcasted_iota(jnp.int32, ch.shape, 1)
    lo_mask = lane < HEAD_DIM
    sq = ch * ch
    lo = jnp.sum(jnp.where(lo_mask, sq, 0.0), axis=-1, keepdims=True)
    hi = jnp.sum(jnp.where(lo_mask, 0.0, sq), axis=-1, keepdims=True)
    ss = jnp.where(lo_mask, lo, hi)
    return ch * lax.rsqrt(ss * (1.0 / HEAD_DIM) + EPS) * gain


def _project(x, ln1, w_ref, qg, kg):
    xn = _rmsnorm_rows(x, ln1).astype(bf16)
    nch = ATTN_W // LANES
    qs = qg * (HEAD_DIM ** -0.5)
    zq = jnp.dot(xn, w_ref[:, 0:ATTN_W], preferred_element_type=f32)
    q = [_head_rmsnorm_chunk(zq[:, c * LANES:(c + 1) * LANES], qs[:, c * LANES:(c + 1) * LANES]) for c in range(nch)]
    zk = jnp.dot(xn, w_ref[:, ATTN_W:2 * ATTN_W], preferred_element_type=f32)
    k = [_head_rmsnorm_chunk(zk[:, c * LANES:(c + 1) * LANES], kg[:, c * LANES:(c + 1) * LANES]) for c in range(nch)]
    zv = jnp.dot(xn, w_ref[:, 2 * ATTN_W:3 * ATTN_W], preferred_element_type=f32)
    v = [zv[:, c * LANES:(c + 1) * LANES] for c in range(nch)]
    u = jnp.dot(xn, w_ref[:, 3 * ATTN_W:3 * ATTN_W + POOL_W], preferred_element_type=f32)
    gates = jnp.dot(xn, w_ref[:, QKVU_W:QKVU_W + 2 * D_MODEL], preferred_element_type=f32)
    sg = (0.5 * jnp.tanh(0.5 * gates) + 0.5).astype(bf16)
    return q, k, v, u, sg


def _proj_prompt_kernel(x_ref, ln1_ref, w_ref, qg_ref, kg_ref, plin_ref, pscale_ref, *rest,
                        tm, n_tiles, tail_tiles, hosted):
    n_hosted_in, n_hosted_out = (7, 4) if hosted else (0, 0)
    hosted_in, rest = rest[:n_hosted_in], rest[n_hosted_in:]
    qkv_refs = rest[:3 * N_GROUPS]
    po_ref, kt_ref, vt_ref, ut_ref, sg_ref = rest[3 * N_GROUPS:3 * N_GROUPS + 5]
    rest = rest[3 * N_GROUPS + 5:]
    hosted_out, (ubuf, pa, pb, sbuf) = rest[:n_hosted_out], rest[n_hosted_out:]
    i = pl.program_id(1)
    hist = POOL_STATE + 1
    ph, rows = POOL_HIST, POOL_HIST + tm

    @pl.when(i == 0)
    def _():
        ubuf[0:ph, :] = jnp.zeros((ph, POOL_W), f32)

    @pl.when(i > 0)
    def _():
        ubuf[0:ph, :] = ubuf[tm:tm + ph, :]

    stages = _cached_attn_stages(*hosted_in, *hosted_out, **hosted) if hosted else ()
    for s in stages:
        s()
    q, k, v, u, sg = _project(x_ref[0], ln1_ref[...], w_ref, qg_ref[...], kg_ref[...])
    sg_ref[0] = sg
    cpg = GROUP_W // LANES
    slot = 0
    for t, chunks in enumerate((q, k, v)):
        for g, dil in enumerate(DILATIONS):
            out_ref = qkv_refs[3 * g + t]
            for c in range(cpg):
                val = chunks[g * cpg + c]
                cols = slice(c * LANES, (c + 1) * LANES)
                if dil == 1:
                    out_ref[0, 0, :, cols] = val.astype(bf16)
                else:
                    sbuf[slot * tm:(slot + 1) * tm, :] = val
                    for r in range(dil):
                        out_ref[0, r, :, cols] = sbuf[pl.ds(slot * tm + r, tm // dil, stride=dil), :].astype(bf16)
                    slot += 1

    ubuf[ph:rows, :] = u
    gw = POOL_GW
    pa[8:rows, :] = ubuf[8:rows, :] + ubuf[7:rows - 1, :]
    pb[16:rows, gw:] = pa[16:rows, gw:] + pa[14:rows - 2, gw:]
    pa[24:rows, 2 * gw:] = pb[24:rows, 2 * gw:] + pb[20:rows - 4, 2 * gw:]
    pb[32:rows, 3 * gw:] = pa[32:rows, 3 * gw:] + pa[24:rows - 8, 3 * gw:]
    pos = i * tm + lax.broadcasted_iota(jnp.int32, (tm, POOL_GW), 0)
    zs = []
    for g, w in enumerate(POOL_WINDOWS):
        cols = slice(g * POOL_GW, (g + 1) * POOL_GW)
        wsum = (pa if g % 2 == 0 else pb)[ph:rows, cols]
        cnt = jnp.minimum(pos + 1, w).astype(f32)
        zs.append(wsum / cnt - u[:, cols])
    z = jnp.concatenate(zs, axis=-1).astype(bf16)
    po = jnp.dot(z, plin_ref[...], preferred_element_type=f32) * pscale_ref[...]
    po_ref[0] = po.astype(bf16)

    @pl.when(i >= n_tiles - tail_tiles)
    def _():
        kt_ref[0] = jnp.concatenate(k, axis=-1).T
        vt_ref[0] = jnp.concatenate(v, axis=-1).T

    @pl.when(i == n_tiles - 1)
    def _():
        ut_ref[0] = ubuf[rows - hist:rows, :]


def _proj_prompt(x, ln1, w_qkvu, qg, kg, plin_bd, pscale, *, tm, hosted=None):
    B, T, D = x.shape
    n_tiles = T // tm
    tail = max(WINDOWS)
    assert T % tm == 0 and tail % tm == 0 and T >= tail
    tail_tiles = tail // tm
    hist = POOL_STATE + 1
    kern = functools.partial(_proj_prompt_kernel, tm=tm, n_tiles=n_tiles, tail_tiles=tail_tiles,
                             hosted=hosted['params'] if hosted else None)
    h_ops, h_in, h_out, h_shape = ((), [], [], []) if not hosted else (
        hosted['operands'], hosted['in_specs'], hosted['out_specs'], hosted['out_shape'])
    const = lambda b, i: (0, 0)
    assert all(tm % (16 * d) == 0 for d in DILATIONS)
    qkv_specs = [pl.BlockSpec((1, d, tm // d, GROUP_W), lambda b, i: (b, 0, i, 0)) for d in DILATIONS for _ in range(3)]
    qkv_shapes = [jax.ShapeDtypeStruct((B, d, T // d, GROUP_W), bf16) for d in DILATIONS for _ in range(3)]
    tail_spec = pl.BlockSpec((1, ATTN_W, tm), lambda b, i: (b, 0, jnp.maximum(i - (n_tiles - tail_tiles), 0)))
    return pl.pallas_call(
        kern,
        grid=(B, n_tiles),
        in_specs=[
            pl.BlockSpec((1, tm, D), lambda b, i: (b, i, 0)),
            pl.BlockSpec((1, D), const),
            pl.BlockSpec((D, IN_W), const),
            pl.BlockSpec((1, ATTN_W), const),
            pl.BlockSpec((1, ATTN_W), const),
            pl.BlockSpec((POOL_W, POOL_W), const),
            pl.BlockSpec((1, POOL_W), const),
        ] + h_in,
        out_specs=qkv_specs
        + [pl.BlockSpec((1, tm, POOL_W), lambda b, i: (b, i, 0)),
           tail_spec, tail_spec,
           pl.BlockSpec((1, hist, POOL_W), lambda b, i: (b, 0, 0)),
           pl.BlockSpec((1, tm, 2 * D), lambda b, i: (b, i, 0))] + h_out,
        out_shape=qkv_shapes
        + [jax.ShapeDtypeStruct((B, T, POOL_W), bf16),
           jax.ShapeDtypeStruct((B, ATTN_W, tail), f32),
           jax.ShapeDtypeStruct((B, ATTN_W, tail), f32),
           jax.ShapeDtypeStruct((B, hist, POOL_W), f32),
           jax.ShapeDtypeStruct((B, T, 2 * D), bf16)] + h_shape,
        scratch_shapes=[pltpu.VMEM((POOL_HIST + tm, POOL_W), f32)] * 3
                       + [pltpu.VMEM((3 * (N_GROUPS - 1) * (GROUP_W // LANES) * tm, LANES), f32)],
        compiler_params=_cparams(("arbitrary", "arbitrary")),
        name="proj_prompt",
    )(x, ln1, w_qkvu, qg, kg, plin_bd, pscale, *h_ops)


def _proj_sample_kernel(x_ref, ln1_ref, w_ref, qg_ref, kg_ref, plin_ref, pscale_ref, state_ref,
                        q_ref, k_ref, v_ref, po_ref, st_ref, sg_ref, *, n_seq, n_new, past_len):
    q, k, v, u, sg = _project(x_ref[...], ln1_ref[...], w_ref, qg_ref[...], kg_ref[...])
    sg_ref[...] = sg
    q_ref[...] = jnp.concatenate(q, axis=-1)
    k_ref[...] = jnp.concatenate(k, axis=-1)
    v_ref[...] = jnp.concatenate(v, axis=-1)
    ext = [state_ref[j] for j in range(POOL_STATE)] + [u[s * n_seq:(s + 1) * n_seq, :] for s in range(n_new)]
    for s in range(n_new):
        zs = []
        for g, w in enumerate(POOL_WINDOWS):
            cols = slice(g * POOL_GW, (g + 1) * POOL_GW)
            cur = ext[POOL_STATE + s][:, cols]
            acc = cur
            for j in range(1, w):
                acc = acc + ext[POOL_STATE + s - j][:, cols]
            cnt = float(min(past_len + s + 1, w))
            zs.append(acc / cnt - cur)
        z = jnp.concatenate(zs, axis=-1).astype(bf16)
        po = jnp.dot(z, plin_ref[...], preferred_element_type=f32) * pscale_ref[...]
        po_ref[s * n_seq:(s + 1) * n_seq, :] = po.astype(bf16)
    for j in range(POOL_STATE):
        st_ref[j] = ext[j + n_new]


def _proj_sample(x, ln1, w_qkvu, qg, kg, plin_bd, pscale, state, *, n_seq, n_new, past_len):
    n = n_seq * n_new
    kern = functools.partial(_proj_sample_kernel, n_seq=n_seq, n_new=n_new, past_len=past_len)
    return pl.pallas_call(
        kern,
        out_shape=[jax.ShapeDtypeStruct((n, ATTN_W), f32)] * 3
        + [jax.ShapeDtypeStruct((n, POOL_W), bf16),
           jax.ShapeDtypeStruct((POOL_STATE, n_seq, POOL_W), f32),
           jax.ShapeDtypeStruct((n, 2 * D_MODEL), bf16)],
        compiler_params=pltpu.CompilerParams(vmem_limit_bytes=VMEM_LIMIT),
        name="proj_sample",
    )(x, ln1, w_qkvu, qg, kg, plin_bd, pscale, state)


def _head_masks(shape):
    lane = lax.broadcasted_iota(jnp.int32, shape, len(shape) - 1)
    return [(lane >= h * HEAD_DIM) & (lane < (h + 1) * HEAD_DIM) for h in range(HEADS_PER_GROUP)]


def _band_attn_kernel(q_ref, kc_ref, kp_ref, vc_ref, vp_ref, bias_ref, *rest, tl, unroll, cast):
    if cast:
        w_ref, o_ref, l_ref, wb_ref, kbuf, vbuf = rest
        wb_ref[...] = w_ref[...].astype(bf16)
    else:
        o_ref, l_ref, kbuf, vbuf = rest
    i = pl.program_id(2)
    kbuf[0:BAND, :] = kp_ref[0, 0]
    kbuf[BAND:2 * BAND, :] = kc_ref[0, 0, 0:BAND, :]
    vbuf[0:BAND, :] = vp_ref[0, 0]
    vbuf[BAND:2 * BAND, :] = vc_ref[0, 0, 0:BAND, :]
    masks = _head_masks((BAND, GROUP_W))

    def sub_block(j, kk, vv, var):
        r0 = j * BAND if isinstance(j, int) else pl.multiple_of(j * BAND, BAND)
        q = q_ref[0, 0, pl.ds(r0, BAND), :]
        qm = jnp.concatenate([jnp.where(m, q, jnp.zeros_like(q)) for m in masks], axis=0)
        s = lax.dot_general(qm, kk, (((1,), (1,)), ((), ())), preferred_element_type=f32)
        s = s + bias_ref[var]
        m = jnp.max(s, axis=-1, keepdims=True)
        p = jnp.exp(s - m)
        den = jnp.sum(p, axis=-1, keepdims=True)
        pv = jnp.dot(p.astype(bf16), vv, preferred_element_type=f32)
        o = jnp.zeros((BAND, GROUP_W), f32)
        ms = jnp.zeros((BAND, GROUP_W), f32)
        ds = jnp.ones((BAND, GROUP_W), f32)
        for h, msk in enumerate(masks):
            rows = slice(h * BAND, (h + 1) * BAND)
            o = jnp.where(msk, pv[rows], o)
            ms = jnp.where(msk, m[rows], ms)
            ds = jnp.where(msk, den[rows], ds)
        o_ref[0, 0, pl.ds(r0, BAND), :] = (o / ds).astype(bf16)
        l_ref[0, 0, pl.ds(r0, BAND), :] = ms + jnp.log(ds)

    sub_block(0, kbuf[...], vbuf[...], jnp.where(i == 0, 0, 1))

    def body(j, carry):
        k0 = pl.multiple_of((j - 1) * BAND, BAND)
        sub_block(j, kc_ref[0, 0, pl.ds(k0, 2 * BAND), :], vc_ref[0, 0, pl.ds(k0, 2 * BAND), :], 1)
        return carry

    if tl > BAND:
        lax.fori_loop(1, tl // BAND, body, 0, unroll=unroll)


def _band_bias(slopes_g, dil):
    qi = jnp.arange(BAND)[:, None]
    kb = jnp.arange(2 * BAND)[None, :]
    rel = qi + BAND - kb
    valid = (rel >= 0) & (rel <= BAND)
    alibi = -slopes_g[:, None, None] * (dil * rel)[None].astype(f32)
    variants = []
    for first in (True, False):
        ok = valid & (kb >= BAND) if first else valid
        variants.append(jnp.where(ok[None], alibi, NEG).reshape(HEADS_PER_GROUP * BAND, 2 * BAND))
    return jnp.stack(variants, axis=0)


def _band_attention(q, k, v, g, slopes_g, *, tl_max=1024, unroll=7, cast_w=None):
    B, dil, L, _ = q.shape
    tl = min(tl_max, L)
    assert dil == DILATIONS[g] and L % tl == 0 and tl % BAND == 0
    nsub = tl // BAND
    nl = L // tl
    bias = _band_bias(slopes_g, dil)
    cur = pl.BlockSpec((1, 1, tl, GROUP_W), lambda b, r, i: (b, r, i, 0))
    prev = pl.BlockSpec((1, 1, BAND, GROUP_W), lambda b, r, i: (b, r, jnp.maximum(i * nsub - 1, 0), 0))
    host_cast = cast_w is not None and cast_w.shape[0] == B * dil * nl
    w_spec, w_ops, w_shape = [], (), []
    if host_cast:
        w_spec = [pl.BlockSpec((1,) + cast_w.shape[1:], lambda b, r, i: ((b * dil + r) * nl + i, 0, 0))]
        w_ops, w_shape = (cast_w,), [jax.ShapeDtypeStruct(cast_w.shape, bf16)]
    res = pl.pallas_call(
        functools.partial(_band_attn_kernel, tl=tl, unroll=max(1, min(unroll, nsub - 1)), cast=host_cast),
        grid=(B, dil, nl),
        in_specs=[cur, cur, prev, cur, prev,
                  pl.BlockSpec((2, HEADS_PER_GROUP * BAND, 2 * BAND), lambda b, r, i: (0, 0, 0))] + w_spec,
        out_specs=[cur, cur] + w_spec,
        out_shape=[jax.ShapeDtypeStruct((B, dil, L, GROUP_W), bf16),
                   jax.ShapeDtypeStruct((B, dil, L, GROUP_W), f32)] + w_shape,
        scratch_shapes=[pltpu.VMEM((2 * BAND, GROUP_W), bf16), pltpu.VMEM((2 * BAND, GROUP_W), bf16)],
        compiler_params=_cparams(("arbitrary", "arbitrary", "arbitrary")),
        name="band_attn_g%d" % g,
    )(q, k, k, v, v, bias, *w_ops)
    if cast_w is None:
        return res
    return res[0], res[1], (res[2] if host_cast else cast_w.astype(bf16))


def _cached_attn_kernel(*refs, **params):
    for stage in _cached_attn_stages(*refs, **params):
        stage()


def _cached_attn_stages(q_ref, kn_ref, vn_ref, kc_ref, vc_ref, bc_ref, bn_ref,
                        o_ref, l_ref, ko_ref, vo_ref, *, nb, n_new, win):
    def roll(c_ref, new_ref, out_ref):
        lane_t = lax.broadcasted_iota(jnp.int32, (nb, GROUP_W, LANES), 2)
        rolled = pltpu.roll(c_ref[...], win - n_new, axis=2)
        out_ref[...] = rolled
        new_t = jnp.swapaxes(jnp.concatenate([new_ref[...], jnp.zeros((nb, LANES - 8, GROUP_W), f32)], axis=1), 1, 2)
        new_t = pltpu.roll(new_t, LANES - n_new, axis=2)
        out_ref[:, :, win - LANES:win] = jnp.where(lane_t >= LANES - n_new, new_t, rolled[:, :, win - LANES:win])

    return (functools.partial(_cached_attn_scores, q_ref, kn_ref, vn_ref, kc_ref, vc_ref, bc_ref, bn_ref, o_ref, l_ref,
                              nb=nb, n_new=n_new),
            functools.partial(roll, kc_ref, kn_ref, ko_ref),
            functools.partial(roll, vc_ref, vn_ref, vo_ref))


def _cached_attn_scores(q_ref, kn_ref, vn_ref, kc_ref, vc_ref, bc_ref, bn_ref, o_ref, l_ref, *, nb, n_new):
    masks8 = _head_masks((nb, 8, GROUP_W))
    q8, kn8, vn8 = q_ref[...], kn_ref[...], vn_ref[...]
    kc, vc = kc_ref[...], vc_ref[...]
    qm = jnp.concatenate([jnp.where(m, q8, 0.0) for m in masks8], axis=1)
    sc = jnp.einsum('bqd,bdk->bqk', qm.astype(bf16), kc.astype(bf16), preferred_element_type=f32) + bc_ref[...]
    m = jnp.max(sc, axis=-1, keepdims=True)
    sn = []
    for t in range(n_new):
        col = jnp.sum(qm * kn8[:, t:t + 1, :], axis=-1, keepdims=True) + bn_ref[:, t:t + 1]
        sn.append(col)
        m = jnp.maximum(m, col)
    pc = jnp.exp(sc - m)
    den = jnp.sum(pc, axis=-1, keepdims=True)
    acc = jnp.einsum('bqk,bdk->bqd', pc.astype(bf16), vc.astype(bf16), preferred_element_type=f32)
    for t in range(n_new):
        pn = jnp.exp(sn[t] - m)
        den = den + pn
        acc = acc + pn * vn8[:, t:t + 1, :]
    acc = acc / den
    lse = m + jnp.log(den)
    o = jnp.zeros((nb, 8, GROUP_W), f32)
    l = jnp.zeros((nb, 8, GROUP_W), f32)
    for h, msk in enumerate(masks8):
        o = jnp.where(msk, acc[:, h * 8:(h + 1) * 8, :], o)
        l = jnp.where(msk, lse[:, h * 8:(h + 1) * 8, :], l)
    o_ref[...] = o
    l_ref[...] = l


def _cached_bias(slopes_g, dil, win, n_new):
    s = jnp.arange(8)[:, None]
    i = jnp.arange(win)[None, :]
    dist = win + s - i
    ok = (dist % dil == 0) & (dist // dil <= BAND) & (s < n_new)
    bc = jnp.where(ok[None], -slopes_g[:, None, None] * dist[None].astype(f32), NEG)
    t = jnp.arange(8)[None, :]
    dn = s - t
    okn = (dn >= 0) & (dn % dil == 0) & (dn // dil <= BAND) & (s < n_new) & (t < n_new)
    bn = jnp.where(okn[None], -slopes_g[:, None, None] * dn[None].astype(f32), NEG)
    pad = (s >= n_new)
    bc = jnp.where(pad[None], 0.0, bc)
    bn = jnp.where(pad[None], 0.0, bn)
    return bc.reshape(HEADS_PER_GROUP * 8, win), bn.reshape(HEADS_PER_GROUP * 8, 8)


def _cached_call_parts(q8, kn8, vn8, kc_t, vc_t, g, slopes_g, *, n_new, nb, step_of):
    Bd, _, win = kc_t.shape
    assert win == WINDOWS[g] and win % LANES == 0 and Bd % nb == 0
    bc, bn = _cached_bias(slopes_g, DILATIONS[g], win, n_new)
    small = pl.BlockSpec((nb, 8, GROUP_W), lambda *idx: (step_of(*idx), 0, 0))
    cache = pl.BlockSpec((nb, GROUP_W, win), lambda *idx: (step_of(*idx), 0, 0))
    const = lambda a: pl.BlockSpec(a.shape, lambda *idx: (0, 0))
    return dict(
        operands=(q8, kn8, vn8, kc_t, vc_t, bc, bn),
        in_specs=[small, small, small, cache, cache, const(bc), const(bn)],
        out_specs=[small, small, cache, cache],
        out_shape=[jax.ShapeDtypeStruct((Bd, 8, GROUP_W), f32)] * 2 + [jax.ShapeDtypeStruct((Bd, GROUP_W, win), f32)] * 2,
        params=dict(nb=nb, n_new=n_new, win=win))


def _cached_attention(q8, kn8, vn8, kc_t, vc_t, g, slopes_g, *, n_new):
    Bd, _, win = kc_t.shape
    nb = max(1, min(Bd, 2048 // win))
    parts = _cached_call_parts(q8, kn8, vn8, kc_t, vc_t, g, slopes_g, n_new=n_new, nb=nb, step_of=lambda b: b)
    return pl.pallas_call(
        functools.partial(_cached_attn_kernel, **parts['params']),
        grid=(Bd // nb,),
        in_specs=parts['in_specs'],
        out_specs=parts['out_specs'],
        out_shape=parts['out_shape'],
        compiler_params=_cparams(("arbitrary",)),
        name="cached_attn_g%d" % g,
    )(*parts['operands'])


def _split_hosted(rest, n_own_out, hosted):
    n_in, n_out = (7, 4) if hosted else (0, 0)
    hosted_in, rest = rest[:n_in], rest[n_in:]
    own_out, rest = rest[:n_own_out], rest[n_own_out:]
    return hosted_in, own_out, rest[:n_out], rest[n_out:]


def _merge_kernel(x_ref, o0_ref, o1_ref, o2_ref, l0_ref, l1_ref, l2_ref, po_ref, sg_ref,
                  wpa_ref, wpb_ref, wo_ref, ln2_ref, wr_ref, br_ref, tri_ref, ltri_ref,
                  *rest, tm, ts, dils, hosted):
    hosted_in, (h_ref, xl_ref, wc_ref, tc_ref), hosted_out, (ibuf,) = _split_hosted(rest, 4, hosted)
    if hosted:
        _cached_attn_kernel(*hosted_in, *hosted_out, **hosted)
    slots = iter(range(IBUF_SLOTS))

    def token_order(ref, dil):
        if dil == 1:
            return ref[0, 0].astype(f32)
        chunks = []
        for c in range(GROUP_W // LANES):
            base = next(slots) * tm
            for r in range(dil):
                ibuf[pl.ds(base + r, tm // dil, stride=dil), :] = ref[0, r, :, c * LANES:(c + 1) * LANES].astype(f32)
            chunks.append(ibuf[base:base + tm, :])
        return jnp.concatenate(chunks, axis=-1)

    x = x_ref[0]
    l0, l1, l2 = (token_order(r, d) for r, d in zip((l0_ref, l1_ref, l2_ref), dils))
    lm = jnp.maximum(jnp.maximum(l0, l1), l2)
    e0, e1, e2 = jnp.exp(l0 - lm), jnp.exp(l1 - lm), jnp.exp(l2 - lm)
    o0, o1, o2 = (token_order(r, d) for r, d in zip((o0_ref, o1_ref, o2_ref), dils))
    attn = (e0 * o0 + e1 * o1 + e2 * o2) / (e0 + e1 + e2)
    ma = jnp.dot(attn.astype(bf16), wpa_ref[...], preferred_element_type=f32)
    mb = jnp.dot(po_ref[0], wpb_ref[...], preferred_element_type=f32)
    mix = sg_ref[0, :, :D_MODEL].astype(f32) * ma + sg_ref[0, :, D_MODEL:].astype(f32) * mb
    h = x + jnp.dot(mix.astype(bf16), wo_ref[...], preferred_element_type=f32)
    h_ref[...] = h
    xn2 = _rmsnorm_rows(h, ln2_ref[...]).astype(bf16)

    lt = (jnp.dot(xn2, wr_ref[...], preferred_element_type=f32) + br_ref[...]).T
    row8 = lax.broadcasted_iota(jnp.int32, (8, tm), 0)
    gl = jnp.where(row8 < N_EXPERT_GROUPS, lt[0:8], -jnp.inf)
    gmax = jnp.max(gl, axis=0, keepdims=True)
    gidx = jnp.min(jnp.where(gl == gmax, row8, 8), axis=0, keepdims=True)
    pg = 1.0 / jnp.sum(jnp.exp(gl - gmax), axis=0, keepdims=True)
    sel = jnp.zeros((8, tm), f32)
    for g in range(N_EXPERT_GROUPS):
        lo = EXPERT_COL0 + g * EXPERTS_PER_GROUP
        sel = jnp.where(gidx == g, lt[lo:lo + EXPERTS_PER_GROUP], sel)
    v0 = jnp.max(sel, axis=0, keepdims=True)
    i0 = jnp.min(jnp.where(sel == v0, row8, 8), axis=0, keepdims=True)
    sel2 = jnp.where(row8 == i0, -jnp.inf, sel)
    v1 = jnp.max(sel2, axis=0, keepdims=True)
    i1 = jnp.min(jnp.where(sel2 == v1, row8, 8), axis=0, keepdims=True)
    t = jnp.exp(v1 - v0)
    w0 = pg / (1.0 + t)
    w1 = pg * t / (1.0 + t)
    eid0 = gidx * EXPERTS_PER_GROUP + i0
    eid1 = gidx * EXPERTS_PER_GROUP + i1
    erow = lax.broadcasted_iota(jnp.int32, (N_EXPERTS, tm), 0)
    oh0 = erow == eid0
    oh1 = erow == eid1
    cnt = jnp.where(oh0, 1.0, jnp.where(oh1, 1.0, 0.0))
    before = jnp.dot(cnt.astype(bf16), tri_ref[...], preferred_element_type=f32)
    tcount = jnp.sum(cnt, axis=1, keepdims=True)
    units = jnp.floor((tcount + (SEG_ALIGN - 1)) * (1.0 / SEG_ALIGN))
    ub = jnp.broadcast_to(units, (N_EXPERTS, LANES)).astype(bf16)
    seg0 = SEG_ALIGN * jnp.dot(ltri_ref[...], ub, preferred_element_type=f32)[:, 0:1]
    pos_e = seg0 + before
    lpos0 = jnp.sum(jnp.where(oh0, pos_e, 0.0), axis=0, keepdims=True)
    lpos1 = jnp.sum(jnp.where(oh1, pos_e, 0.0), axis=0, keepdims=True)
    prow = lax.broadcasted_iota(jnp.int32, (ts, tm), 0)
    perm = jnp.where(prow == lpos0.astype(jnp.int32), 1.0, jnp.where(prow == lpos1.astype(jnp.int32), 1.0, 0.0))
    xl_ref[...] = _pack_rows(jnp.dot(perm.astype(bf16), xn2, preferred_element_type=f32))
    tc_ref[...] = jnp.broadcast_to(tcount, (N_EXPERTS, LANES))
    rowl = lax.broadcasted_iota(jnp.int32, (LANES, tm), 0)
    wslab = jnp.zeros((LANES, tm), f32)
    for r, val in enumerate((w0, w1, lpos0, lpos1)):
        wslab = jnp.where(rowl == r, val, wslab)
    wc_ref[...] = wslab.T


def _hosted_parts(hosted):
    if not hosted:
        return (), [], [], [], None
    return hosted['operands'], hosted['in_specs'], hosted['out_specs'], hosted['out_shape'], hosted['params']


def _merge(x, o, l, po, sg, w_pa, w_pb, w_o, ln2, w_router, b_router, *, tm, hosted=None):
    B, T, _ = x.shape
    h_ops, h_in, h_out, h_shape, h_params = _hosted_parts(hosted)
    assert T % tm == 0
    nt = T // tm
    n = B * T
    dils = tuple(a.shape[1] for a in o)
    assert all(tm % (8 * d) == 0 for d in dils)
    assert 2 * tm // SEG_ALIGN <= 256
    ts = _sorted_tile_rows(tm)
    tri = (jnp.arange(tm)[:, None] < jnp.arange(tm)[None, :]).astype(bf16)
    ltri = (jnp.arange(N_EXPERTS)[None, :] < jnp.arange(N_EXPERTS)[:, None]).astype(bf16)
    rows3 = lambda w: pl.BlockSpec((1, tm, w), lambda b, i: (b, i, 0))
    flat = lambda r, w: pl.BlockSpec((r, w), lambda b, i: (b * nt + i, 0))
    grp = [pl.BlockSpec((1, d, tm // d, GROUP_W), lambda b, i: (b, 0, i, 0)) for d in dils]
    full = lambda a: pl.BlockSpec(a.shape, lambda b, i: (0,) * a.ndim)
    weights = (w_pa, w_pb, w_o, ln2, w_router, b_router, tri, ltri)
    return pl.pallas_call(
        functools.partial(_merge_kernel, tm=tm, ts=ts, dils=dils, hosted=h_params),
        grid=(B, nt),
        in_specs=[rows3(D_MODEL)] + grp + grp + [rows3(POOL_W), rows3(2 * D_MODEL)]
        + [full(a) for a in weights] + h_in,
        out_specs=[flat(tm, D_MODEL), flat(ts, D_MODEL // 2), flat(tm, LANES),
                   pl.BlockSpec((N_EXPERTS, LANES), lambda b, i: (0, b * nt + i))] + h_out,
        out_shape=[jax.ShapeDtypeStruct((n, D_MODEL), f32),
                   jax.ShapeDtypeStruct((B * nt * ts, D_MODEL // 2), jnp.uint32),
                   jax.ShapeDtypeStruct((n, LANES), f32),
                   jax.ShapeDtypeStruct((N_EXPERTS, B * nt * LANES), f32)] + h_shape,
        scratch_shapes=[pltpu.VMEM((IBUF_SLOTS * tm, LANES), f32)],
        compiler_params=_cparams(("arbitrary", "arbitrary")),
        name="merge_router",
    )(x, *o, *l, po, sg, *weights, *h_ops)


def _sorted_tile_rows(tm):
    return -(-(2 * tm + N_EXPERTS * (SEG_ALIGN - 1)) // LANES) * LANES


def _moe_seg_kernel(blk_e_ref, blk_r0_ref, blk_n_ref, seg_g_ref, seg_c_ref, seg_src_ref, used_ref,
                    xl_hbm, wg_ref, wu_ref, wd_ref, yl_hbm,
                    xbuf, ybuf, sem_in, sem_out, sem_zero, ptr, *, blk, ts, n_tiles, n_blocks):
    b = pl.program_id(0)
    slot = lax.rem(b, 2)
    unit_bits = (blk // SEG_ALIGN).bit_length()

    def row_pieces(rows):
        units = lax.shift_right_logical(rows, jnp.int32(SEG_ALIGN.bit_length() - 1))
        off = jnp.int32(0)
        for bit in reversed(range(unit_bits)):
            on = lax.shift_right_logical(units, jnp.int32(bit)) & 1
            yield on == 1, off, SEG_ALIGN << bit
            off = off + on * (SEG_ALIGN << bit)

    def copy_rows(src, src_row, dst, dst_row, rows, sem):
        def start(off, size):
            pltpu.make_async_copy(src.at[pl.ds(pl.multiple_of(src_row + off, SEG_ALIGN), size)],
                                  dst.at[pl.ds(pl.multiple_of(dst_row + off, SEG_ALIGN), size)], sem).start()

        chunk = SEG_ALIGN << 3
        n_chunks = lax.shift_right_logical(rows, jnp.int32(chunk.bit_length() - 1))

        def whole_chunk(j, c):
            start(j * chunk, chunk)
            return c

        lax.fori_loop(0, n_chunks, whole_chunk, 0)
        off = n_chunks * chunk
        rest = rows - off
        for bit in (2, 1, 0):
            size = SEG_ALIGN << bit
            on = (rest & size) != 0

            @pl.when(on)
            def _():
                start(off, size)

            off = off + jnp.where(on, size, 0)

    def wait_rows(src, dst, rows, sem):
        for on, _, size in row_pieces(rows):
            @pl.when(on)
            def _():
                pltpu.make_async_copy(src.at[pl.ds(0, size)], dst.at[pl.ds(0, size)], sem).wait()

    def for_pieces(bb, stream, fn):
        e, r0, n = blk_e_ref[bb], blk_r0_ref[bb], blk_n_ref[bb]

        @pl.when(n > 0)
        def _():
            def seg(i):
                return jnp.minimum(i, n_tiles - 1) * N_EXPERTS + e

            def cond(i):
                return (i < n_tiles) & (seg_g_ref[seg(i)] < r0 + n)

            def body(i):
                g = seg_g_ref[seg(i)]
                lo = jnp.maximum(g, r0)
                hi = jnp.minimum(g + seg_c_ref[seg(i)], r0 + n)

                @pl.when(hi > lo)
                def _():
                    fn(seg_src_ref[seg(i)] + (lo - g), lo - r0, hi - lo)

                return i + 1

            end = lax.while_loop(cond, body, jnp.where(r0 == 0, 0, ptr[stream]))
            ptr[stream] = jnp.maximum(end - 1, 0)

    def gather(bb):
        s = lax.rem(bb, 2)
        for_pieces(bb, 0, lambda lrow, brow, rows: copy_rows(xl_hbm, lrow, xbuf.at[s], brow, rows, sem_in.at[s]))

    def scatter(bb):
        s = lax.rem(bb, 2)
        for_pieces(bb, 1, lambda lrow, brow, rows: copy_rows(ybuf.at[s], brow, yl_hbm, lrow, rows, sem_out.at[s]))

    def zero_tail(i, go):
        row0 = i * ts + used_ref[i]
        rows = ts - used_ref[i]
        whole = lax.shift_right_logical(rows, jnp.int32(blk.bit_length() - 1))
        zsrc = xbuf.at[1]

        def whole_block(j, c):
            cp = pltpu.make_async_copy(zsrc, yl_hbm.at[pl.ds(pl.multiple_of(row0 + j * blk, SEG_ALIGN), blk)], sem_zero)
            cp.start() if go else cp.wait()
            return c

        lax.fori_loop(0, whole, whole_block, 0)
        rest = rows - whole * blk
        if go:
            copy_rows(zsrc, 0, yl_hbm, row0 + whole * blk, rest, sem_zero)
        else:
            wait_rows(zsrc, yl_hbm, rest, sem_zero)

    @pl.when(b == 0)
    def _():
        xbuf[...] = jnp.zeros_like(xbuf)
        ptr[0] = 0
        ptr[1] = 0
        gather(0)
        for go in (True, False):
            def per_tile(i, c, go=go):
                zero_tail(i, go)
                return c

            lax.fori_loop(0, n_tiles, per_tile, 0)

    n_b = blk_n_ref[b]
    wait_rows(xl_hbm, xbuf.at[slot], n_b, sem_in.at[slot])

    @pl.when(b + 1 < n_blocks)
    def _():
        gather(b + 1)

    @pl.when(b >= 2)
    def _():
        wait_rows(ybuf.at[slot], yl_hbm, blk_n_ref[b - 2], sem_out.at[slot])

    @pl.when(n_b > 0)
    def _():
        x = _unpack_rows(xbuf[slot])
        hid = jax.nn.silu(jnp.dot(x, wg_ref[0], preferred_element_type=f32)) * jnp.dot(x, wu_ref[0], preferred_element_type=f32)
        y = jnp.dot(hid.astype(bf16), wd_ref[0], preferred_element_type=f32)
        ybuf[slot] = _pack_rows(y.astype(bf16).astype(f32))
        scatter(b)

    @pl.when(b == n_blocks - 1)
    def _():
        @pl.when(b >= 1)
        def _():
            wait_rows(ybuf.at[1 - slot], yl_hbm, blk_n_ref[b - 1], sem_out.at[1 - slot])

        wait_rows(ybuf.at[slot], yl_hbm, n_b, sem_out.at[slot])


def _moe_segments(xl, tables, w_gate, w_up, w_down, *, blk, ts, n_tiles):
    blk_e = tables[0]
    n_blocks = blk_e.shape[0]
    assert blk & (blk - 1) == 0 and blk % SEG_ALIGN == 0
    wspec = lambda shape: pl.BlockSpec((1,) + shape, lambda b, be, *_: (be[b], 0, 0))
    hbm = pl.BlockSpec(memory_space=pl.ANY)
    return pl.pallas_call(
        functools.partial(_moe_seg_kernel, blk=blk, ts=ts, n_tiles=n_tiles, n_blocks=n_blocks),
        grid_spec=pltpu.PrefetchScalarGridSpec(
            num_scalar_prefetch=len(tables),
            grid=(n_blocks,),
            in_specs=[hbm, wspec((D_MODEL, D_EXPERT)), wspec((D_MODEL, D_EXPERT)), wspec((D_EXPERT, D_MODEL))],
            out_specs=hbm,
            scratch_shapes=[pltpu.VMEM((2, blk, D_MODEL // 2), jnp.uint32), pltpu.VMEM((2, blk, D_MODEL // 2), jnp.uint32),
                            pltpu.SemaphoreType.DMA((2,)), pltpu.SemaphoreType.DMA((2,)), pltpu.SemaphoreType.DMA,
                            pltpu.SMEM((2,), jnp.int32)],
        ),
        out_shape=jax.ShapeDtypeStruct(xl.shape, jnp.uint32),
        compiler_params=_cparams(("arbitrary",)),
        name="moe_experts",
    )(*tables, xl, w_gate, w_up, w_down)


def _unsort_kernel(h_ref, wc_ref, yl_ref, *rest, ts, hosted):
    hosted_in, (y_ref,), hosted_out, _ = _split_hosted(rest, 1, hosted)
    if hosted:
        _cached_attn_kernel(*hosted_in, *hosted_out, **hosted)
    w = wc_ref[...]
    yl = _unpack_rows(yl_ref[...])
    col = lax.broadcasted_iota(jnp.int32, (w.shape[0], ts), 1)
    y = h_ref[...]
    for k in range(2):
        pick = jnp.where(col == w[:, 2 + k:3 + k].astype(jnp.int32), 1.0, 0.0).astype(bf16)
        y = y + w[:, k:k + 1] * jnp.dot(pick, yl, preferred_element_type=f32)
    y_ref[...] = y


def _unsort(h, wc, yl, *, tm, ts, hosted=None):
    n = h.shape[0]
    h_ops, h_in, h_out, h_shape, h_params = _hosted_parts(hosted)
    rows = lambda r, w: pl.BlockSpec((r, w), lambda i: (i, 0))
    return pl.pallas_call(
        functools.partial(_unsort_kernel, ts=ts, hosted=h_params),
        grid=(n // tm,),
        in_specs=[rows(tm, D_MODEL), rows(tm, LANES), rows(ts, D_MODEL // 2)] + h_in,
        out_specs=[rows(tm, D_MODEL)] + h_out,
        out_shape=[jax.ShapeDtypeStruct((n, D_MODEL), f32)] + h_shape,
        compiler_params=_cparams(("arbitrary",)),
        name="moe_unsort",
    )(h, wc, yl, *h_ops)


def _mix_and_moe(x, o, l, po, sg, wts, *, tm, blk, host_merge=None, host_unsort=None):
    n = x.shape[0] * x.shape[1]
    n_tiles = n // tm
    nt = x.shape[1] // tm
    ts = _sorted_tile_rows(tm)
    hosted = host_merge(n_tiles, lambda b, i: b * nt + i) if host_merge else None
    h, xl, wc, tc, *merge_hosted = _merge(x, o, l, po, sg, wts['w_pa'], wts['w_pb'], wts['w_o'], wts['ln2'],
                                          wts['w_router'], wts['b_router'], tm=tm, hosted=hosted)
    c8 = (tc[:, ::LANES].T.astype(jnp.int32) + (SEG_ALIGN - 1)) // SEG_ALIGN * SEG_ALIGN
    seg_src = jnp.arange(n_tiles, dtype=jnp.int32)[:, None] * ts + jnp.cumsum(c8, axis=1) - c8
    seg_g = jnp.cumsum(c8, axis=0) - c8
    tot = jnp.sum(c8, axis=0)
    padded = (tot + blk - 1) // blk * blk
    pad_ends = jnp.cumsum(padded)
    n_blocks = -(-(2 * n + n_tiles * N_EXPERTS * (SEG_ALIGN - 1) + N_EXPERTS * (blk - 1)) // blk)
    blk_start = jnp.arange(n_blocks, dtype=jnp.int32) * blk
    blk_e = jnp.minimum(jnp.sum(pad_ends[None, :] <= blk_start[:, None], axis=1), N_EXPERTS - 1).astype(jnp.int32)
    pick = blk_e[:, None] == jnp.arange(N_EXPERTS, dtype=jnp.int32)[None, :]
    blk_r0 = blk_start - jnp.sum(jnp.where(pick, (pad_ends - padded)[None, :], 0), axis=1)
    blk_n = jnp.clip(jnp.sum(jnp.where(pick, tot[None, :], 0), axis=1) - blk_r0, 0, blk)
    used = jnp.sum(c8, axis=1)
    tables = tuple(a.astype(jnp.int32).reshape(-1) for a in (blk_e, blk_r0, blk_n, seg_g, c8, seg_src, used))
    yl = _moe_segments(xl, tables, wts['w_gate'], wts['w_up'], wts['w_down'], blk=blk, ts=ts, n_tiles=n_tiles)
    hosted = host_unsort(n_tiles, lambda i: i) if host_unsort else None
    y, *unsort_hosted = _unsort(h, wc, yl, tm=tm, ts=ts, hosted=hosted)
    return y, merge_hosted, unsort_hosted


def kernel(x_prompt, x_sample, cache_k_w128, cache_v_w128, cache_k_w512, cache_v_w512, cache_k_w2048, cache_v_w2048, state_pool, ln1, w_in, q_gain, k_gain, pool_lin, pool_scale, w_pa, w_pb, w_o, ln2, w_rg, b_rg, w_re, b_re, w_gate, w_up, w_down):
    B, T, D = x_prompt.shape
    Bd, S, _ = x_sample.shape
    past_len = PAST_LEN
    caches = ((cache_k_w128, cache_v_w128), (cache_k_w512, cache_v_w512), (cache_k_w2048, cache_v_w2048))
    slopes = jnp.exp2(-8.0 * jnp.arange(1, N_HEADS + 1, dtype=f32) / N_HEADS).reshape(N_GROUPS, HEADS_PER_GROUP)

    plin_bd = jnp.zeros((POOL_W, POOL_W), f32)
    for g in range(len(POOL_WINDOWS)):
        plin_bd = plin_bd.at[g * POOL_GW:(g + 1) * POOL_GW, g * POOL_GW:(g + 1) * POOL_GW].set(pool_lin[g])
    w_router = jnp.zeros((D, ROUTER_W), f32).at[:, :N_EXPERT_GROUPS].set(w_rg)
    w_router = w_router.at[:, EXPERT_COL0:EXPERT_COL0 + N_EXPERTS].set(w_re)
    b_router = jnp.zeros((1, ROUTER_W), f32).at[0, :N_EXPERT_GROUPS].set(b_rg)
    b_router = b_router.at[0, EXPERT_COL0:EXPERT_COL0 + N_EXPERTS].set(b_re)
    wts = dict(w_pa=w_pa.astype(bf16), w_pb=w_pb.astype(bf16), w_o=w_o.astype(bf16), ln2=ln2.reshape(1, D),
               w_router=w_router.astype(bf16), b_router=b_router)
    proj_w = (ln1.reshape(1, D), w_in.astype(bf16), q_gain.reshape(1, ATTN_W), k_gain.reshape(1, ATTN_W),
              plin_bd.astype(bf16), pool_scale.reshape(1, POOL_W))

    n_s = Bd * S
    xs = x_sample.transpose(1, 0, 2).reshape(n_s, D)
    qs, ks, vs, pos, st, sgs = _proj_sample(xs, *proj_w, state_pool.transpose(1, 0, 2),
                                            n_seq=Bd, n_new=S, past_len=past_len)
    pad8 = lambda a: jnp.pad(a.reshape(S, Bd, GROUP_W).transpose(1, 0, 2), ((0, 0), (0, 8 - S), (0, 0)))

    def cached_operands(g):
        cols = slice(g * GROUP_W, (g + 1) * GROUP_W)
        kc, vc = caches[g]
        w = WINDOWS[g]
        return (pad8(qs[:, cols]), pad8(ks[:, cols]), pad8(vs[:, cols]),
                kc.transpose(0, 2, 3, 1).reshape(Bd, GROUP_W, w), vc.transpose(0, 2, 3, 1).reshape(Bd, GROUP_W, w))

    def host(g):
        def parts(steps, step_of):
            if Bd % steps:
                return None
            return _cached_call_parts(*cached_operands(g), g, slopes[g], n_new=S, nb=Bd // steps, step_of=step_of)
        return parts

    tm_proj = TM_PROJ
    n_t = T // tm_proj
    hosted = host(2)(B * n_t, lambda b, i: b * n_t + i)
    outs = _proj_prompt(x_prompt, *proj_w, tm=tm_proj, hosted=hosted)
    qkv, (po, kt, vt, ut, sg) = outs[:3 * N_GROUPS], outs[3 * N_GROUPS:3 * N_GROUPS + 5]
    cached_out = {}
    if hosted:
        cached_out[2] = outs[3 * N_GROUPS + 5:]
    o, l, (wts['w_gate'], wts['w_up'], wts['w_down']) = zip(*[
        _band_attention(*qkv[3 * g:3 * g + 3], g, slopes[g], cast_w=w) for g, w in enumerate((w_gate, w_up, w_down))])
    y_prompt, in_merge, in_unsort = _mix_and_moe(x_prompt, o, l, po, sg, wts, tm=TM_TOKENS, blk=MOE_BLOCK_PROMPT,
                                                 host_merge=host(1), host_unsort=host(0))
    y_prompt = y_prompt.reshape(B, T, D)
    if in_merge:
        cached_out[1] = in_merge
    if in_unsort:
        cached_out[0] = in_unsort
    tail = kt.shape[2]
    pkv = []
    for g, w in enumerate(WINDOWS):
        for a in (kt, vt):
            a = a.reshape(B, N_HEADS, HEAD_DIM, tail)[:, g * HEADS_PER_GROUP:(g + 1) * HEADS_PER_GROUP, :, tail - w:]
            pkv.append(a.transpose(0, 3, 1, 2))
    p_pool = ut[:, 1:]

    so, sl, skv = [], [], []
    for g, w in enumerate(WINDOWS):
        if g in cached_out:
            og, lg, ko, vo = cached_out[g]
        else:
            og, lg, ko, vo = _cached_attention(*cached_operands(g), g, slopes[g], n_new=S)
        so.append(og[:, :S].transpose(1, 0, 2).reshape(1, 1, n_s, GROUP_W).astype(bf16))
        sl.append(lg[:, :S].transpose(1, 0, 2).reshape(1, 1, n_s, GROUP_W))
        for a in (ko, vo):
            skv.append(a.reshape(Bd, HEADS_PER_GROUP, HEAD_DIM, w).transpose(0, 3, 1, 2))
    y_sample = _mix_and_moe(xs[None], so, sl, pos[None], sgs[None], wts, tm=n_s, blk=MOE_BLOCK_SAMPLE)[0]
    y_sample = y_sample.reshape(S, Bd, D).transpose(1, 0, 2)
    s_pool = st.transpose(1, 0, 2)

    return (y_prompt, y_sample, *pkv, p_pool, *skv, s_pool)
```

```python
import functools

import jax
import jax.numpy as jnp
from jax import lax
from jax.experimental import pallas as pl
from jax.experimental.pallas import tpu as pltpu

D_MODEL = 1024
HEAD_DIM = 64
HEADS_PER_GROUP = 4
WINDOWS = (128, 512, 2048)
DILATIONS = (1, 4, 16)
N_GROUPS = len(WINDOWS)
N_HEADS = HEADS_PER_GROUP * N_GROUPS
ATTN_W = N_HEADS * HEAD_DIM
GROUP_W = HEADS_PER_GROUP * HEAD_DIM
BAND = 128
POOL_WINDOWS = (2, 4, 8, 16)
POOL_GW = 128
POOL_W = len(POOL_WINDOWS) * POOL_GW
POOL_STATE = max(POOL_WINDOWS) - 1
POOL_HIST = 32
assert POOL_WINDOWS == (2, 4, 8, 16)
N_EXPERT_GROUPS = 4
EXPERTS_PER_GROUP = 8
N_EXPERTS = N_EXPERT_GROUPS * EXPERTS_PER_GROUP
D_EXPERT = 512
QKVU_W = 3 * ATTN_W + POOL_W
IN_W = QKVU_W + 2 * D_MODEL
NEG = -1e30
EPS = 1e-6
LANES = 128
ROUTER_W = LANES
EXPERT_COL0 = 8
VMEM_LIMIT = 56 * 1024 * 1024
IBUF_SLOTS = 2 * (N_GROUPS - 1) * (GROUP_W // LANES)
SEG_ALIGN = 8

PAST_LEN = 8192
TM_PROJ = 256
TM_TOKENS = 512
MOE_BLOCK_PROMPT = 512
MOE_BLOCK_SAMPLE = 128

assert all(w // d == BAND for w, d in zip(WINDOWS, DILATIONS))

f32 = jnp.float32
bf16 = jnp.bfloat16


def _cparams(sem):
    return pltpu.CompilerParams(dimension_semantics=sem, vmem_limit_bytes=VMEM_LIMIT)


def _pack_rows(v):
    k = v.shape[1] // 2
    hi = lax.bitcast_convert_type(v[:, :k], jnp.uint32)
    lo = lax.bitcast_convert_type(v[:, k:], jnp.uint32)
    return hi | lax.shift_right_logical(lo, jnp.uint32(16))


def _unpack_rows(w):
    hi = lax.bitcast_convert_type(w & jnp.uint32(0xFFFF0000), f32)
    lo = lax.bitcast_convert_type(lax.shift_left(w, jnp.uint32(16)), f32)
    return jnp.concatenate([hi, lo], axis=-1).astype(bf16)


def _rmsnorm_rows(x, g):
    ms = jnp.mean(x * x, axis=-1, keepdims=True)
    return x * lax.rsqrt(ms + EPS) * g


def _head_rmsnorm_chunk(ch, gain):
    lane = lax.broadcasted_iota(jnp.int32, ch.shape, 1)
    lo_mask = lane < HEAD_DIM
    sq = ch * ch
    lo = jnp.sum(jnp.where(lo_mask, sq, 0.0), axis=-1, keepdims=True)
    hi = jnp.sum(jnp.where(lo_mask, 0.0, sq), axis=-1, keepdims=True)
    ss = jnp.where(lo_mask, lo, hi)
    return ch * lax.rsqrt(ss * (1.0 / HEAD_DIM) + EPS) * gain


def _project(x, ln1, w_ref, qg, kg):
    xn = _rmsnorm_rows(x, ln1).astype(bf16)
    nch = ATTN_W // LANES
    qs = qg * (HEAD_DIM ** -0.5)
    zq = jnp.dot(xn, w_ref[:, 0:ATTN_W], preferred_element_type=f32)
    q = [_head_rmsnorm_chunk(zq[:, c * LANES:(c + 1) * LANES], qs[:, c * LANES:(c + 1) * LANES]) for c in range(nch)]
    zk = jnp.dot(xn, w_ref[:, ATTN_W:2 * ATTN_W], preferred_element_type=f32)
    k = [_head_rmsnorm_chunk(zk[:, c * LANES:(c + 1) * LANES], kg[:, c * LANES:(c + 1) * LANES]) for c in range(nch)]
    zv = jnp.dot(xn, w_ref[:, 2 * ATTN_W:3 * ATTN_W], preferred_element_type=f32)
    v = [zv[:, c * LANES:(c + 1) * LANES] for c in range(nch)]
    u = jnp.dot(xn, w_ref[:, 3 * ATTN_W:3 * ATTN_W + POOL_W], preferred_element_type=f32)
    gates = jnp.dot(xn, w_ref[:, QKVU_W:QKVU_W + 2 * D_MODEL], preferred_element_type=f32)
    sg = (0.5 * jnp.tanh(0.5 * gates) + 0.5).astype(bf16)
    return q, k, v, u, sg


def _proj_prompt_kernel(x_ref, ln1_ref, w_ref, qg_ref, kg_ref, plin_ref, pscale_ref, *rest,
                        tm, n_tiles, tail_tiles, hosted):
    n_hosted_in, n_hosted_out = (7, 4) if hosted else (0, 0)
    hosted_in, rest = rest[:n_hosted_in], rest[n_hosted_in:]
    qkv_refs = rest[:3 * N_GROUPS]
    po_ref, kt_ref, vt_ref, ut_ref, sg_ref = rest[3 * N_GROUPS:3 * N_GROUPS + 5]
    rest = rest[3 * N_GROUPS + 5:]
    hosted_out, (ubuf, pa, pb, sbuf) = rest[:n_hosted_out], rest[n_hosted_out:]
    i = pl.program_id(1)
    hist = POOL_STATE + 1
    ph, rows = POOL_HIST, POOL_HIST + tm

    @pl.when(i == 0)
    def _():
        ubuf[0:ph, :] = jnp.zeros((ph, POOL_W), f32)

    @pl.when(i > 0)
    def _():
        ubuf[0:ph, :] = ubuf[tm:tm + ph, :]

    stages = _cached_attn_stages(*hosted_in, *hosted_out, **hosted) if hosted else ()
    for s in stages:
        s()
    q, k, v, u, sg = _project(x_ref[0], ln1_ref[...], w_ref, qg_ref[...], kg_ref[...])
    sg_ref[0] = sg
    cpg = GROUP_W // LANES
    slot = 0
    for t, chunks in enumerate((q, k, v)):
        for g, dil in enumerate(DILATIONS):
            out_ref = qkv_refs[3 * g + t]
            for c in range(cpg):
                val = chunks[g * cpg + c]
                cols = slice(c * LANES, (c + 1) * LANES)
                if dil == 1:
                    out_ref[0, 0, :, cols] = val.astype(bf16)
                else:
                    sbuf[slot * tm:(slot + 1) * tm, :] = val
                    for r in range(dil):
                        out_ref[0, r, :, cols] = sbuf[pl.ds(slot * tm + r, tm // dil, stride=dil), :].astype(bf16)
                    slot += 1

    ubuf[ph:rows, :] = u
    gw = POOL_GW
    pa[8:rows, :] = ubuf[8:rows, :] + ubuf[7:rows - 1, :]
    pb[16:rows, gw:] = pa[16:rows, gw:] + pa[14:rows - 2, gw:]
    pa[24:rows, 2 * gw:] = pb[24:rows, 2 * gw:] + pb[20:rows - 4, 2 * gw:]
    pb[32:rows, 3 * gw:] = pa[32:rows, 3 * gw:] + pa[24:rows - 8, 3 * gw:]
    pos = i * tm + lax.broadcasted_iota(jnp.int32, (tm, POOL_GW), 0)
    zs = []
    for g, w in enumerate(POOL_WINDOWS):
        cols = slice(g * POOL_GW, (g + 1) * POOL_GW)
        wsum = (pa if g % 2 == 0 else pb)[ph:rows, cols]
        cnt = jnp.minimum(pos + 1, w).astype(f32)
        zs.append(wsum / cnt - u[:, cols])
    z = jnp.concatenate(zs, axis=-1).astype(bf16)
    po = jnp.dot(z, plin_ref[...], preferred_element_type=f32) * pscale_ref[...]
    po_ref[0] = po.astype(bf16)

    @pl.when(i >= n_tiles - tail_tiles)
    def _():
        kt_ref[0] = jnp.concatenate(k, axis=-1).T
        vt_ref[0] = jnp.concatenate(v, axis=-1).T

    @pl.when(i == n_tiles - 1)
    def _():
        ut_ref[0] = ubuf[rows - hist:rows, :]


def _proj_prompt(x, ln1, w_qkvu, qg, kg, plin_bd, pscale, *, tm, hosted=None):
    B, T, D = x.shape
    n_tiles = T // tm
    tail = max(WINDOWS)
    assert T % tm == 0 and tail % tm == 0 and T >= tail
    tail_tiles = tail // tm
    hist = POOL_STATE + 1
    kern = functools.partial(_proj_prompt_kernel, tm=tm, n_tiles=n_tiles, tail_tiles=tail_tiles,
                             hosted=hosted['params'] if hosted else None)
    h_ops, h_in, h_out, h_shape = ((), [], [], []) if not hosted else (
        hosted['operands'], hosted['in_specs'], hosted['out_specs'], hosted['out_shape'])
    const = lambda b, i: (0, 0)
    assert all(tm % (16 * d) == 0 for d in DILATIONS)
    qkv_specs = [pl.BlockSpec((1, d, tm // d, GROUP_W), lambda b, i: (b, 0, i, 0)) for d in DILATIONS for _ in range(3)]
    qkv_shapes = [jax.ShapeDtypeStruct((B, d, T // d, GROUP_W), bf16) for d in DILATIONS for _ in range(3)]
    tail_spec = pl.BlockSpec((1, ATTN_W, tm), lambda b, i: (b, 0, jnp.maximum(i - (n_tiles - tail_tiles), 0)))
    return pl.pallas_call(
        kern,
        grid=(B, n_tiles),
        in_specs=[
            pl.BlockSpec((1, tm, D), lambda b, i: (b, i, 0)),
            pl.BlockSpec((1, D), const),
            pl.BlockSpec((D, IN_W), const),
            pl.BlockSpec((1, ATTN_W), const),
            pl.BlockSpec((1, ATTN_W), const),
            pl.BlockSpec((POOL_W, POOL_W), const),
            pl.BlockSpec((1, POOL_W), const),
        ] + h_in,
        out_specs=qkv_specs
        + [pl.BlockSpec((1, tm, POOL_W), lambda b, i: (b, i, 0)),
           tail_spec, tail_spec,
           pl.BlockSpec((1, hist, POOL_W), lambda b, i: (b, 0, 0)),
           pl.BlockSpec((1, tm, 2 * D), lambda b, i: (b, i, 0))] + h_out,
        out_shape=qkv_shapes
        + [jax.ShapeDtypeStruct((B, T, POOL_W), bf16),
           jax.ShapeDtypeStruct((B, ATTN_W, tail), f32),
           jax.ShapeDtypeStruct((B, ATTN_W, tail), f32),
           jax.ShapeDtypeStruct((B, hist, POOL_W), f32),
           jax.ShapeDtypeStruct((B, T, 2 * D), bf16)] + h_shape,
        scratch_shapes=[pltpu.VMEM((POOL_HIST + tm, POOL_W), f32)] * 3
                       + [pltpu.VMEM((3 * (N_GROUPS - 1) * (GROUP_W // LANES) * tm, LANES), f32)],
        compiler_params=_cparams(("arbitrary", "arbitrary")),
        name="proj_prompt",
    )(x, ln1, w_qkvu, qg, kg, plin_bd, pscale, *h_ops)


def _proj_sample_kernel(x_ref, ln1_ref, w_ref, qg_ref, kg_ref, plin_ref, pscale_ref, state_ref,
                        q_ref, k_ref, v_ref, po_ref, st_ref, sg_ref, *, n_seq, n_new, past_len):
    q, k, v, u, sg = _project(x_ref[...], ln1_ref[...], w_ref, qg_ref[...], kg_ref[...])
    sg_ref[...] = sg
    q_ref[...] = jnp.concatenate(q, axis=-1)
    k_ref[...] = jnp.concatenate(k, axis=-1)
    v_ref[...] = jnp.concatenate(v, axis=-1)
    ext = [state_ref[j] for j in range(POOL_STATE)] + [u[s * n_seq:(s + 1) * n_seq, :] for s in range(n_new)]
    for s in range(n_new):
        zs = []
        for g, w in enumerate(POOL_WINDOWS):
            cols = slice(g * POOL_GW, (g + 1) * POOL_GW)
            cur = ext[POOL_STATE + s][:, cols]
            acc = cur
            for j in range(1, w):
                acc = acc + ext[POOL_STATE + s - j][:, cols]
            cnt = float(min(past_len + s + 1, w))
            zs.append(acc / cnt - cur)
        z = jnp.concatenate(zs, axis=-1).astype(bf16)
        po = jnp.dot(z, plin_ref[...], preferred_element_type=f32) * pscale_ref[...]
        po_ref[s * n_seq:(s + 1) * n_seq, :] = po.astype(bf16)
    for j in range(POOL_STATE):
        st_ref[j] = ext[j + n_new]


def _proj_sample(x, ln1, w_qkvu, qg, kg, plin_bd, pscale, state, *, n_seq, n_new, past_len):
    n = n_seq * n_new
    kern = functools.partial(_proj_sample_kernel, n_seq=n_seq, n_new=n_new, past_len=past_len)
    return pl.pallas_call(
        kern,
        out_shape=[jax.ShapeDtypeStruct((n, ATTN_W), f32)] * 3
        + [jax.ShapeDtypeStruct((n, POOL_W), bf16),
           jax.ShapeDtypeStruct((POOL_STATE, n_seq, POOL_W), f32),
           jax.ShapeDtypeStruct((n, 2 * D_MODEL), bf16)],
        compiler_params=pltpu.CompilerParams(vmem_limit_bytes=VMEM_LIMIT),
        name="proj_sample",
    )(x, ln1, w_qkvu, qg, kg, plin_bd, pscale, state)


def _head_masks(shape):
    lane = lax.broadcasted_iota(jnp.int32, shape, len(shape) - 1)
    return [(lane >= h * HEAD_DIM) & (lane < (h + 1) * HEAD_DIM) for h in range(HEADS_PER_GROUP)]


def _band_attn_kernel(q_ref, kc_ref, kp_ref, vc_ref, vp_ref, bias_ref, *rest, tl, unroll, cast):
    if cast:
        w_ref, o_ref, l_ref, wb_ref, kbuf, vbuf = rest
        wb_ref[...] = w_ref[...].astype(bf16)
    else:
        o_ref, l_ref, kbuf, vbuf = rest
    i = pl.program_id(2)
    kbuf[0:BAND, :] = kp_ref[0, 0]
    kbuf[BAND:2 * BAND, :] = kc_ref[0, 0, 0:BAND, :]
    vbuf[0:BAND, :] = vp_ref[0, 0]
    vbuf[BAND:2 * BAND, :] = vc_ref[0, 0, 0:BAND, :]
    masks = _head_masks((BAND, GROUP_W))

    def sub_block(j, kk, vv, var):
        r0 = j * BAND if isinstance(j, int) else pl.multiple_of(j * BAND, BAND)
        q = q_ref[0, 0, pl.ds(r0, BAND), :]
        qm = jnp.concatenate([jnp.where(m, q, jnp.zeros_like(q)) for m in masks], axis=0)
        s = lax.dot_general(qm, kk, (((1,), (1,)), ((), ())), preferred_element_type=f32)
        s = s + bias_ref[var]
        m = jnp.max(s, axis=-1, keepdims=True)
        p = jnp.exp(s - m)
        den = jnp.sum(p, axis=-1, keepdims=True)
        pv = jnp.dot(p.astype(bf16), vv, preferred_element_type=f32)
        o = jnp.zeros((BAND, GROUP_W), f32)
        ms = jnp.zeros((BAND, GROUP_W), f32)
        ds = jnp.ones((BAND, GROUP_W), f32)
        for h, msk in enumerate(masks):
            rows = slice(h * BAND, (h + 1) * BAND)
            o = jnp.where(msk, pv[rows], o)
            ms = jnp.where(msk, m[rows], ms)
            ds = jnp.where(msk, den[rows], ds)
        o_ref[0, 0, pl.ds(r0, BAND), :] = (o / ds).astype(bf16)
        l_ref[0, 0, pl.ds(r0, BAND), :] = ms + jnp.log(ds)

    sub_block(0, kbuf[...], vbuf[...], jnp.where(i == 0, 0, 1))

    def body(j, carry):
        k0 = pl.multiple_of((j - 1) * BAND, BAND)
        sub_block(j, kc_ref[0, 0, pl.ds(k0, 2 * BAND), :], vc_ref[0, 0, pl.ds(k0, 2 * BAND), :], 1)
        return carry

    if tl > BAND:
        lax.fori_loop(1, tl // BAND, body, 0, unroll=unroll)


def _band_bias(slopes_g, dil):
    qi = jnp.arange(BAND)[:, None]
    kb = jnp.arange(2 * BAND)[None, :]
    rel = qi + BAND - kb
    valid = (rel >= 0) & (rel <= BAND)
    alibi = -slopes_g[:, None, None] * (dil * rel)[None].astype(f32)
    variants = []
    for first in (True, False):
        ok = valid & (kb >= BAND) if first else valid
        variants.append(jnp.where(ok[None], alibi, NEG).reshape(HEADS_PER_GROUP * BAND, 2 * BAND))
    return jnp.stack(variants, axis=0)


def _band_attention(q, k, v, g, slopes_g, *, tl_max=1024, unroll=7, cast_w=None):
    B, dil, L, _ = q.shape
    tl = min(tl_max, L)
    assert dil == DILATIONS[g] and L % tl == 0 and tl % BAND == 0
    nsub = tl // BAND
    nl = L // tl
    bias = _band_bias(slopes_g, dil)
    cur = pl.BlockSpec((1, 1, tl, GROUP_W), lambda b, r, i: (b, r, i, 0))
    prev = pl.BlockSpec((1, 1, BAND, GROUP_W), lambda b, r, i: (b, r, jnp.maximum(i * nsub - 1, 0), 0))
    host_cast = cast_w is not None and cast_w.shape[0] == B * dil * nl
    w_spec, w_ops, w_shape = [], (), []
    if host_cast:
        w_spec = [pl.BlockSpec((1,) + cast_w.shape[1:], lambda b, r, i: ((b * dil + r) * nl + i, 0, 0))]
        w_ops, w_shape = (cast_w,), [jax.ShapeDtypeStruct(cast_w.shape, bf16)]
    res = pl.pallas_call(
        functools.partial(_band_attn_kernel, tl=tl, unroll=max(1, min(unroll, nsub - 1)), cast=host_cast),
        grid=(B, dil, nl),
        in_specs=[cur, cur, prev, cur, prev,
                  pl.BlockSpec((2, HEADS_PER_GROUP * BAND, 2 * BAND), lambda b, r, i: (0, 0, 0))] + w_spec,
        out_specs=[cur, cur] + w_spec,
        out_shape=[jax.ShapeDtypeStruct((B, dil, L, GROUP_W), bf16),
                   jax.ShapeDtypeStruct((B, dil, L, GROUP_W), f32)] + w_shape,
        scratch_shapes=[pltpu.VMEM((2 * BAND, GROUP_W), bf16), pltpu.VMEM((2 * BAND, GROUP_W), bf16)],
        compiler_params=_cparams(("arbitrary", "arbitrary", "arbitrary")),
        name="band_attn_g%d" % g,
    )(q, k, k, v, v, bias, *w_ops)
    if cast_w is None:
        return res
    return res[0], res[1], (res[2] if host_cast else cast_w.astype(bf16))


def _cached_attn_kernel(*refs, **params):
    for stage in _cached_attn_stages(*refs, **params):
        stage()


def _cached_attn_stages(q_ref, kn_ref, vn_ref, kc_ref, vc_ref, bc_ref, bn_ref,
                        o_ref, l_ref, ko_ref, vo_ref, *, nb, n_new, win):
    def roll(c_ref, new_ref, out_ref):
        lane_t = lax.broadcasted_iota(jnp.int32, (nb, GROUP_W, LANES), 2)
        rolled = pltpu.roll(c_ref[...], win - n_new, axis=2)
        out_ref[...] = rolled
        new_t = jnp.swapaxes(jnp.concatenate([new_ref[...], jnp.zeros((nb, LANES - 8, GROUP_W), f32)], axis=1), 1, 2)
        new_t = pltpu.roll(new_t, LANES - n_new, axis=2)
        out_ref[:, :, win - LANES:win] = jnp.where(lane_t >= LANES - n_new, new_t, rolled[:, :, win - LANES:win])

    return (functools.partial(_cached_attn_scores, q_ref, kn_ref, vn_ref, kc_ref, vc_ref, bc_ref, bn_ref, o_ref, l_ref,
                              nb=nb, n_new=n_new),
            functools.partial(roll, kc_ref, kn_ref, ko_ref),
            functools.partial(roll, vc_ref, vn_ref, vo_ref))


def _cached_attn_scores(q_ref, kn_ref, vn_ref, kc_ref, vc_ref, bc_ref, bn_ref, o_ref, l_ref, *, nb, n_new):
    masks8 = _head_masks((nb, 8, GROUP_W))
    q8, kn8, vn8 = q_ref[...], kn_ref[...], vn_ref[...]
    kc, vc = kc_ref[...], vc_ref[...]
    qm = jnp.concatenate([jnp.where(m, q8, 0.0) for m in masks8], axis=1)
    sc = jnp.einsum('bqd,bdk->bqk', qm.astype(bf16), kc.astype(bf16), preferred_element_type=f32) + bc_ref[...]
    m = jnp.max(sc, axis=-1, keepdims=True)
    sn = []
    for t in range(n_new):
        col = jnp.sum(qm * kn8[:, t:t + 1, :], axis=-1, keepdims=True) + bn_ref[:, t:t + 1]
        sn.append(col)
        m = jnp.maximum(m, col)
    pc = jnp.exp(sc - m)
    den = jnp.sum(pc, axis=-1, keepdims=True)
    acc = jnp.einsum('bqk,bdk->bqd', pc.astype(bf16), vc.astype(bf16), preferred_element_type=f32)
    for t in range(n_new):
        pn = jnp.exp(sn[t] - m)
        den = den + pn
        acc = acc + pn * vn8[:, t:t + 1, :]
    acc = acc / den
    lse = m + jnp.log(den)
    o = jnp.zeros((nb, 8, GROUP_W), f32)
    l = jnp.zeros((nb, 8, GROUP_W), f32)
    for h, msk in enumerate(masks8):
        o = jnp.where(msk, acc[:, h * 8:(h + 1) * 8, :], o)
        l = jnp.where(msk, lse[:, h * 8:(h + 1) * 8, :], l)
    o_ref[...] = o
    l_ref[...] = l


def _cached_bias(slopes_g, dil, win, n_new):
    s = jnp.arange(8)[:, None]
    i = jnp.arange(win)[None, :]
    dist = win + s - i
    ok = (dist % dil == 0) & (dist // dil <= BAND) & (s < n_new)
    bc = jnp.where(ok[None], -slopes_g[:, None, None] * dist[None].astype(f32), NEG)
    t = jnp.arange(8)[None, :]
    dn = s - t
    okn = (dn >= 0) & (dn % dil == 0) & (dn // dil <= BAND) & (s < n_new) & (t < n_new)
    bn = jnp.where(okn[None], -slopes_g[:, None, None] * dn[None].astype(f32), NEG)
    pad = (s >= n_new)
    bc = jnp.where(pad[None], 0.0, bc)
    bn = jnp.where(pad[None], 0.0, bn)
    return bc.reshape(HEADS_PER_GROUP * 8, win), bn.reshape(HEADS_PER_GROUP * 8, 8)


def _cached_call_parts(q8, kn8, vn8, kc_t, vc_t, g, slopes_g, *, n_new, nb, step_of):
    Bd, _, win = kc_t.shape
    assert win == WINDOWS[g] and win % LANES == 0 and Bd % nb == 0
    bc, bn = _cached_bias(slopes_g, DILATIONS[g], win, n_new)
    small = pl.BlockSpec((nb, 8, GROUP_W), lambda *idx: (step_of(*idx), 0, 0))
    cache = pl.BlockSpec((nb, GROUP_W, win), lambda *idx: (step_of(*idx), 0, 0))
    const = lambda a: pl.BlockSpec(a.shape, lambda *idx: (0, 0))
    return dict(
        operands=(q8, kn8, vn8, kc_t, vc_t, bc, bn),
        in_specs=[small, small, small, cache, cache, const(bc), const(bn)],
        out_specs=[small, small, cache, cache],
        out_shape=[jax.ShapeDtypeStruct((Bd, 8, GROUP_W), f32)] * 2 + [jax.ShapeDtypeStruct((Bd, GROUP_W, win), f32)] * 2,
        params=dict(nb=nb, n_new=n_new, win=win))


def _cached_attention(q8, kn8, vn8, kc_t, vc_t, g, slopes_g, *, n_new):
    Bd, _, win = kc_t.shape
    nb = max(1, min(Bd, 2048 // win))
    parts = _cached_call_parts(q8, kn8, vn8, kc_t, vc_t, g, slopes_g, n_new=n_new, nb=nb, step_of=lambda b: b)
    return pl.pallas_call(
        functools.partial(_cached_attn_kernel, **parts['params']),
        grid=(Bd // nb,),
        in_specs=parts['in_specs'],
        out_specs=parts['out_specs'],
        out_shape=parts['out_shape'],
        compiler_params=_cparams(("arbitrary",)),
        name="cached_attn_g%d" % g,
    )(*parts['operands'])


def _split_hosted(rest, n_own_out, hosted):
    n_in, n_out = (7, 4) if hosted else (0, 0)
    hosted_in, rest = rest[:n_in], rest[n_in:]
    own_out, rest = rest[:n_own_out], rest[n_own_out:]
    return hosted_in, own_out, rest[:n_out], rest[n_out:]


def _merge_kernel(x_ref, o0_ref, o1_ref, o2_ref, l0_ref, l1_ref, l2_ref, po_ref, sg_ref,
                  wpa_ref, wpb_ref, wo_ref, ln2_ref, wr_ref, br_ref, tri_ref, ltri_ref,
                  *rest, tm, ts, dils, hosted):
    hosted_in, (h_ref, xl_ref, wc_ref, tc_ref), hosted_out, (ibuf,) = _split_hosted(rest, 4, hosted)
    if hosted:
        _cached_attn_kernel(*hosted_in, *hosted_out, **hosted)
    slots = iter(range(IBUF_SLOTS))

    def token_order(ref, dil):
        if dil == 1:
            return ref[0, 0].astype(f32)
        chunks = []
        for c in range(GROUP_W // LANES):
            base = next(slots) * tm
            for r in range(dil):
                ibuf[pl.ds(base + r, tm // dil, stride=dil), :] = ref[0, r, :, c * LANES:(c + 1) * LANES].astype(f32)
            chunks.append(ibuf[base:base + tm, :])
        return jnp.concatenate(chunks, axis=-1)

    x = x_ref[0]
    l0, l1, l2 = (token_order(r, d) for r, d in zip((l0_ref, l1_ref, l2_ref), dils))
    lm = jnp.maximum(jnp.maximum(l0, l1), l2)
    e0, e1, e2 = jnp.exp(l0 - lm), jnp.exp(l1 - lm), jnp.exp(l2 - lm)
    o0, o1, o2 = (token_order(r, d) for r, d in zip((o0_ref, o1_ref, o2_ref), dils))
    attn = (e0 * o0 + e1 * o1 + e2 * o2) / (e0 + e1 + e2)
    ma = jnp.dot(attn.astype(bf16), wpa_ref[...], preferred_element_type=f32)
    mb = jnp.dot(po_ref[0], wpb_ref[...], preferred_element_type=f32)
    mix = sg_ref[0, :, :D_MODEL].astype(f32) * ma + sg_ref[0, :, D_MODEL:].astype(f32) * mb
    h = x + jnp.dot(mix.astype(bf16), wo_ref[...], preferred_element_type=f32)
    h_ref[...] = h
    xn2 = _rmsnorm_rows(h, ln2_ref[...]).astype(bf16)

    lt = (jnp.dot(xn2, wr_ref[...], preferred_element_type=f32) + br_ref[...]).T
    row8 = lax.broadcasted_iota(jnp.int32, (8, tm), 0)
    gl = jnp.where(row8 < N_EXPERT_GROUPS, lt[0:8], -jnp.inf)
    gmax = jnp.max(gl, axis=0, keepdims=True)
    gidx = jnp.min(jnp.where(gl == gmax, row8, 8), axis=0, keepdims=True)
    pg = 1.0 / jnp.sum(jnp.exp(gl - gmax), axis=0, keepdims=True)
    sel = jnp.zeros((8, tm), f32)
    for g in range(N_EXPERT_GROUPS):
        lo = EXPERT_COL0 + g * EXPERTS_PER_GROUP
        sel = jnp.where(gidx == g, lt[lo:lo + EXPERTS_PER_GROUP], sel)
    v0 = jnp.max(sel, axis=0, keepdims=True)
    i0 = jnp.min(jnp.where(sel == v0, row8, 8), axis=0, keepdims=True)
    sel2 = jnp.where(row8 == i0, -jnp.inf, sel)
    v1 = jnp.max(sel2, axis=0, keepdims=True)
    i1 = jnp.min(jnp.where(sel2 == v1, row8, 8), axis=0, keepdims=True)
    t = jnp.exp(v1 - v0)
    w0 = pg / (1.0 + t)
    w1 = pg * t / (1.0 + t)
    eid0 = gidx * EXPERTS_PER_GROUP + i0
    eid1 = gidx * EXPERTS_PER_GROUP + i1
    erow = lax.broadcasted_iota(jnp.int32, (N_EXPERTS, tm), 0)
    oh0 = erow == eid0
    oh1 = erow == eid1
    cnt = jnp.where(oh0, 1.0, jnp.where(oh1, 1.0, 0.0))
    before = jnp.dot(cnt.astype(bf16), tri_ref[...], preferred_element_type=f32)
    tcount = jnp.sum(cnt, axis=1, keepdims=True)
    units = jnp.floor((tcount + (SEG_ALIGN - 1)) * (1.0 / SEG_ALIGN))
    ub = jnp.broadcast_to(units, (N_EXPERTS, LANES)).astype(bf16)
    seg0 = SEG_ALIGN * jnp.dot(ltri_ref[...], ub, preferred_element_type=f32)[:, 0:1]
    pos_e = seg0 + before
    lpos0 = jnp.sum(jnp.where(oh0, pos_e, 0.0), axis=0, keepdims=True)
    lpos1 = jnp.sum(jnp.where(oh1, pos_e, 0.0), axis=0, keepdims=True)
    prow = lax.broadcasted_iota(jnp.int32, (ts, tm), 0)
    perm = jnp.where(prow == lpos0.astype(jnp.int32), 1.0, jnp.where(prow == lpos1.astype(jnp.int32), 1.0, 0.0))
    xl_ref[...] = _pack_rows(jnp.dot(perm.astype(bf16), xn2, preferred_element_type=f32))
    tc_ref[...] = jnp.broadcast_to(tcount, (N_EXPERTS, LANES))
    rowl = lax.broadcasted_iota(jnp.int32, (LANES, tm), 0)
    wslab = jnp.zeros((LANES, tm), f32)
    for r, val in enumerate((w0, w1, lpos0, lpos1)):
        wslab = jnp.where(rowl == r, val, wslab)
    wc_ref[...] = wslab.T


def _hosted_parts(hosted):
    if not hosted:
        return (), [], [], [], None
    return hosted['operands'], hosted['in_specs'], hosted['out_specs'], hosted['out_shape'], hosted['params']


def _merge(x, o, l, po, sg, w_pa, w_pb, w_o, ln2, w_router, b_router, *, tm, hosted=None):
    B, T, _ = x.shape
    h_ops, h_in, h_out, h_shape, h_params = _hosted_parts(hosted)
    assert T % tm == 0
    nt = T // tm
    n = B * T
    dils = tuple(a.shape[1] for a in o)
    assert all(tm % (8 * d) == 0 for d in dils)
    assert 2 * tm // SEG_ALIGN <= 256
    ts = _sorted_tile_rows(tm)
    tri = (jnp.arange(tm)[:, None] < jnp.arange(tm)[None, :]).astype(bf16)
    ltri = (jnp.arange(N_EXPERTS)[None, :] < jnp.arange(N_EXPERTS)[:, None]).astype(bf16)
    rows3 = lambda w: pl.BlockSpec((1, tm, w), lambda b, i: (b, i, 0))
    flat = lambda r, w: pl.BlockSpec((r, w), lambda b, i: (b * nt + i, 0))
    grp = [pl.BlockSpec((1, d, tm // d, GROUP_W), lambda b, i: (b, 0, i, 0)) for d in dils]
    full = lambda a: pl.BlockSpec(a.shape, lambda b, i: (0,) * a.ndim)
    weights = (w_pa, w_pb, w_o, ln2, w_router, b_router, tri, ltri)
    return pl.pallas_call(
        functools.partial(_merge_kernel, tm=tm, ts=ts, dils=dils, hosted=h_params),
        grid=(B, nt),
        in_specs=[rows3(D_MODEL)] + grp + grp + [rows3(POOL_W), rows3(2 * D_MODEL)]
        + [full(a) for a in weights] + h_in,
        out_specs=[flat(tm, D_MODEL), flat(ts, D_MODEL // 2), flat(tm, LANES),
                   pl.BlockSpec((N_EXPERTS, LANES), lambda b, i: (0, b * nt + i))] + h_out,
        out_shape=[jax.ShapeDtypeStruct((n, D_MODEL), f32),
                   jax.ShapeDtypeStruct((B * nt * ts, D_MODEL // 2), jnp.uint32),
                   jax.ShapeDtypeStruct((n, LANES), f32),
                   jax.ShapeDtypeStruct((N_EXPERTS, B * nt * LANES), f32)] + h_shape,
        scratch_shapes=[pltpu.VMEM((IBUF_SLOTS * tm, LANES), f32)],
        compiler_params=_cparams(("arbitrary", "arbitrary")),
        name="merge_router",
    )(x, *o, *l, po, sg, *weights, *h_ops)


def _sorted_tile_rows(tm):
    return -(-(2 * tm + N_EXPERTS * (SEG_ALIGN - 1)) // LANES) * LANES


def _moe_seg_kernel(blk_e_ref, blk_r0_ref, blk_n_ref, seg_g_ref, seg_c_ref, seg_src_ref, used_ref,
                    xl_hbm, wg_ref, wu_ref, wd_ref, yl_hbm,
                    xbuf, ybuf, sem_in, sem_out, sem_zero, ptr, *, blk, ts, n_tiles, n_blocks):
    b = pl.program_id(0)
    slot = lax.rem(b, 2)
    def copy_rows(src, src_row, dst, dst_row, rows, sem):
        n = pl.multiple_of(rows, SEG_ALIGN)
        pltpu.make_async_copy(src.at[pl.ds(pl.multiple_of(src_row, SEG_ALIGN), n)],
                              dst.at[pl.ds(pl.multiple_of(dst_row, SEG_ALIGN), n)], sem).start()

    def wait_rows(src, dst, rows, sem):
        @pl.when(rows > 0)
        def _():
            n = pl.multiple_of(rows, SEG_ALIGN)
            pltpu.make_async_copy(src.at[pl.ds(0, n)], dst.at[pl.ds(0, n)], sem).wait()

    def for_pieces(bb, stream, fn):
        e, r0, n = blk_e_ref[bb], blk_r0_ref[bb], blk_n_ref[bb]

        @pl.when(n > 0)
        def _():
            def seg(i):
                return jnp.minimum(i, n_tiles - 1) * N_EXPERTS + e

            def cond(i):
                return (i < n_tiles) & (seg_g_ref[seg(i)] < r0 + n)

            def body(i):
                g = seg_g_ref[seg(i)]
                lo = jnp.maximum(g, r0)
                hi = jnp.minimum(g + seg_c_ref[seg(i)], r0 + n)

                @pl.when(hi > lo)
                def _():
                    fn(seg_src_ref[seg(i)] + (lo - g), lo - r0, hi - lo)

                return i + 1

            end = lax.while_loop(cond, body, jnp.where(r0 == 0, 0, ptr[stream]))
            ptr[stream] = jnp.maximum(end - 1, 0)

    def gather(bb):
        s = lax.rem(bb, 2)
        for_pieces(bb, 0, lambda lrow, brow, rows: copy_rows(xl_hbm, lrow, xbuf.at[s], brow, rows, sem_in.at[s]))

    def scatter(bb):
        s = lax.rem(bb, 2)
        for_pieces(bb, 1, lambda lrow, brow, rows: copy_rows(ybuf.at[s], brow, yl_hbm, lrow, rows, sem_out.at[s]))

    def zero_tail(i, go):
        row0 = i * ts + used_ref[i]
        rows = ts - used_ref[i]
        whole = lax.shift_right_logical(rows, jnp.int32(blk.bit_length() - 1))
        zsrc = xbuf.at[1]

        def whole_block(j, c):
            cp = pltpu.make_async_copy(zsrc, yl_hbm.at[pl.ds(pl.multiple_of(row0 + j * blk, SEG_ALIGN), blk)], sem_zero)
            cp.start() if go else cp.wait()
            return c

        lax.fori_loop(0, whole, whole_block, 0)
        rest = rows - whole * blk
        if go:
            @pl.when(rest > 0)
            def _():
                copy_rows(zsrc, 0, yl_hbm, row0 + whole * blk, rest, sem_zero)
        else:
            wait_rows(zsrc, yl_hbm, rest, sem_zero)

    @pl.when(b == 0)
    def _():
        xbuf[...] = jnp.zeros_like(xbuf)
        ptr[0] = 0
        ptr[1] = 0
        gather(0)
        for go in (True, False):
            def per_tile(i, c, go=go):
                zero_tail(i, go)
                return c

            lax.fori_loop(0, n_tiles, per_tile, 0)

    n_b = blk_n_ref[b]
    wait_rows(xl_hbm, xbuf.at[slot], n_b, sem_in.at[slot])

    @pl.when(b + 1 < n_blocks)
    def _():
        gather(b + 1)

    @pl.when(b >= 2)
    def _():
        wait_rows(ybuf.at[slot], yl_hbm, blk_n_ref[b - 2], sem_out.at[slot])

    @pl.when(n_b > 0)
    def _():
        x = _unpack_rows(xbuf[slot])
        hid = jax.nn.silu(jnp.dot(x, wg_ref[0], preferred_element_type=f32)) * jnp.dot(x, wu_ref[0], preferred_element_type=f32)
        y = jnp.dot(hid.astype(bf16), wd_ref[0], preferred_element_type=f32)
        ybuf[slot] = _pack_rows(y.astype(bf16).astype(f32))
        scatter(b)

    @pl.when(b == n_blocks - 1)
    def _():
        @pl.when(b >= 1)
        def _():
            wait_rows(ybuf.at[1 - slot], yl_hbm, blk_n_ref[b - 1], sem_out.at[1 - slot])

        wait_rows(ybuf.at[slot], yl_hbm, n_b, sem_out.at[slot])


def _moe_segments(xl, tables, w_gate, w_up, w_down, *, blk, ts, n_tiles):
    blk_e = tables[0]
    n_blocks = blk_e.shape[0]
    assert blk & (blk - 1) == 0 and blk % SEG_ALIGN == 0
    wspec = lambda shape: pl.BlockSpec((1,) + shape, lambda b, be, *_: (be[b], 0, 0))
    hbm = pl.BlockSpec(memory_space=pl.ANY)
    return pl.pallas_call(
        functools.partial(_moe_seg_kernel, blk=blk, ts=ts, n_tiles=n_tiles, n_blocks=n_blocks),
        grid_spec=pltpu.PrefetchScalarGridSpec(
            num_scalar_prefetch=len(tables),
            grid=(n_blocks,),
            in_specs=[hbm, wspec((D_MODEL, D_EXPERT)), wspec((D_MODEL, D_EXPERT)), wspec((D_EXPERT, D_MODEL))],
            out_specs=hbm,
            scratch_shapes=[pltpu.VMEM((2, blk, D_MODEL // 2), jnp.uint32), pltpu.VMEM((2, blk, D_MODEL // 2), jnp.uint32),
                            pltpu.SemaphoreType.DMA((2,)), pltpu.SemaphoreType.DMA((2,)), pltpu.SemaphoreType.DMA,
                            pltpu.SMEM((2,), jnp.int32)],
        ),
        out_shape=jax.ShapeDtypeStruct(xl.shape, jnp.uint32),
        compiler_params=_cparams(("arbitrary",)),
        name="moe_experts",
    )(*tables, xl, w_gate, w_up, w_down)


def _unsort_kernel(h_ref, wc_ref, yl_ref, *rest, ts, hosted):
    hosted_in, (y_ref,), hosted_out, _ = _split_hosted(rest, 1, hosted)
    if hosted:
        _cached_attn_kernel(*hosted_in, *hosted_out, **hosted)
    w = wc_ref[...]
    yl = _unpack_rows(yl_ref[...])
    col = lax.broadcasted_iota(jnp.int32, (w.shape[0], ts), 1)
    y = h_ref[...]
    for k in range(2):
        pick = jnp.where(col == w[:, 2 + k:3 + k].astype(jnp.int32), 1.0, 0.0).astype(bf16)
        y = y + w[:, k:k + 1] * jnp.dot(pick, yl, preferred_element_type=f32)
    y_ref[...] = y


def _unsort(h, wc, yl, *, tm, ts, hosted=None):
    n = h.shape[0]
    h_ops, h_in, h_out, h_shape, h_params = _hosted_parts(hosted)
    rows = lambda r, w: pl.BlockSpec((r, w), lambda i: (i, 0))
    return pl.pallas_call(
        functools.partial(_unsort_kernel, ts=ts, hosted=h_params),
        grid=(n // tm,),
        in_specs=[rows(tm, D_MODEL), rows(tm, LANES), rows(ts, D_MODEL // 2)] + h_in,
        out_specs=[rows(tm, D_MODEL)] + h_out,
        out_shape=[jax.ShapeDtypeStruct((n, D_MODEL), f32)] + h_shape,
        compiler_params=_cparams(("arbitrary",)),
        name="moe_unsort",
    )(h, wc, yl, *h_ops)


def _mix_and_moe(x, o, l, po, sg, wts, *, tm, blk, host_merge=None, host_unsort=None):
    n = x.shape[0] * x.shape[1]
    n_tiles = n // tm
    nt = x.shape[1] // tm
    ts = _sorted_tile_rows(tm)
    hosted = host_merge(n_tiles, lambda b, i: b * nt + i) if host_merge else None
    h, xl, wc, tc, *merge_hosted = _merge(x, o, l, po, sg, wts['w_pa'], wts['w_pb'], wts['w_o'], wts['ln2'],
                                          wts['w_router'], wts['b_router'], tm=tm, hosted=hosted)
    c8 = (tc[:, ::LANES].T.astype(jnp.int32) + (SEG_ALIGN - 1)) // SEG_ALIGN * SEG_ALIGN
    seg_src = jnp.arange(n_tiles, dtype=jnp.int32)[:, None] * ts + jnp.cumsum(c8, axis=1) - c8
    seg_g = jnp.cumsum(c8, axis=0) - c8
    tot = jnp.sum(c8, axis=0)
    padded = (tot + blk - 1) // blk * blk
    pad_ends = jnp.cumsum(padded)
    n_blocks = -(-(2 * n + n_tiles * N_EXPERTS * (SEG_ALIGN - 1) + N_EXPERTS * (blk - 1)) // blk)
    blk_start = jnp.arange(n_blocks, dtype=jnp.int32) * blk
    blk_e = jnp.minimum(jnp.sum(pad_ends[None, :] <= blk_start[:, None], axis=1), N_EXPERTS - 1).astype(jnp.int32)
    pick = blk_e[:, None] == jnp.arange(N_EXPERTS, dtype=jnp.int32)[None, :]
    blk_r0 = blk_start - jnp.sum(jnp.where(pick, (pad_ends - padded)[None, :], 0), axis=1)
    blk_n = jnp.clip(jnp.sum(jnp.where(pick, tot[None, :], 0), axis=1) - blk_r0, 0, blk)
    used = jnp.sum(c8, axis=1)
    tables = tuple(a.astype(jnp.int32).reshape(-1) for a in (blk_e, blk_r0, blk_n, seg_g, c8, seg_src, used))
    yl = _moe_segments(xl, tables, wts['w_gate'], wts['w_up'], wts['w_down'], blk=blk, ts=ts, n_tiles=n_tiles)
    hosted = host_unsort(n_tiles, lambda i: i) if host_unsort else None
    y, *unsort_hosted = _unsort(h, wc, yl, tm=tm, ts=ts, hosted=hosted)
    return y, merge_hosted, unsort_hosted


def kernel(x_prompt, x_sample, cache_k_w128, cache_v_w128, cache_k_w512, cache_v_w512, cache_k_w2048, cache_v_w2048, state_pool, ln1, w_in, q_gain, k_gain, pool_lin, pool_scale, w_pa, w_pb, w_o, ln2, w_rg, b_rg, w_re, b_re, w_gate, w_up, w_down):
    B, T, D = x_prompt.shape
    Bd, S, _ = x_sample.shape
    past_len = PAST_LEN
    caches = ((cache_k_w128, cache_v_w128), (cache_k_w512, cache_v_w512), (cache_k_w2048, cache_v_w2048))
    slopes = jnp.exp2(-8.0 * jnp.arange(1, N_HEADS + 1, dtype=f32) / N_HEADS).reshape(N_GROUPS, HEADS_PER_GROUP)

    plin_bd = jnp.zeros((POOL_W, POOL_W), f32)
    for g in range(len(POOL_WINDOWS)):
        plin_bd = plin_bd.at[g * POOL_GW:(g + 1) * POOL_GW, g * POOL_GW:(g + 1) * POOL_GW].set(pool_lin[g])
    w_router = jnp.zeros((D, ROUTER_W), f32).at[:, :N_EXPERT_GROUPS].set(w_rg)
    w_router = w_router.at[:, EXPERT_COL0:EXPERT_COL0 + N_EXPERTS].set(w_re)
    b_router = jnp.zeros((1, ROUTER_W), f32).at[0, :N_EXPERT_GROUPS].set(b_rg)
    b_router = b_router.at[0, EXPERT_COL0:EXPERT_COL0 + N_EXPERTS].set(b_re)
    wts = dict(w_pa=w_pa.astype(bf16), w_pb=w_pb.astype(bf16), w_o=w_o.astype(bf16), ln2=ln2.reshape(1, D),
               w_router=w_router.astype(bf16), b_router=b_router)
    proj_w = (ln1.reshape(1, D), w_in.astype(bf16), q_gain.reshape(1, ATTN_W), k_gain.reshape(1, ATTN_W),
              plin_bd.astype(bf16), pool_scale.reshape(1, POOL_W))

    n_s = Bd * S
    xs = x_sample.transpose(1, 0, 2).reshape(n_s, D)
    qs, ks, vs, pos, st, sgs = _proj_sample(xs, *proj_w, state_pool.transpose(1, 0, 2),
                                            n_seq=Bd, n_new=S, past_len=past_len)
    pad8 = lambda a: jnp.pad(a.reshape(S, Bd, GROUP_W).transpose(1, 0, 2), ((0, 0), (0, 8 - S), (0, 0)))

    def cached_operands(g):
        cols = slice(g * GROUP_W, (g + 1) * GROUP_W)
        kc, vc = caches[g]
        w = WINDOWS[g]
        return (pad8(qs[:, cols]), pad8(ks[:, cols]), pad8(vs[:, cols]),
                kc.transpose(0, 2, 3, 1).reshape(Bd, GROUP_W, w), vc.transpose(0, 2, 3, 1).reshape(Bd, GROUP_W, w))

    def host(g):
        def parts(steps, step_of):
            if Bd % steps:
                return None
            return _cached_call_parts(*cached_operands(g), g, slopes[g], n_new=S, nb=Bd // steps, step_of=step_of)
        return parts

    tm_proj = TM_PROJ
    n_t = T // tm_proj
    hosted = host(2)(B * n_t, lambda b, i: b * n_t + i)
    outs = _proj_prompt(x_prompt, *proj_w, tm=tm_proj, hosted=hosted)
    qkv, (po, kt, vt, ut, sg) = outs[:3 * N_GROUPS], outs[3 * N_GROUPS:3 * N_GROUPS + 5]
    cached_out = {}
    if hosted:
        cached_out[2] = outs[3 * N_GROUPS + 5:]
    o, l, (wts['w_gate'], wts['w_up'], wts['w_down']) = zip(*[
        _band_attention(*qkv[3 * g:3 * g + 3], g, slopes[g], cast_w=w) for g, w in enumerate((w_gate, w_up, w_down))])
    y_prompt, in_merge, in_unsort = _mix_and_moe(x_prompt, o, l, po, sg, wts, tm=TM_TOKENS, blk=MOE_BLOCK_PROMPT,
                                                 host_merge=host(1), host_unsort=host(0))
    y_prompt = y_prompt.reshape(B, T, D)
    if in_merge:
        cached_out[1] = in_merge
    if in_unsort:
        cached_out[0] = in_unsort
    tail = kt.shape[2]
    pkv = []
    for g, w in enumerate(WINDOWS):
        for a in (kt, vt):
            a = a.reshape(B, N_HEADS, HEAD_DIM, tail)[:, g * HEADS_PER_GROUP:(g + 1) * HEADS_PER_GROUP, :, tail - w:]
            pkv.append(a.transpose(0, 3, 1, 2))
    p_pool = ut[:, 1:]

    so, sl, skv = [], [], []
    for g, w in enumerate(WINDOWS):
        if g in cached_out:
            og, lg, ko, vo = cached_out[g]
        else:
            og, lg, ko, vo = _cached_attention(*cached_operands(g), g, slopes[g], n_new=S)
        so.append(og[:, :S].transpose(1, 0, 2).reshape(1, 1, n_s, GROUP_W).astype(bf16))
        sl.append(lg[:, :S].transpose(1, 0, 2).reshape(1, 1, n_s, GROUP_W))
        for a in (ko, vo):
            skv.append(a.reshape(Bd, HEADS_PER_GROUP, HEAD_DIM, w).transpose(0, 3, 1, 2))
    y_sample = _mix_and_moe(xs[None], so, sl, pos[None], sgs[None], wts, tm=n_s, blk=MOE_BLOCK_SAMPLE)[0]
    y_sample = y_sample.reshape(S, Bd, D).transpose(1, 0, 2)
    s_pool = st.transpose(1, 0, 2)

    return (y_prompt, y_sample, *pkv, p_pool, *skv, s_pool)
```

```python
import functools

import jax
import jax.numpy as jnp
from jax import lax
from jax.experimental import pallas as pl
from jax.experimental.pallas import tpu as pltpu

D_MODEL = 1024
HEAD_DIM = 64
HEADS_PER_GROUP = 4
WINDOWS = (128, 512, 2048)
DILATIONS = (1, 4, 16)
N_GROUPS = len(WINDOWS)
N_HEADS = HEADS_PER_GROUP * N_GROUPS
ATTN_W = N_HEADS * HEAD_DIM
GROUP_W = HEADS_PER_GROUP * HEAD_DIM
BAND = 128
POOL_WINDOWS = (2, 4, 8, 16)
POOL_GW = 128
POOL_W = len(POOL_WINDOWS) * POOL_GW
POOL_STATE = max(POOL_WINDOWS) - 1
POOL_HIST = 32
assert POOL_WINDOWS == (2, 4, 8, 16)
N_EXPERT_GROUPS = 4
EXPERTS_PER_GROUP = 8
N_EXPERTS = N_EXPERT_GROUPS * EXPERTS_PER_GROUP
D_EXPERT = 512
QKVU_W = 3 * ATTN_W + POOL_W
IN_W = QKVU_W + 2 * D_MODEL
NEG = -1e30
EPS = 1e-6
LANES = 128
ROUTER_W = LANES
EXPERT_COL0 = 8
VMEM_LIMIT = 56 * 1024 * 1024
IBUF_SLOTS = 2 * (N_GROUPS - 1) * (GROUP_W // LANES)
SEG_ALIGN = 8

PAST_LEN = 8192
TM_PROJ = 256
TM_TOKENS = 512
MOE_BLOCK_PROMPT = 1024
MOE_BLOCK_SAMPLE = 128

assert all(w // d == BAND for w, d in zip(WINDOWS, DILATIONS))

f32 = jnp.float32
bf16 = jnp.bfloat16


def _cparams(sem):
    return pltpu.CompilerParams(dimension_semantics=sem, vmem_limit_bytes=VMEM_LIMIT)


def _pack_rows(v):
    k = v.shape[1] // 2
    hi = lax.bitcast_convert_type(v[:, :k], jnp.uint32)
    lo = lax.bitcast_convert_type(v[:, k:], jnp.uint32)
    return hi | lax.shift_right_logical(lo, jnp.uint32(16))


def _unpack_rows(w):
    hi = lax.bitcast_convert_type(w & jnp.uint32(0xFFFF0000), f32)
    lo = lax.bitcast_convert_type(lax.shift_left(w, jnp.uint32(16)), f32)
    return jnp.concatenate([hi, lo], axis=-1).astype(bf16)


def _rmsnorm_rows(x, g):
    ms = jnp.mean(x * x, axis=-1, keepdims=True)
    return x * lax.rsqrt(ms + EPS) * g


def _head_rmsnorm_chunk(ch, gain):
    lane = lax.broadcasted_iota(jnp.int32, ch.shape, 1)
    lo_mask = lane < HEAD_DIM
    sq = ch * ch
    lo = jnp.sum(jnp.where(lo_mask, sq, 0.0), axis=-1, keepdims=True)
    hi = jnp.sum(jnp.where(lo_mask, 0.0, sq), axis=-1, keepdims=True)
    ss = jnp.where(lo_mask, lo, hi)
    return ch * lax.rsqrt(ss * (1.0 / HEAD_DIM) + EPS) * gain


def _project(x, ln1, w_ref, qg, kg):
    xn = _rmsnorm_rows(x, ln1).astype(bf16)
    nch = ATTN_W // LANES
    qs = qg * (HEAD_DIM ** -0.5)
    zq = jnp.dot(xn, w_ref[:, 0:ATTN_W], preferred_element_type=f32)
    q = [_head_rmsnorm_chunk(zq[:, c * LANES:(c + 1) * LANES], qs[:, c * LANES:(c + 1) * LANES]) for c in range(nch)]
    zk = jnp.dot(xn, w_ref[:, ATTN_W:2 * ATTN_W], preferred_element_type=f32)
    k = [_head_rmsnorm_chunk(zk[:, c * LANES:(c + 1) * LANES], kg[:, c * LANES:(c + 1) * LANES]) for c in range(nch)]
    zv = jnp.dot(xn, w_ref[:, 2 * ATTN_W:3 * ATTN_W], preferred_element_type=f32)
    v = [zv[:, c * LANES:(c + 1) * LANES] for c in range(nch)]
    u = jnp.dot(xn, w_ref[:, 3 * ATTN_W:3 * ATTN_W + POOL_W], preferred_element_type=f32)
    gates = jnp.dot(xn, w_ref[:, QKVU_W:QKVU_W + 2 * D_MODEL], preferred_element_type=f32)
    sg = (0.5 * jnp.tanh(0.5 * gates) + 0.5).astype(bf16)
    return q, k, v, u, sg


def _proj_prompt_kernel(x_ref, ln1_ref, w_ref, qg_ref, kg_ref, plin_ref, pscale_ref, *rest,
                        tm, n_tiles, tail_tiles, hosted):
    n_hosted_in, n_hosted_out = (7, 4) if hosted else (0, 0)
    hosted_in, rest = rest[:n_hosted_in], rest[n_hosted_in:]
    qkv_refs = rest[:3 * N_GROUPS]
    po_ref, kt_ref, vt_ref, ut_ref, sg_ref = rest[3 * N_GROUPS:3 * N_GROUPS + 5]
    rest = rest[3 * N_GROUPS + 5:]
    hosted_out, (ubuf, pa, pb, sbuf) = rest[:n_hosted_out], rest[n_hosted_out:]
    i = pl.program_id(1)
    hist = POOL_STATE + 1
    ph, rows = POOL_HIST, POOL_HIST + tm

    @pl.when(i == 0)
    def _():
        ubuf[0:ph, :] = jnp.zeros((ph, POOL_W), f32)

    @pl.when(i > 0)
    def _():
        ubuf[0:ph, :] = ubuf[tm:tm + ph, :]

    stages = _cached_attn_stages(*hosted_in, *hosted_out, **hosted) if hosted else ()
    for s in stages:
        s()
    q, k, v, u, sg = _project(x_ref[0], ln1_ref[...], w_ref, qg_ref[...], kg_ref[...])
    sg_ref[0] = sg
    cpg = GROUP_W // LANES
    slot = 0
    for t, chunks in enumerate((q, k, v)):
        for g, dil in enumerate(DILATIONS):
            out_ref = qkv_refs[3 * g + t]
            for c in range(cpg):
                val = chunks[g * cpg + c]
                cols = slice(c * LANES, (c + 1) * LANES)
                if dil == 1:
                    out_ref[0, 0, :, cols] = val.astype(bf16)
                else:
                    sbuf[slot * tm:(slot + 1) * tm, :] = val
                    for r in range(dil):
                        out_ref[0, r, :, cols] = sbuf[pl.ds(slot * tm + r, tm // dil, stride=dil), :].astype(bf16)
                    slot += 1

    ubuf[ph:rows, :] = u
    gw = POOL_GW
    pa[8:rows, :] = ubuf[8:rows, :] + ubuf[7:rows - 1, :]
    pb[16:rows, gw:] = pa[16:rows, gw:] + pa[14:rows - 2, gw:]
    pa[24:rows, 2 * gw:] = pb[24:rows, 2 * gw:] + pb[20:rows - 4, 2 * gw:]
    pb[32:rows, 3 * gw:] = pa[32:rows, 3 * gw:] + pa[24:rows - 8, 3 * gw:]
    pos = i * tm + lax.broadcasted_iota(jnp.int32, (tm, POOL_GW), 0)
    zs = []
    for g, w in enumerate(POOL_WINDOWS):
        cols = slice(g * POOL_GW, (g + 1) * POOL_GW)
        wsum = (pa if g % 2 == 0 else pb)[ph:rows, cols]
        cnt = jnp.minimum(pos + 1, w).astype(f32)
        zs.append(wsum / cnt - u[:, cols])
    z = jnp.concatenate(zs, axis=-1).astype(bf16)
    po = jnp.dot(z, plin_ref[...], preferred_element_type=f32) * pscale_ref[...]
    po_ref[0] = po.astype(bf16)

    @pl.when(i >= n_tiles - tail_tiles)
    def _():
        kt_ref[0] = jnp.concatenate(k, axis=-1).T
        vt_ref[0] = jnp.concatenate(v, axis=-1).T

    @pl.when(i == n_tiles - 1)
    def _():
        ut_ref[0] = ubuf[rows - hist:rows, :]


def _proj_prompt(x, ln1, w_qkvu, qg, kg, plin_bd, pscale, *, tm, hosted=None):
    B, T, D = x.shape
    n_tiles = T // tm
    tail = max(WINDOWS)
    assert T % tm == 0 and tail % tm == 0 and T >= tail
    tail_tiles = tail // tm
    hist = POOL_STATE + 1
    kern = functools.partial(_proj_prompt_kernel, tm=tm, n_tiles=n_tiles, tail_tiles=tail_tiles,
                             hosted=hosted['params'] if hosted else None)
    h_ops, h_in, h_out, h_shape = ((), [], [], []) if not hosted else (
        hosted['operands'], hosted['in_specs'], hosted['out_specs'], hosted['out_shape'])
    const = lambda b, i: (0, 0)
    assert all(tm % (16 * d) == 0 for d in DILATIONS)
    qkv_specs = [pl.BlockSpec((1, d, tm // d, GROUP_W), lambda b, i: (b, 0, i, 0)) for d in DILATIONS for _ in range(3)]
    qkv_shapes = [jax.ShapeDtypeStruct((B, d, T // d, GROUP_W), bf16) for d in DILATIONS for _ in range(3)]
    tail_spec = pl.BlockSpec((1, ATTN_W, tm), lambda b, i: (b, 0, jnp.maximum(i - (n_tiles - tail_tiles), 0)))
    return pl.pallas_call(
        kern,
        grid=(B, n_tiles),
        in_specs=[
            pl.BlockSpec((1, tm, D), lambda b, i: (b, i, 0)),
            pl.BlockSpec((1, D), const),
            pl.BlockSpec((D, IN_W), const),
            pl.BlockSpec((1, ATTN_W), const),
            pl.BlockSpec((1, ATTN_W), const),
            pl.BlockSpec((POOL_W, POOL_W), const),
            pl.BlockSpec((1, POOL_W), const),
        ] + h_in,
        out_specs=qkv_specs
        + [pl.BlockSpec((1, tm, POOL_W), lambda b, i: (b, i, 0)),
           tail_spec, tail_spec,
           pl.BlockSpec((1, hist, POOL_W), lambda b, i: (b, 0, 0)),
           pl.BlockSpec((1, tm, 2 * D), lambda b, i: (b, i, 0))] + h_out,
        out_shape=qkv_shapes
        + [jax.ShapeDtypeStruct((B, T, POOL_W), bf16),
           jax.ShapeDtypeStruct((B, ATTN_W, tail), f32),
           jax.ShapeDtypeStruct((B, ATTN_W, tail), f32),
           jax.ShapeDtypeStruct((B, hist, POOL_W), f32),
           jax.ShapeDtypeStruct((B, T, 2 * D), bf16)] + h_shape,
        scratch_shapes=[pltpu.VMEM((POOL_HIST + tm, POOL_W), f32)] * 3
                       + [pltpu.VMEM((3 * (N_GROUPS - 1) * (GROUP_W // LANES) * tm, LANES), f32)],
        compiler_params=_cparams(("arbitrary", "arbitrary")),
        name="proj_prompt",
    )(x, ln1, w_qkvu, qg, kg, plin_bd, pscale, *h_ops)


def _proj_sample_kernel(x_ref, ln1_ref, w_ref, qg_ref, kg_ref, plin_ref, pscale_ref, state_ref,
                        q_ref, k_ref, v_ref, po_ref, st_ref, sg_ref, *, n_seq, n_new, past_len):
    q, k, v, u, sg = _project(x_ref[...], ln1_ref[...], w_ref, qg_ref[...], kg_ref[...])
    sg_ref[...] = sg
    q_ref[...] = jnp.concatenate(q, axis=-1)
    k_ref[...] = jnp.concatenate(k, axis=-1)
    v_ref[...] = jnp.concatenate(v, axis=-1)
    ext = [state_ref[j] for j in range(POOL_STATE)] + [u[s * n_seq:(s + 1) * n_seq, :] for s in range(n_new)]
    for s in range(n_new):
        zs = []
        for g, w in enumerate(POOL_WINDOWS):
            cols = slice(g * POOL_GW, (g + 1) * POOL_GW)
            cur = ext[POOL_STATE + s][:, cols]
            acc = cur
            for j in range(1, w):
                acc = acc + ext[POOL_STATE + s - j][:, cols]
            cnt = float(min(past_len + s + 1, w))
            zs.append(acc / cnt - cur)
        z = jnp.concatenate(zs, axis=-1).astype(bf16)
        po = jnp.dot(z, plin_ref[...], preferred_element_type=f32) * pscale_ref[...]
        po_ref[s * n_seq:(s + 1) * n_seq, :] = po.astype(bf16)
    for j in range(POOL_STATE):
        st_ref[j] = ext[j + n_new]


def _proj_sample(x, ln1, w_qkvu, qg, kg, plin_bd, pscale, state, *, n_seq, n_new, past_len):
    n = n_seq * n_new
    kern = functools.partial(_proj_sample_kernel, n_seq=n_seq, n_new=n_new, past_len=past_len)
    return pl.pallas_call(
        kern,
        out_shape=[jax.ShapeDtypeStruct((n, ATTN_W), f32)] * 3
        + [jax.ShapeDtypeStruct((n, POOL_W), bf16),
           jax.ShapeDtypeStruct((POOL_STATE, n_seq, POOL_W), f32),
           jax.ShapeDtypeStruct((n, 2 * D_MODEL), bf16)],
        compiler_params=pltpu.CompilerParams(vmem_limit_bytes=VMEM_LIMIT),
        name="proj_sample",
    )(x, ln1, w_qkvu, qg, kg, plin_bd, pscale, state)


def _head_masks(shape):
    lane = lax.broadcasted_iota(jnp.int32, shape, len(shape) - 1)
    return [(lane >= h * HEAD_DIM) & (lane < (h + 1) * HEAD_DIM) for h in range(HEADS_PER_GROUP)]


def _band_attn_kernel(q_ref, kc_ref, kp_ref, vc_ref, vp_ref, bias_ref, *rest, tl, unroll, cast):
    if cast:
        w_ref, o_ref, l_ref, wb_ref, kbuf, vbuf = rest
        wb_ref[...] = w_ref[...].astype(bf16)
    else:
        o_ref, l_ref, kbuf, vbuf = rest
    i = pl.program_id(2)
    kbuf[0:BAND, :] = kp_ref[0, 0]
    kbuf[BAND:2 * BAND, :] = kc_ref[0, 0, 0:BAND, :]
    vbuf[0:BAND, :] = vp_ref[0, 0]
    vbuf[BAND:2 * BAND, :] = vc_ref[0, 0, 0:BAND, :]
    masks = _head_masks((BAND, GROUP_W))

    def sub_block(j, kk, vv, var):
        r0 = j * BAND if isinstance(j, int) else pl.multiple_of(j * BAND, BAND)
        q = q_ref[0, 0, pl.ds(r0, BAND), :]
        qm = jnp.concatenate([jnp.where(m, q, jnp.zeros_like(q)) for m in masks], axis=0)
        s = lax.dot_general(qm, kk, (((1,), (1,)), ((), ())), preferred_element_type=f32)
        s = s + bias_ref[var]
        m = jnp.max(s, axis=-1, keepdims=True)
        p = jnp.exp(s - m)
        den = jnp.sum(p, axis=-1, keepdims=True)
        pv = jnp.dot(p.astype(bf16), vv, preferred_element_type=f32)
        o = jnp.zeros((BAND, GROUP_W), f32)
        ms = jnp.zeros((BAND, GROUP_W), f32)
        ds = jnp.ones((BAND, GROUP_W), f32)
        for h, msk in enumerate(masks):
            rows = slice(h * BAND, (h + 1) * BAND)
            o = jnp.where(msk, pv[rows], o)
            ms = jnp.where(msk, m[rows], ms)
            ds = jnp.where(msk, den[rows], ds)
        o_ref[0, 0, pl.ds(r0, BAND), :] = (o / ds).astype(bf16)
        l_ref[0, 0, pl.ds(r0, BAND), :] = ms + jnp.log(ds)

    sub_block(0, kbuf[...], vbuf[...], jnp.where(i == 0, 0, 1))

    def body(j, carry):
        k0 = pl.multiple_of((j - 1) * BAND, BAND)
        sub_block(j, kc_ref[0, 0, pl.ds(k0, 2 * BAND), :], vc_ref[0, 0, pl.ds(k0, 2 * BAND), :], 1)
        return carry

    if tl > BAND:
        lax.fori_loop(1, tl // BAND, body, 0, unroll=unroll)


def _band_bias(slopes_g, dil):
    qi = jnp.arange(BAND)[:, None]
    kb = jnp.arange(2 * BAND)[None, :]
    rel = qi + BAND - kb
    valid = (rel >= 0) & (rel <= BAND)
    alibi = -slopes_g[:, None, None] * (dil * rel)[None].astype(f32)
    variants = []
    for first in (True, False):
        ok = valid & (kb >= BAND) if first else valid
        variants.append(jnp.where(ok[None], alibi, NEG).reshape(HEADS_PER_GROUP * BAND, 2 * BAND))
    return jnp.stack(variants, axis=0)


def _band_attention(q, k, v, g, slopes_g, *, tl_max=1024, unroll=7, cast_w=None):
    B, dil, L, _ = q.shape
    tl = min(tl_max, L)
    assert dil == DILATIONS[g] and L % tl == 0 and tl % BAND == 0
    nsub = tl // BAND
    nl = L // tl
    bias = _band_bias(slopes_g, dil)
    cur = pl.BlockSpec((1, 1, tl, GROUP_W), lambda b, r, i: (b, r, i, 0))
    prev = pl.BlockSpec((1, 1, BAND, GROUP_W), lambda b, r, i: (b, r, jnp.maximum(i * nsub - 1, 0), 0))
    host_cast = cast_w is not None and cast_w.shape[0] == B * dil * nl
    w_spec, w_ops, w_shape = [], (), []
    if host_cast:
        w_spec = [pl.BlockSpec((1,) + cast_w.shape[1:], lambda b, r, i: ((b * dil + r) * nl + i, 0, 0))]
        w_ops, w_shape = (cast_w,), [jax.ShapeDtypeStruct(cast_w.shape, bf16)]
    res = pl.pallas_call(
        functools.partial(_band_attn_kernel, tl=tl, unroll=max(1, min(unroll, nsub - 1)), cast=host_cast),
        grid=(B, dil, nl),
        in_specs=[cur, cur, prev, cur, prev,
                  pl.BlockSpec((2, HEADS_PER_GROUP * BAND, 2 * BAND), lambda b, r, i: (0, 0, 0))] + w_spec,
        out_specs=[cur, cur] + w_spec,
        out_shape=[jax.ShapeDtypeStruct((B, dil, L, GROUP_W), bf16),
                   jax.ShapeDtypeStruct((B, dil, L, GROUP_W), f32)] + w_shape,
        scratch_shapes=[pltpu.VMEM((2 * BAND, GROUP_W), bf16), pltpu.VMEM((2 * BAND, GROUP_W), bf16)],
        compiler_params=_cparams(("arbitrary", "arbitrary", "arbitrary")),
        name="band_attn_g%d" % g,
    )(q, k, k, v, v, bias, *w_ops)
    if cast_w is None:
        return res
    return res[0], res[1], (res[2] if host_cast else cast_w.astype(bf16))


def _cached_attn_kernel(*refs, **params):
    for stage in _cached_attn_stages(*refs, **params):
        stage()


def _cached_attn_stages(q_ref, kn_ref, vn_ref, kc_ref, vc_ref, bc_ref, bn_ref,
                        o_ref, l_ref, ko_ref, vo_ref, *, nb, n_new, win):
    def roll(c_ref, new_ref, out_ref):
        lane_t = lax.broadcasted_iota(jnp.int32, (nb, GROUP_W, LANES), 2)
        rolled = pltpu.roll(c_ref[...], win - n_new, axis=2)
        out_ref[...] = rolled
        new_t = jnp.swapaxes(jnp.concatenate([new_ref[...], jnp.zeros((nb, LANES - 8, GROUP_W), f32)], axis=1), 1, 2)
        new_t = pltpu.roll(new_t, LANES - n_new, axis=2)
        out_ref[:, :, win - LANES:win] = jnp.where(lane_t >= LANES - n_new, new_t, rolled[:, :, win - LANES:win])

    return (functools.partial(_cached_attn_scores, q_ref, kn_ref, vn_ref, kc_ref, vc_ref, bc_ref, bn_ref, o_ref, l_ref,
                              nb=nb, n_new=n_new),
            functools.partial(roll, kc_ref, kn_ref, ko_ref),
            functools.partial(roll, vc_ref, vn_ref, vo_ref))


def _cached_attn_scores(q_ref, kn_ref, vn_ref, kc_ref, vc_ref, bc_ref, bn_ref, o_ref, l_ref, *, nb, n_new):
    masks8 = _head_masks((nb, 8, GROUP_W))
    q8, kn8, vn8 = q_ref[...], kn_ref[...], vn_ref[...]
    kc, vc = kc_ref[...], vc_ref[...]
    qm = jnp.concatenate([jnp.where(m, q8, 0.0) for m in masks8], axis=1)
    sc = jnp.einsum('bqd,bdk->bqk', qm.astype(bf16), kc.astype(bf16), preferred_element_type=f32) + bc_ref[...]
    m = jnp.max(sc, axis=-1, keepdims=True)
    sn = []
    for t in range(n_new):
        col = jnp.sum(qm * kn8[:, t:t + 1, :], axis=-1, keepdims=True) + bn_ref[:, t:t + 1]
        sn.append(col)
        m = jnp.maximum(m, col)
    pc = jnp.exp(sc - m)
    den = jnp.sum(pc, axis=-1, keepdims=True)
    acc = jnp.einsum('bqk,bdk->bqd', pc.astype(bf16), vc.astype(bf16), preferred_element_type=f32)
    for t in range(n_new):
        pn = jnp.exp(sn[t] - m)
        den = den + pn
        acc = acc + pn * vn8[:, t:t + 1, :]
    acc = acc / den
    lse = m + jnp.log(den)
    o = jnp.zeros((nb, 8, GROUP_W), f32)
    l = jnp.zeros((nb, 8, GROUP_W), f32)
    for h, msk in enumerate(masks8):
        o = jnp.where(msk, acc[:, h * 8:(h + 1) * 8, :], o)
        l = jnp.where(msk, lse[:, h * 8:(h + 1) * 8, :], l)
    o_ref[...] = o
    l_ref[...] = l


def _cached_bias(slopes_g, dil, win, n_new):
    s = jnp.arange(8)[:, None]
    i = jnp.arange(win)[None, :]
    dist = win + s - i
    ok = (dist % dil == 0) & (dist // dil <= BAND) & (s < n_new)
    bc = jnp.where(ok[None], -slopes_g[:, None, None] * dist[None].astype(f32), NEG)
    t = jnp.arange(8)[None, :]
    dn = s - t
    okn = (dn >= 0) & (dn % dil == 0) & (dn // dil <= BAND) & (s < n_new) & (t < n_new)
    bn = jnp.where(okn[None], -slopes_g[:, None, None] * dn[None].astype(f32), NEG)
    pad = (s >= n_new)
    bc = jnp.where(pad[None], 0.0, bc)
    bn = jnp.where(pad[None], 0.0, bn)
    return bc.reshape(HEADS_PER_GROUP * 8, win), bn.reshape(HEADS_PER_GROUP * 8, 8)


def _cached_call_parts(q8, kn8, vn8, kc_t, vc_t, g, slopes_g, *, n_new, nb, step_of):
    Bd, _, win = kc_t.shape
    assert win == WINDOWS[g] and win % LANES == 0 and Bd % nb == 0
    bc, bn = _cached_bias(slopes_g, DILATIONS[g], win, n_new)
    small = pl.BlockSpec((nb, 8, GROUP_W), lambda *idx: (step_of(*idx), 0, 0))
    cache = pl.BlockSpec((nb, GROUP_W, win), lambda *idx: (step_of(*idx), 0, 0))
    const = lambda a: pl.BlockSpec(a.shape, lambda *idx: (0, 0))
    return dict(
        operands=(q8, kn8, vn8, kc_t, vc_t, bc, bn),
        in_specs=[small, small, small, cache, cache, const(bc), const(bn)],
        out_specs=[small, small, cache, cache],
        out_shape=[jax.ShapeDtypeStruct((Bd, 8, GROUP_W), f32)] * 2 + [jax.ShapeDtypeStruct((Bd, GROUP_W, win), f32)] * 2,
        params=dict(nb=nb, n_new=n_new, win=win))


def _cached_attention(q8, kn8, vn8, kc_t, vc_t, g, slopes_g, *, n_new):
    Bd, _, win = kc_t.shape
    nb = max(1, min(Bd, 2048 // win))
    parts = _cached_call_parts(q8, kn8, vn8, kc_t, vc_t, g, slopes_g, n_new=n_new, nb=nb, step_of=lambda b: b)
    return pl.pallas_call(
        functools.partial(_cached_attn_kernel, **parts['params']),
        grid=(Bd // nb,),
        in_specs=parts['in_specs'],
        out_specs=parts['out_specs'],
        out_shape=parts['out_shape'],
        compiler_params=_cparams(("arbitrary",)),
        name="cached_attn_g%d" % g,
    )(*parts['operands'])


def _split_hosted(rest, n_own_out, hosted):
    n_in, n_out = (7, 4) if hosted else (0, 0)
    hosted_in, rest = rest[:n_in], rest[n_in:]
    own_out, rest = rest[:n_own_out], rest[n_own_out:]
    return hosted_in, own_out, rest[:n_out], rest[n_out:]


def _merge_kernel(x_ref, o0_ref, o1_ref, o2_ref, l0_ref, l1_ref, l2_ref, po_ref, sg_ref,
                  wpa_ref, wpb_ref, wo_ref, ln2_ref, wr_ref, br_ref, tri_ref, ltri_ref,
                  *rest, tm, ts, dils, hosted):
    hosted_in, (h_ref, xl_ref, wc_ref, tc_ref), hosted_out, (ibuf,) = _split_hosted(rest, 4, hosted)
    if hosted:
        _cached_attn_kernel(*hosted_in, *hosted_out, **hosted)
    slots = iter(range(IBUF_SLOTS))

    def token_order(ref, dil):
        if dil == 1:
            return ref[0, 0].astype(f32)
        chunks = []
        for c in range(GROUP_W // LANES):
            base = next(slots) * tm
            for r in range(dil):
                ibuf[pl.ds(base + r, tm // dil, stride=dil), :] = ref[0, r, :, c * LANES:(c + 1) * LANES].astype(f32)
            chunks.append(ibuf[base:base + tm, :])
        return jnp.concatenate(chunks, axis=-1)

    x = x_ref[0]
    l0, l1, l2 = (token_order(r, d) for r, d in zip((l0_ref, l1_ref, l2_ref), dils))
    lm = jnp.maximum(jnp.maximum(l0, l1), l2)
    e0, e1, e2 = jnp.exp(l0 - lm), jnp.exp(l1 - lm), jnp.exp(l2 - lm)
    o0, o1, o2 = (token_order(r, d) for r, d in zip((o0_ref, o1_ref, o2_ref), dils))
    attn = (e0 * o0 + e1 * o1 + e2 * o2) / (e0 + e1 + e2)
    ma = jnp.dot(attn.astype(bf16), wpa_ref[...], preferred_element_type=f32)
    mb = jnp.dot(po_ref[0], wpb_ref[...], preferred_element_type=f32)
    mix = sg_ref[0, :, :D_MODEL].astype(f32) * ma + sg_ref[0, :, D_MODEL:].astype(f32) * mb
    h = x + jnp.dot(mix.astype(bf16), wo_ref[...], preferred_element_type=f32)
    h_ref[...] = h
    xn2 = _rmsnorm_rows(h, ln2_ref[...]).astype(bf16)

    lt = (jnp.dot(xn2, wr_ref[...], preferred_element_type=f32) + br_ref[...]).T
    row8 = lax.broadcasted_iota(jnp.int32, (8, tm), 0)
    gl = jnp.where(row8 < N_EXPERT_GROUPS, lt[0:8], -jnp.inf)
    gmax = jnp.max(gl, axis=0, keepdims=True)
    gidx = jnp.min(jnp.where(gl == gmax, row8, 8), axis=0, keepdims=True)
    pg = 1.0 / jnp.sum(jnp.exp(gl - gmax), axis=0, keepdims=True)
    sel = jnp.zeros((8, tm), f32)
    for g in range(N_EXPERT_GROUPS):
        lo = EXPERT_COL0 + g * EXPERTS_PER_GROUP
        sel = jnp.where(gidx == g, lt[lo:lo + EXPERTS_PER_GROUP], sel)
    v0 = jnp.max(sel, axis=0, keepdims=True)
    i0 = jnp.min(jnp.where(sel == v0, row8, 8), axis=0, keepdims=True)
    sel2 = jnp.where(row8 == i0, -jnp.inf, sel)
    v1 = jnp.max(sel2, axis=0, keepdims=True)
    i1 = jnp.min(jnp.where(sel2 == v1, row8, 8), axis=0, keepdims=True)
    t = jnp.exp(v1 - v0)
    w0 = pg / (1.0 + t)
    w1 = pg * t / (1.0 + t)
    eid0 = gidx * EXPERTS_PER_GROUP + i0
    eid1 = gidx * EXPERTS_PER_GROUP + i1
    erow = lax.broadcasted_iota(jnp.int32, (N_EXPERTS, tm), 0)
    oh0 = erow == eid0
    oh1 = erow == eid1
    cnt = jnp.where(oh0, 1.0, jnp.where(oh1, 1.0, 0.0))
    before = jnp.dot(cnt.astype(bf16), tri_ref[...], preferred_element_type=f32)
    tcount = jnp.sum(cnt, axis=1, keepdims=True)
    units = jnp.floor((tcount + (SEG_ALIGN - 1)) * (1.0 / SEG_ALIGN))
    ub = jnp.broadcast_to(units, (N_EXPERTS, LANES)).astype(bf16)
    seg0 = SEG_ALIGN * jnp.dot(ltri_ref[...], ub, preferred_element_type=f32)[:, 0:1]
    pos_e = seg0 + before
    lpos0 = jnp.sum(jnp.where(oh0, pos_e, 0.0), axis=0, keepdims=True)
    lpos1 = jnp.sum(jnp.where(oh1, pos_e, 0.0), axis=0, keepdims=True)
    prow = lax.broadcasted_iota(jnp.int32, (ts, tm), 0)
    perm = jnp.where(prow == lpos0.astype(jnp.int32), 1.0, jnp.where(prow == lpos1.astype(jnp.int32), 1.0, 0.0))
    xl_ref[...] = _pack_rows(jnp.dot(perm.astype(bf16), xn2, preferred_element_type=f32))
    tc_ref[...] = jnp.broadcast_to(tcount, (N_EXPERTS, LANES))
    rowl = lax.broadcasted_iota(jnp.int32, (LANES, tm), 0)
    wslab = jnp.zeros((LANES, tm), f32)
    for r, val in enumerate((w0, w1, lpos0, lpos1)):
        wslab = jnp.where(rowl == r, val, wslab)
    wc_ref[...] = wslab.T


def _hosted_parts(hosted):
    if not hosted:
        return (), [], [], [], None
    return hosted['operands'], hosted['in_specs'], hosted['out_specs'], hosted['out_shape'], hosted['params']


def _merge(x, o, l, po, sg, w_pa, w_pb, w_o, ln2, w_router, b_router, *, tm, hosted=None):
    B, T, _ = x.shape
    h_ops, h_in, h_out, h_shape, h_params = _hosted_parts(hosted)
    assert T % tm == 0
    nt = T // tm
    n = B * T
    dils = tuple(a.shape[1] for a in o)
    assert all(tm % (8 * d) == 0 for d in dils)
    assert 2 * tm // SEG_ALIGN <= 256
    ts = _sorted_tile_rows(tm)
    tri = (jnp.arange(tm)[:, None] < jnp.arange(tm)[None, :]).astype(bf16)
    ltri = (jnp.arange(N_EXPERTS)[None, :] < jnp.arange(N_EXPERTS)[:, None]).astype(bf16)
    rows3 = lambda w: pl.BlockSpec((1, tm, w), lambda b, i: (b, i, 0))
    flat = lambda r, w: pl.BlockSpec((r, w), lambda b, i: (b * nt + i, 0))
    grp = [pl.BlockSpec((1, d, tm // d, GROUP_W), lambda b, i: (b, 0, i, 0)) for d in dils]
    full = lambda a: pl.BlockSpec(a.shape, lambda b, i: (0,) * a.ndim)
    weights = (w_pa, w_pb, w_o, ln2, w_router, b_router, tri, ltri)
    return pl.pallas_call(
        functools.partial(_merge_kernel, tm=tm, ts=ts, dils=dils, hosted=h_params),
        grid=(B, nt),
        in_specs=[rows3(D_MODEL)] + grp + grp + [rows3(POOL_W), rows3(2 * D_MODEL)]
        + [full(a) for a in weights] + h_in,
        out_specs=[flat(tm, D_MODEL), flat(ts, D_MODEL // 2), flat(tm, LANES),
                   pl.BlockSpec((N_EXPERTS, LANES), lambda b, i: (0, b * nt + i))] + h_out,
        out_shape=[jax.ShapeDtypeStruct((n, D_MODEL), f32),
                   jax.ShapeDtypeStruct((B * nt * ts, D_MODEL // 2), jnp.uint32),
                   jax.ShapeDtypeStruct((n, LANES), f32),
                   jax.ShapeDtypeStruct((N_EXPERTS, B * nt * LANES), f32)] + h_shape,
        scratch_shapes=[pltpu.VMEM((IBUF_SLOTS * tm, LANES), f32)],
        compiler_params=_cparams(("arbitrary", "arbitrary")),
        name="merge_router",
    )(x, *o, *l, po, sg, *weights, *h_ops)


def _sorted_tile_rows(tm):
    return -(-(2 * tm + N_EXPERTS * (SEG_ALIGN - 1)) // LANES) * LANES


def _moe_seg_kernel(blk_e_ref, blk_r0_ref, blk_n_ref, seg_g_ref, seg_c_ref, seg_src_ref, used_ref,
                    xl_hbm, wg_ref, wu_ref, wd_ref, yl_hbm,
                    xbuf, ybuf, sem_in, sem_out, sem_zero, ptr, *, blk, ts, n_tiles, n_blocks):
    b = pl.program_id(0)
    slot = lax.rem(b, 2)

    def copy_rows(src, src_row, dst, dst_row, rows, sem):
        n = pl.multiple_of(rows, SEG_ALIGN)
        pltpu.make_async_copy(src.at[pl.ds(pl.multiple_of(src_row, SEG_ALIGN), n)],
                              dst.at[pl.ds(pl.multiple_of(dst_row, SEG_ALIGN), n)], sem).start()

    def wait_rows(src, dst, rows, sem):
        @pl.when(rows > 0)
        def _():
            n = pl.multiple_of(rows, SEG_ALIGN)
            pltpu.make_async_copy(src.at[pl.ds(0, n)], dst.at[pl.ds(0, n)], sem).wait()

    def for_pieces(bb, stream, fn):
        e, r0, n = blk_e_ref[bb], blk_r0_ref[bb], blk_n_ref[bb]

        @pl.when(n > 0)
        def _():
            def seg(i):
                return jnp.minimum(i, n_tiles - 1) * N_EXPERTS + e

            def cond(i):
                return (i < n_tiles) & (seg_g_ref[seg(i)] < r0 + n)

            def body(i):
                g = seg_g_ref[seg(i)]
                lo = jnp.maximum(g, r0)
                hi = jnp.minimum(g + seg_c_ref[seg(i)], r0 + n)

                @pl.when(hi > lo)
                def _():
                    fn(seg_src_ref[seg(i)] + (lo - g), lo - r0, hi - lo)

                return i + 1

            end = lax.while_loop(cond, body, jnp.where(r0 == 0, 0, ptr[stream]))
            ptr[stream] = jnp.maximum(end - 1, 0)

    def gather(bb):
        s = lax.rem(bb, 2)
        for_pieces(bb, 0, lambda lrow, brow, rows: copy_rows(xl_hbm, lrow, xbuf.at[s], brow, rows, sem_in.at[s]))

    def scatter(bb):
        s = lax.rem(bb, 2)
        for_pieces(bb, 1, lambda lrow, brow, rows: copy_rows(ybuf.at[s], brow, yl_hbm, lrow, rows, sem_out.at[s]))

    def zero_tail(i, go):
        row0 = i * ts + used_ref[i]
        rows = ts - used_ref[i]
        whole = lax.shift_right_logical(rows, jnp.int32(blk.bit_length() - 1))
        zsrc = xbuf.at[1]

        def whole_block(j, c):
            cp = pltpu.make_async_copy(zsrc, yl_hbm.at[pl.ds(pl.multiple_of(row0 + j * blk, SEG_ALIGN), blk)], sem_zero)
            cp.start() if go else cp.wait()
            return c

        lax.fori_loop(0, whole, whole_block, 0)
        rest = rows - whole * blk
        if go:
            @pl.when(rest > 0)
            def _():
                copy_rows(zsrc, 0, yl_hbm, row0 + whole * blk, rest, sem_zero)
        else:
            wait_rows(zsrc, yl_hbm, rest, sem_zero)

    @pl.when(b == 0)
    def _():
        xbuf[...] = jnp.zeros_like(xbuf)
        ptr[0] = 0
        ptr[1] = 0
        gather(0)
        for go in (True, False):
            def per_tile(i, c, go=go):
                zero_tail(i, go)
                return c

            lax.fori_loop(0, n_tiles, per_tile, 0)

    n_b = blk_n_ref[b]
    wait_rows(xl_hbm, xbuf.at[slot], n_b, sem_in.at[slot])

    @pl.when(b + 1 < n_blocks)
    def _():
        gather(b + 1)

    @pl.when(b >= 2)
    def _():
        wait_rows(ybuf.at[slot], yl_hbm, blk_n_ref[b - 2], sem_out.at[slot])

    @pl.when(n_b > 0)
    def _():
        x = _unpack_rows(xbuf[slot])
        hid = jax.nn.silu(jnp.dot(x, wg_ref[0], preferred_element_type=f32)) * jnp.dot(x, wu_ref[0], preferred_element_type=f32)
        y = jnp.dot(hid.astype(bf16), wd_ref[0], preferred_element_type=f32)
        ybuf[slot] = _pack_rows(y.astype(bf16).astype(f32))
        scatter(b)

    @pl.when(b == n_blocks - 1)
    def _():
        @pl.when(b >= 1)
        def _():
            wait_rows(ybuf.at[1 - slot], yl_hbm, blk_n_ref[b - 1], sem_out.at[1 - slot])

        wait_rows(ybuf.at[slot], yl_hbm, n_b, sem_out.at[slot])


def _moe_segments(xl, tables, w_gate, w_up, w_down, *, blk, ts, n_tiles):
    blk_e = tables[0]
    n_blocks = blk_e.shape[0]
    assert blk & (blk - 1) == 0 and blk % SEG_ALIGN == 0
    wspec = lambda shape: pl.BlockSpec((1,) + shape, lambda b, be, *_: (be[b], 0, 0))
    hbm = pl.BlockSpec(memory_space=pl.ANY)
    return pl.pallas_call(
        functools.partial(_moe_seg_kernel, blk=blk, ts=ts, n_tiles=n_tiles, n_blocks=n_blocks),
        grid_spec=pltpu.PrefetchScalarGridSpec(
            num_scalar_prefetch=len(tables),
            grid=(n_blocks,),
            in_specs=[hbm, wspec((D_MODEL, D_EXPERT)), wspec((D_MODEL, D_EXPERT)), wspec((D_EXPERT, D_MODEL))],
            out_specs=hbm,
            scratch_shapes=[pltpu.VMEM((2, blk, D_MODEL // 2), jnp.uint32), pltpu.VMEM((2, blk, D_MODEL // 2), jnp.uint32),
                            pltpu.SemaphoreType.DMA((2,)), pltpu.SemaphoreType.DMA((2,)), pltpu.SemaphoreType.DMA,
                            pltpu.SMEM((2,), jnp.int32)],
        ),
        out_shape=jax.ShapeDtypeStruct(xl.shape, jnp.uint32),
        compiler_params=_cparams(("arbitrary",)),
        name="moe_experts",
    )(*tables, xl, w_gate, w_up, w_down)


def _unsort_kernel(h_ref, wc_ref, yl_ref, *rest, ts, hosted):
    hosted_in, (y_ref,), hosted_out, _ = _split_hosted(rest, 1, hosted)
    if hosted:
        _cached_attn_kernel(*hosted_in, *hosted_out, **hosted)
    w = wc_ref[...]
    yl = _unpack_rows(yl_ref[...])
    col = lax.broadcasted_iota(jnp.int32, (w.shape[0], ts), 1)
    y = h_ref[...]
    for k in range(2):
        pick = jnp.where(col == w[:, 2 + k:3 + k].astype(jnp.int32), 1.0, 0.0).astype(bf16)
        y = y + w[:, k:k + 1] * jnp.dot(pick, yl, preferred_element_type=f32)
    y_ref[...] = y


def _unsort(h, wc, yl, *, tm, ts, hosted=None):
    n = h.shape[0]
    h_ops, h_in, h_out, h_shape, h_params = _hosted_parts(hosted)
    rows = lambda r, w: pl.BlockSpec((r, w), lambda i: (i, 0))
    return pl.pallas_call(
        functools.partial(_unsort_kernel, ts=ts, hosted=h_params),
        grid=(n // tm,),
        in_specs=[rows(tm, D_MODEL), rows(tm, LANES), rows(ts, D_MODEL // 2)] + h_in,
        out_specs=[rows(tm, D_MODEL)] + h_out,
        out_shape=[jax.ShapeDtypeStruct((n, D_MODEL), f32)] + h_shape,
        compiler_params=_cparams(("arbitrary",)),
        name="moe_unsort",
    )(h, wc, yl, *h_ops)


def _mix_and_moe(x, o, l, po, sg, wts, *, tm, blk, host_merge=None, host_unsort=None):
    n = x.shape[0] * x.shape[1]
    n_tiles = n // tm
    nt = x.shape[1] // tm
    ts = _sorted_tile_rows(tm)
    hosted = host_merge(n_tiles, lambda b, i: b * nt + i) if host_merge else None
    h, xl, wc, tc, *merge_hosted = _merge(x, o, l, po, sg, wts['w_pa'], wts['w_pb'], wts['w_o'], wts['ln2'],
                                          wts['w_router'], wts['b_router'], tm=tm, hosted=hosted)
    c8 = (tc[:, ::LANES].T.astype(jnp.int32) + (SEG_ALIGN - 1)) // SEG_ALIGN * SEG_ALIGN
    seg_src = jnp.arange(n_tiles, dtype=jnp.int32)[:, None] * ts + jnp.cumsum(c8, axis=1) - c8
    seg_g = jnp.cumsum(c8, axis=0) - c8
    tot = jnp.sum(c8, axis=0)
    padded = (tot + blk - 1) // blk * blk
    pad_ends = jnp.cumsum(padded)
    n_blocks = -(-(2 * n + n_tiles * N_EXPERTS * (SEG_ALIGN - 1) + N_EXPERTS * (blk - 1)) // blk)
    blk_start = jnp.arange(n_blocks, dtype=jnp.int32) * blk
    blk_e = jnp.minimum(jnp.sum(pad_ends[None, :] <= blk_start[:, None], axis=1), N_EXPERTS - 1).astype(jnp.int32)
    pick = blk_e[:, None] == jnp.arange(N_EXPERTS, dtype=jnp.int32)[None, :]
    blk_r0 = blk_start - jnp.sum(jnp.where(pick, (pad_ends - padded)[None, :], 0), axis=1)
    blk_n = jnp.clip(jnp.sum(jnp.where(pick, tot[None, :], 0), axis=1) - blk_r0, 0, blk)
    used = jnp.sum(c8, axis=1)
    tables = tuple(a.astype(jnp.int32).reshape(-1) for a in (blk_e, blk_r0, blk_n, seg_g, c8, seg_src, used))
    yl = _moe_segments(xl, tables, wts['w_gate'], wts['w_up'], wts['w_down'], blk=blk, ts=ts, n_tiles=n_tiles)
    hosted = host_unsort(n_tiles, lambda i: i) if host_unsort else None
    y, *unsort_hosted = _unsort(h, wc, yl, tm=tm, ts=ts, hosted=hosted)
    return y, merge_hosted, unsort_hosted


def kernel(x_prompt, x_sample, cache_k_w128, cache_v_w128, cache_k_w512, cache_v_w512, cache_k_w2048, cache_v_w2048, state_pool, ln1, w_in, q_gain, k_gain, pool_lin, pool_scale, w_pa, w_pb, w_o, ln2, w_rg, b_rg, w_re, b_re, w_gate, w_up, w_down):
    B, T, D = x_prompt.shape
    Bd, S, _ = x_sample.shape
    past_len = PAST_LEN
    caches = ((cache_k_w128, cache_v_w128), (cache_k_w512, cache_v_w512), (cache_k_w2048, cache_v_w2048))
    slopes = jnp.exp2(-8.0 * jnp.arange(1, N_HEADS + 1, dtype=f32) / N_HEADS).reshape(N_GROUPS, HEADS_PER_GROUP)

    plin_bd = jnp.zeros((POOL_W, POOL_W), f32)
    for g in range(len(POOL_WINDOWS)):
        plin_bd = plin_bd.at[g * POOL_GW:(g + 1) * POOL_GW, g * POOL_GW:(g + 1) * POOL_GW].set(pool_lin[g])
    w_router = jnp.zeros((D, ROUTER_W), f32).at[:, :N_EXPERT_GROUPS].set(w_rg)
    w_router = w_router.at[:, EXPERT_COL0:EXPERT_COL0 + N_EXPERTS].set(w_re)
    b_router = jnp.zeros((1, ROUTER_W), f32).at[0, :N_EXPERT_GROUPS].set(b_rg)
    b_router = b_router.at[0, EXPERT_COL0:EXPERT_COL0 + N_EXPERTS].set(b_re)
    wts = dict(w_pa=w_pa.astype(bf16), w_pb=w_pb.astype(bf16), w_o=w_o.astype(bf16), ln2=ln2.reshape(1, D),
               w_router=w_router.astype(bf16), b_router=b_router)
    proj_w = (ln1.reshape(1, D), w_in.astype(bf16), q_gain.reshape(1, ATTN_W), k_gain.reshape(1, ATTN_W),
              plin_bd.astype(bf16), pool_scale.reshape(1, POOL_W))

    n_s = Bd * S
    xs = x_sample.transpose(1, 0, 2).reshape(n_s, D)
    qs, ks, vs, pos, st, sgs = _proj_sample(xs, *proj_w, state_pool.transpose(1, 0, 2),
                                            n_seq=Bd, n_new=S, past_len=past_len)
    pad8 = lambda a: jnp.pad(a.reshape(S, Bd, GROUP_W).transpose(1, 0, 2), ((0, 0), (0, 8 - S), (0, 0)))

    def cached_operands(g):
        cols = slice(g * GROUP_W, (g + 1) * GROUP_W)
        kc, vc = caches[g]
        w = WINDOWS[g]
        return (pad8(qs[:, cols]), pad8(ks[:, cols]), pad8(vs[:, cols]),
                kc.transpose(0, 2, 3, 1).reshape(Bd, GROUP_W, w), vc.transpose(0, 2, 3, 1).reshape(Bd, GROUP_W, w))

    def host(g):
        def parts(steps, step_of):
            if Bd % steps:
                return None
            return _cached_call_parts(*cached_operands(g), g, slopes[g], n_new=S, nb=Bd // steps, step_of=step_of)
        return parts

    tm_proj = TM_PROJ
    n_t = T // tm_proj
    hosted = host(2)(B * n_t, lambda b, i: b * n_t + i)
    outs = _proj_prompt(x_prompt, *proj_w, tm=tm_proj, hosted=hosted)
    qkv, (po, kt, vt, ut, sg) = outs[:3 * N_GROUPS], outs[3 * N_GROUPS:3 * N_GROUPS + 5]
    cached_out = {}
    if hosted:
        cached_out[2] = outs[3 * N_GROUPS + 5:]
    o, l, (wts['w_gate'], wts['w_up'], wts['w_down']) = zip(*[
        _band_attention(*qkv[3 * g:3 * g + 3], g, slopes[g], cast_w=w) for g, w in enumerate((w_gate, w_up, w_down))])
    y_prompt, in_merge, in_unsort = _mix_and_moe(x_prompt, o, l, po, sg, wts, tm=TM_TOKENS, blk=MOE_BLOCK_PROMPT,
                                                 host_merge=host(1), host_unsort=host(0))
    y_prompt = y_prompt.reshape(B, T, D)
    if in_merge:
        cached_out[1] = in_merge
    if in_unsort:
        cached_out[0] = in_unsort
    tail = kt.shape[2]
    pkv = []
    for g, w in enumerate(WINDOWS):
        for a in (kt, vt):
            a = a.reshape(B, N_HEADS, HEAD_DIM, tail)[:, g * HEADS_PER_GROUP:(g + 1) * HEADS_PER_GROUP, :, tail - w:]
            pkv.append(a.transpose(0, 3, 1, 2))
    p_pool = ut[:, 1:]

    so, sl, skv = [], [], []
    for g, w in enumerate(WINDOWS):
        if g in cached_out:
            og, lg, ko, vo = cached_out[g]
        else:
            og, lg, ko, vo = _cached_attention(*cached_operands(g), g, slopes[g], n_new=S)
        so.append(og[:, :S].transpose(1, 0, 2).reshape(1, 1, n_s, GROUP_W).astype(bf16))
        sl.append(lg[:, :S].transpose(1, 0, 2).reshape(1, 1, n_s, GROUP_W))
        for a in (ko, vo):
            skv.append(a.reshape(Bd, HEADS_PER_GROUP, HEAD_DIM, w).transpose(0, 3, 1, 2))
    y_sample = _mix_and_moe(xs[None], so, sl, pos[None], sgs[None], wts, tm=n_s, blk=MOE_BLOCK_SAMPLE)[0]
    y_sample = y_sample.reshape(S, Bd, D).transpose(1, 0, 2)
    s_pool = st.transpose(1, 0, 2)

    return (y_prompt, y_sample, *pkv, p_pool, *skv, s_pool)
```

```python
import functools

import jax
import jax.numpy as jnp
from jax import lax
from jax.experimental import pallas as pl
from jax.experimental.pallas import tpu as pltpu

D_MODEL = 1024
HEAD_DIM = 64
HEADS_PER_GROUP = 4
WINDOWS = (128, 512, 2048)
DILATIONS = (1, 4, 16)
N_GROUPS = len(WINDOWS)
N_HEADS = HEADS_PER_GROUP * N_GROUPS
ATTN_W = N_HEADS * HEAD_DIM
GROUP_W = HEADS_PER_GROUP * HEAD_DIM
BAND = 128
POOL_WINDOWS = (2, 4, 8, 16)
POOL_GW = 128
POOL_W = len(POOL_WINDOWS) * POOL_GW
POOL_STATE = max(POOL_WINDOWS) - 1
POOL_HIST = 32
assert POOL_WINDOWS == (2, 4, 8, 16)
N_EXPERT_GROUPS = 4
EXPERTS_PER_GROUP = 8
N_EXPERTS = N_EXPERT_GROUPS * EXPERTS_PER_GROUP
D_EXPERT = 512
QKVU_W = 3 * ATTN_W + POOL_W
IN_W = QKVU_W + 2 * D_MODEL
NEG = -1e30
EPS = 1e-6
LANES = 128
ROUTER_W = LANES
EXPERT_COL0 = 8
VMEM_LIMIT = 56 * 1024 * 1024
IBUF_SLOTS = 2 * (N_GROUPS - 1) * (GROUP_W // LANES)
SEG_ALIGN = 8

PAST_LEN = 8192
TM_PROJ = 256
TM_TOKENS = 512
MOE_BLOCK_PROMPT = 512
MOE_BLOCK_SAMPLE = 128
CACHE_RING = 3

assert all(w // d == BAND for w, d in zip(WINDOWS, DILATIONS))

f32 = jnp.float32
bf16 = jnp.bfloat16


def _cparams(sem):
    return pltpu.CompilerParams(dimension_semantics=sem, vmem_limit_bytes=VMEM_LIMIT)


def _pack_rows(v):
    k = v.shape[1] // 2
    hi = lax.bitcast_convert_type(v[:, :k], jnp.uint32)
    lo = lax.bitcast_convert_type(v[:, k:], jnp.uint32)
    return hi | lax.shift_right_logical(lo, jnp.uint32(16))


def _unpack_rows(w):
    hi = lax.bitcast_convert_type(w & jnp.uint32(0xFFFF0000), f32)
    lo = lax.bitcast_convert_type(lax.shift_left(w, jnp.uint32(16)), f32)
    return jnp.concatenate([hi, lo], axis=-1).astype(bf16)


def _rmsnorm_rows(x, g):
    ms = jnp.mean(x * x, axis=-1, keepdims=True)
    return x * lax.rsqrt(ms + EPS) * g


def _head_rmsnorm_chunk(ch, gain):
    lane = lax.broadcasted_iota(jnp.int32, ch.shape, 1)
    lo_mask = lane < HEAD_DIM
    sq = ch * ch
    lo = jnp.sum(jnp.where(lo_mask, sq, 0.0), axis=-1, keepdims=True)
    hi = jnp.sum(jnp.where(lo_mask, 0.0, sq), axis=-1, keepdims=True)
    ss = jnp.where(lo_mask, lo, hi)
    return ch * lax.rsqrt(ss * (1.0 / HEAD_DIM) + EPS) * gain


def _project(x, ln1, w_ref, qg, kg):
    xn = _rmsnorm_rows(x, ln1).astype(bf16)
    nch = ATTN_W // LANES
    qs = qg * (HEAD_DIM ** -0.5)
    zq = jnp.dot(xn, w_ref[:, 0:ATTN_W], preferred_element_type=f32)
    q = [_head_rmsnorm_chunk(zq[:, c * LANES:(c + 1) * LANES], qs[:, c * LANES:(c + 1) * LANES]) for c in range(nch)]
    zk = jnp.dot(xn, w_ref[:, ATTN_W:2 * ATTN_W], preferred_element_type=f32)
    k = [_head_rmsnorm_chunk(zk[:, c * LANES:(c + 1) * LANES], kg[:, c * LANES:(c + 1) * LANES]) for c in range(nch)]
    zv = jnp.dot(xn, w_ref[:, 2 * ATTN_W:3 * ATTN_W], preferred_element_type=f32)
    v = [zv[:, c * LANES:(c + 1) * LANES] for c in range(nch)]
    u = jnp.dot(xn, w_ref[:, 3 * ATTN_W:3 * ATTN_W + POOL_W], preferred_element_type=f32)
    gates = jnp.dot(xn, w_ref[:, QKVU_W:QKVU_W + 2 * D_MODEL], preferred_element_type=f32)
    sg = (0.5 * jnp.tanh(0.5 * gates) + 0.5).astype(bf16)
    return q, k, v, u, sg


def _proj_prompt_kernel(x_ref, ln1_ref, w_ref, qg_ref, kg_ref, plin_ref, pscale_ref, *rest,
                        tm, n_tiles, tail_tiles, hosted):
    n_hosted_in, n_hosted_out = (7, 4) if hosted else (0, 0)
    hosted_in, rest = rest[:n_hosted_in], rest[n_hosted_in:]
    qkv_refs = rest[:3 * N_GROUPS]
    po_ref, kt_ref, vt_ref, ut_ref, sg_ref = rest[3 * N_GROUPS:3 * N_GROUPS + 5]
    rest = rest[3 * N_GROUPS + 5:]
    hosted_out, (ubuf, pa, pb, sbuf, *ring) = rest[:n_hosted_out], rest[n_hosted_out:]
    i = pl.program_id(1)
    hist = POOL_STATE + 1
    ph, rows = POOL_HIST, POOL_HIST + tm

    if hosted:
        hosted = dict(hosted)
        n_steps = hosted.pop('n_steps')
        cin, cout, sem_in, sem_out = ring
        q8_ref, kn8_ref, vn8_ref, kc_hbm, vc_hbm, bc_ref, bn_ref = hosted_in
        oc_ref, lc_ref, ko_hbm, vo_hbm = hosted_out
        step = pl.program_id(0) * n_tiles + i

        def in_copies(seq):
            sl = lax.rem(seq, CACHE_RING)
            return [pltpu.make_async_copy(src.at[pl.ds(seq, 1)], cin.at[sl, j], sem_in.at[sl])
                    for j, src in enumerate((kc_hbm, vc_hbm))]

        def out_copies(seq):
            sl = lax.rem(seq, 2)
            return [pltpu.make_async_copy(cout.at[sl, j], dst.at[pl.ds(seq, 1)], sem_out.at[sl])
                    for j, dst in enumerate((ko_hbm, vo_hbm))]

        @pl.when(step == 0)
        def _():
            for ahead in range(CACHE_RING - 1):
                for c in in_copies(jnp.int32(ahead)):
                    c.start()

        @pl.when(step + CACHE_RING - 1 < n_steps)
        def _():
            for c in in_copies(step + CACHE_RING - 1):
                c.start()

        for c in in_copies(step):
            c.wait()

        @pl.when(step >= 2)
        def _():
            for c in out_copies(step - 2):
                c.wait()

        s_in, s_out = lax.rem(step, CACHE_RING), lax.rem(step, 2)
        hosted_in = (q8_ref, kn8_ref, vn8_ref, cin.at[s_in, 0], cin.at[s_in, 1], bc_ref, bn_ref)
        hosted_out = (oc_ref, lc_ref, cout.at[s_out, 0], cout.at[s_out, 1])

    @pl.when(i == 0)
    def _():
        ubuf[0:ph, :] = jnp.zeros((ph, POOL_W), f32)

    @pl.when(i > 0)
    def _():
        ubuf[0:ph, :] = ubuf[tm:tm + ph, :]

    stages = _cached_attn_stages(*hosted_in, *hosted_out, **hosted) if hosted else ()
    for s in stages:
        s()
    q, k, v, u, sg = _project(x_ref[0], ln1_ref[...], w_ref, qg_ref[...], kg_ref[...])
    sg_ref[0] = sg
    cpg = GROUP_W // LANES
    slot = 0
    for t, chunks in enumerate((q, k, v)):
        for g, dil in enumerate(DILATIONS):
            out_ref = qkv_refs[3 * g + t]
            for c in range(cpg):
                val = chunks[g * cpg + c]
                cols = slice(c * LANES, (c + 1) * LANES)
                if dil == 1:
                    out_ref[0, 0, :, cols] = val.astype(bf16)
                else:
                    sbuf[slot * tm:(slot + 1) * tm, :] = val
                    for r in range(dil):
                        out_ref[0, r, :, cols] = sbuf[pl.ds(slot * tm + r, tm // dil, stride=dil), :].astype(bf16)
                    slot += 1

    ubuf[ph:rows, :] = u
    gw = POOL_GW
    pa[8:rows, :] = ubuf[8:rows, :] + ubuf[7:rows - 1, :]
    pb[16:rows, gw:] = pa[16:rows, gw:] + pa[14:rows - 2, gw:]
    pa[24:rows, 2 * gw:] = pb[24:rows, 2 * gw:] + pb[20:rows - 4, 2 * gw:]
    pb[32:rows, 3 * gw:] = pa[32:rows, 3 * gw:] + pa[24:rows - 8, 3 * gw:]
    pos = i * tm + lax.broadcasted_iota(jnp.int32, (tm, POOL_GW), 0)
    zs = []
    for g, w in enumerate(POOL_WINDOWS):
        cols = slice(g * POOL_GW, (g + 1) * POOL_GW)
        wsum = (pa if g % 2 == 0 else pb)[ph:rows, cols]
        cnt = jnp.minimum(pos + 1, w).astype(f32)
        zs.append(wsum / cnt - u[:, cols])
    z = jnp.concatenate(zs, axis=-1).astype(bf16)
    po = jnp.dot(z, plin_ref[...], preferred_element_type=f32) * pscale_ref[...]
    po_ref[0] = po.astype(bf16)

    if hosted:
        for c in out_copies(step):
            c.start()

        @pl.when(step == n_steps - 1)
        def _():
            for c in out_copies(step - 1) + out_copies(step):
                c.wait()

    @pl.when(i >= n_tiles - tail_tiles)
    def _():
        kt_ref[0] = jnp.concatenate(k, axis=-1).T
        vt_ref[0] = jnp.concatenate(v, axis=-1).T

    @pl.when(i == n_tiles - 1)
    def _():
        ut_ref[0] = ubuf[rows - hist:rows, :]


def _proj_prompt(x, ln1, w_qkvu, qg, kg, plin_bd, pscale, *, tm, hosted=None):
    B, T, D = x.shape
    n_tiles = T // tm
    tail = max(WINDOWS)
    assert T % tm == 0 and tail % tm == 0 and T >= tail
    tail_tiles = tail // tm
    hist = POOL_STATE + 1
    kern = functools.partial(_proj_prompt_kernel, tm=tm, n_tiles=n_tiles, tail_tiles=tail_tiles,
                             hosted=hosted['params'] if hosted else None)
    h_ops, h_in, h_out, h_shape = ((), [], [], []) if not hosted else (
        hosted['operands'], hosted['in_specs'], hosted['out_specs'], hosted['out_shape'])
    ring = []
    if hosted:
        blk_shape = (1,) + hosted['out_shape'][2].shape[1:]
        ring = [pltpu.VMEM((CACHE_RING, 2) + blk_shape, f32), pltpu.VMEM((2, 2) + blk_shape, f32),
                pltpu.SemaphoreType.DMA((CACHE_RING,)), pltpu.SemaphoreType.DMA((2,))]
    const = lambda b, i: (0, 0)
    assert all(tm % (16 * d) == 0 for d in DILATIONS)
    qkv_specs = [pl.BlockSpec((1, d, tm // d, GROUP_W), lambda b, i: (b, 0, i, 0)) for d in DILATIONS for _ in range(3)]
    qkv_shapes = [jax.ShapeDtypeStruct((B, d, T // d, GROUP_W), bf16) for d in DILATIONS for _ in range(3)]
    tail_spec = pl.BlockSpec((1, ATTN_W, tm), lambda b, i: (b, 0, jnp.maximum(i - (n_tiles - tail_tiles), 0)))
    return pl.pallas_call(
        kern,
        grid=(B, n_tiles),
        in_specs=[
            pl.BlockSpec((1, tm, D), lambda b, i: (b, i, 0)),
            pl.BlockSpec((1, D), const),
            pl.BlockSpec((D, IN_W), const),
            pl.BlockSpec((1, ATTN_W), const),
            pl.BlockSpec((1, ATTN_W), const),
            pl.BlockSpec((POOL_W, POOL_W), const),
            pl.BlockSpec((1, POOL_W), const),
        ] + h_in,
        out_specs=qkv_specs
        + [pl.BlockSpec((1, tm, POOL_W), lambda b, i: (b, i, 0)),
           tail_spec, tail_spec,
           pl.BlockSpec((1, hist, POOL_W), lambda b, i: (b, 0, 0)),
           pl.BlockSpec((1, tm, 2 * D), lambda b, i: (b, i, 0))] + h_out,
        out_shape=qkv_shapes
        + [jax.ShapeDtypeStruct((B, T, POOL_W), bf16),
           jax.ShapeDtypeStruct((B, ATTN_W, tail), f32),
           jax.ShapeDtypeStruct((B, ATTN_W, tail), f32),
           jax.ShapeDtypeStruct((B, hist, POOL_W), f32),
           jax.ShapeDtypeStruct((B, T, 2 * D), bf16)] + h_shape,
        scratch_shapes=[pltpu.VMEM((POOL_HIST + tm, POOL_W), f32)] * 3
                       + [pltpu.VMEM((3 * (N_GROUPS - 1) * (GROUP_W // LANES) * tm, LANES), f32)] + ring,
        compiler_params=_cparams(("arbitrary", "arbitrary")),
        name="proj_prompt",
    )(x, ln1, w_qkvu, qg, kg, plin_bd, pscale, *h_ops)


def _proj_sample_kernel(x_ref, ln1_ref, w_ref, qg_ref, kg_ref, plin_ref, pscale_ref, state_ref,
                        q_ref, k_ref, v_ref, po_ref, st_ref, sg_ref, *, n_seq, n_new, past_len):
    q, k, v, u, sg = _project(x_ref[...], ln1_ref[...], w_ref, qg_ref[...], kg_ref[...])
    sg_ref[...] = sg
    q_ref[...] = jnp.concatenate(q, axis=-1)
    k_ref[...] = jnp.concatenate(k, axis=-1)
    v_ref[...] = jnp.concatenate(v, axis=-1)
    ext = [state_ref[j] for j in range(POOL_STATE)] + [u[s * n_seq:(s + 1) * n_seq, :] for s in range(n_new)]
    for s in range(n_new):
        zs = []
        for g, w in enumerate(POOL_WINDOWS):
            cols = slice(g * POOL_GW, (g + 1) * POOL_GW)
            cur = ext[POOL_STATE + s][:, cols]
            acc = cur
            for j in range(1, w):
                acc = acc + ext[POOL_STATE + s - j][:, cols]
            cnt = float(min(past_len + s + 1, w))
            zs.append(acc / cnt - cur)
        z = jnp.concatenate(zs, axis=-1).astype(bf16)
        po = jnp.dot(z, plin_ref[...], preferred_element_type=f32) * pscale_ref[...]
        po_ref[s * n_seq:(s + 1) * n_seq, :] = po.astype(bf16)
    for j in range(POOL_STATE):
        st_ref[j] = ext[j + n_new]


def _proj_sample(x, ln1, w_qkvu, qg, kg, plin_bd, pscale, state, *, n_seq, n_new, past_len):
    n = n_seq * n_new
    kern = functools.partial(_proj_sample_kernel, n_seq=n_seq, n_new=n_new, past_len=past_len)
    return pl.pallas_call(
        kern,
        out_shape=[jax.ShapeDtypeStruct((n, ATTN_W), f32)] * 3
        + [jax.ShapeDtypeStruct((n, POOL_W), bf16),
           jax.ShapeDtypeStruct((POOL_STATE, n_seq, POOL_W), f32),
           jax.ShapeDtypeStruct((n, 2 * D_MODEL), bf16)],
        compiler_params=pltpu.CompilerParams(vmem_limit_bytes=VMEM_LIMIT),
        name="proj_sample",
    )(x, ln1, w_qkvu, qg, kg, plin_bd, pscale, state)


def _head_masks(shape):
    lane = lax.broadcasted_iota(jnp.int32, shape, len(shape) - 1)
    return [(lane >= h * HEAD_DIM) & (lane < (h + 1) * HEAD_DIM) for h in range(HEADS_PER_GROUP)]


def _band_attn_kernel(q_ref, kc_ref, kp_ref, vc_ref, vp_ref, bias_ref, *rest, tl, unroll, cast):
    if cast:
        w_ref, o_ref, l_ref, wb_ref, kbuf, vbuf = rest
        wb_ref[...] = w_ref[...].astype(bf16)
    else:
        o_ref, l_ref, kbuf, vbuf = rest
    i = pl.program_id(2)
    kbuf[0:BAND, :] = kp_ref[0, 0]
    kbuf[BAND:2 * BAND, :] = kc_ref[0, 0, 0:BAND, :]
    vbuf[0:BAND, :] = vp_ref[0, 0]
    vbuf[BAND:2 * BAND, :] = vc_ref[0, 0, 0:BAND, :]
    masks = _head_masks((BAND, GROUP_W))

    def sub_block(j, kk, vv, var):
        r0 = j * BAND if isinstance(j, int) else pl.multiple_of(j * BAND, BAND)
        q = q_ref[0, 0, pl.ds(r0, BAND), :]
        qm = jnp.concatenate([jnp.where(m, q, jnp.zeros_like(q)) for m in masks], axis=0)
        s = lax.dot_general(qm, kk, (((1,), (1,)), ((), ())), preferred_element_type=f32)
        s = s + bias_ref[var]
        m = jnp.max(s, axis=-1, keepdims=True)
        p = jnp.exp(s - m)
        den = jnp.sum(p, axis=-1, keepdims=True)
        pv = jnp.dot(p.astype(bf16), vv, preferred_element_type=f32)
        o = jnp.zeros((BAND, GROUP_W), f32)
        ms = jnp.zeros((BAND, GROUP_W), f32)
        ds = jnp.ones((BAND, GROUP_W), f32)
        for h, msk in enumerate(masks):
            rows = slice(h * BAND, (h + 1) * BAND)
            o = jnp.where(msk, pv[rows], o)
            ms = jnp.where(msk, m[rows], ms)
            ds = jnp.where(msk, den[rows], ds)
        o_ref[0, 0, pl.ds(r0, BAND), :] = (o / ds).astype(bf16)
        l_ref[0, 0, pl.ds(r0, BAND), :] = ms + jnp.log(ds)

    sub_block(0, kbuf[...], vbuf[...], jnp.where(i == 0, 0, 1))

    def body(j, carry):
        k0 = pl.multiple_of((j - 1) * BAND, BAND)
        sub_block(j, kc_ref[0, 0, pl.ds(k0, 2 * BAND), :], vc_ref[0, 0, pl.ds(k0, 2 * BAND), :], 1)
        return carry

    if tl > BAND:
        lax.fori_loop(1, tl // BAND, body, 0, unroll=unroll)


def _band_bias(slopes_g, dil):
    qi = jnp.arange(BAND)[:, None]
    kb = jnp.arange(2 * BAND)[None, :]
    rel = qi + BAND - kb
    valid = (rel >= 0) & (rel <= BAND)
    alibi = -slopes_g[:, None, None] * (dil * rel)[None].astype(f32)
    variants = []
    for first in (True, False):
        ok = valid & (kb >= BAND) if first else valid
        variants.append(jnp.where(ok[None], alibi, NEG).reshape(HEADS_PER_GROUP * BAND, 2 * BAND))
    return jnp.stack(variants, axis=0)


def _band_attention(q, k, v, g, slopes_g, *, tl_max=1024, unroll=7, cast_w=None):
    B, dil, L, _ = q.shape
    tl = min(tl_max, L)
    assert dil == DILATIONS[g] and L % tl == 0 and tl % BAND == 0
    nsub = tl // BAND
    nl = L // tl
    bias = _band_bias(slopes_g, dil)
    cur = pl.BlockSpec((1, 1, tl, GROUP_W), lambda b, r, i: (b, r, i, 0))
    prev = pl.BlockSpec((1, 1, BAND, GROUP_W), lambda b, r, i: (b, r, jnp.maximum(i * nsub - 1, 0), 0))
    host_cast = cast_w is not None and cast_w.shape[0] == B * dil * nl
    w_spec, w_ops, w_shape = [], (), []
    if host_cast:
        w_spec = [pl.BlockSpec((1,) + cast_w.shape[1:], lambda b, r, i: ((b * dil + r) * nl + i, 0, 0))]
        w_ops, w_shape = (cast_w,), [jax.ShapeDtypeStruct(cast_w.shape, bf16)]
    res = pl.pallas_call(
        functools.partial(_band_attn_kernel, tl=tl, unroll=max(1, min(unroll, nsub - 1)), cast=host_cast),
        grid=(B, dil, nl),
        in_specs=[cur, cur, prev, cur, prev,
                  pl.BlockSpec((2, HEADS_PER_GROUP * BAND, 2 * BAND), lambda b, r, i: (0, 0, 0))] + w_spec,
        out_specs=[cur, cur] + w_spec,
        out_shape=[jax.ShapeDtypeStruct((B, dil, L, GROUP_W), bf16),
                   jax.ShapeDtypeStruct((B, dil, L, GROUP_W), f32)] + w_shape,
        scratch_shapes=[pltpu.VMEM((2 * BAND, GROUP_W), bf16), pltpu.VMEM((2 * BAND, GROUP_W), bf16)],
        compiler_params=_cparams(("arbitrary", "arbitrary", "arbitrary")),
        name="band_attn_g%d" % g,
    )(q, k, k, v, v, bias, *w_ops)
    if cast_w is None:
        return res
    return res[0], res[1], (res[2] if host_cast else cast_w.astype(bf16))


def _cached_attn_kernel(*refs, **params):
    for stage in _cached_attn_stages(*refs, **params):
        stage()


def _cached_attn_stages(q_ref, kn_ref, vn_ref, kc_ref, vc_ref, bc_ref, bn_ref,
                        o_ref, l_ref, ko_ref, vo_ref, *, nb, n_new, win):
    def roll(c_ref, new_ref, out_ref):
        lane_t = lax.broadcasted_iota(jnp.int32, (nb, GROUP_W, LANES), 2)
        rolled = pltpu.roll(c_ref[...], win - n_new, axis=2)
        out_ref[...] = rolled
        new_t = jnp.swapaxes(jnp.concatenate([new_ref[...], jnp.zeros((nb, LANES - 8, GROUP_W), f32)], axis=1), 1, 2)
        new_t = pltpu.roll(new_t, LANES - n_new, axis=2)
        out_ref[:, :, win - LANES:win] = jnp.where(lane_t >= LANES - n_new, new_t, rolled[:, :, win - LANES:win])

    return (functools.partial(_cached_attn_scores, q_ref, kn_ref, vn_ref, kc_ref, vc_ref, bc_ref, bn_ref, o_ref, l_ref,
                              nb=nb, n_new=n_new),
            functools.partial(roll, kc_ref, kn_ref, ko_ref),
            functools.partial(roll, vc_ref, vn_ref, vo_ref))


def _cached_attn_scores(q_ref, kn_ref, vn_ref, kc_ref, vc_ref, bc_ref, bn_ref, o_ref, l_ref, *, nb, n_new):
    masks8 = _head_masks((nb, 8, GROUP_W))
    q8, kn8, vn8 = q_ref[...], kn_ref[...], vn_ref[...]
    kc, vc = kc_ref[...], vc_ref[...]
    qm = jnp.concatenate([jnp.where(m, q8, 0.0) for m in masks8], axis=1)
    sc = jnp.einsum('bqd,bdk->bqk', qm.astype(bf16), kc.astype(bf16), preferred_element_type=f32) + bc_ref[...]
    m = jnp.max(sc, axis=-1, keepdims=True)
    sn = []
    for t in range(n_new):
        col = jnp.sum(qm * kn8[:, t:t + 1, :], axis=-1, keepdims=True) + bn_ref[:, t:t + 1]
        sn.append(col)
        m = jnp.maximum(m, col)
    pc = jnp.exp(sc - m)
    den = jnp.sum(pc, axis=-1, keepdims=True)
    acc = jnp.einsum('bqk,bdk->bqd', pc.astype(bf16), vc.astype(bf16), preferred_element_type=f32)
    for t in range(n_new):
        pn = jnp.exp(sn[t] - m)
        den = den + pn
        acc = acc + pn * vn8[:, t:t + 1, :]
    acc = acc / den
    lse = m + jnp.log(den)
    o = jnp.zeros((nb, 8, GROUP_W), f32)
    l = jnp.zeros((nb, 8, GROUP_W), f32)
    for h, msk in enumerate(masks8):
        o = jnp.where(msk, acc[:, h * 8:(h + 1) * 8, :], o)
        l = jnp.where(msk, lse[:, h * 8:(h + 1) * 8, :], l)
    o_ref[...] = o
    l_ref[...] = l


def _cached_bias(slopes_g, dil, win, n_new):
    s = jnp.arange(8)[:, None]
    i = jnp.arange(win)[None, :]
    dist = win + s - i
    ok = (dist % dil == 0) & (dist // dil <= BAND) & (s < n_new)
    bc = jnp.where(ok[None], -slopes_g[:, None, None] * dist[None].astype(f32), NEG)
    t = jnp.arange(8)[None, :]
    dn = s - t
    okn = (dn >= 0) & (dn % dil == 0) & (dn // dil <= BAND) & (s < n_new) & (t < n_new)
    bn = jnp.where(okn[None], -slopes_g[:, None, None] * dn[None].astype(f32), NEG)
    pad = (s >= n_new)
    bc = jnp.where(pad[None], 0.0, bc)
    bn = jnp.where(pad[None], 0.0, bn)
    return bc.reshape(HEADS_PER_GROUP * 8, win), bn.reshape(HEADS_PER_GROUP * 8, 8)


def _cached_call_parts(q8, kn8, vn8, kc_t, vc_t, g, slopes_g, *, n_new, nb, step_of, ring_steps=None):
    Bd, _, win = kc_t.shape
    assert win == WINDOWS[g] and win % LANES == 0 and Bd % nb == 0
    bc, bn = _cached_bias(slopes_g, DILATIONS[g], win, n_new)
    small = pl.BlockSpec((nb, 8, GROUP_W), lambda *idx: (step_of(*idx), 0, 0))
    cache = pl.BlockSpec((nb, GROUP_W, win), lambda *idx: (step_of(*idx), 0, 0))
    params = dict(nb=nb, n_new=n_new, win=win)
    if ring_steps is not None:
        assert nb == 1 and ring_steps == Bd
        cache = pl.BlockSpec(memory_space=pl.ANY)
        params['n_steps'] = ring_steps
    const = lambda a: pl.BlockSpec(a.shape, lambda *idx: (0, 0))
    return dict(
        operands=(q8, kn8, vn8, kc_t, vc_t, bc, bn),
        in_specs=[small, small, small, cache, cache, const(bc), const(bn)],
        out_specs=[small, small, cache, cache],
        out_shape=[jax.ShapeDtypeStruct((Bd, 8, GROUP_W), f32)] * 2 + [jax.ShapeDtypeStruct((Bd, GROUP_W, win), f32)] * 2,
        params=params)


def _cached_attention(q8, kn8, vn8, kc_t, vc_t, g, slopes_g, *, n_new):
    Bd, _, win = kc_t.shape
    nb = max(1, min(Bd, 2048 // win))
    parts = _cached_call_parts(q8, kn8, vn8, kc_t, vc_t, g, slopes_g, n_new=n_new, nb=nb, step_of=lambda b: b)
    return pl.pallas_call(
        functools.partial(_cached_attn_kernel, **parts['params']),
        grid=(Bd // nb,),
        in_specs=parts['in_specs'],
        out_specs=parts['out_specs'],
        out_shape=parts['out_shape'],
        compiler_params=_cparams(("arbitrary",)),
        name="cached_attn_g%d" % g,
    )(*parts['operands'])


def _split_hosted(rest, n_own_out, hosted):
    n_in, n_out = (7, 4) if hosted else (0, 0)
    hosted_in, rest = rest[:n_in], rest[n_in:]
    own_out, rest = rest[:n_own_out], rest[n_own_out:]
    return hosted_in, own_out, rest[:n_out], rest[n_out:]


def _merge_kernel(x_ref, o0_ref, o1_ref, o2_ref, l0_ref, l1_ref, l2_ref, po_ref, sg_ref,
                  wpa_ref, wpb_ref, wo_ref, ln2_ref, wr_ref, br_ref, tri_ref, ltri_ref,
                  *rest, tm, ts, dils, hosted):
    hosted_in, (h_ref, xl_ref, wc_ref, tc_ref), hosted_out, (ibuf,) = _split_hosted(rest, 4, hosted)
    if hosted:
        _cached_attn_kernel(*hosted_in, *hosted_out, **hosted)
    slots = iter(range(IBUF_SLOTS))

    def token_order(ref, dil):
        if dil == 1:
            return ref[0, 0].astype(f32)
        chunks = []
        for c in range(GROUP_W // LANES):
            base = next(slots) * tm
            for r in range(dil):
                ibuf[pl.ds(base + r, tm // dil, stride=dil), :] = ref[0, r, :, c * LANES:(c + 1) * LANES].astype(f32)
            chunks.append(ibuf[base:base + tm, :])
        return jnp.concatenate(chunks, axis=-1)

    x = x_ref[0]
    l0, l1, l2 = (token_order(r, d) for r, d in zip((l0_ref, l1_ref, l2_ref), dils))
    lm = jnp.maximum(jnp.maximum(l0, l1), l2)
    e0, e1, e2 = jnp.exp(l0 - lm), jnp.exp(l1 - lm), jnp.exp(l2 - lm)
    o0, o1, o2 = (token_order(r, d) for r, d in zip((o0_ref, o1_ref, o2_ref), dils))
    attn = (e0 * o0 + e1 * o1 + e2 * o2) / (e0 + e1 + e2)
    ma = jnp.dot(attn.astype(bf16), wpa_ref[...], preferred_element_type=f32)
    mb = jnp.dot(po_ref[0], wpb_ref[...], preferred_element_type=f32)
    mix = sg_ref[0, :, :D_MODEL].astype(f32) * ma + sg_ref[0, :, D_MODEL:].astype(f32) * mb
    h = x + jnp.dot(mix.astype(bf16), wo_ref[...], preferred_element_type=f32)
    h_ref[...] = h
    xn2 = _rmsnorm_rows(h, ln2_ref[...]).astype(bf16)

    lt = (jnp.dot(xn2, wr_ref[...], preferred_element_type=f32) + br_ref[...]).T
    row8 = lax.broadcasted_iota(jnp.int32, (8, tm), 0)
    gl = jnp.where(row8 < N_EXPERT_GROUPS, lt[0:8], -jnp.inf)
    gmax = jnp.max(gl, axis=0, keepdims=True)
    gidx = jnp.min(jnp.where(gl == gmax, row8, 8), axis=0, keepdims=True)
    pg = 1.0 / jnp.sum(jnp.exp(gl - gmax), axis=0, keepdims=True)
    sel = jnp.zeros((8, tm), f32)
    for g in range(N_EXPERT_GROUPS):
        lo = EXPERT_COL0 + g * EXPERTS_PER_GROUP
        sel = jnp.where(gidx == g, lt[lo:lo + EXPERTS_PER_GROUP], sel)
    v0 = jnp.max(sel, axis=0, keepdims=True)
    i0 = jnp.min(jnp.where(sel == v0, row8, 8), axis=0, keepdims=True)
    sel2 = jnp.where(row8 == i0, -jnp.inf, sel)
    v1 = jnp.max(sel2, axis=0, keepdims=True)
    i1 = jnp.min(jnp.where(sel2 == v1, row8, 8), axis=0, keepdims=True)
    t = jnp.exp(v1 - v0)
    w0 = pg / (1.0 + t)
    w1 = pg * t / (1.0 + t)
    eid0 = gidx * EXPERTS_PER_GROUP + i0
    eid1 = gidx * EXPERTS_PER_GROUP + i1
    erow = lax.broadcasted_iota(jnp.int32, (N_EXPERTS, tm), 0)
    oh0 = erow == eid0
    oh1 = erow == eid1
    cnt = jnp.where(oh0, 1.0, jnp.where(oh1, 1.0, 0.0))
    before = jnp.dot(cnt.astype(bf16), tri_ref[...], preferred_element_type=f32)
    tcount = jnp.sum(cnt, axis=1, keepdims=True)
    units = jnp.floor((tcount + (SEG_ALIGN - 1)) * (1.0 / SEG_ALIGN))
    ub = jnp.broadcast_to(units, (N_EXPERTS, LANES)).astype(bf16)
    seg0 = SEG_ALIGN * jnp.dot(ltri_ref[...], ub, preferred_element_type=f32)[:, 0:1]
    pos_e = seg0 + before
    lpos0 = jnp.sum(jnp.where(oh0, pos_e, 0.0), axis=0, keepdims=True)
    lpos1 = jnp.sum(jnp.where(oh1, pos_e, 0.0), axis=0, keepdims=True)
    prow = lax.broadcasted_iota(jnp.int32, (ts, tm), 0)
    perm = jnp.where(prow == lpos0.astype(jnp.int32), 1.0, jnp.where(prow == lpos1.astype(jnp.int32), 1.0, 0.0))
    xl_ref[...] = _pack_rows(jnp.dot(perm.astype(bf16), xn2, preferred_element_type=f32))
    tc_ref[...] = jnp.broadcast_to(tcount, (N_EXPERTS, LANES))
    rowl = lax.broadcasted_iota(jnp.int32, (LANES, tm), 0)
    wslab = jnp.zeros((LANES, tm), f32)
    for r, val in enumerate((w0, w1, lpos0, lpos1)):
        wslab = jnp.where(rowl == r, val, wslab)
    wc_ref[...] = wslab.T


def _hosted_parts(hosted):
    if not hosted:
        return (), [], [], [], None
    return hosted['operands'], hosted['in_specs'], hosted['out_specs'], hosted['out_shape'], hosted['params']


def _merge(x, o, l, po, sg, w_pa, w_pb, w_o, ln2, w_router, b_router, *, tm, hosted=None):
    B, T, _ = x.shape
    h_ops, h_in, h_out, h_shape, h_params = _hosted_parts(hosted)
    assert T % tm == 0
    nt = T // tm
    n = B * T
    dils = tuple(a.shape[1] for a in o)
    assert all(tm % (8 * d) == 0 for d in dils)
    assert 2 * tm // SEG_ALIGN <= 256
    ts = _sorted_tile_rows(tm)
    tri = (jnp.arange(tm)[:, None] < jnp.arange(tm)[None, :]).astype(bf16)
    ltri = (jnp.arange(N_EXPERTS)[None, :] < jnp.arange(N_EXPERTS)[:, None]).astype(bf16)
    rows3 = lambda w: pl.BlockSpec((1, tm, w), lambda b, i: (b, i, 0))
    flat = lambda r, w: pl.BlockSpec((r, w), lambda b, i: (b * nt + i, 0))
    grp = [pl.BlockSpec((1, d, tm // d, GROUP_W), lambda b, i: (b, 0, i, 0)) for d in dils]
    full = lambda a: pl.BlockSpec(a.shape, lambda b, i: (0,) * a.ndim)
    weights = (w_pa, w_pb, w_o, ln2, w_router, b_router, tri, ltri)
    return pl.pallas_call(
        functools.partial(_merge_kernel, tm=tm, ts=ts, dils=dils, hosted=h_params),
        grid=(B, nt),
        in_specs=[rows3(D_MODEL)] + grp + grp + [rows3(POOL_W), rows3(2 * D_MODEL)]
        + [full(a) for a in weights] + h_in,
        out_specs=[flat(tm, D_MODEL), flat(ts, D_MODEL // 2), flat(tm, LANES),
                   pl.BlockSpec((N_EXPERTS, LANES), lambda b, i: (0, b * nt + i))] + h_out,
        out_shape=[jax.ShapeDtypeStruct((n, D_MODEL), f32),
                   jax.ShapeDtypeStruct((B * nt * ts, D_MODEL // 2), jnp.uint32),
                   jax.ShapeDtypeStruct((n, LANES), f32),
                   jax.ShapeDtypeStruct((N_EXPERTS, B * nt * LANES), f32)] + h_shape,
        scratch_shapes=[pltpu.VMEM((IBUF_SLOTS * tm, LANES), f32)],
        compiler_params=_cparams(("arbitrary", "arbitrary")),
        name="merge_router",
    )(x, *o, *l, po, sg, *weights, *h_ops)


def _sorted_tile_rows(tm):
    return -(-(2 * tm + N_EXPERTS * (SEG_ALIGN - 1)) // LANES) * LANES


def _moe_seg_kernel(blk_e_ref, blk_r0_ref, blk_n_ref, seg_g_ref, seg_c_ref, seg_src_ref, used_ref,
                    xl_hbm, wg_ref, wu_ref, wd_ref, yl_hbm,
                    xbuf, ybuf, sem_in, sem_out, sem_zero, ptr, *, blk, ts, n_tiles, n_blocks):
    b = pl.program_id(0)
    slot = lax.rem(b, 2)
    def copy_rows(src, src_row, dst, dst_row, rows, sem):
        n = pl.multiple_of(rows, SEG_ALIGN)
        pltpu.make_async_copy(src.at[pl.ds(pl.multiple_of(src_row, SEG_ALIGN), n)],
                              dst.at[pl.ds(pl.multiple_of(dst_row, SEG_ALIGN), n)], sem).start()

    def wait_rows(src, dst, rows, sem):
        @pl.when(rows > 0)
        def _():
            n = pl.multiple_of(rows, SEG_ALIGN)
            pltpu.make_async_copy(src.at[pl.ds(0, n)], dst.at[pl.ds(0, n)], sem).wait()

    def for_pieces(bb, stream, fn):
        e, r0, n = blk_e_ref[bb], blk_r0_ref[bb], blk_n_ref[bb]

        @pl.when(n > 0)
        def _():
            def seg(i):
                return jnp.minimum(i, n_tiles - 1) * N_EXPERTS + e

            def cond(i):
                return (i < n_tiles) & (seg_g_ref[seg(i)] < r0 + n)

            def body(i):
                g = seg_g_ref[seg(i)]
                lo = jnp.maximum(g, r0)
                hi = jnp.minimum(g + seg_c_ref[seg(i)], r0 + n)

                @pl.when(hi > lo)
                def _():
                    fn(seg_src_ref[seg(i)] + (lo - g), lo - r0, hi - lo)

                return i + 1

            end = lax.while_loop(cond, body, jnp.where(r0 == 0, 0, ptr[stream]))
            ptr[stream] = jnp.maximum(end - 1, 0)

    def gather(bb):
        s = lax.rem(bb, 2)
        for_pieces(bb, 0, lambda lrow, brow, rows: copy_rows(xl_hbm, lrow, xbuf.at[s], brow, rows, sem_in.at[s]))

    def scatter(bb):
        s = lax.rem(bb, 2)
        for_pieces(bb, 1, lambda lrow, brow, rows: copy_rows(ybuf.at[s], brow, yl_hbm, lrow, rows, sem_out.at[s]))

    def zero_tail(i, go):
        row0 = i * ts + used_ref[i]
        rows = ts - used_ref[i]
        whole = lax.shift_right_logical(rows, jnp.int32(blk.bit_length() - 1))
        zsrc = xbuf.at[1]

        def whole_block(j, c):
            cp = pltpu.make_async_copy(zsrc, yl_hbm.at[pl.ds(pl.multiple_of(row0 + j * blk, SEG_ALIGN), blk)], sem_zero)
            cp.start() if go else cp.wait()
            return c

        lax.fori_loop(0, whole, whole_block, 0)
        rest = rows - whole * blk
        if go:
            @pl.when(rest > 0)
            def _():
                copy_rows(zsrc, 0, yl_hbm, row0 + whole * blk, rest, sem_zero)
        else:
            wait_rows(zsrc, yl_hbm, rest, sem_zero)

    @pl.when(b == 0)
    def _():
        xbuf[...] = jnp.zeros_like(xbuf)
        ptr[0] = 0
        ptr[1] = 0
        gather(0)
        for go in (True, False):
            def per_tile(i, c, go=go):
                zero_tail(i, go)
                return c

            lax.fori_loop(0, n_tiles, per_tile, 0)

    n_b = blk_n_ref[b]
    wait_rows(xl_hbm, xbuf.at[slot], n_b, sem_in.at[slot])

    @pl.when(b + 1 < n_blocks)
    def _():
        gather(b + 1)

    @pl.when(b >= 2)
    def _():
        wait_rows(ybuf.at[slot], yl_hbm, blk_n_ref[b - 2], sem_out.at[slot])

    @pl.when(n_b > 0)
    def _():
        x = _unpack_rows(xbuf[slot])
        hid = jax.nn.silu(jnp.dot(x, wg_ref[0], preferred_element_type=f32)) * jnp.dot(x, wu_ref[0], preferred_element_type=f32)
        y = jnp.dot(hid.astype(bf16), wd_ref[0], preferred_element_type=f32)
        ybuf[slot] = _pack_rows(y.astype(bf16).astype(f32))
        scatter(b)

    @pl.when(b == n_blocks - 1)
    def _():
        @pl.when(b >= 1)
        def _():
            wait_rows(ybuf.at[1 - slot], yl_hbm, blk_n_ref[b - 1], sem_out.at[1 - slot])

        wait_rows(ybuf.at[slot], yl_hbm, n_b, sem_out.at[slot])


def _moe_segments(xl, tables, w_gate, w_up, w_down, *, blk, ts, n_tiles):
    blk_e = tables[0]
    n_blocks = blk_e.shape[0]
    assert blk & (blk - 1) == 0 and blk % SEG_ALIGN == 0
    wspec = lambda shape: pl.BlockSpec((1,) + shape, lambda b, be, *_: (be[b], 0, 0))
    hbm = pl.BlockSpec(memory_space=pl.ANY)
    return pl.pallas_call(
        functools.partial(_moe_seg_kernel, blk=blk, ts=ts, n_tiles=n_tiles, n_blocks=n_blocks),
        grid_spec=pltpu.PrefetchScalarGridSpec(
            num_scalar_prefetch=len(tables),
            grid=(n_blocks,),
            in_specs=[hbm, wspec((D_MODEL, D_EXPERT)), wspec((D_MODEL, D_EXPERT)), wspec((D_EXPERT, D_MODEL))],
            out_specs=hbm,
            scratch_shapes=[pltpu.VMEM((2, blk, D_MODEL // 2), jnp.uint32), pltpu.VMEM((2, blk, D_MODEL // 2), jnp.uint32),
                            pltpu.SemaphoreType.DMA((2,)), pltpu.SemaphoreType.DMA((2,)), pltpu.SemaphoreType.DMA,
                            pltpu.SMEM((2,), jnp.int32)],
        ),
        out_shape=jax.ShapeDtypeStruct(xl.shape, jnp.uint32),
        compiler_params=_cparams(("arbitrary",)),
        name="moe_experts",
    )(*tables, xl, w_gate, w_up, w_down)


def _unsort_kernel(h_ref, wc_ref, yl_ref, *rest, ts, hosted):
    hosted_in, (y_ref,), hosted_out, _ = _split_hosted(rest, 1, hosted)
    if hosted:
        _cached_attn_kernel(*hosted_in, *hosted_out, **hosted)
    w = wc_ref[...]
    yl = _unpack_rows(yl_ref[...])
    col = lax.broadcasted_iota(jnp.int32, (w.shape[0], ts), 1)
    y = h_ref[...]
    for k in range(2):
        pick = jnp.where(col == w[:, 2 + k:3 + k].astype(jnp.int32), 1.0, 0.0).astype(bf16)
        y = y + w[:, k:k + 1] * jnp.dot(pick, yl, preferred_element_type=f32)
    y_ref[...] = y


def _unsort(h, wc, yl, *, tm, ts, hosted=None):
    n = h.shape[0]
    h_ops, h_in, h_out, h_shape, h_params = _hosted_parts(hosted)
    rows = lambda r, w: pl.BlockSpec((r, w), lambda i: (i, 0))
    return pl.pallas_call(
        functools.partial(_unsort_kernel, ts=ts, hosted=h_params),
        grid=(n // tm,),
        in_specs=[rows(tm, D_MODEL), rows(tm, LANES), rows(ts, D_MODEL // 2)] + h_in,
        out_specs=[rows(tm, D_MODEL)] + h_out,
        out_shape=[jax.ShapeDtypeStruct((n, D_MODEL), f32)] + h_shape,
        compiler_params=_cparams(("arbitrary",)),
        name="moe_unsort",
    )(h, wc, yl, *h_ops)


def _mix_and_moe(x, o, l, po, sg, wts, *, tm, blk, host_merge=None, host_unsort=None):
    n = x.shape[0] * x.shape[1]
    n_tiles = n // tm
    nt = x.shape[1] // tm
    ts = _sorted_tile_rows(tm)
    hosted = host_merge(n_tiles, lambda b, i: b * nt + i) if host_merge else None
    h, xl, wc, tc, *merge_hosted = _merge(x, o, l, po, sg, wts['w_pa'], wts['w_pb'], wts['w_o'], wts['ln2'],
                                          wts['w_router'], wts['b_router'], tm=tm, hosted=hosted)
    c8 = (tc[:, ::LANES].T.astype(jnp.int32) + (SEG_ALIGN - 1)) // SEG_ALIGN * SEG_ALIGN
    seg_src = jnp.arange(n_tiles, dtype=jnp.int32)[:, None] * ts + jnp.cumsum(c8, axis=1) - c8
    seg_g = jnp.cumsum(c8, axis=0) - c8
    tot = jnp.sum(c8, axis=0)
    padded = (tot + blk - 1) // blk * blk
    pad_ends = jnp.cumsum(padded)
    n_blocks = -(-(2 * n + n_tiles * N_EXPERTS * (SEG_ALIGN - 1) + N_EXPERTS * (blk - 1)) // blk)
    blk_start = jnp.arange(n_blocks, dtype=jnp.int32) * blk
    blk_e = jnp.minimum(jnp.sum(pad_ends[None, :] <= blk_start[:, None], axis=1), N_EXPERTS - 1).astype(jnp.int32)
    pick = blk_e[:, None] == jnp.arange(N_EXPERTS, dtype=jnp.int32)[None, :]
    blk_r0 = blk_start - jnp.sum(jnp.where(pick, (pad_ends - padded)[None, :], 0), axis=1)
    blk_n = jnp.clip(jnp.sum(jnp.where(pick, tot[None, :], 0), axis=1) - blk_r0, 0, blk)
    used = jnp.sum(c8, axis=1)
    tables = tuple(a.astype(jnp.int32).reshape(-1) for a in (blk_e, blk_r0, blk_n, seg_g, c8, seg_src, used))
    yl = _moe_segments(xl, tables, wts['w_gate'], wts['w_up'], wts['w_down'], blk=blk, ts=ts, n_tiles=n_tiles)
    hosted = host_unsort(n_tiles, lambda i: i) if host_unsort else None
    y, *unsort_hosted = _unsort(h, wc, yl, tm=tm, ts=ts, hosted=hosted)
    return y, merge_hosted, unsort_hosted


def kernel(x_prompt, x_sample, cache_k_w128, cache_v_w128, cache_k_w512, cache_v_w512, cache_k_w2048, cache_v_w2048, state_pool, ln1, w_in, q_gain, k_gain, pool_lin, pool_scale, w_pa, w_pb, w_o, ln2, w_rg, b_rg, w_re, b_re, w_gate, w_up, w_down):
    B, T, D = x_prompt.shape
    Bd, S, _ = x_sample.shape
    past_len = PAST_LEN
    caches = ((cache_k_w128, cache_v_w128), (cache_k_w512, cache_v_w512), (cache_k_w2048, cache_v_w2048))
    slopes = jnp.exp2(-8.0 * jnp.arange(1, N_HEADS + 1, dtype=f32) / N_HEADS).reshape(N_GROUPS, HEADS_PER_GROUP)

    plin_bd = jnp.zeros((POOL_W, POOL_W), f32)
    for g in range(len(POOL_WINDOWS)):
        plin_bd = plin_bd.at[g * POOL_GW:(g + 1) * POOL_GW, g * POOL_GW:(g + 1) * POOL_GW].set(pool_lin[g])
    w_router = jnp.zeros((D, ROUTER_W), f32).at[:, :N_EXPERT_GROUPS].set(w_rg)
    w_router = w_router.at[:, EXPERT_COL0:EXPERT_COL0 + N_EXPERTS].set(w_re)
    b_router = jnp.zeros((1, ROUTER_W), f32).at[0, :N_EXPERT_GROUPS].set(b_rg)
    b_router = b_router.at[0, EXPERT_COL0:EXPERT_COL0 + N_EXPERTS].set(b_re)
    wts = dict(w_pa=w_pa.astype(bf16), w_pb=w_pb.astype(bf16), w_o=w_o.astype(bf16), ln2=ln2.reshape(1, D),
               w_router=w_router.astype(bf16), b_router=b_router)
    proj_w = (ln1.reshape(1, D), w_in.astype(bf16), q_gain.reshape(1, ATTN_W), k_gain.reshape(1, ATTN_W),
              plin_bd.astype(bf16), pool_scale.reshape(1, POOL_W))

    n_s = Bd * S
    xs = x_sample.transpose(1, 0, 2).reshape(n_s, D)
    qs, ks, vs, pos, st, sgs = _proj_sample(xs, *proj_w, state_pool.transpose(1, 0, 2),
                                            n_seq=Bd, n_new=S, past_len=past_len)
    pad8 = lambda a: jnp.pad(a.reshape(S, Bd, GROUP_W).transpose(1, 0, 2), ((0, 0), (0, 8 - S), (0, 0)))

    def cached_operands(g):
        cols = slice(g * GROUP_W, (g + 1) * GROUP_W)
        kc, vc = caches[g]
        w = WINDOWS[g]
        return (pad8(qs[:, cols]), pad8(ks[:, cols]), pad8(vs[:, cols]),
                kc.transpose(0, 2, 3, 1).reshape(Bd, GROUP_W, w), vc.transpose(0, 2, 3, 1).reshape(Bd, GROUP_W, w))

    def host(g, ring=False):
        def parts(steps, step_of):
            if Bd % steps or (ring and Bd != steps):
                return None
            return _cached_call_parts(*cached_operands(g), g, slopes[g], n_new=S, nb=Bd // steps, step_of=step_of,
                                      ring_steps=steps if ring else None)
        return parts

    tm_proj = TM_PROJ
    n_t = T // tm_proj
    hosted = host(2, ring=True)(B * n_t, lambda b, i: b * n_t + i)
    outs = _proj_prompt(x_prompt, *proj_w, tm=tm_proj, hosted=hosted)
    qkv, (po, kt, vt, ut, sg) = outs[:3 * N_GROUPS], outs[3 * N_GROUPS:3 * N_GROUPS + 5]
    cached_out = {}
    if hosted:
        cached_out[2] = outs[3 * N_GROUPS + 5:]
    o, l, (wts['w_gate'], wts['w_up'], wts['w_down']) = zip(*[
        _band_attention(*qkv[3 * g:3 * g + 3], g, slopes[g], cast_w=w) for g, w in enumerate((w_gate, w_up, w_down))])
    y_prompt, in_merge, in_unsort = _mix_and_moe(x_prompt, o, l, po, sg, wts, tm=TM_TOKENS, blk=MOE_BLOCK_PROMPT,
                                                 host_merge=host(1), host_unsort=host(0))
    y_prompt = y_prompt.reshape(B, T, D)
    if in_merge:
        cached_out[1] = in_merge
    if in_unsort:
        cached_out[0] = in_unsort
    tail = kt.shape[2]
    pkv = []
    for g, w in enumerate(WINDOWS):
        for a in (kt, vt):
            a = a.reshape(B, N_HEADS, HEAD_DIM, tail)[:, g * HEADS_PER_GROUP:(g + 1) * HEADS_PER_GROUP, :, tail - w:]
            pkv.append(a.transpose(0, 3, 1, 2))
    p_pool = ut[:, 1:]

    so, sl, skv = [], [], []
    for g, w in enumerate(WINDOWS):
        if g in cached_out:
            og, lg, ko, vo = cached_out[g]
        else:
            og, lg, ko, vo = _cached_attention(*cached_operands(g), g, slopes[g], n_new=S)
        so.append(og[:, :S].transpose(1, 0, 2).reshape(1, 1, n_s, GROUP_W).astype(bf16))
        sl.append(lg[:, :S].transpose(1, 0, 2).reshape(1, 1, n_s, GROUP_W))
        for a in (ko, vo):
            skv.append(a.reshape(Bd, HEADS_PER_GROUP, HEAD_DIM, w).transpose(0, 3, 1, 2))
    y_sample = _mix_and_moe(xs[None], so, sl, pos[None], sgs[None], wts, tm=n_s, blk=MOE_BLOCK_SAMPLE)[0]
    y_sample = y_sample.reshape(S, Bd, D).transpose(1, 0, 2)
    s_pool = st.transpose(1, 0, 2)

    return (y_prompt, y_sample, *pkv, p_pool, *skv, s_pool)
```
